```python
import jax, jax.numpy as jnp
from jax import lax
import numpy as np

D_MODEL = 1024
BATCH = 16
SEQ = 256
DEPTH = 1
DEC_BATCH = 4
DEC_SEQ = 2048
PAST_LEN = 512

GRID_W = 64
D_A = 512
NH_A = 4
DH_A = D_A // NH_A
CHUNK = 128
D_B = 512
CONV_W = 31
D_FF = 2816
FFN_CONV = 3
N_MOD = 6
SPLIT_SIZES = (D_A, D_A, D_A, D_A, NH_A, NH_A, NH_A, NH_A, 2 * D_B, D_MODEL, D_MODEL)
IN_COLS = 4 * D_A + 4 * NH_A + 2 * D_B + 2 * D_MODEL
FORGET_BIAS = 3.0
EPS = 1e-6

kernel_name = "hybrid_mlstm_conformer_dit_step"


def rmsnorm(x, g):
    xf = x.astype(jnp.float32)
    y = xf * lax.rsqrt(jnp.mean(xf * xf, axis=-1, keepdims=True) + EPS)
    return y.astype(x.dtype) * g


def layernorm(x, g, b):
    xf = x.astype(jnp.float32)
    mu = jnp.mean(xf, axis=-1, keepdims=True)
    var = jnp.mean(jnp.square(xf - mu), axis=-1, keepdims=True)
    return ((xf - mu) * lax.rsqrt(var + EPS)).astype(x.dtype) * g + b


def dwconv1d(x, w, b):
    C = x.shape[-1]
    y = lax.conv_general_dilated(x, w.astype(x.dtype)[:, None, :], window_strides=(1,), padding='SAME',
                                 dimension_numbers=('NWC', 'WIO', 'NWC'), feature_group_count=C)
    return y + b


def dwconv_grid(x, w, b):
    B, T, C = x.shape
    rows = T // GRID_W
    xg = x.reshape(B, rows, GRID_W, C)
    y = lax.conv_general_dilated(xg, w.astype(x.dtype)[:, :, None, :], window_strides=(1, 1), padding='SAME',
                                 dimension_numbers=('NHWC', 'HWIO', 'NHWC'), feature_group_count=C)
    return y.reshape(B, T, C) + b


def mlstm_chunked(q, k, v, ig, fg, C0, n0, m0):
    B, NH, T, DH = q.shape
    nc = T // CHUNK
    f32 = jnp.float32
    q, k, v, ig = (a.astype(f32) for a in (q, k, v, ig))
    logf = jax.nn.log_sigmoid(fg.astype(f32))

    def chunks(a):
        return jnp.moveaxis(a.reshape(B, NH, nc, CHUNK, *a.shape[3:]), 2, 0)

    xs = (chunks(q), chunks(k), chunks(v), chunks(ig), chunks(logf))
    tril = jnp.tril(jnp.ones((CHUNK, CHUNK), dtype=bool))

    def step(carry, xc):
        C, n, m = carry
        qc, kc, vc, ic, lfc = xc
        b = jnp.cumsum(lfc, axis=-1)
        g = b + m[..., None]
        Dm = jnp.where(tril, b[..., :, None] - b[..., None, :] + ic[..., None, :], -jnp.inf)
        mj = jnp.maximum(g, jnp.max(Dm, axis=-1))
        inter = jnp.exp(g - mj)
        S = jnp.einsum('bhld,bhsd->bhls', qc, kc) * jnp.exp(Dm - mj[..., None])
        num = inter[..., None] * jnp.einsum('bhld,bhde->bhle', qc, C) + jnp.einsum('bhls,bhse->bhle', S, vc)
        den = inter * jnp.einsum('bhld,bhd->bhl', qc, n) + jnp.sum(S, axis=-1)
        h = num / jnp.maximum(jnp.abs(den), jnp.exp(-mj))[..., None]
        gL = b[..., -1] + m
        wk = b[..., -1:] - b + ic
        m_new = jnp.maximum(gL, jnp.max(wk, axis=-1))
        decay = jnp.exp(gL - m_new)
        wkn = jnp.exp(wk - m_new[..., None])
        C_new = decay[..., None, None] * C + jnp.einsum('bhs,bhsd,bhse->bhde', wkn, kc, vc)
        n_new = decay[..., None] * n + jnp.einsum('bhs,bhsd->bhd', wkn, kc)
        return (C_new, n_new, m_new), h

    (C, n, m), hs = lax.scan(step, (C0.astype(f32), n0.astype(f32), m0.astype(f32)), xs)
    h = jnp.moveaxis(hs, 0, 2).reshape(B, NH, T, DH)
    return h, C, n, m


def mlstm_bidir(q, k, v, i_f, f_f, i_b, f_b, st_f, st_b):
    h_f, Cf, nf, mf = mlstm_chunked(q, k, v, i_f, f_f, *st_f)
    fl = lambda a: jnp.flip(a, axis=2)
    h_b, Cb, nb, mb = mlstm_chunked(fl(q), fl(k), fl(v), fl(i_b), fl(f_b), *st_b)
    return h_f + fl(h_b), (Cf, nf, mf), (Cb, nb, mb)


def trunk_layer(x, mod, st_f, st_b, on_grid, p):
    B, T, _ = x.shape
    sh1, sc1, g1, sh2, sc2, g2 = jnp.split(mod.astype(x.dtype), N_MOD, axis=-1)

    h = rmsnorm(x, p['norm1_g']) * (1 + sc1) + sh1
    proj = jnp.einsum('btd,dc->btc', h, p['w_in']) + p['b_in']
    offs = [int(o) for o in np.cumsum(SPLIT_SIZES)[:-1]]
    q, k, v, o, i_f, f_f, i_b, f_b, u, ga, gb = jnp.split(proj, offs, axis=-1)
    heads = lambda a: a.reshape(B, T, NH_A, DH_A).transpose(0, 2, 1, 3)
    gts = lambda a: a.transpose(0, 2, 1)
    hA, st_f, st_b = mlstm_bidir(heads(q) * (DH_A ** -0.5), heads(k), heads(v),
                                 gts(i_f), gts(f_f), gts(i_b), gts(f_b), st_f, st_b)
    hA = hA * lax.rsqrt(jnp.mean(hA * hA, axis=-1, keepdims=True) + EPS)
    hA = hA.transpose(0, 2, 1, 3).reshape(B, T, D_A).astype(x.dtype) * p['mlstm_norm_g']
    branch_a = jnp.einsum('bte,ed->btd', jax.nn.sigmoid(o) * hA, p['w_proj_a'])

    ua, ub = jnp.split(u, 2, axis=-1)
    z = dwconv1d(ua * jax.nn.sigmoid(ub), p['conv_dw_w'], p['conv_dw_b'])
    z = jax.nn.silu(layernorm(z, p['conv_ln_g'], p['conv_ln_b']))
    branch_b = jnp.einsum('bte,ed->btd', z, p['w_proj_b'])

    merged = jax.nn.sigmoid(ga) * branch_a + jax.nn.sigmoid(gb) * branch_b
    x = x + g1 * jnp.einsum('btd,de->bte', merged, p['w_out'])

    h2 = rmsnorm(x, p['norm2_g']) * (1 + sc2) + sh2
    gate, val = jnp.split(jnp.einsum('btd,df->btf', h2, p['w_up']), 2, axis=-1)
    if on_grid:
        gate = dwconv_grid(gate, p['ffn_dw_w'], p['ffn_dw_b'])
    else:
        gate = dwconv1d(gate, p['ffn_dw_w'][FFN_CONV // 2], p['ffn_dw_b'])
    x = x + g2 * jnp.einsum('btf,fd->btd', jax.nn.gelu(gate) * val, p['w_down'])
    return x, st_f, st_b


def setup_inputs(seed: int = 0) -> dict:
    key = jax.random.key(seed)
    ks = jax.random.split(key, 32)
    nrm = lambda i, shape, s: jax.random.normal(ks[i], shape, jnp.float32) * s
    D = D_MODEL
    b_in = nrm(10, (DEPTH, IN_COLS), 0.02)
    off_ff = 4 * D_A + NH_A
    off_fb = 4 * D_A + 3 * NH_A
    b_in = b_in.at[:, off_ff:off_ff + NH_A].add(FORGET_BIAS).at[:, off_fb:off_fb + NH_A].add(FORGET_BIAS)
    return {
        'x_prompt': nrm(0, (BATCH, SEQ, D), 1.0),
        'x_sample': nrm(1, (DEC_BATCH, DEC_SEQ, D), 1.0),
        'c': nrm(2, (DEC_BATCH, D), 1.0),
        'state_C': nrm(3, (DEC_BATCH, DEPTH, 2, NH_A, DH_A, DH_A), 0.1),
        'state_n': nrm(4, (DEC_BATCH, DEPTH, 2, NH_A, DH_A), 0.1),
        'state_m': nrm(5, (DEC_BATCH, DEPTH, 2, NH_A), 0.5),
        'c_ctx': nrm(6, (D,), 1.0),
        'w_ada': nrm(7, (DEPTH, D, N_MOD * D), 0.5 * D ** -0.5),
        'b_ada': nrm(8, (DEPTH, N_MOD * D), 0.02),
        'norm1_g': 1.0 + nrm(9, (DEPTH, D), 0.02),
        'w_in': nrm(11, (DEPTH, D, IN_COLS), D ** -0.5),
        'b_in': b_in,
        'mlstm_norm_g': 1.0 + nrm(12, (DEPTH, D_A), 0.02),
        'w_proj_a': nrm(13, (DEPTH, D_A, D), D_A ** -0.5),
        'conv_dw_w': nrm(14, (DEPTH, CONV_W, D_B), CONV_W ** -0.5),
        'conv_dw_b': nrm(15, (DEPTH, D_B), 0.02),
        'conv_ln_g': 1.0 + nrm(16, (DEPTH, D_B), 0.02),
        'conv_ln_b': nrm(17, (DEPTH, D_B), 0.02),
        'w_proj_b': nrm(18, (DEPTH, D_B, D), D_B ** -0.5),
        'w_out': nrm(19, (DEPTH, D, D), D ** -0.5),
        'norm2_g': 1.0 + nrm(20, (DEPTH, D), 0.02),
        'w_up': nrm(21, (DEPTH, D, 2 * D_FF), D ** -0.5),
        'ffn_dw_w': nrm(22, (DEPTH, FFN_CONV, FFN_CONV, D_FF), 1.0 / FFN_CONV),
        'ffn_dw_b': nrm(23, (DEPTH, D_FF), 0.02),
        'w_down': nrm(24, (DEPTH, D_FF, D), D_FF ** -0.5),
        'final_norm_g': 1.0 + nrm(25, (D,), 0.02),
    }


def reference(x_prompt, x_sample, c, state_C, state_n, state_m, c_ctx,
              w_ada, b_ada, norm1_g, w_in, b_in, mlstm_norm_g, w_proj_a,
              conv_dw_w, conv_dw_b, conv_ln_g, conv_ln_b, w_proj_b, w_out,
              norm2_g, w_up, ffn_dw_w, ffn_dw_b, w_down, final_norm_g):
    Bp = x_prompt.shape[0]
    f32 = jnp.float32
    xc = x_prompt
    xl = x_sample
    Cs, ns, ms = [], [], []
    for l in range(DEPTH):
        p = {'norm1_g': norm1_g[l], 'w_in': w_in[l], 'b_in': b_in[l], 'mlstm_norm_g': mlstm_norm_g[l],
             'w_proj_a': w_proj_a[l], 'conv_dw_w': conv_dw_w[l], 'conv_dw_b': conv_dw_b[l],
             'conv_ln_g': conv_ln_g[l], 'conv_ln_b': conv_ln_b[l], 'w_proj_b': w_proj_b[l],
             'w_out': w_out[l], 'norm2_g': norm2_g[l], 'w_up': w_up[l], 'ffn_dw_w': ffn_dw_w[l],
             'ffn_dw_b': ffn_dw_b[l], 'w_down': w_down[l]}
        mod_ctx = (jax.nn.silu(c_ctx) @ w_ada[l] + b_ada[l])[None, None, :]
        zC = jnp.zeros((Bp, NH_A, DH_A, DH_A), f32)
        zn = jnp.zeros((Bp, NH_A, DH_A), f32)
        zm = jnp.zeros((Bp, NH_A), f32)
        xc, sf, sb = trunk_layer(xc, mod_ctx, (zC, zn, zm), (zC, zn, zm), False, p)
        Cs.append(jnp.stack([sf[0], sb[0]], axis=1))
        ns.append(jnp.stack([sf[1], sb[1]], axis=1))
        ms.append(jnp.stack([sf[2], sb[2]], axis=1))
        mod_lat = (jax.nn.silu(c) @ w_ada[l] + b_ada[l])[:, None, :]
        st_f = (state_C[:, l, 0], state_n[:, l, 0], state_m[:, l, 0])
        st_b = (state_C[:, l, 1], state_n[:, l, 1], state_m[:, l, 1])
        xl, _, _ = trunk_layer(xl, mod_lat, st_f, st_b, True, p)
    y_prompt = rmsnorm(xc, final_norm_g)
    y_sample = rmsnorm(xl, final_norm_g)
    new_state_C = jnp.stack(Cs, axis=1)
    new_state_n = jnp.stack(ns, axis=1)
    new_state_m = jnp.stack(ms, axis=1)
    return (y_prompt, y_sample, new_state_C, new_state_n, new_state_m)
```

```python
import functools

import jax
import jax.numpy as jnp
from jax import lax
from jax.experimental import pallas as pl
from jax.experimental.pallas import tpu as pltpu

D_MODEL = 1024
D_A = 512
NH_A = 4
DH_A = 128
CHUNK = 128
D_B = 512
CONV_W = 31
CONV_HALO = 16
D_FF = 2816
FF_CHUNK = 256
N_FF_CHUNKS = D_FF // FF_CHUNK
GRID_W = 64
N_MOD = 6
GATE_PAD = 128
EPS = 1e-6
Q_SCALE = DH_A ** -0.5
VMEM_LIMIT = 56 * 1024 * 1024

f32 = jnp.float32
bf16 = jnp.bfloat16


def _rms(x, g):
    return x * lax.rsqrt(jnp.mean(x * x, axis=-1, keepdims=True) + EPS) * g


def _sigmoid(x):
    return 1.0 / (1.0 + jnp.exp(-x))


def _log_sigmoid(x):
    return jnp.minimum(x, 0.0) - jnp.log(1.0 + jnp.exp(-jnp.abs(x)))


def _gelu_tanh(x):
    return 0.5 * x * (1.0 + jnp.tanh(0.7978845608028654 * (x + 0.044715 * (x * x * x))))


def _dot(a, b):
    return jnp.dot(a, b, preferred_element_type=f32)


def _mod_kernel(ct_ref, w_ref, b_ref, o_ref):
    ct = ct_ref[...]
    st = ct * _sigmoid(ct)
    w = w_ref[...]
    rows = [jnp.sum(w * st[:, r:r + 1], axis=0, keepdims=True) for r in range(8)]
    o_ref[...] = jnp.concatenate(rows, axis=0) + b_ref[...]


def _adaln_mod(ct, w_ada, b_ada):
    n = w_ada.shape[1]
    tn = 512
    return pl.pallas_call(
        _mod_kernel,
        grid=(n // tn,),
        in_specs=[
            pl.BlockSpec((D_MODEL, 8), lambda j: (0, 0)),
            pl.BlockSpec((D_MODEL, tn), lambda j: (0, j)),
            pl.BlockSpec((1, tn), lambda j: (0, j)),
        ],
        out_specs=pl.BlockSpec((8, tn), lambda j: (0, j)),
        out_shape=jax.ShapeDtypeStruct((8, n), f32),
        compiler_params=pltpu.CompilerParams(dimension_semantics=("arbitrary",), vmem_limit_bytes=VMEM_LIMIT),
        name="adaln_mod",
    )(ct, w_ada, b_ada)


def _inproj_kernel(x_ref, mod_ref, g_ref, wqkvo_ref, bqkvo_ref, wu_ref, bu_ref, wgab_ref, bgab_ref,
                   wgate_ref, bgate_ref,
                   q_ref, k_ref, v_ref, so_ref, glu_ref, sga_ref, sgb_ref, gates_ref):
    x = x_ref[0]
    mod = mod_ref[0]
    sh1 = mod[:, 0:D_MODEL]
    sc1 = mod[:, D_MODEL:2 * D_MODEL]
    h = (_rms(x, g_ref[...]) * (1.0 + sc1) + sh1).astype(bf16)

    def proj(w_ref, b_ref, lo, hi):
        return _dot(h, w_ref[:, lo:hi]) + b_ref[:, lo:hi]

    q_ref[0] = (proj(wqkvo_ref, bqkvo_ref, 0, D_A) * Q_SCALE).astype(bf16)
    k_ref[0] = proj(wqkvo_ref, bqkvo_ref, D_A, 2 * D_A).astype(bf16)
    v_ref[0] = proj(wqkvo_ref, bqkvo_ref, 2 * D_A, 3 * D_A).astype(bf16)
    so_ref[0] = _sigmoid(proj(wqkvo_ref, bqkvo_ref, 3 * D_A, 4 * D_A))
    ua = proj(wu_ref, bu_ref, 0, D_B)
    ub = proj(wu_ref, bu_ref, D_B, 2 * D_B)
    glu_ref[0] = ua * _sigmoid(ub)
    sga_ref[0] = _sigmoid(proj(wgab_ref, bgab_ref, 0, D_MODEL))
    sgb_ref[0] = _sigmoid(proj(wgab_ref, bgab_ref, D_MODEL, 2 * D_MODEL))
    gates_ref[0] = proj(wgate_ref, bgate_ref, 0, GATE_PAD)


def _in_proj(x, mod3, per_batch_mod, norm_g, wts, tm):
    B, T, _ = x.shape
    nt = T // tm
    tok = lambda n: pl.BlockSpec((1, tm, n), lambda b, t: (b, t, 0))
    full = lambda a: pl.BlockSpec(a.shape, lambda b, t: (0,) * a.ndim)
    mod_map = (lambda b, t: (b, 0, 0)) if per_batch_mod else (lambda b, t: (0, 0, 0))
    sds = lambda n, dt: jax.ShapeDtypeStruct((B, T, n), dt)
    return pl.pallas_call(
        _inproj_kernel,
        grid=(B, nt),
        in_specs=[tok(D_MODEL), pl.BlockSpec((1, 1, N_MOD * D_MODEL), mod_map), full(norm_g)]
        + [full(w) for w in wts],
        out_specs=[tok(D_A), tok(D_A), tok(D_A), tok(D_A), tok(D_B), tok(D_MODEL), tok(D_MODEL), tok(GATE_PAD)],
        out_shape=[sds(D_A, bf16), sds(D_A, bf16), sds(D_A, bf16), sds(D_A, f32), sds(D_B, f32),
                   sds(D_MODEL, f32), sds(D_MODEL, f32), sds(GATE_PAD, f32)],
        compiler_params=pltpu.CompilerParams(dimension_semantics=("arbitrary", "arbitrary"),
                                             vmem_limit_bytes=VMEM_LIMIT),
        name="in_proj",
    )(x, mod3, norm_g, *wts)


def _split3(x):
    hi = x.astype(bf16)
    r1 = x - hi.astype(f32)
    mid = r1.astype(bf16)
    lo = (r1 - mid.astype(f32)).astype(bf16)
    return hi, mid, lo


def _mlstm_kernel(q_ref, k_ref, v_ref, g_ref, c0_ref, n0_ref, m0_ref,
                  hn_ref, cn_ref, nn_ref, mn_ref, *, nc):
    L = CHUNK
    row = lax.broadcasted_iota(jnp.int32, (L, L), 0)
    col = lax.broadcasted_iota(jnp.int32, (L, L), 1)
    tril = row >= col
    triu = row <= col
    tril_b = tril.astype(bf16)
    triu_b = triu.astype(bf16)

    def run_direction(d):
        fwd = d == 0
        mask = tril if fwd else triu
        t_left = tril_b if fwd else triu_b
        t_right = triu_b if fwd else tril_b
        edge = L - 1 if fwd else 0

        def body(j, carry):
            Cs, ns, ms = carry
            cidx = j if fwd else nc - 1 - j
            r0 = pl.multiple_of(cidx * L, L)
            g = g_ref[0, pl.ds(r0, L), :]
            gt = g.T
            bc = sum(_dot(t_left, p) for p in _split3(_log_sigmoid(g)))
            br = sum(_dot(p, t_right) for p in _split3(_log_sigmoid(gt)))
            new_C, new_n, new_m = [], [], []
            for h in range(NH_A):
                ci = 8 * d + h
                cf = 8 * d + NH_A + h
                lanes = slice(h * DH_A, (h + 1) * DH_A)
                qh = q_ref[0, pl.ds(r0, L), lanes]
                kh = k_ref[0, pl.ds(r0, L), lanes]
                vh = v_ref[0, pl.ds(r0, L), lanes]
                C, n, m = Cs[h], ns[h], ms[h]
                i_c = g[:, ci:ci + 1]
                i_r = gt[ci:ci + 1, :]
                b_c = bc[:, cf:cf + 1]
                b_r = br[cf:cf + 1, :]
                dm = jnp.where(mask, b_c - b_r + i_r, -jnp.inf)
                gg = b_c + m
                mj = jnp.maximum(gg, jnp.max(dm, axis=-1, keepdims=True))
                inter = jnp.exp(gg - mj)
                s = lax.dot_general(qh, kh, (((1,), (1,)), ((), ())), preferred_element_type=f32) * jnp.exp(dm - mj)
                num = inter * _dot(qh, C.astype(bf16)) + _dot(s.astype(bf16), vh)
                den = inter * jnp.sum(qh.astype(f32) * n, axis=-1, keepdims=True) + jnp.sum(s, axis=-1, keepdims=True)
                hh = num / jnp.maximum(jnp.abs(den), jnp.exp(-mj))
                if fwd:
                    hn_ref[0, pl.ds(r0, L), lanes] = hh
                else:
                    hs = hn_ref[0, pl.ds(r0, L), lanes] + hh
                    hn_ref[0, pl.ds(r0, L), lanes] = hs * lax.rsqrt(jnp.mean(hs * hs, axis=-1, keepdims=True) + EPS)
                b_last = b_c[edge:edge + 1, :]
                g_last = b_last + m
                wk = b_last - b_c + i_c
                m_new = jnp.maximum(g_last, jnp.max(wk, axis=0, keepdims=True))
                decay = jnp.exp(g_last - m_new)
                kw = jnp.exp(wk - m_new) * kh.astype(f32)
                new_C.append(decay * C + lax.dot_general(kw.astype(bf16), vh, (((0,), (0,)), ((), ())),
                                                         preferred_element_type=f32))
                new_n.append(decay * n + jnp.sum(kw, axis=0, keepdims=True))
                new_m.append(m_new)
            return tuple(new_C), tuple(new_n), tuple(new_m)

        init = (tuple(c0_ref[0, d, h] for h in range(NH_A)),
                tuple(n0_ref[0, NH_A * d + h:NH_A * d + h + 1, :] for h in range(NH_A)),
                tuple(m0_ref[0, NH_A * d + h:NH_A * d + h + 1, 0:1] for h in range(NH_A)))
        Cs, ns, ms = lax.fori_loop(0, nc, body, init)
        for h in range(NH_A):
            cn_ref[0, d, h] = Cs[h]
            nn_ref[0, NH_A * d + h:NH_A * d + h + 1, :] = ns[h]
            mn_ref[0, NH_A * d + h:NH_A * d + h + 1, :] = jnp.broadcast_to(ms[h], (1, DH_A))

    run_direction(0)
    run_direction(1)


def _mlstm(q, k, v, gates, c0, n0, m0):
    B, T, _ = q.shape
    seq = lambda n: pl.BlockSpec((1, T, n), lambda b: (b, 0, 0))
    st_c = pl.BlockSpec((1, 2, NH_A, DH_A, DH_A), lambda b: (b, 0, 0, 0, 0))
    st_v = pl.BlockSpec((1, 2 * NH_A, DH_A), lambda b: (b, 0, 0))
    return pl.pallas_call(
        functools.partial(_mlstm_kernel, nc=T // CHUNK),
        grid=(B,),
        in_specs=[seq(D_A), seq(D_A), seq(D_A), seq(GATE_PAD), st_c, st_v, st_v],
        out_specs=[seq(D_A), st_c, st_v, st_v],
        out_shape=[jax.ShapeDtypeStruct((B, T, D_A), f32),
                   jax.ShapeDtypeStruct((B, 2, NH_A, DH_A, DH_A), f32),
                   jax.ShapeDtypeStruct((B, 2 * NH_A, DH_A), f32),
                   jax.ShapeDtypeStruct((B, 2 * NH_A, DH_A), f32)],
        compiler_params=pltpu.CompilerParams(dimension_semantics=("arbitrary",), vmem_limit_bytes=VMEM_LIMIT),
        name="mlstm",
    )(q, k, v, gates, c0, n0, m0)


def _mix_kernel(x_ref, mod_ref, hn_ref, so_ref, glu_ref, glu_prev_ref, glu_next_ref, sga_ref, sgb_ref,
                ng_ref, wpa_ref, cw_ref, cb_ref, lng_ref, lnb_ref, wpb_ref, wout_ref,
                x1_ref, ext_ref, *, tm, nt):
    t = pl.program_id(1)
    mod = mod_ref[0]
    g1 = mod[:, 2 * D_MODEL:3 * D_MODEL]

    a_in = (so_ref[0] * (hn_ref[0] * ng_ref[...])).astype(bf16)
    branch_a = _dot(a_in, wpa_ref[...])

    H = CONV_HALO
    ext_ref[0:H, :] = jnp.where(t > 0, glu_prev_ref[0], 0.0)
    ext_ref[H:H + tm, :] = glu_ref[0]
    ext_ref[H + tm:2 * H + tm, :] = jnp.where(t < nt - 1, glu_next_ref[0], 0.0)
    off = H - CONV_W // 2
    z = jnp.zeros((tm, D_B), f32)
    for w in range(CONV_W):
        z = z + ext_ref[off + w:off + w + tm, :] * cw_ref[w:w + 1, :]
    z = z + cb_ref[...]
    mu = jnp.mean(z, axis=-1, keepdims=True)
    zc = z - mu
    var = jnp.mean(zc * zc, axis=-1, keepdims=True)
    zn = zc * lax.rsqrt(var + EPS) * lng_ref[...] + lnb_ref[...]
    zs = zn * _sigmoid(zn)
    branch_b = _dot(zs.astype(bf16), wpb_ref[...])

    merged = sga_ref[0] * branch_a + sgb_ref[0] * branch_b
    x1_ref[0] = x_ref[0] + g1 * _dot(merged.astype(bf16), wout_ref[...])


def _mix_out(x, mod3, per_batch_mod, hn, so, glu, sga, sgb, wts, tm):
    B, T, _ = x.shape
    nt = T // tm
    H = CONV_HALO
    r = tm // H
    tok = lambda n: pl.BlockSpec((1, tm, n), lambda b, t: (b, t, 0))
    full = lambda a: pl.BlockSpec(a.shape, lambda b, t: (0,) * a.ndim)
    mod_map = (lambda b, t: (b, 0, 0)) if per_batch_mod else (lambda b, t: (0, 0, 0))
    prev = pl.BlockSpec((1, H, D_B), lambda b, t: (b, jnp.maximum(t * r - 1, 0), 0))
    nxt = pl.BlockSpec((1, H, D_B), lambda b, t: (b, jnp.minimum((t + 1) * r, T // H - 1), 0))
    return pl.pallas_call(
        functools.partial(_mix_kernel, tm=tm, nt=nt),
        grid=(B, nt),
        in_specs=[tok(D_MODEL), pl.BlockSpec((1, 1, N_MOD * D_MODEL), mod_map), tok(D_A), tok(D_A),
                  tok(D_B), prev, nxt, tok(D_MODEL), tok(D_MODEL)] + [full(w) for w in wts],
        out_specs=tok(D_MODEL),
        out_shape=jax.ShapeDtypeStruct((B, T, D_MODEL), f32),
        scratch_shapes=[pltpu.VMEM((tm + 2 * H, D_B), f32)],
        compiler_params=pltpu.CompilerParams(dimension_semantics=("arbitrary", "arbitrary"),
                                             vmem_limit_bytes=VMEM_LIMIT),
        name="mix_out",
    )(x, mod3, hn, so, glu, glu, glu, sga, sgb, *wts)


def _ffn_kernel(*refs, tm, nt, on_grid):
    if on_grid:
        (x_ref, xp_ref, xn_ref, mod_ref, g2n_ref, wg_ref, wv_ref, wd_ref, cw_ref, cb_ref, fg_ref,
         y_ref, h_ref, acc_ref) = refs
    else:
        (x_ref, mod_ref, g2n_ref, wg_ref, wv_ref, wd_ref, cw_ref, cb_ref, fg_ref,
         y_ref, h_ref, acc_ref) = refs
    t = pl.program_id(1)
    halo = GRID_W if on_grid else 0
    te = tm + 2 * halo
    mod = mod_ref[0]
    sh2 = mod[:, 3 * D_MODEL:4 * D_MODEL]
    sc2 = mod[:, 4 * D_MODEL:5 * D_MODEL]
    g2 = mod[:, 5 * D_MODEL:6 * D_MODEL]

    def norm_mod(x):
        return (_rms(x, g2n_ref[...]) * (1.0 + sc2) + sh2).astype(bf16)

    h_ref[halo:halo + tm, :] = norm_mod(x_ref[0])
    if on_grid:
        h_ref[0:halo, :] = norm_mod(xp_ref[0])
        h_ref[halo + tm:te, :] = norm_mod(xn_ref[0])
    acc_ref[...] = jnp.zeros_like(acc_ref)

    pos = lax.broadcasted_iota(jnp.int32, (te, 1), 0)
    if on_grid:
        colpos = jnp.bitwise_and(pos, GRID_W - 1)
        left_ok = colpos >= 1
        right_ok = colpos <= GRID_W - 2
        row_lo = jnp.where(t > 0, 0, halo)
        row_hi = jnp.where(t < nt - 1, te, halo + tm)
        row_ok = jnp.logical_and(pos >= row_lo, pos < row_hi)
    else:
        left_ok = pos >= 1
        right_ok = pos <= te - 2

    def chunk(fc, carry):
        he = h_ref[...]
        gate = _dot(he, wg_ref[fc])
        val = _dot(h_ref[halo:halo + tm, :], wv_ref[fc])
        if on_grid:
            gate = jnp.where(row_ok, gate, 0.0)
        g_l = jnp.where(left_ok, pltpu.roll(gate, 1, axis=0), 0.0)
        g_r = jnp.where(right_ok, pltpu.roll(gate, te - 1, axis=0), 0.0)
        cw = cw_ref[fc]
        if on_grid:
            conv = jnp.zeros((tm, FF_CHUNK), f32)
            for kh in range(3):
                lo = kh * GRID_W
                conv = conv + g_l[lo:lo + tm] * cw[3 * kh:3 * kh + 1]
                conv = conv + gate[lo:lo + tm] * cw[3 * kh + 1:3 * kh + 2]
                conv = conv + g_r[lo:lo + tm] * cw[3 * kh + 2:3 * kh + 3]
        else:
            conv = g_l * cw[3:4] + gate * cw[4:5] + g_r * cw[5:6]
        act = _gelu_tanh(conv + cb_ref[fc]) * val
        acc_ref[...] += _dot(act.astype(bf16), wd_ref[fc])
        return carry

    lax.fori_loop(0, N_FF_CHUNKS, chunk, 0)
    y_ref[0] = _rms(x_ref[0] + g2 * acc_ref[...], fg_ref[...])


def _ffn(x1, mod3, per_batch_mod, wts, tm, on_grid):
    B, T, _ = x1.shape
    nt = T // tm
    halo = GRID_W if on_grid else 0
    tok = pl.BlockSpec((1, tm, D_MODEL), lambda b, t: (b, t, 0))
    full = lambda a: pl.BlockSpec(a.shape, lambda b, t: (0,) * a.ndim)
    mod_map = (lambda b, t: (b, 0, 0)) if per_batch_mod else (lambda b, t: (0, 0, 0))
    in_specs = [tok]
    args = [x1]
    if on_grid:
        r = tm // GRID_W
        in_specs += [pl.BlockSpec((1, GRID_W, D_MODEL), lambda b, t: (b, jnp.maximum(t * r - 1, 0), 0)),
                     pl.BlockSpec((1, GRID_W, D_MODEL), lambda b, t: (b, jnp.minimum((t + 1) * r, T // GRID_W - 1), 0))]
        args += [x1, x1]
    in_specs += [pl.BlockSpec((1, 1, N_MOD * D_MODEL), mod_map)] + [full(w) for w in wts]
    args += [mod3] + list(wts)
    return pl.pallas_call(
        functools.partial(_ffn_kernel, tm=tm, nt=nt, on_grid=on_grid),
        grid=(B, nt),
        in_specs=in_specs,
        out_specs=tok,
        out_shape=jax.ShapeDtypeStruct((B, T, D_MODEL), f32),
        scratch_shapes=[pltpu.VMEM((tm + 2 * halo, D_MODEL), bf16), pltpu.VMEM((tm, D_MODEL), f32)],
        compiler_params=pltpu.CompilerParams(dimension_semantics=("arbitrary", "arbitrary"),
                                             vmem_limit_bytes=VMEM_LIMIT),
        name="ffn_grid" if on_grid else "ffn_ctx",
    )(*args)


def _trunk(x, mod3, per_batch_mod, c0, n0, m0, on_grid, w, tm_proj, tm_mix, tm_ffn):
    q, k, v, so, glu, sga, sgb, gates = _in_proj(x, mod3, per_batch_mod, w["norm1_g"], w["in"], tm_proj)
    hn, cn, nn, mn = _mlstm(q, k, v, gates, c0, n0, m0)
    x1 = _mix_out(x, mod3, per_batch_mod, hn, so, glu, sga, sgb, w["mix"], tm_mix)
    y = _ffn(x1, mod3, per_batch_mod, w["ffn"], tm_ffn, on_grid)
    return y, cn, nn, mn


def kernel(x_prompt, x_sample, c, state_C, state_n, state_m, c_ctx, w_ada, b_ada, norm1_g, w_in, b_in,
           mlstm_norm_g, w_proj_a, conv_dw_w, conv_dw_b, conv_ln_g, conv_ln_b, w_proj_b, w_out, norm2_g,
           w_up, ffn_dw_w, ffn_dw_b, w_down, final_norm_g):
    Bp = x_prompt.shape[0]
    Bl = x_sample.shape[0]
    l = 0
    row = lambda a: a.reshape(1, -1)

    ct = jnp.concatenate([c, c_ctx[None, :], jnp.zeros((8 - Bl - 1, D_MODEL), f32)], axis=0).T
    mod = _adaln_mod(ct, w_ada[l], row(b_ada[l]))
    mod_lat = mod[0:Bl].reshape(Bl, 1, N_MOD * D_MODEL)
    mod_ctx = mod[Bl:Bl + 1].reshape(1, 1, N_MOD * D_MODEL)

    wi, bi = w_in[l], b_in[l]
    o_g = 4 * D_A
    o_u = o_g + 4 * NH_A
    o_ab = o_u + 2 * D_B
    pad_g = GATE_PAD - 4 * NH_A
    w_in_parts = (
        wi[:, 0:o_g].astype(bf16), row(bi[0:o_g]),
        wi[:, o_u:o_ab].astype(bf16), row(bi[o_u:o_ab]),
        wi[:, o_ab:].astype(bf16), row(bi[o_ab:]),
        jnp.pad(wi[:, o_g:o_u], ((0, 0), (0, pad_g))).astype(bf16), row(jnp.pad(bi[o_g:o_u], (0, pad_g))),
    )
    w_mix = (row(mlstm_norm_g[l]), w_proj_a[l].astype(bf16),
             jnp.pad(conv_dw_w[l], ((0, 1), (0, 0))), row(conv_dw_b[l]), row(conv_ln_g[l]), row(conv_ln_b[l]),
             w_proj_b[l].astype(bf16), w_out[l].astype(bf16))
    wu = w_up[l].astype(bf16)
    split_cols = lambda a: a.reshape(D_MODEL, N_FF_CHUNKS, FF_CHUNK).transpose(1, 0, 2)
    w_ffn = (row(norm2_g[l]), split_cols(wu[:, :D_FF]), split_cols(wu[:, D_FF:]),
             w_down[l].astype(bf16).reshape(N_FF_CHUNKS, FF_CHUNK, D_MODEL),
             ffn_dw_w[l].reshape(9, N_FF_CHUNKS, FF_CHUNK).transpose(1, 0, 2),
             ffn_dw_b[l].reshape(N_FF_CHUNKS, 1, FF_CHUNK), row(final_norm_g))
    w = {"norm1_g": row(norm1_g[l]), "in": w_in_parts, "mix": w_mix, "ffn": w_ffn}

    zc = jnp.zeros((Bp, 2, NH_A, DH_A, DH_A), f32)
    zv = jnp.zeros((Bp, 2 * NH_A, DH_A), f32)
    y_prompt, cn, nn, mn = _trunk(x_prompt, mod_ctx, False, zc, zv, zv, False, w, 256, 256, 256)

    c0 = state_C[:, l]
    n0 = state_n[:, l].reshape(Bl, 2 * NH_A, DH_A)
    m0 = jnp.broadcast_to(state_m[:, l].reshape(Bl, 2 * NH_A, 1), (Bl, 2 * NH_A, DH_A))
    y_sample, _, _, _ = _trunk(x_sample, mod_lat, True, c0, n0, m0, True, w, 256, 256, 512)

    new_state_C = cn[:, None]
    new_state_n = nn.reshape(Bp, 1, 2, NH_A, DH_A)
    new_state_m = mn[:, :, 0].reshape(Bp, 1, 2, NH_A)
    return (y_prompt, y_sample, new_state_C, new_state_n, new_state_m)
```

```python
import functools

import jax
import jax.numpy as jnp
from jax import lax
from jax.experimental import pallas as pl
from jax.experimental.pallas import tpu as pltpu

D_MODEL = 1024
D_A = 512
NH_A = 4
DH_A = 128
CHUNK = 128
D_B = 512
CONV_W = 31
CONV_HALO = 16
D_FF = 2816
FF_CHUNK = 256
N_FF_CHUNKS = D_FF // FF_CHUNK
GRID_W = 64
N_MOD = 6
GATE_PAD = 128
EPS = 1e-6
Q_SCALE = DH_A ** -0.5
VMEM_LIMIT = 56 * 1024 * 1024

f32 = jnp.float32
bf16 = jnp.bfloat16


def _rms(x, g):
    return x * lax.rsqrt(jnp.mean(x * x, axis=-1, keepdims=True) + EPS) * g


def _sigmoid(x):
    return 1.0 / (1.0 + jnp.exp(-x))


def _log_sigmoid(x):
    return jnp.minimum(x, 0.0) - jnp.log(1.0 + jnp.exp(-jnp.abs(x)))


GELU_C0 = 0.7978845608028654
GELU_C1 = GELU_C0 * 0.044715


def _dot(a, b):
    return jnp.dot(a, b, preferred_element_type=f32)


def _mod_kernel(ct_ref, w_ref, b_ref, o_ref):
    ct = ct_ref[...]
    st = ct * _sigmoid(ct)
    w = w_ref[...]
    rows = [jnp.sum(w * st[:, r:r + 1], axis=0, keepdims=True) for r in range(8)]
    o_ref[...] = jnp.concatenate(rows, axis=0) + b_ref[...]


def _adaln_mod(ct, w_ada, b_ada):
    n = w_ada.shape[1]
    tn = 512
    return pl.pallas_call(
        _mod_kernel,
        grid=(n // tn,),
        in_specs=[
            pl.BlockSpec((D_MODEL, 8), lambda j: (0, 0)),
            pl.BlockSpec((D_MODEL, tn), lambda j: (0, j)),
            pl.BlockSpec((1, tn), lambda j: (0, j)),
        ],
        out_specs=pl.BlockSpec((8, tn), lambda j: (0, j)),
        out_shape=jax.ShapeDtypeStruct((8, n), f32),
        compiler_params=pltpu.CompilerParams(dimension_semantics=("arbitrary",), vmem_limit_bytes=VMEM_LIMIT),
        name="adaln_mod",
    )(ct, w_ada, b_ada)


def _inproj_kernel(x_ref, mod_ref, g_ref, wqkvo_ref, bqkvo_ref, wu_ref, bu_ref, wgab_ref, bgab_ref,
                   wgate_ref, bgate_ref,
                   q_ref, k_ref, v_ref, so_ref, glu_ref, sga_ref, sgb_ref, gates_ref):
    x = x_ref[0]
    mod = mod_ref[0]
    sh1 = mod[:, 0:D_MODEL]
    sc1 = mod[:, D_MODEL:2 * D_MODEL]
    h = (_rms(x, g_ref[...]) * (1.0 + sc1) + sh1).astype(bf16)

    def proj(w_ref, b_ref, lo, hi):
        return _dot(h, w_ref[:, lo:hi]) + b_ref[:, lo:hi]

    q_ref[0] = (proj(wqkvo_ref, bqkvo_ref, 0, D_A) * Q_SCALE).astype(bf16)
    k_ref[0] = proj(wqkvo_ref, bqkvo_ref, D_A, 2 * D_A).astype(bf16)
    v_ref[0] = proj(wqkvo_ref, bqkvo_ref, 2 * D_A, 3 * D_A).astype(bf16)
    so_ref[0] = _sigmoid(proj(wqkvo_ref, bqkvo_ref, 3 * D_A, 4 * D_A))
    ua = proj(wu_ref, bu_ref, 0, D_B)
    ub = proj(wu_ref, bu_ref, D_B, 2 * D_B)
    glu_ref[0] = ua * _sigmoid(ub)
    sga_ref[0] = _sigmoid(proj(wgab_ref, bgab_ref, 0, D_MODEL))
    sgb_ref[0] = _sigmoid(proj(wgab_ref, bgab_ref, D_MODEL, 2 * D_MODEL))
    gates_ref[0] = proj(wgate_ref, bgate_ref, 0, GATE_PAD)


def _in_proj(x, mod3, per_batch_mod, norm_g, wts, tm):
    B, T, _ = x.shape
    nt = T // tm
    tok = lambda n: pl.BlockSpec((1, tm, n), lambda b, t: (b, t, 0))
    full = lambda a: pl.BlockSpec(a.shape, lambda b, t: (0,) * a.ndim)
    mod_map = (lambda b, t: (b, 0, 0)) if per_batch_mod else (lambda b, t: (0, 0, 0))
    sds = lambda n, dt: jax.ShapeDtypeStruct((B, T, n), dt)
    return pl.pallas_call(
        _inproj_kernel,
        grid=(B, nt),
        in_specs=[tok(D_MODEL), pl.BlockSpec((1, 1, N_MOD * D_MODEL), mod_map), full(norm_g)]
        + [full(w) for w in wts],
        out_specs=[tok(D_A), tok(D_A), tok(D_A), tok(D_A), tok(D_B), tok(D_MODEL), tok(D_MODEL), tok(GATE_PAD)],
        out_shape=[sds(D_A, bf16), sds(D_A, bf16), sds(D_A, bf16), sds(D_A, f32), sds(D_B, f32),
                   sds(D_MODEL, f32), sds(D_MODEL, f32), sds(GATE_PAD, f32)],
        compiler_params=pltpu.CompilerParams(dimension_semantics=("arbitrary", "arbitrary"),
                                             vmem_limit_bytes=VMEM_LIMIT),
        name="in_proj",
    )(x, mod3, norm_g, *wts)


def _split3(x):
    hi = x.astype(bf16)
    r1 = x - hi.astype(f32)
    mid = r1.astype(bf16)
    lo = (r1 - mid.astype(f32)).astype(bf16)
    return hi, mid, lo


def _mlstm_kernel(q_ref, k_ref, v_ref, g_ref, c0_ref, n0_ref, m0_ref,
                  hn_ref, cn_ref, nn_ref, mn_ref, *, nc):
    L = CHUNK
    row = lax.broadcasted_iota(jnp.int32, (L, L), 0)
    col = lax.broadcasted_iota(jnp.int32, (L, L), 1)
    tril = row >= col
    triu = row <= col
    tril_b = tril.astype(bf16)
    triu_b = triu.astype(bf16)

    def run_direction(d):
        fwd = d == 0
        mask = tril if fwd else triu
        t_left = tril_b if fwd else triu_b
        t_right = triu_b if fwd else tril_b
        edge = L - 1 if fwd else 0

        def body(j, carry):
            Cs, ns, ms = carry
            cidx = j if fwd else nc - 1 - j
            r0 = pl.multiple_of(cidx * L, L)
            g = g_ref[0, pl.ds(r0, L), :]
            gt = g.T
            bc = sum(_dot(t_left, p) for p in _split3(_log_sigmoid(g)))
            br = sum(_dot(p, t_right) for p in _split3(_log_sigmoid(gt)))
            new_C, new_n, new_m = [], [], []
            for h in range(NH_A):
                ci = 8 * d + h
                cf = 8 * d + NH_A + h
                lanes = slice(h * DH_A, (h + 1) * DH_A)
                qh = q_ref[0, pl.ds(r0, L), lanes]
                kh = k_ref[0, pl.ds(r0, L), lanes]
                vh = v_ref[0, pl.ds(r0, L), lanes]
                C, n, m = Cs[h], ns[h], ms[h]
                i_c = g[:, ci:ci + 1]
                i_r = gt[ci:ci + 1, :]
                b_c = bc[:, cf:cf + 1]
                b_r = br[cf:cf + 1, :]
                dm = jnp.where(mask, b_c - b_r + i_r, -jnp.inf)
                gg = b_c + m
                mj = jnp.maximum(gg, jnp.max(dm, axis=-1, keepdims=True))
                inter = jnp.exp(gg - mj)
                s = lax.dot_general(qh, kh, (((1,), (1,)), ((), ())), preferred_element_type=f32) * jnp.exp(dm - mj)
                num = inter * _dot(qh, C.astype(bf16)) + _dot(s.astype(bf16), vh)
                den = inter * jnp.sum(qh.astype(f32) * n, axis=-1, keepdims=True) + jnp.sum(s, axis=-1, keepdims=True)
                hh = num / jnp.maximum(jnp.abs(den), jnp.exp(-mj))
                if fwd:
                    hn_ref[0, pl.ds(r0, L), lanes] = hh
                else:
                    hs = hn_ref[0, pl.ds(r0, L), lanes] + hh
                    hn_ref[0, pl.ds(r0, L), lanes] = hs * lax.rsqrt(jnp.mean(hs * hs, axis=-1, keepdims=True) + EPS)
                b_last = b_c[edge:edge + 1, :]
                g_last = b_last + m
                wk = b_last - b_c + i_c
                m_new = jnp.maximum(g_last, jnp.max(wk, axis=0, keepdims=True))
                decay = jnp.exp(g_last - m_new)
                kw = jnp.exp(wk - m_new) * kh.astype(f32)
                new_C.append(decay * C + lax.dot_general(kw.astype(bf16), vh, (((0,), (0,)), ((), ())),
                                                         preferred_element_type=f32))
                new_n.append(decay * n + jnp.sum(kw, axis=0, keepdims=True))
                new_m.append(m_new)
            return tuple(new_C), tuple(new_n), tuple(new_m)

        init = (tuple(c0_ref[0, d, h] for h in range(NH_A)),
                tuple(n0_ref[0, NH_A * d + h:NH_A * d + h + 1, :] for h in range(NH_A)),
                tuple(m0_ref[0, NH_A * d + h:NH_A * d + h + 1, 0:1] for h in range(NH_A)))
        Cs, ns, ms = lax.fori_loop(0, nc, body, init)
        for h in range(NH_A):
            cn_ref[0, d, h] = Cs[h]
            nn_ref[0, NH_A * d + h:NH_A * d + h + 1, :] = ns[h]
            mn_ref[0, NH_A * d + h:NH_A * d + h + 1, :] = jnp.broadcast_to(ms[h], (1, DH_A))

    run_direction(0)
    run_direction(1)


def _mlstm(q, k, v, gates, c0, n0, m0):
    B, T, _ = q.shape
    seq = lambda n: pl.BlockSpec((1, T, n), lambda b: (b, 0, 0))
    st_c = pl.BlockSpec((1, 2, NH_A, DH_A, DH_A), lambda b: (b, 0, 0, 0, 0))
    st_v = pl.BlockSpec((1, 2 * NH_A, DH_A), lambda b: (b, 0, 0))
    return pl.pallas_call(
        functools.partial(_mlstm_kernel, nc=T // CHUNK),
        grid=(B,),
        in_specs=[seq(D_A), seq(D_A), seq(D_A), seq(GATE_PAD), st_c, st_v, st_v],
        out_specs=[seq(D_A), st_c, st_v, st_v],
        out_shape=[jax.ShapeDtypeStruct((B, T, D_A), f32),
                   jax.ShapeDtypeStruct((B, 2, NH_A, DH_A, DH_A), f32),
                   jax.ShapeDtypeStruct((B, 2 * NH_A, DH_A), f32),
                   jax.ShapeDtypeStruct((B, 2 * NH_A, DH_A), f32)],
        compiler_params=pltpu.CompilerParams(dimension_semantics=("arbitrary",), vmem_limit_bytes=VMEM_LIMIT),
        name="mlstm",
    )(q, k, v, gates, c0, n0, m0)


def _mix_kernel(x_ref, mod_ref, hn_ref, so_ref, glu_ref, glu_prev_ref, glu_next_ref, sga_ref, sgb_ref,
                ng_ref, wpa_ref, cw_ref, cb_ref, lng_ref, lnb_ref, wpb_ref, wout_ref,
                x1_ref, ext_ref, *, tm, nt):
    t = pl.program_id(1)
    mod = mod_ref[0]
    g1 = mod[:, 2 * D_MODEL:3 * D_MODEL]

    a_in = (so_ref[0] * (hn_ref[0] * ng_ref[...])).astype(bf16)
    branch_a = _dot(a_in, wpa_ref[...])

    H = CONV_HALO
    ext_ref[0:H, :] = jnp.where(t > 0, glu_prev_ref[0], 0.0)
    ext_ref[H:H + tm, :] = glu_ref[0]
    ext_ref[H + tm:2 * H + tm, :] = jnp.where(t < nt - 1, glu_next_ref[0], 0.0)
    off = H - CONV_W // 2
    z = jnp.zeros((tm, D_B), f32)
    for w in range(CONV_W):
        z = z + ext_ref[off + w:off + w + tm, :] * cw_ref[w:w + 1, :]
    z = z + cb_ref[...]
    mu = jnp.mean(z, axis=-1, keepdims=True)
    zc = z - mu
    var = jnp.mean(zc * zc, axis=-1, keepdims=True)
    zn = zc * lax.rsqrt(var + EPS) * lng_ref[...] + lnb_ref[...]
    zs = zn * _sigmoid(zn)
    branch_b = _dot(zs.astype(bf16), wpb_ref[...])

    merged = sga_ref[0] * branch_a + sgb_ref[0] * branch_b
    x1_ref[0] = x_ref[0] + g1 * _dot(merged.astype(bf16), wout_ref[...])


def _mix_out(x, mod3, per_batch_mod, hn, so, glu, sga, sgb, wts, tm):
    B, T, _ = x.shape
    nt = T // tm
    H = CONV_HALO
    r = tm // H
    tok = lambda n: pl.BlockSpec((1, tm, n), lambda b, t: (b, t, 0))
    full = lambda a: pl.BlockSpec(a.shape, lambda b, t: (0,) * a.ndim)
    mod_map = (lambda b, t: (b, 0, 0)) if per_batch_mod else (lambda b, t: (0, 0, 0))
    prev = pl.BlockSpec((1, H, D_B), lambda b, t: (b, jnp.maximum(t * r - 1, 0), 0))
    nxt = pl.BlockSpec((1, H, D_B), lambda b, t: (b, jnp.minimum((t + 1) * r, T // H - 1), 0))
    return pl.pallas_call(
        functools.partial(_mix_kernel, tm=tm, nt=nt),
        grid=(B, nt),
        in_specs=[tok(D_MODEL), pl.BlockSpec((1, 1, N_MOD * D_MODEL), mod_map), tok(D_A), tok(D_A),
                  tok(D_B), prev, nxt, tok(D_MODEL), tok(D_MODEL)] + [full(w) for w in wts],
        out_specs=tok(D_MODEL),
        out_shape=jax.ShapeDtypeStruct((B, T, D_MODEL), f32),
        scratch_shapes=[pltpu.VMEM((tm + 2 * H, D_B), f32)],
        compiler_params=pltpu.CompilerParams(dimension_semantics=("arbitrary", "arbitrary"),
                                             vmem_limit_bytes=VMEM_LIMIT),
        name="mix_out",
    )(x, mod3, hn, so, glu, glu, glu, sga, sgb, *wts)


def _ffn_kernel(*refs, tm, nt, on_grid):
    if on_grid:
        (x_ref, xp_ref, xn_ref, mod_ref, g2n_ref, wg_ref, wv_ref, wd_ref, cw_ref, cb_ref, fg_ref,
         y_ref, h_ref, gbuf_ref, vbuf_ref, gl_ref, gr_ref, act_ref, acc_ref) = refs
    else:
        (x_ref, mod_ref, g2n_ref, wg_ref, wv_ref, wd_ref, cw_ref, cb_ref, fg_ref,
         y_ref, h_ref, gbuf_ref, vbuf_ref, gl_ref, gr_ref, act_ref, acc_ref) = refs
    t = pl.program_id(1)
    halo = GRID_W if on_grid else 0
    te = tm + 2 * halo
    mod = mod_ref[0]
    sh2 = mod[:, 3 * D_MODEL:4 * D_MODEL]
    sc2 = mod[:, 4 * D_MODEL:5 * D_MODEL]
    g2 = mod[:, 5 * D_MODEL:6 * D_MODEL]

    def norm_mod(x):
        return _rms(x, g2n_ref[...]) * (1.0 + sc2) + sh2

    h_ref[halo:halo + tm, :] = norm_mod(x_ref[0]).astype(bf16)
    if on_grid:
        h_ref[0:halo, :] = jnp.where(t > 0, norm_mod(xp_ref[0]), 0.0).astype(bf16)
        h_ref[halo + tm:te, :] = jnp.where(t < nt - 1, norm_mod(xn_ref[0]), 0.0).astype(bf16)

    SUB = 8
    LANE = 128
    seg = GRID_W if on_grid else te
    sub_row = lax.broadcasted_iota(jnp.int32, (SUB, LANE), 0)
    first_row = sub_row == 0
    last_row = sub_row == SUB - 1
    row_taps = (0, 1, 2) if on_grid else (1,)

    def up(fc, slot):
        gbuf_ref[slot] = _dot(h_ref[...], wg_ref[fc])
        vbuf_ref[slot] = _dot(h_ref[halo:halo + tm, :], wv_ref[fc])

    def gate_act(fc, slot):
        cw = cw_ref[fc]
        cb = cb_ref[fc]
        for c in range(FF_CHUNK // LANE):
            lanes = slice(c * LANE, (c + 1) * LANE)
            for s0 in range(0, te, seg):
                gate = gbuf_ref[slot, s0:s0 + seg, lanes]
                g_l = pltpu.roll(gate, 1, axis=0)
                g_r = pltpu.roll(gate, seg - 1, axis=0)
                gl_ref[s0:s0 + SUB, lanes] = jnp.where(first_row, 0.0, g_l[0:SUB])
                gl_ref[s0 + SUB:s0 + seg, lanes] = g_l[SUB:seg]
                gr_ref[s0:s0 + seg - SUB, lanes] = g_r[0:seg - SUB]
                gr_ref[s0 + seg - SUB:s0 + seg, lanes] = jnp.where(last_row, 0.0, g_r[seg - SUB:seg])
            for r0 in range(0, tm, GRID_W):
                conv = jnp.broadcast_to(cb[:, lanes], (GRID_W, LANE))
                for kh in row_taps:
                    lo = r0 + kh * GRID_W if on_grid else r0
                    conv = conv + gl_ref[lo:lo + GRID_W, lanes] * cw[3 * kh:3 * kh + 1, lanes]
                    conv = conv + gbuf_ref[slot, lo:lo + GRID_W, lanes] * cw[3 * kh + 1:3 * kh + 2, lanes]
                    conv = conv + gr_ref[lo:lo + GRID_W, lanes] * cw[3 * kh + 2:3 * kh + 3, lanes]
                gelu = conv * (0.5 + 0.5 * jnp.tanh(conv * (GELU_C0 + GELU_C1 * (conv * conv))))
                act_ref[fc, r0:r0 + GRID_W, lanes] = (gelu * vbuf_ref[slot, r0:r0 + GRID_W, lanes]).astype(bf16)

    def down(fc):
        return _dot(act_ref[fc], wd_ref[fc])

    def pair(p, with_down):
        fc = 2 * p
        up(fc + 1, 1)
        gate_act(fc, 0)
        up(fc + 2, 0)
        gate_act(fc + 1, 1)
        if with_down:
            acc_ref[...] += down(fc - 2) + down(fc - 1)

    n_pairs = (N_FF_CHUNKS - 1) // 2
    up(0, 0)
    acc_ref[...] = jnp.zeros_like(acc_ref)
    pair(0, False)

    def pair_body(p, carry):
        pair(p, True)
        return carry

    lax.fori_loop(1, n_pairs, pair_body, 0)
    last = N_FF_CHUNKS - 1
    gate_act(last, 0)
    ffn_out = acc_ref[...] + (down(last - 2) + down(last - 1) + down(last))
    y_ref[0] = _rms(x_ref[0] + g2 * ffn_out, fg_ref[...])


def _ffn(x1, mod3, per_batch_mod, wts, tm, on_grid):
    B, T, _ = x1.shape
    nt = T // tm
    halo = GRID_W if on_grid else 0
    tok = pl.BlockSpec((1, tm, D_MODEL), lambda b, t: (b, t, 0))
    full = lambda a: pl.BlockSpec(a.shape, lambda b, t: (0,) * a.ndim)
    mod_map = (lambda b, t: (b, 0, 0)) if per_batch_mod else (lambda b, t: (0, 0, 0))
    in_specs = [tok]
    args = [x1]
    if on_grid:
        r = tm // GRID_W
        in_specs += [pl.BlockSpec((1, GRID_W, D_MODEL), lambda b, t: (b, jnp.maximum(t * r - 1, 0), 0)),
                     pl.BlockSpec((1, GRID_W, D_MODEL), lambda b, t: (b, jnp.minimum((t + 1) * r, T // GRID_W - 1), 0))]
        args += [x1, x1]
    in_specs += [pl.BlockSpec((1, 1, N_MOD * D_MODEL), mod_map)] + [full(w) for w in wts]
    args += [mod3] + list(wts)
    return pl.pallas_call(
        functools.partial(_ffn_kernel, tm=tm, nt=nt, on_grid=on_grid),
        grid=(B, nt),
        in_specs=in_specs,
        out_specs=tok,
        out_shape=jax.ShapeDtypeStruct((B, T, D_MODEL), f32),
        scratch_shapes=[pltpu.VMEM((tm + 2 * halo, D_MODEL), bf16), pltpu.VMEM((2, tm + 2 * halo, FF_CHUNK), f32),
                        pltpu.VMEM((2, tm, FF_CHUNK), f32), pltpu.VMEM((tm + 2 * halo, FF_CHUNK), f32),
                        pltpu.VMEM((tm + 2 * halo, FF_CHUNK), f32), pltpu.VMEM((N_FF_CHUNKS, tm, FF_CHUNK), bf16),
                        pltpu.VMEM((tm, D_MODEL), f32)],
        compiler_params=pltpu.CompilerParams(dimension_semantics=("arbitrary", "arbitrary"),
                                             vmem_limit_bytes=VMEM_LIMIT),
        name="ffn_grid" if on_grid else "ffn_ctx",
    )(*args)


def _trunk(x, mod3, per_batch_mod, c0, n0, m0, on_grid, w, tm_proj, tm_mix, tm_ffn):
    q, k, v, so, glu, sga, sgb, gates = _in_proj(x, mod3, per_batch_mod, w["norm1_g"], w["in"], tm_proj)
    hn, cn, nn, mn = _mlstm(q, k, v, gates, c0, n0, m0)
    x1 = _mix_out(x, mod3, per_batch_mod, hn, so, glu, sga, sgb, w["mix"], tm_mix)
    y = _ffn(x1, mod3, per_batch_mod, w["ffn"], tm_ffn, on_grid)
    return y, cn, nn, mn


def kernel(x_prompt, x_sample, c, state_C, state_n, state_m, c_ctx, w_ada, b_ada, norm1_g, w_in, b_in,
           mlstm_norm_g, w_proj_a, conv_dw_w, conv_dw_b, conv_ln_g, conv_ln_b, w_proj_b, w_out, norm2_g,
           w_up, ffn_dw_w, ffn_dw_b, w_down, final_norm_g):
    Bp = x_prompt.shape[0]
    Bl = x_sample.shape[0]
    l = 0
    row = lambda a: a.reshape(1, -1)

    ct = jnp.concatenate([c, c_ctx[None, :], jnp.zeros((8 - Bl - 1, D_MODEL), f32)], axis=0).T
    mod = _adaln_mod(ct, w_ada[l], row(b_ada[l]))
    mod_lat = mod[0:Bl].reshape(Bl, 1, N_MOD * D_MODEL)
    mod_ctx = mod[Bl:Bl + 1].reshape(1, 1, N_MOD * D_MODEL)

    wi, bi = w_in[l], b_in[l]
    o_g = 4 * D_A
    o_u = o_g + 4 * NH_A
    o_ab = o_u + 2 * D_B
    pad_g = GATE_PAD - 4 * NH_A
    w_in_parts = (
        wi[:, 0:o_g].astype(bf16), row(bi[0:o_g]),
        wi[:, o_u:o_ab].astype(bf16), row(bi[o_u:o_ab]),
        wi[:, o_ab:].astype(bf16), row(bi[o_ab:]),
        jnp.pad(wi[:, o_g:o_u], ((0, 0), (0, pad_g))).astype(bf16), row(jnp.pad(bi[o_g:o_u], (0, pad_g))),
    )
    w_mix = (row(mlstm_norm_g[l]), w_proj_a[l].astype(bf16),
             jnp.pad(conv_dw_w[l], ((0, 1), (0, 0))), row(conv_dw_b[l]), row(conv_ln_g[l]), row(conv_ln_b[l]),
             w_proj_b[l].astype(bf16), w_out[l].astype(bf16))
    wu = w_up[l].astype(bf16)
    split_cols = lambda a: a.reshape(D_MODEL, N_FF_CHUNKS, FF_CHUNK).transpose(1, 0, 2)
    w_ffn = (row(norm2_g[l]), split_cols(wu[:, :D_FF]), split_cols(wu[:, D_FF:]),
             w_down[l].astype(bf16).reshape(N_FF_CHUNKS, FF_CHUNK, D_MODEL),
             ffn_dw_w[l].reshape(9, N_FF_CHUNKS, FF_CHUNK).transpose(1, 0, 2),
             ffn_dw_b[l].reshape(N_FF_CHUNKS, 1, FF_CHUNK), row(final_norm_g))
    w = {"norm1_g": row(norm1_g[l]), "in": w_in_parts, "mix": w_mix, "ffn": w_ffn}

    zc = jnp.zeros((Bp, 2, NH_A, DH_A, DH_A), f32)
    zv = jnp.zeros((Bp, 2 * NH_A, DH_A), f32)
    y_prompt, cn, nn, mn = _trunk(x_prompt, mod_ctx, False, zc, zv, zv, False, w, 256, 256, 256)

    c0 = state_C[:, l]
    n0 = state_n[:, l].reshape(Bl, 2 * NH_A, DH_A)
    m0 = jnp.broadcast_to(state_m[:, l].reshape(Bl, 2 * NH_A, 1), (Bl, 2 * NH_A, DH_A))
    y_sample, _, _, _ = _trunk(x_sample, mod_lat, True, c0, n0, m0, True, w, 256, 256, 512)

    new_state_C = cn[:, None]
    new_state_n = nn.reshape(Bp, 1, 2, NH_A, DH_A)
    new_state_m = mn[:, :, 0].reshape(Bp, 1, 2, NH_A)
    return (y_prompt, y_sample, new_state_C, new_state_n, new_state_m)
```

```python
import functools

import jax
import jax.numpy as jnp
from jax import lax
from jax.experimental import pallas as pl
from jax.experimental.pallas import tpu as pltpu

D_MODEL = 1024
D_A = 512
NH_A = 4
DH_A = 128
CHUNK = 128
D_B = 512
CONV_W = 31
CONV_HALO = 16
CONV_ROWS = 64
D_FF = 2816
FF_CHUNK = 256
N_FF_CHUNKS = D_FF // FF_CHUNK
GRID_W = 64
N_MOD = 6
GATE_PAD = 128
EPS = 1e-6
Q_SCALE = DH_A ** -0.5
VMEM_LIMIT = 56 * 1024 * 1024

f32 = jnp.float32
bf16 = jnp.bfloat16


def _rms(x, g):
    return x * lax.rsqrt(jnp.mean(x * x, axis=-1, keepdims=True) + EPS) * g


def _sigmoid(x):
    return 1.0 / (1.0 + jnp.exp(-x))


def _log_sigmoid(x):
    return jnp.minimum(x, 0.0) - jnp.log(1.0 + jnp.exp(-jnp.abs(x)))


GELU_C0 = 0.7978845608028654
GELU_C1 = GELU_C0 * 0.044715


def _dot(a, b):
    return jnp.dot(a, b, preferred_element_type=f32)


def _mod_kernel(ct_ref, w_ref, b_ref, o_ref):
    ct = ct_ref[...]
    st = ct * _sigmoid(ct)
    w = w_ref[...]
    rows = [jnp.sum(w * st[:, r:r + 1], axis=0, keepdims=True) for r in range(8)]
    o_ref[...] = jnp.concatenate(rows, axis=0) + b_ref[...]


def _adaln_mod(ct, w_ada, b_ada):
    n = w_ada.shape[1]
    tn = 512
    return pl.pallas_call(
        _mod_kernel,
        grid=(n // tn,),
        in_specs=[
            pl.BlockSpec((D_MODEL, 8), lambda j: (0, 0)),
            pl.BlockSpec((D_MODEL, tn), lambda j: (0, j)),
            pl.BlockSpec((1, tn), lambda j: (0, j)),
        ],
        out_specs=pl.BlockSpec((8, tn), lambda j: (0, j)),
        out_shape=jax.ShapeDtypeStruct((8, n), f32),
        compiler_params=pltpu.CompilerParams(dimension_semantics=("arbitrary",), vmem_limit_bytes=VMEM_LIMIT),
        name="adaln_mod",
    )(ct, w_ada, b_ada)


def _inproj_kernel(x_ref, mod_ref, g_ref, wqkvo_ref, bqkvo_ref, wu_ref, bu_ref, wgab_ref, bgab_ref,
                   wgate_ref, bgate_ref,
                   q_ref, k_ref, v_ref, so_ref, glu_ref, sga_ref, sgb_ref, gates_ref):
    x = x_ref[0]
    mod = mod_ref[0]
    sh1 = mod[:, 0:D_MODEL]
    sc1 = mod[:, D_MODEL:2 * D_MODEL]
    h = (_rms(x, g_ref[...]) * (1.0 + sc1) + sh1).astype(bf16)

    def proj(w_ref, b_ref, lo, hi):
        return _dot(h, w_ref[:, lo:hi]) + b_ref[:, lo:hi]

    q_ref[0] = (proj(wqkvo_ref, bqkvo_ref, 0, D_A) * Q_SCALE).astype(bf16)
    k_ref[0] = proj(wqkvo_ref, bqkvo_ref, D_A, 2 * D_A).astype(bf16)
    v_ref[0] = proj(wqkvo_ref, bqkvo_ref, 2 * D_A, 3 * D_A).astype(bf16)
    so_ref[0] = _sigmoid(proj(wqkvo_ref, bqkvo_ref, 3 * D_A, 4 * D_A))
    ua = proj(wu_ref, bu_ref, 0, D_B)
    ub = proj(wu_ref, bu_ref, D_B, 2 * D_B)
    glu_ref[0] = ua * _sigmoid(ub)
    sga_ref[0] = _sigmoid(proj(wgab_ref, bgab_ref, 0, D_MODEL))
    sgb_ref[0] = _sigmoid(proj(wgab_ref, bgab_ref, D_MODEL, 2 * D_MODEL))
    gates_ref[0] = proj(wgate_ref, bgate_ref, 0, GATE_PAD)


def _in_proj(x, mod3, per_batch_mod, norm_g, wts, tm):
    B, T, _ = x.shape
    nt = T // tm
    tok = lambda n: pl.BlockSpec((1, tm, n), lambda b, t: (b, t, 0))
    full = lambda a: pl.BlockSpec(a.shape, lambda b, t: (0,) * a.ndim)
    mod_map = (lambda b, t: (b, 0, 0)) if per_batch_mod else (lambda b, t: (0, 0, 0))
    sds = lambda n, dt: jax.ShapeDtypeStruct((B, T, n), dt)
    return pl.pallas_call(
        _inproj_kernel,
        grid=(B, nt),
        in_specs=[tok(D_MODEL), pl.BlockSpec((1, 1, N_MOD * D_MODEL), mod_map), full(norm_g)]
        + [full(w) for w in wts],
        out_specs=[tok(D_A), tok(D_A), tok(D_A), tok(D_A), tok(D_B), tok(D_MODEL), tok(D_MODEL), tok(GATE_PAD)],
        out_shape=[sds(D_A, bf16), sds(D_A, bf16), sds(D_A, bf16), sds(D_A, f32), sds(D_B, f32),
                   sds(D_MODEL, f32), sds(D_MODEL, f32), sds(GATE_PAD, f32)],
        compiler_params=pltpu.CompilerParams(dimension_semantics=("arbitrary", "arbitrary"),
                                             vmem_limit_bytes=VMEM_LIMIT),
        name="in_proj",
    )(x, mod3, norm_g, *wts)


NEG_BIG = -1e30
ST_U, ST_INTER, ST_EMJ, ST_WKN, ST_DECAY = 0, 8, 16, 24, 32
ST_ROWS = 40


def _chunk_scan(x, op, fill, prefix, lane, width):
    k = 1
    while k < CHUNK:
        if prefix:
            shifted = jnp.where(lane >= k, pltpu.roll(x, k, axis=1), fill)
        else:
            shifted = jnp.where(lane < CHUNK - k, pltpu.roll(x, width - k, axis=1), fill)
        x = op(x, shifted)
        k *= 2
    return x


def _mlstm_kernel(q_ref, k_ref, v_ref, g_ref, c0_ref, n0_ref, m0_ref,
                  hn_ref, cn_ref, nn_ref, mn_ref,
                  st_ref, wt_ref, vt_ref, qt_ref, u_ref, cp_ref, npf_ref, npb_ref, cst_ref, nsf_ref, nsb_ref, *, nseq, nc):
    L = CHUNK
    nct = nseq * nc
    T = nct * L
    NR = 2 * NH_A

    lane = jnp.bitwise_and(lax.broadcasted_iota(jnp.int32, (NR, T), 1), L - 1)
    is_fwd = lax.broadcasted_iota(jnp.int32, (NR, T), 0) < NH_A
    is_fwd_c = lax.broadcasted_iota(jnp.int32, (NR, L), 0) < NH_A
    i_parts, f_parts = [], []
    for c in range(nct):
        gt = g_ref[0, c * L:(c + 1) * L, :].T
        i_parts.append(gt[0:NR])
        f_parts.append(gt[NR:2 * NR])
    ig = jnp.concatenate(i_parts, axis=1)
    lf = _log_sigmoid(jnp.concatenate(f_parts, axis=1))
    scan = functools.partial(_chunk_scan, lane=lane, width=T)
    ps = scan(lf, jnp.add, 0.0, True)
    ss = scan(lf, jnp.add, 0.0, False)
    b = jnp.where(is_fwd, ps, ss)
    btot = ps + ss - lf
    w = ig - b
    cmw = jnp.where(is_fwd, scan(w, jnp.maximum, -jnp.inf, True), scan(w, jnp.maximum, -jnp.inf, False))
    wk = btot - b + ig
    a = jnp.maximum(scan(wk, jnp.maximum, -jnp.inf, True), scan(wk, jnp.maximum, -jnp.inf, False))
    chunk_lanes = lambda cg: slice(cg * L, (cg + 1) * L)
    pad_rows = jnp.zeros((L - NR, L), f32)
    for s in range(nseq):
        ms = [m0_ref[s]]
        for t in range(nc):
            cf, cb = chunk_lanes(s * nc + t), chunk_lanes(s * nc + nc - 1 - t)
            bt = jnp.where(is_fwd_c, btot[:, cf], btot[:, cb])
            at = jnp.where(is_fwd_c, a[:, cf], a[:, cb])
            ms.append(jnp.maximum(bt + ms[-1], at))
        mn_ref[s] = ms[nc]
        for c in range(nc):
            cg = s * nc + c
            sl = chunk_lanes(cg)
            m_prev = jnp.where(is_fwd_c, ms[c], ms[nc - 1 - c])
            m_new = jnp.where(is_fwd_c, ms[c + 1], ms[nc - c])
            mx = jnp.maximum(m_prev, cmw[:, sl])
            st_ref[cg, ST_U:ST_U + NR] = -mx
            st_ref[cg, ST_INTER:ST_INTER + NR] = jnp.exp(m_prev - mx)
            st_ref[cg, ST_EMJ:ST_EMJ + NR] = jnp.exp(-mx - b[:, sl])
            st_ref[cg, ST_WKN:ST_WKN + NR] = jnp.exp(wk[:, sl] - m_new)
            st_ref[cg, ST_DECAY:ST_DECAY + NR] = jnp.exp(btot[:, sl] + m_prev - m_new)
            wt_ref[cg] = jnp.concatenate([w[:, sl], pad_rows], axis=0).T

    row = lax.broadcasted_iota(jnp.int32, (L, L), 0)
    col = lax.broadcasted_iota(jnp.int32, (L, L), 1)
    masks = (row <= col, row >= col)
    first_of8 = lax.broadcasted_iota(jnp.int32, (8, L), 0) == 0

    def row_tile(x):
        return jnp.where(first_of8, x, 0.0)

    head_lanes = [slice(h * DH_A, (h + 1) * DH_A) for h in range(NH_A)]

    def increments(j, carry):
        rows = pl.ds(pl.multiple_of(j * L, L), L)
        for h in range(NH_A):
            kh = k_ref[0, rows, head_lanes[h]]
            vt = v_ref[0, rows, head_lanes[h]].astype(f32).T
            vt_ref[j, h] = vt.astype(bf16)
            qt_ref[j, h] = q_ref[0, rows, head_lanes[h]].astype(f32).T.astype(bf16)
            wf = st_ref[j, ST_WKN + h:ST_WKN + h + 1, :]
            wb = st_ref[j, ST_WKN + NH_A + h:ST_WKN + NH_A + h + 1, :]
            lhs = jnp.concatenate([vt * wf, vt * wb, row_tile(wf), row_tile(wb)], axis=0).astype(bf16)
            u_ref[j, h] = _dot(lhs, kh)
        return carry

    lax.fori_loop(0, nct, increments, 0, unroll=2)

    seq_heads = [(s, h) for s in range(nseq) for h in range(NH_A)]
    for s, h in seq_heads:
        i = s * NH_A + h
        cst_ref[i, 0:DH_A] = c0_ref[s, 0, h].T
        cst_ref[i, DH_A:2 * DH_A] = c0_ref[s, 1, h].T
        nsf_ref[i] = row_tile(n0_ref[s, h:h + 1, :])
        nsb_ref[i] = row_tile(n0_ref[s, NH_A + h:NH_A + h + 1, :])

    def recur(t, carry):
        for s, h in seq_heads:
            i = s * NH_A + h
            jf = s * nc + t
            jb = s * nc + nc - 1 - t
            dec_f = st_ref[jf, ST_DECAY + h:ST_DECAY + h + 1, :]
            dec_b = st_ref[jb, ST_DECAY + NH_A + h:ST_DECAY + NH_A + h + 1, :]
            c_f = cst_ref[i, 0:DH_A]
            c_b = cst_ref[i, DH_A:2 * DH_A]
            cp_ref[jf, h, 0:DH_A] = c_f.astype(bf16)
            cp_ref[jb, h, DH_A:2 * DH_A] = c_b.astype(bf16)
            cst_ref[i, 0:DH_A] = dec_f * c_f + u_ref[jf, h, 0:DH_A]
            cst_ref[i, DH_A:2 * DH_A] = dec_b * c_b + u_ref[jb, h, DH_A:2 * DH_A]
            n_f = nsf_ref[i]
            n_b = nsb_ref[i]
            npf_ref[jf, h] = n_f
            npb_ref[jb, h] = n_b
            nsf_ref[i] = dec_f * n_f + u_ref[jf, h, 2 * DH_A:2 * DH_A + 8]
            nsb_ref[i] = dec_b * n_b + u_ref[jb, h, 2 * DH_A + 8:2 * DH_A + 16]
        return carry

    lax.fori_loop(0, nc, recur, 0)
    for s, h in seq_heads:
        i = s * NH_A + h
        cn_ref[s, 0, h] = cst_ref[i, 0:DH_A].T
        cn_ref[s, 1, h] = cst_ref[i, DH_A:2 * DH_A].T
        nn_ref[s, h:h + 1, :] = nsf_ref[i, 0:1]
        nn_ref[s, NH_A + h:NH_A + h + 1, :] = nsb_ref[i, 0:1]

    def outputs(j, carry):
        rows = pl.ds(pl.multiple_of(j * L, L), L)
        wt = wt_ref[j]
        heads = range(NH_A)
        kqs = []
        for h in heads:
            kh = k_ref[0, rows, head_lanes[h]]
            n_rows = jnp.concatenate([npf_ref[j, h], npb_ref[j, h]], axis=0).astype(bf16)
            kqs.append(_dot(jnp.concatenate([kh, n_rows, cp_ref[j, h]], axis=0), qt_ref[j, h]))
        decays = [[jnp.exp(jnp.where(masks[d], wt[:, NH_A * d + h:NH_A * d + h + 1]
                                     + st_ref[j, ST_U + NH_A * d + h:ST_U + NH_A * d + h + 1, :], NEG_BIG))
                   for d in range(2)] for h in heads]
        s_sums, h_ts = [], []
        for h in heads:
            kq = kqs[h]
            s_sum = None
            h_t = None
            for d in range(2):
                r = NH_A * d + h
                inter = st_ref[j, ST_INTER + r:ST_INTER + r + 1, :]
                emj = st_ref[j, ST_EMJ + r:ST_EMJ + r + 1, :]
                s_t = kq[0:L] * decays[h][d]
                qn = kq[L + 8 * d:L + 8 * d + 1]
                den = inter * qn + jnp.sum(s_t, axis=0, keepdims=True)
                rr = 1.0 / jnp.maximum(jnp.abs(den), emj)
                s_sum = s_t * rr if d == 0 else s_sum + s_t * rr
                part = kq[L + 16 + d * DH_A:L + 16 + (d + 1) * DH_A] * (inter * rr)
                h_t = part if d == 0 else h_t + part
            s_sums.append(s_sum.astype(bf16))
            h_ts.append(h_t)
        h_ts = [h_ts[h] + _dot(vt_ref[j, h], s_sums[h]) for h in heads]
        h_ts = [x * lax.rsqrt(jnp.mean(x * x, axis=0, keepdims=True) + EPS) for x in h_ts]
        for h in heads:
            hn_ref[0, rows, head_lanes[h]] = h_ts[h].T
        return carry

    lax.fori_loop(0, nct, outputs, 0, unroll=2)


def _mlstm(q, k, v, gates, c0, n0, m0, nseq):
    B, T, _ = q.shape
    nc = T // CHUNK
    nct = nseq * nc
    G = B // nseq
    fold = lambda a: a.reshape(G, nseq * T, a.shape[-1])
    seq = lambda n: pl.BlockSpec((1, nseq * T, n), lambda b: (b, 0, 0))
    st_c = pl.BlockSpec((nseq, 2, NH_A, DH_A, DH_A), lambda b: (b, 0, 0, 0, 0))
    st_v = pl.BlockSpec((nseq, 2 * NH_A, DH_A), lambda b: (b, 0, 0))
    hn, cn, nn, mn = pl.pallas_call(
        functools.partial(_mlstm_kernel, nseq=nseq, nc=nc),
        grid=(G,),
        in_specs=[seq(D_A), seq(D_A), seq(D_A), seq(GATE_PAD), st_c, st_v, st_v],
        out_specs=[seq(D_A), st_c, st_v, st_v],
        scratch_shapes=[pltpu.VMEM((nct, ST_ROWS, CHUNK), f32), pltpu.VMEM((nct, CHUNK, GATE_PAD), f32),
                        pltpu.VMEM((nct, NH_A, DH_A, CHUNK), bf16), pltpu.VMEM((nct, NH_A, DH_A, CHUNK), bf16),
                        pltpu.VMEM((nct, NH_A, 2 * DH_A + 16, DH_A), f32),
                        pltpu.VMEM((nct, NH_A, 2 * DH_A, DH_A), bf16),
                        pltpu.VMEM((nct, NH_A, 8, DH_A), f32), pltpu.VMEM((nct, NH_A, 8, DH_A), f32),
                        pltpu.VMEM((nseq * NH_A, 2 * DH_A, DH_A), f32),
                        pltpu.VMEM((nseq * NH_A, 8, DH_A), f32), pltpu.VMEM((nseq * NH_A, 8, DH_A), f32)],
        out_shape=[jax.ShapeDtypeStruct((G, nseq * T, D_A), f32),
                   jax.ShapeDtypeStruct((B, 2, NH_A, DH_A, DH_A), f32),
                   jax.ShapeDtypeStruct((B, 2 * NH_A, DH_A), f32),
                   jax.ShapeDtypeStruct((B, 2 * NH_A, DH_A), f32)],
        compiler_params=pltpu.CompilerParams(dimension_semantics=("arbitrary",), vmem_limit_bytes=VMEM_LIMIT),
        name="mlstm",
    )(fold(q), fold(k), fold(v), fold(gates), c0, n0, m0)
    return hn.reshape(B, T, D_A), cn, nn, mn


def _mix_kernel(x_ref, mod_ref, hn_ref, so_ref, glu_ref, glu_prev_ref, glu_next_ref, sga_ref, sgb_ref,
                ng_ref, wpa_ref, cw_ref, cb_ref, lng_ref, lnb_ref, wpb_ref, wout_ref,
                x1_ref, ext_ref, zs_ref, *, tm, nt):
    t = pl.program_id(1)
    mod = mod_ref[0]
    g1 = mod[:, 2 * D_MODEL:3 * D_MODEL]

    a_in = (so_ref[0] * (hn_ref[0] * ng_ref[...])).astype(bf16)
    branch_a = _dot(a_in, wpa_ref[...])

    H = CONV_HALO
    SUB = 8
    LANE = 128
    te = tm + 2 * H
    ext_ref[0, 0:H, :] = jnp.where(t > 0, glu_prev_ref[0], 0.0)
    ext_ref[0, H:H + tm, :] = glu_ref[0]
    ext_ref[0, H + tm:te, :] = jnp.where(t < nt - 1, glu_next_ref[0], 0.0)
    for c in range(D_B // LANE):
        lanes = slice(c * LANE, (c + 1) * LANE)
        base = ext_ref[0, :, lanes]
        for k in range(1, SUB):
            ext_ref[k, :, lanes] = pltpu.roll(base, te - k, axis=0)
    off = H - CONV_W // 2
    for r0 in range(0, tm, CONV_ROWS):
        parts = []
        for c in range(D_B // LANE):
            lanes = slice(c * LANE, (c + 1) * LANE)
            acc = jnp.broadcast_to(cb_ref[:, lanes], (CONV_ROWS, LANE))
            for w in range(CONV_W):
                k, a = (off + w) % SUB, (off + w) // SUB * SUB
                acc = acc + ext_ref[k, r0 + a:r0 + a + CONV_ROWS, lanes] * cw_ref[w:w + 1, lanes]
            parts.append(acc)
        z = jnp.concatenate(parts, axis=1)
        mu = jnp.mean(z, axis=-1, keepdims=True)
        zc = z - mu
        var = jnp.mean(zc * zc, axis=-1, keepdims=True)
        zn = zc * lax.rsqrt(var + EPS) * lng_ref[...] + lnb_ref[...]
        zs_ref[r0:r0 + CONV_ROWS, :] = (zn * _sigmoid(zn)).astype(bf16)
    branch_b = _dot(zs_ref[...], wpb_ref[...])

    merged = sga_ref[0] * branch_a + sgb_ref[0] * branch_b
    x1_ref[0] = x_ref[0] + g1 * _dot(merged.astype(bf16), wout_ref[...])


def _mix_out(x, mod3, per_batch_mod, hn, so, glu, sga, sgb, wts, tm):
    B, T, _ = x.shape
    nt = T // tm
    H = CONV_HALO
    r = tm // H
    tok = lambda n: pl.BlockSpec((1, tm, n), lambda b, t: (b, t, 0))
    full = lambda a: pl.BlockSpec(a.shape, lambda b, t: (0,) * a.ndim)
    mod_map = (lambda b, t: (b, 0, 0)) if per_batch_mod else (lambda b, t: (0, 0, 0))
    prev = pl.BlockSpec((1, H, D_B), lambda b, t: (b, jnp.maximum(t * r - 1, 0), 0))
    nxt = pl.BlockSpec((1, H, D_B), lambda b, t: (b, jnp.minimum((t + 1) * r, T // H - 1), 0))
    return pl.pallas_call(
        functools.partial(_mix_kernel, tm=tm, nt=nt),
        grid=(B, nt),
        in_specs=[tok(D_MODEL), pl.BlockSpec((1, 1, N_MOD * D_MODEL), mod_map), tok(D_A), tok(D_A),
                  tok(D_B), prev, nxt, tok(D_MODEL), tok(D_MODEL)] + [full(w) for w in wts],
        out_specs=tok(D_MODEL),
        out_shape=jax.ShapeDtypeStruct((B, T, D_MODEL), f32),
        scratch_shapes=[pltpu.VMEM((8, tm + 2 * H, D_B), f32), pltpu.VMEM((tm, D_B), bf16)],
        compiler_params=pltpu.CompilerParams(dimension_semantics=("arbitrary", "arbitrary"),
                                             vmem_limit_bytes=VMEM_LIMIT),
        name="mix_out",
    )(x, mod3, hn, so, glu, glu, glu, sga, sgb, *wts)


def _ffn_kernel(*refs, tm, nt, on_grid):
    if on_grid:
        (x_ref, xp_ref, xn_ref, mod_ref, g2n_ref, wg_ref, wv_ref, wd_ref, cw_ref, cb_ref, fg_ref,
         y_ref, h_ref, gbuf_ref, vbuf_ref, gl_ref, gr_ref, act_ref, acc_ref) = refs
    else:
        (x_ref, mod_ref, g2n_ref, wg_ref, wv_ref, wd_ref, cw_ref, cb_ref, fg_ref,
         y_ref, h_ref, gbuf_ref, vbuf_ref, gl_ref, gr_ref, act_ref, acc_ref) = refs
    t = pl.program_id(1)
    halo = GRID_W if on_grid else 0
    te = tm + 2 * halo
    mod = mod_ref[0]
    sh2 = mod[:, 3 * D_MODEL:4 * D_MODEL]
    sc2 = mod[:, 4 * D_MODEL:5 * D_MODEL]
    g2 = mod[:, 5 * D_MODEL:6 * D_MODEL]

    def norm_mod(x):
        return _rms(x, g2n_ref[...]) * (1.0 + sc2) + sh2

    h_ref[halo:halo + tm, :] = norm_mod(x_ref[0]).astype(bf16)
    if on_grid:
        h_ref[0:halo, :] = jnp.where(t > 0, norm_mod(xp_ref[0]), 0.0).astype(bf16)
        h_ref[halo + tm:te, :] = jnp.where(t < nt - 1, norm_mod(xn_ref[0]), 0.0).astype(bf16)

    SUB = 8
    LANE = 128
    seg = GRID_W if on_grid else te
    sub_row = lax.broadcasted_iota(jnp.int32, (SUB, LANE), 0)
    first_row = sub_row == 0
    last_row = sub_row == SUB - 1
    row_taps = (0, 1, 2) if on_grid else (1,)

    def up(fc, slot):
        gbuf_ref[slot] = _dot(h_ref[...], wg_ref[fc])
        vbuf_ref[slot] = _dot(h_ref[halo:halo + tm, :], wv_ref[fc])

    def gate_act(fc, slot):
        cw = cw_ref[fc]
        cb = cb_ref[fc]
        for c in range(FF_CHUNK // LANE):
            lanes = slice(c * LANE, (c + 1) * LANE)
            for s0 in range(0, te, seg):
                gate = gbuf_ref[slot, s0:s0 + seg, lanes]
                g_l = pltpu.roll(gate, 1, axis=0)
                g_r = pltpu.roll(gate, seg - 1, axis=0)
                gl_ref[s0:s0 + SUB, lanes] = jnp.where(first_row, 0.0, g_l[0:SUB])
                gl_ref[s0 + SUB:s0 + seg, lanes] = g_l[SUB:seg]
                gr_ref[s0:s0 + seg - SUB, lanes] = g_r[0:seg - SUB]
                gr_ref[s0 + seg - SUB:s0 + seg, lanes] = jnp.where(last_row, 0.0, g_r[seg - SUB:seg])
            for r0 in range(0, tm, GRID_W):
                conv = jnp.broadcast_to(cb[:, lanes], (GRID_W, LANE))
                for kh in row_taps:
                    lo = r0 + kh * GRID_W if on_grid else r0
                    conv = conv + gl_ref[lo:lo + GRID_W, lanes] * cw[3 * kh:3 * kh + 1, lanes]
                    conv = conv + gbuf_ref[slot, lo:lo + GRID_W, lanes] * cw[3 * kh + 1:3 * kh + 2, lanes]
                    conv = conv + gr_ref[lo:lo + GRID_W, lanes] * cw[3 * kh + 2:3 * kh + 3, lanes]
                gelu = conv * (0.5 + 0.5 * jnp.tanh(conv * (GELU_C0 + GELU_C1 * (conv * conv))))
                act_ref[fc, r0:r0 + GRID_W, lanes] = (gelu * vbuf_ref[slot, r0:r0 + GRID_W, lanes]).astype(bf16)

    def down(fc):
        return _dot(act_ref[fc], wd_ref[fc])

    def pair(p, with_down):
        fc = 2 * p
        up(fc + 1, 1)
        gate_act(fc, 0)
        up(fc + 2, 0)
        gate_act(fc + 1, 1)
        if with_down:
            acc_ref[...] += down(fc - 2) + down(fc - 1)

    n_pairs = (N_FF_CHUNKS - 1) // 2
    up(0, 0)
    acc_ref[...] = jnp.zeros_like(acc_ref)
    pair(0, False)

    def pair_body(p, carry):
        pair(p, True)
        return carry

    lax.fori_loop(1, n_pairs, pair_body, 0)
    last = N_FF_CHUNKS - 1
    gate_act(last, 0)
    ffn_out = acc_ref[...] + (down(last - 2) + down(last - 1) + down(last))
    y_ref[0] = _rms(x_ref[0] + g2 * ffn_out, fg_ref[...])


def _ffn(x1, mod3, per_batch_mod, wts, tm, on_grid):
    B, T, _ = x1.shape
    nt = T // tm
    halo = GRID_W if on_grid else 0
    tok = pl.BlockSpec((1, tm, D_MODEL), lambda b, t: (b, t, 0))
    full = lambda a: pl.BlockSpec(a.shape, lambda b, t: (0,) * a.ndim)
    mod_map = (lambda b, t: (b, 0, 0)) if per_batch_mod else (lambda b, t: (0, 0, 0))
    in_specs = [tok]
    args = [x1]
    if on_grid:
        r = tm // GRID_W
        in_specs += [pl.BlockSpec((1, GRID_W, D_MODEL), lambda b, t: (b, jnp.maximum(t * r - 1, 0), 0)),
                     pl.BlockSpec((1, GRID_W, D_MODEL), lambda b, t: (b, jnp.minimum((t + 1) * r, T // GRID_W - 1), 0))]
        args += [x1, x1]
    in_specs += [pl.BlockSpec((1, 1, N_MOD * D_MODEL), mod_map)] + [full(w) for w in wts]
    args += [mod3] + list(wts)
    return pl.pallas_call(
        functools.partial(_ffn_kernel, tm=tm, nt=nt, on_grid=on_grid),
        grid=(B, nt),
        in_specs=in_specs,
        out_specs=tok,
        out_shape=jax.ShapeDtypeStruct((B, T, D_MODEL), f32),
        scratch_shapes=[pltpu.VMEM((tm + 2 * halo, D_MODEL), bf16), pltpu.VMEM((2, tm + 2 * halo, FF_CHUNK), f32),
                        pltpu.VMEM((2, tm, FF_CHUNK), f32), pltpu.VMEM((tm + 2 * halo, FF_CHUNK), f32),
                        pltpu.VMEM((tm + 2 * halo, FF_CHUNK), f32), pltpu.VMEM((N_FF_CHUNKS, tm, FF_CHUNK), bf16),
                        pltpu.VMEM((tm, D_MODEL), f32)],
        compiler_params=pltpu.CompilerParams(dimension_semantics=("arbitrary", "arbitrary"),
                                             vmem_limit_bytes=VMEM_LIMIT),
        name="ffn_grid" if on_grid else "ffn_ctx",
    )(*args)


def _trunk(x, mod3, per_batch_mod, c0, n0, m0, on_grid, w, tm_proj, tm_mix, tm_ffn, mlstm_nseq):
    q, k, v, so, glu, sga, sgb, gates = _in_proj(x, mod3, per_batch_mod, w["norm1_g"], w["in"], tm_proj)
    hn, cn, nn, mn = _mlstm(q, k, v, gates, c0, n0, m0, mlstm_nseq)
    x1 = _mix_out(x, mod3, per_batch_mod, hn, so, glu, sga, sgb, w["mix"], tm_mix)
    y = _ffn(x1, mod3, per_batch_mod, w["ffn"], tm_ffn, on_grid)
    return y, cn, nn, mn


def kernel(x_prompt, x_sample, c, state_C, state_n, state_m, c_ctx, w_ada, b_ada, norm1_g, w_in, b_in,
           mlstm_norm_g, w_proj_a, conv_dw_w, conv_dw_b, conv_ln_g, conv_ln_b, w_proj_b, w_out, norm2_g,
           w_up, ffn_dw_w, ffn_dw_b, w_down, final_norm_g):
    Bp = x_prompt.shape[0]
    Bl = x_sample.shape[0]
    l = 0
    row = lambda a: a.reshape(1, -1)

    ct = jnp.concatenate([c, c_ctx[None, :], jnp.zeros((8 - Bl - 1, D_MODEL), f32)], axis=0).T
    mod = _adaln_mod(ct, w_ada[l], row(b_ada[l]))
    mod_lat = mod[0:Bl].reshape(Bl, 1, N_MOD * D_MODEL)
    mod_ctx = mod[Bl:Bl + 1].reshape(1, 1, N_MOD * D_MODEL)

    wi, bi = w_in[l], b_in[l]
    o_g = 4 * D_A
    o_u = o_g + 4 * NH_A
    o_ab = o_u + 2 * D_B
    pad_g = GATE_PAD - 4 * NH_A
    gate_cols = o_g + jnp.array([0, 1, 2, 3, 8, 9, 10, 11, 4, 5, 6, 7, 12, 13, 14, 15])
    w_in_parts = (
        wi[:, 0:o_g].astype(bf16), row(bi[0:o_g]),
        wi[:, o_u:o_ab].astype(bf16), row(bi[o_u:o_ab]),
        wi[:, o_ab:].astype(bf16), row(bi[o_ab:]),
        jnp.pad(wi[:, gate_cols], ((0, 0), (0, pad_g))).astype(bf16), row(jnp.pad(bi[gate_cols], (0, pad_g))),
    )
    w_mix = (row(mlstm_norm_g[l]), w_proj_a[l].astype(bf16),
             jnp.pad(conv_dw_w[l], ((0, 1), (0, 0))), row(conv_dw_b[l]), row(conv_ln_g[l]), row(conv_ln_b[l]),
             w_proj_b[l].astype(bf16), w_out[l].astype(bf16))
    wu = w_up[l].astype(bf16)
    split_cols = lambda a: a.reshape(D_MODEL, N_FF_CHUNKS, FF_CHUNK).transpose(1, 0, 2)
    w_ffn = (row(norm2_g[l]), split_cols(wu[:, :D_FF]), split_cols(wu[:, D_FF:]),
             w_down[l].astype(bf16).reshape(N_FF_CHUNKS, FF_CHUNK, D_MODEL),
             ffn_dw_w[l].reshape(9, N_FF_CHUNKS, FF_CHUNK).transpose(1, 0, 2),
             ffn_dw_b[l].reshape(N_FF_CHUNKS, 1, FF_CHUNK), row(final_norm_g))
    w = {"norm1_g": row(norm1_g[l]), "in": w_in_parts, "mix": w_mix, "ffn": w_ffn}

    zc = jnp.zeros((Bp, 2, NH_A, DH_A, DH_A), f32)
    zv = jnp.zeros((Bp, 2 * NH_A, DH_A), f32)
    y_prompt, cn, nn, mn = _trunk(x_prompt, mod_ctx, False, zc, zv, zv, False, w, 256, 256, 256, 4)

    c0 = state_C[:, l]
    n0 = state_n[:, l].reshape(Bl, 2 * NH_A, DH_A)
    m0 = jnp.broadcast_to(state_m[:, l].reshape(Bl, 2 * NH_A, 1), (Bl, 2 * NH_A, DH_A))
    y_sample, _, _, _ = _trunk(x_sample, mod_lat, True, c0, n0, m0, True, w, 256, 256, 512, 1)

    new_state_C = cn[:, None]
    new_state_n = nn.reshape(Bp, 1, 2, NH_A, DH_A)
    new_state_m = mn[:, :, 0].reshape(Bp, 1, 2, NH_A)
    return (y_prompt, y_sample, new_state_C, new_state_n, new_state_m)
```

```python
import functools

import jax
import jax.numpy as jnp
from jax import lax
from jax.experimental import pallas as pl
from jax.experimental.pallas import tpu as pltpu

D_MODEL = 1024
D_A = 512
NH_A = 4
DH_A = 128
CHUNK = 128
D_B = 512
CONV_W = 31
CONV_HALO = 16
CONV_ROWS = 64
D_FF = 2816
FF_CHUNK = 256
N_FF_CHUNKS = D_FF // FF_CHUNK
GRID_W = 64
N_MOD = 6
GATE_PAD = 128
EPS = 1e-6
Q_SCALE = DH_A ** -0.5
VMEM_LIMIT = 56 * 1024 * 1024

f32 = jnp.float32
bf16 = jnp.bfloat16


def _rms(x, g):
    return x * lax.rsqrt(jnp.mean(x * x, axis=-1, keepdims=True) + EPS) * g


def _sigmoid(x):
    return 1.0 / (1.0 + jnp.exp(-x))


def _log_sigmoid(x):
    return jnp.minimum(x, 0.0) - jnp.log(1.0 + jnp.exp(-jnp.abs(x)))


GELU_C0 = 0.7978845608028654
GELU_C1 = GELU_C0 * 0.044715


def _dot(a, b):
    return jnp.dot(a, b, preferred_element_type=f32)


def _mod_kernel(ct_ref, w_ref, b_ref, o_ref, *, n_rows):
    ct = ct_ref[...]
    st = ct * _sigmoid(ct)
    w = w_ref[...]
    rows = [jnp.sum(w * st[:, r:r + 1], axis=0, keepdims=True) for r in range(n_rows)]
    rows.append(jnp.zeros((8 - n_rows, w.shape[1]), f32))
    o_ref[...] = jnp.concatenate(rows, axis=0) + b_ref[...]


def _adaln_mod(ct, w_ada, b_ada, n_rows):
    n = w_ada.shape[1]
    tn = 512
    return pl.pallas_call(
        functools.partial(_mod_kernel, n_rows=n_rows),
        grid=(n // tn,),
        in_specs=[
            pl.BlockSpec((D_MODEL, 8), lambda j: (0, 0)),
            pl.BlockSpec((D_MODEL, tn), lambda j: (0, j)),
            pl.BlockSpec((1, tn), lambda j: (0, j)),
        ],
        out_specs=pl.BlockSpec((8, tn), lambda j: (0, j)),
        out_shape=jax.ShapeDtypeStruct((8, n), f32),
        compiler_params=pltpu.CompilerParams(dimension_semantics=("arbitrary",), vmem_limit_bytes=VMEM_LIMIT),
        name="adaln_mod",
    )(ct, w_ada, b_ada)


IN_O = 3 * D_A
IN_U = 4 * D_A
IN_GA = IN_U + 2 * D_B
IN_GATE = IN_GA + 2 * D_MODEL
IN_COLS_PACKED = IN_GATE + GATE_PAD


def _inproj_kernel(x_ref, mod_ref, g_ref, w_ref, b_ref,
                   q_ref, k_ref, v_ref, so_ref, glu_ref, sga_ref, sgb_ref, gates_ref):
    x = x_ref[0]
    mod = mod_ref[0]
    sh1 = mod[:, 0:D_MODEL]
    sc1 = mod[:, D_MODEL:2 * D_MODEL]
    h = (_rms(x, g_ref[...]) * (1.0 + sc1) + sh1).astype(bf16)

    def proj(lo, n):
        return _dot(h, w_ref[:, lo:lo + n]) + b_ref[:, lo:lo + n]

    q_ref[0] = (proj(0, D_A) * Q_SCALE).astype(bf16)
    k_ref[0] = proj(D_A, D_A).astype(bf16)
    v_ref[0] = proj(2 * D_A, D_A).astype(bf16)
    so_ref[0] = _sigmoid(proj(IN_O, D_A))
    glu_ref[0] = proj(IN_U, D_B) * _sigmoid(proj(IN_U + D_B, D_B))
    sga_ref[0] = _sigmoid(proj(IN_GA, D_MODEL))
    sgb_ref[0] = _sigmoid(proj(IN_GA + D_MODEL, D_MODEL))
    gates_ref[0] = proj(IN_GATE, GATE_PAD)


def _in_proj(x, mod3, per_batch_mod, norm_g, wts, tm):
    B0, T0, _ = x.shape
    if tm > T0:
        assert not per_batch_mod and tm % T0 == 0 and B0 % (tm // T0) == 0
        x = x.reshape(B0 * T0 // tm, tm, D_MODEL)
    B, T, _ = x.shape
    nt = T // tm
    tok = lambda n: pl.BlockSpec((1, tm, n), lambda b, t: (b, t, 0))
    full = lambda a: pl.BlockSpec(a.shape, lambda b, t: (0,) * a.ndim)
    mod_map = (lambda b, t: (b, 0, 0)) if per_batch_mod else (lambda b, t: (0, 0, 0))
    sds = lambda n, dt: jax.ShapeDtypeStruct((B, T, n), dt)
    outs = pl.pallas_call(
        _inproj_kernel,
        grid=(B, nt),
        in_specs=[tok(D_MODEL), pl.BlockSpec((1, 1, N_MOD * D_MODEL), mod_map), full(norm_g)]
        + [full(w) for w in wts],
        out_specs=[tok(D_A), tok(D_A), tok(D_A), tok(D_A), tok(D_B), tok(D_MODEL), tok(D_MODEL), tok(GATE_PAD)],
        out_shape=[sds(D_A, bf16), sds(D_A, bf16), sds(D_A, bf16), sds(D_A, f32), sds(D_B, f32),
                   sds(D_MODEL, f32), sds(D_MODEL, f32), sds(GATE_PAD, f32)],
        compiler_params=pltpu.CompilerParams(dimension_semantics=("arbitrary", "arbitrary"),
                                             vmem_limit_bytes=VMEM_LIMIT),
        name="in_proj",
    )(x, mod3, norm_g, *wts)
    return [o.reshape(B0, T0, o.shape[-1]) for o in outs]


NEG_BIG = -1e30
ST_U, ST_INTER, ST_EMJ, ST_WKN, ST_DECAY = 0, 8, 16, 24, 32
ST_ROWS = 40


def _chunk_scan(x, op, fill, prefix, lane, width):
    k = 1
    while k < CHUNK:
        if prefix:
            shifted = jnp.where(lane >= k, pltpu.roll(x, k, axis=1), fill)
        else:
            shifted = jnp.where(lane < CHUNK - k, pltpu.roll(x, width - k, axis=1), fill)
        x = op(x, shifted)
        k *= 2
    return x


def _mlstm_kernel(*refs, nseq, nc, state_in, state_out):
    refs = list(refs)
    q_ref, k_ref, v_ref, g_ref = refs[:4]
    del refs[:4]
    if state_in:
        c0_ref, n0_ref, m0_ref = refs[:3]
        del refs[:3]
    hn_ref = refs.pop(0)
    if state_out:
        cn_ref, nn_ref, mn_ref = refs[:3]
        del refs[:3]
    st_ref, wt_ref, vt_ref, qt_ref, u_ref, cp_ref, npf_ref, npb_ref, cst_ref, nsf_ref, nsb_ref = refs
    L = CHUNK
    nct = nseq * nc
    T = nct * L
    NR = 2 * NH_A

    lane = jnp.bitwise_and(lax.broadcasted_iota(jnp.int32, (NR, T), 1), L - 1)
    is_fwd = lax.broadcasted_iota(jnp.int32, (NR, T), 0) < NH_A
    is_fwd_c = lax.broadcasted_iota(jnp.int32, (NR, L), 0) < NH_A
    i_parts, f_parts = [], []
    for c in range(nct):
        gt = g_ref[0, c * L:(c + 1) * L, :].T
        i_parts.append(gt[0:NR])
        f_parts.append(gt[NR:2 * NR])
    ig = jnp.concatenate(i_parts, axis=1)
    lf = _log_sigmoid(jnp.concatenate(f_parts, axis=1))
    scan = functools.partial(_chunk_scan, lane=lane, width=T)
    ps = scan(lf, jnp.add, 0.0, True)
    ss = scan(lf, jnp.add, 0.0, False)
    b = jnp.where(is_fwd, ps, ss)
    btot = ps + ss - lf
    w = ig - b
    cmw = jnp.where(is_fwd, scan(w, jnp.maximum, -jnp.inf, True), scan(w, jnp.maximum, -jnp.inf, False))
    wk = btot - b + ig
    a = jnp.maximum(scan(wk, jnp.maximum, -jnp.inf, True), scan(wk, jnp.maximum, -jnp.inf, False))
    chunk_lanes = lambda cg: slice(cg * L, (cg + 1) * L)
    pad_rows = jnp.zeros((L - NR, L), f32)
    for s in range(nseq):
        ms = [m0_ref[s] if state_in else jnp.zeros((NR, L), f32)]
        for t in range(nc):
            cf, cb = chunk_lanes(s * nc + t), chunk_lanes(s * nc + nc - 1 - t)
            bt = jnp.where(is_fwd_c, btot[:, cf], btot[:, cb])
            at = jnp.where(is_fwd_c, a[:, cf], a[:, cb])
            ms.append(jnp.maximum(bt + ms[-1], at))
        if state_out:
            mn_ref[s] = ms[nc]
        for c in range(nc):
            cg = s * nc + c
            sl = chunk_lanes(cg)
            m_prev = jnp.where(is_fwd_c, ms[c], ms[nc - 1 - c])
            m_new = jnp.where(is_fwd_c, ms[c + 1], ms[nc - c])
            mx = jnp.maximum(m_prev, cmw[:, sl])
            st_ref[cg, ST_U:ST_U + NR] = -mx
            st_ref[cg, ST_INTER:ST_INTER + NR] = jnp.exp(m_prev - mx)
            st_ref[cg, ST_EMJ:ST_EMJ + NR] = jnp.exp(-mx - b[:, sl])
            st_ref[cg, ST_WKN:ST_WKN + NR] = jnp.exp(wk[:, sl] - m_new)
            st_ref[cg, ST_DECAY:ST_DECAY + NR] = jnp.exp(btot[:, sl] + m_prev - m_new)
            wt_ref[cg] = jnp.concatenate([w[:, sl], pad_rows], axis=0).T

    row = lax.broadcasted_iota(jnp.int32, (L, L), 0)
    col = lax.broadcasted_iota(jnp.int32, (L, L), 1)
    masks = (row <= col, row >= col)
    first_of8 = lax.broadcasted_iota(jnp.int32, (8, L), 0) == 0

    def row_tile(x):
        return jnp.where(first_of8, x, 0.0)

    head_lanes = [slice(h * DH_A, (h + 1) * DH_A) for h in range(NH_A)]

    def increments(j, carry):
        rows = pl.ds(pl.multiple_of(j * L, L), L)
        for h in range(NH_A):
            kh = k_ref[0, rows, head_lanes[h]]
            vt = v_ref[0, rows, head_lanes[h]].astype(f32).T
            vt_ref[j, h] = vt.astype(bf16)
            qt_ref[j, h] = q_ref[0, rows, head_lanes[h]].astype(f32).T.astype(bf16)
            wf = st_ref[j, ST_WKN + h:ST_WKN + h + 1, :]
            wb = st_ref[j, ST_WKN + NH_A + h:ST_WKN + NH_A + h + 1, :]
            lhs = jnp.concatenate([vt * wf, vt * wb, row_tile(wf), row_tile(wb)], axis=0).astype(bf16)
            u_ref[j, h] = _dot(lhs, kh)
        return carry

    lax.fori_loop(0, nct, increments, 0, unroll=2)

    seq_heads = [(s, h) for s in range(nseq) for h in range(NH_A)]
    for s, h in seq_heads:
        i = s * NH_A + h
        if state_in:
            cst_ref[i, 0:DH_A] = c0_ref[s, 0, h].T
            cst_ref[i, DH_A:2 * DH_A] = c0_ref[s, 1, h].T
            nsf_ref[i] = row_tile(n0_ref[s, h:h + 1, :])
            nsb_ref[i] = row_tile(n0_ref[s, NH_A + h:NH_A + h + 1, :])
        else:
            cst_ref[i] = jnp.zeros((2 * DH_A, DH_A), f32)
            nsf_ref[i] = jnp.zeros((8, DH_A), f32)
            nsb_ref[i] = jnp.zeros((8, DH_A), f32)

    def recur(t, carry):
        for s, h in seq_heads:
            i = s * NH_A + h
            jf = s * nc + t
            jb = s * nc + nc - 1 - t
            dec_f = st_ref[jf, ST_DECAY + h:ST_DECAY + h + 1, :]
            dec_b = st_ref[jb, ST_DECAY + NH_A + h:ST_DECAY + NH_A + h + 1, :]
            c_f = cst_ref[i, 0:DH_A]
            c_b = cst_ref[i, DH_A:2 * DH_A]
            cp_ref[jf, h, 0:DH_A] = c_f.astype(bf16)
            cp_ref[jb, h, DH_A:2 * DH_A] = c_b.astype(bf16)
            cst_ref[i, 0:DH_A] = dec_f * c_f + u_ref[jf, h, 0:DH_A]
            cst_ref[i, DH_A:2 * DH_A] = dec_b * c_b + u_ref[jb, h, DH_A:2 * DH_A]
            n_f = nsf_ref[i]
            n_b = nsb_ref[i]
            npf_ref[jf, h] = n_f
            npb_ref[jb, h] = n_b
            nsf_ref[i] = dec_f * n_f + u_ref[jf, h, 2 * DH_A:2 * DH_A + 8]
            nsb_ref[i] = dec_b * n_b + u_ref[jb, h, 2 * DH_A + 8:2 * DH_A + 16]
        return carry

    lax.fori_loop(0, nc, recur, 0)
    for s, h in seq_heads if state_out else ():
        i = s * NH_A + h
        cn_ref[s, 0, h] = cst_ref[i, 0:DH_A].T
        cn_ref[s, 1, h] = cst_ref[i, DH_A:2 * DH_A].T
        nn_ref[s, h:h + 1, :] = nsf_ref[i, 0:1]
        nn_ref[s, NH_A + h:NH_A + h + 1, :] = nsb_ref[i, 0:1]

    def outputs(j, carry):
        rows = pl.ds(pl.multiple_of(j * L, L), L)
        wt = wt_ref[j]
        heads = range(NH_A)
        kqs = []
        for h in heads:
            kh = k_ref[0, rows, head_lanes[h]]
            n_rows = jnp.concatenate([npf_ref[j, h], npb_ref[j, h]], axis=0).astype(bf16)
            kqs.append(_dot(jnp.concatenate([kh, n_rows, cp_ref[j, h]], axis=0), qt_ref[j, h]))
        decays = [[jnp.exp(jnp.where(masks[d], wt[:, NH_A * d + h:NH_A * d + h + 1]
                                     + st_ref[j, ST_U + NH_A * d + h:ST_U + NH_A * d + h + 1, :], NEG_BIG))
                   for d in range(2)] for h in heads]
        s_sums, h_ts = [], []
        for h in heads:
            kq = kqs[h]
            s_sum = None
            h_t = None
            for d in range(2):
                r = NH_A * d + h
                inter = st_ref[j, ST_INTER + r:ST_INTER + r + 1, :]
                emj = st_ref[j, ST_EMJ + r:ST_EMJ + r + 1, :]
                s_t = kq[0:L] * decays[h][d]
                qn = kq[L + 8 * d:L + 8 * d + 1]
                den = inter * qn + jnp.sum(s_t, axis=0, keepdims=True)
                rr = 1.0 / jnp.maximum(jnp.abs(den), emj)
                s_sum = s_t * rr if d == 0 else s_sum + s_t * rr
                part = kq[L + 16 + d * DH_A:L + 16 + (d + 1) * DH_A] * (inter * rr)
                h_t = part if d == 0 else h_t + part
            s_sums.append(s_sum.astype(bf16))
            h_ts.append(h_t)
        h_ts = [h_ts[h] + _dot(vt_ref[j, h], s_sums[h]) for h in heads]
        h_ts = [x * lax.rsqrt(jnp.mean(x * x, axis=0, keepdims=True) + EPS) for x in h_ts]
        for h in heads:
            hn_ref[0, rows, head_lanes[h]] = h_ts[h].T
        return carry

    lax.fori_loop(0, nct, outputs, 0, unroll=2)


def _mlstm(q, k, v, gates, state, nseq, state_out):
    B, T, _ = q.shape
    nc = T // CHUNK
    nct = nseq * nc
    G = B // nseq
    fold = lambda a: a.reshape(G, nseq * T, a.shape[-1])
    seq = lambda n: pl.BlockSpec((1, nseq * T, n), lambda b: (b, 0, 0))
    st_c = pl.BlockSpec((nseq, 2, NH_A, DH_A, DH_A), lambda b: (b, 0, 0, 0, 0))
    st_v = pl.BlockSpec((nseq, 2 * NH_A, DH_A), lambda b: (b, 0, 0))
    state_specs = [st_c, st_v, st_v]
    state_shapes = [jax.ShapeDtypeStruct((B, 2, NH_A, DH_A, DH_A), f32),
                    jax.ShapeDtypeStruct((B, 2 * NH_A, DH_A), f32),
                    jax.ShapeDtypeStruct((B, 2 * NH_A, DH_A), f32)]
    outs = pl.pallas_call(
        functools.partial(_mlstm_kernel, nseq=nseq, nc=nc, state_in=state is not None, state_out=state_out),
        grid=(G,),
        in_specs=[seq(D_A), seq(D_A), seq(D_A), seq(GATE_PAD)] + (state_specs if state is not None else []),
        out_specs=[seq(D_A)] + (state_specs if state_out else []),
        scratch_shapes=[pltpu.VMEM((nct, ST_ROWS, CHUNK), f32), pltpu.VMEM((nct, CHUNK, GATE_PAD), f32),
                        pltpu.VMEM((nct, NH_A, DH_A, CHUNK), bf16), pltpu.VMEM((nct, NH_A, DH_A, CHUNK), bf16),
                        pltpu.VMEM((nct, NH_A, 2 * DH_A + 16, DH_A), f32),
                        pltpu.VMEM((nct, NH_A, 2 * DH_A, DH_A), bf16),
                        pltpu.VMEM((nct, NH_A, 8, DH_A), f32), pltpu.VMEM((nct, NH_A, 8, DH_A), f32),
                        pltpu.VMEM((nseq * NH_A, 2 * DH_A, DH_A), f32),
                        pltpu.VMEM((nseq * NH_A, 8, DH_A), f32), pltpu.VMEM((nseq * NH_A, 8, DH_A), f32)],
        out_shape=[jax.ShapeDtypeStruct((G, nseq * T, D_A), f32)] + (state_shapes if state_out else []),
        compiler_params=pltpu.CompilerParams(dimension_semantics=("arbitrary",), vmem_limit_bytes=VMEM_LIMIT),
        name="mlstm",
    )(fold(q), fold(k), fold(v), fold(gates), *(state if state is not None else ()))
    return (outs[0].reshape(B, T, D_A),) + tuple(outs[1:])


def _mix_kernel(x_ref, mod_ref, hn_ref, so_ref, glu_ref, glu_prev_ref, glu_next_ref, sga_ref, sgb_ref,
                ng_ref, wpa_ref, cw_ref, cb_ref, lng_ref, lnb_ref, wpb_ref, wout_ref,
                x1_ref, ext_ref, zs_ref, *, tm, nt):
    t = pl.program_id(1)
    mod = mod_ref[0]
    g1 = mod[:, 2 * D_MODEL:3 * D_MODEL]

    a_in = (so_ref[0] * (hn_ref[0] * ng_ref[...])).astype(bf16)
    branch_a = _dot(a_in, wpa_ref[...])

    H = CONV_HALO
    SUB = 8
    LANE = 128
    te = tm + 2 * H
    ext_ref[0, 0:H, :] = jnp.where(t > 0, glu_prev_ref[0], 0.0)
    ext_ref[0, H:H + tm, :] = glu_ref[0]
    ext_ref[0, H + tm:te, :] = jnp.where(t < nt - 1, glu_next_ref[0], 0.0)
    for c in range(D_B // LANE):
        lanes = slice(c * LANE, (c + 1) * LANE)
        base = ext_ref[0, :, lanes]
        for k in range(1, SUB):
            ext_ref[k, :, lanes] = pltpu.roll(base, te - k, axis=0)
    off = H - CONV_W // 2
    for r0 in range(0, tm, CONV_ROWS):
        parts = []
        for c in range(D_B // LANE):
            lanes = slice(c * LANE, (c + 1) * LANE)
            acc = jnp.broadcast_to(cb_ref[:, lanes], (CONV_ROWS, LANE))
            for w in range(CONV_W):
                k, a = (off + w) % SUB, (off + w) // SUB * SUB
                acc = acc + ext_ref[k, r0 + a:r0 + a + CONV_ROWS, lanes] * cw_ref[w:w + 1, lanes]
            parts.append(acc)
        z = jnp.concatenate(parts, axis=1)
        mu = jnp.mean(z, axis=-1, keepdims=True)
        zc = z - mu
        var = jnp.mean(zc * zc, axis=-1, keepdims=True)
        zn = zc * lax.rsqrt(var + EPS) * lng_ref[...] + lnb_ref[...]
        zs_ref[r0:r0 + CONV_ROWS, :] = (zn * _sigmoid(zn)).astype(bf16)
    branch_b = _dot(zs_ref[...], wpb_ref[...])

    merged = sga_ref[0] * branch_a + sgb_ref[0] * branch_b
    x1_ref[0] = x_ref[0] + g1 * _dot(merged.astype(bf16), wout_ref[...])


def _mix_out(x, mod3, per_batch_mod, hn, so, glu, sga, sgb, wts, tm):
    B, T, _ = x.shape
    nt = T // tm
    H = CONV_HALO
    r = tm // H
    tok = lambda n: pl.BlockSpec((1, tm, n), lambda b, t: (b, t, 0))
    full = lambda a: pl.BlockSpec(a.shape, lambda b, t: (0,) * a.ndim)
    mod_map = (lambda b, t: (b, 0, 0)) if per_batch_mod else (lambda b, t: (0, 0, 0))
    prev = pl.BlockSpec((1, H, D_B), lambda b, t: (b, jnp.maximum(t * r - 1, 0), 0))
    nxt = pl.BlockSpec((1, H, D_B), lambda b, t: (b, jnp.minimum((t + 1) * r, T // H - 1), 0))
    return pl.pallas_call(
        functools.partial(_mix_kernel, tm=tm, nt=nt),
        grid=(B, nt),
        in_specs=[tok(D_MODEL), pl.BlockSpec((1, 1, N_MOD * D_MODEL), mod_map), tok(D_A), tok(D_A),
                  tok(D_B), prev, nxt, tok(D_MODEL), tok(D_MODEL)] + [full(w) for w in wts],
        out_specs=tok(D_MODEL),
        out_shape=jax.ShapeDtypeStruct((B, T, D_MODEL), f32),
        scratch_shapes=[pltpu.VMEM((8, tm + 2 * H, D_B), f32), pltpu.VMEM((tm, D_B), bf16)],
        compiler_params=pltpu.CompilerParams(dimension_semantics=("arbitrary", "arbitrary"),
                                             vmem_limit_bytes=VMEM_LIMIT),
        name="mix_out",
    )(x, mod3, hn, so, glu, glu, glu, sga, sgb, *wts)


def _ffn_kernel(*refs, tm, nt, on_grid):
    if on_grid:
        (x_ref, xp_ref, xn_ref, mod_ref, g2n_ref, wu_ref, wd_ref, cw_ref, cb_ref, fg_ref,
         y_ref, h_ref, gbuf_ref, vbuf_ref, gl_ref, gr_ref, act_ref, acc_ref) = refs
    else:
        (x_ref, mod_ref, g2n_ref, wu_ref, wd_ref, cw_ref, cb_ref, fg_ref,
         y_ref, h_ref, gbuf_ref, vbuf_ref, gl_ref, gr_ref, act_ref, acc_ref) = refs
    t = pl.program_id(1)
    halo = GRID_W if on_grid else 0
    te = tm + 2 * halo
    mod = mod_ref[0]
    sh2 = mod[:, 3 * D_MODEL:4 * D_MODEL]
    sc2 = mod[:, 4 * D_MODEL:5 * D_MODEL]
    g2 = mod[:, 5 * D_MODEL:6 * D_MODEL]

    def norm_mod(x):
        return _rms(x, g2n_ref[...]) * (1.0 + sc2) + sh2

    h_ref[halo:halo + tm, :] = norm_mod(x_ref[0]).astype(bf16)
    if on_grid:
        h_ref[0:halo, :] = jnp.where(t > 0, norm_mod(xp_ref[0]), 0.0).astype(bf16)
        h_ref[halo + tm:te, :] = jnp.where(t < nt - 1, norm_mod(xn_ref[0]), 0.0).astype(bf16)

    SUB = 8
    LANE = 128
    seg = GRID_W if on_grid else te
    sub_row = lax.broadcasted_iota(jnp.int32, (SUB, LANE), 0)
    first_row = sub_row == 0
    last_row = sub_row == SUB - 1
    row_taps = (0, 1, 2) if on_grid else (1,)

    def ff_cols(fc, base):
        if isinstance(fc, int):
            return slice(base + fc * FF_CHUNK, base + (fc + 1) * FF_CHUNK)
        return pl.ds(pl.multiple_of(base + fc * FF_CHUNK, 128), FF_CHUNK)

    def up(fc, slot):
        gbuf_ref[slot] = _dot(h_ref[...], wu_ref[:, ff_cols(fc, 0)])
        vbuf_ref[slot] = _dot(h_ref[halo:halo + tm, :], wu_ref[:, ff_cols(fc, D_FF)])

    def gate_act(fc, slot):
        cw = cw_ref[:, ff_cols(fc, 0)]
        cb = cb_ref[:, ff_cols(fc, 0)]
        for c in range(FF_CHUNK // LANE):
            lanes = slice(c * LANE, (c + 1) * LANE)
            for s0 in range(0, te, seg):
                gate = gbuf_ref[slot, s0:s0 + seg, lanes]
                g_l = pltpu.roll(gate, 1, axis=0)
                g_r = pltpu.roll(gate, seg - 1, axis=0)
                gl_ref[s0:s0 + SUB, lanes] = jnp.where(first_row, 0.0, g_l[0:SUB])
                gl_ref[s0 + SUB:s0 + seg, lanes] = g_l[SUB:seg]
                gr_ref[s0:s0 + seg - SUB, lanes] = g_r[0:seg - SUB]
                gr_ref[s0 + seg - SUB:s0 + seg, lanes] = jnp.where(last_row, 0.0, g_r[seg - SUB:seg])
            for r0 in range(0, tm, GRID_W):
                conv = jnp.broadcast_to(cb[:, lanes], (GRID_W, LANE))
                for kh in row_taps:
                    lo = r0 + kh * GRID_W if on_grid else r0
                    conv = conv + gl_ref[lo:lo + GRID_W, lanes] * cw[3 * kh:3 * kh + 1, lanes]
                    conv = conv + gbuf_ref[slot, lo:lo + GRID_W, lanes] * cw[3 * kh + 1:3 * kh + 2, lanes]
                    conv = conv + gr_ref[lo:lo + GRID_W, lanes] * cw[3 * kh + 2:3 * kh + 3, lanes]
                gelu = conv * (0.5 + 0.5 * jnp.tanh(conv * (GELU_C0 + GELU_C1 * (conv * conv))))
                act_ref[fc, r0:r0 + GRID_W, lanes] = (gelu * vbuf_ref[slot, r0:r0 + GRID_W, lanes]).astype(bf16)

    def down(fc):
        return _dot(act_ref[fc], wd_ref[fc])

    def pair(p, with_down):
        fc = 2 * p
        if with_down:
            acc_ref[...] += down(fc - 2) + down(fc - 1)
        up(fc + 1, 1)
        gate_act(fc, 0)
        up(fc + 2, 0)
        gate_act(fc + 1, 1)

    n_pairs = (N_FF_CHUNKS - 1) // 2
    up(0, 0)
    acc_ref[...] = jnp.zeros_like(acc_ref)
    pair(0, False)

    def pair_body(p, carry):
        pair(p, True)
        return carry

    lax.fori_loop(1, n_pairs, pair_body, 0)
    last = N_FF_CHUNKS - 1
    gate_act(last, 0)
    ffn_out = acc_ref[...] + (down(last - 2) + down(last - 1) + down(last))
    y_ref[0] = _rms(x_ref[0] + g2 * ffn_out, fg_ref[...])


def _ffn(x1, mod3, per_batch_mod, wts, tm, on_grid):
    B, T, _ = x1.shape
    nt = T // tm
    halo = GRID_W if on_grid else 0
    tok = pl.BlockSpec((1, tm, D_MODEL), lambda b, t: (b, t, 0))
    full = lambda a: pl.BlockSpec(a.shape, lambda b, t: (0,) * a.ndim)
    mod_map = (lambda b, t: (b, 0, 0)) if per_batch_mod else (lambda b, t: (0, 0, 0))
    in_specs = [tok]
    args = [x1]
    if on_grid:
        r = tm // GRID_W
        in_specs += [pl.BlockSpec((1, GRID_W, D_MODEL), lambda b, t: (b, jnp.maximum(t * r - 1, 0), 0)),
                     pl.BlockSpec((1, GRID_W, D_MODEL), lambda b, t: (b, jnp.minimum((t + 1) * r, T // GRID_W - 1), 0))]
        args += [x1, x1]
    in_specs += [pl.BlockSpec((1, 1, N_MOD * D_MODEL), mod_map)] + [full(w) for w in wts]
    args += [mod3] + list(wts)
    return pl.pallas_call(
        functools.partial(_ffn_kernel, tm=tm, nt=nt, on_grid=on_grid),
        grid=(B, nt),
        in_specs=in_specs,
        out_specs=tok,
        out_shape=jax.ShapeDtypeStruct((B, T, D_MODEL), f32),
        scratch_shapes=[pltpu.VMEM((tm + 2 * halo, D_MODEL), bf16), pltpu.VMEM((2, tm + 2 * halo, FF_CHUNK), f32),
                        pltpu.VMEM((2, tm, FF_CHUNK), f32), pltpu.VMEM((tm + 2 * halo, FF_CHUNK), f32),
                        pltpu.VMEM((tm + 2 * halo, FF_CHUNK), f32), pltpu.VMEM((N_FF_CHUNKS, tm, FF_CHUNK), bf16),
                        pltpu.VMEM((tm, D_MODEL), f32)],
        compiler_params=pltpu.CompilerParams(dimension_semantics=("arbitrary", "arbitrary"),
                                             vmem_limit_bytes=VMEM_LIMIT),
        name="ffn_grid" if on_grid else "ffn_ctx",
    )(*args)


def _trunk(x, mod3, per_batch_mod, state, on_grid, w, tm_proj, tm_mix, tm_ffn, mlstm_nseq):
    q, k, v, so, glu, sga, sgb, gates = _in_proj(x, mod3, per_batch_mod, w["norm1_g"], w["in"], tm_proj)
    hn, *new_state = _mlstm(q, k, v, gates, state, mlstm_nseq, state is None)
    x1 = _mix_out(x, mod3, per_batch_mod, hn, so, glu, sga, sgb, w["mix"], tm_mix)
    y = _ffn(x1, mod3, per_batch_mod, w["ffn"], tm_ffn, on_grid)
    return y, new_state


def kernel(x_prompt, x_sample, c, state_C, state_n, state_m, c_ctx, w_ada, b_ada, norm1_g, w_in, b_in,
           mlstm_norm_g, w_proj_a, conv_dw_w, conv_dw_b, conv_ln_g, conv_ln_b, w_proj_b, w_out, norm2_g,
           w_up, ffn_dw_w, ffn_dw_b, w_down, final_norm_g):
    Bp = x_prompt.shape[0]
    Bl = x_sample.shape[0]
    l = 0
    row = lambda a: a.reshape(1, -1)

    ct = jnp.concatenate([c, c_ctx[None, :], jnp.zeros((8 - Bl - 1, D_MODEL), f32)], axis=0).T
    mod = _adaln_mod(ct, w_ada[l], row(b_ada[l]), Bl + 1)
    mod_lat = mod[0:Bl].reshape(Bl, 1, N_MOD * D_MODEL)
    mod_ctx = mod[Bl:Bl + 1].reshape(1, 1, N_MOD * D_MODEL)

    o_g = 4 * D_A
    o_u = o_g + 4 * NH_A
    nh = NH_A

    def repack_in(a):
        g = a[..., o_g:o_u]
        pad = jnp.zeros(a.shape[:-1] + (GATE_PAD - 4 * nh,), a.dtype)
        return jnp.concatenate([a[..., :o_g], a[..., o_u:], g[..., 0:nh], g[..., 2 * nh:3 * nh],
                                g[..., nh:2 * nh], g[..., 3 * nh:], pad], axis=-1)

    w_in_parts = (repack_in(w_in[l]).astype(bf16), repack_in(row(b_in[l])))
    w_mix = (row(mlstm_norm_g[l]), w_proj_a[l].astype(bf16),
             jnp.pad(conv_dw_w[l], ((0, 1), (0, 0))), row(conv_dw_b[l]), row(conv_ln_g[l]), row(conv_ln_b[l]),
             w_proj_b[l].astype(bf16), w_out[l].astype(bf16))
    w_ffn = (row(norm2_g[l]), w_up[l].astype(bf16),
             w_down[l].astype(bf16).reshape(N_FF_CHUNKS, FF_CHUNK, D_MODEL),
             ffn_dw_w[l].reshape(9, D_FF), row(ffn_dw_b[l]), row(final_norm_g))
    w = {"norm1_g": row(norm1_g[l]), "in": w_in_parts, "mix": w_mix, "ffn": w_ffn}

    y_prompt, (cn, nn, mn) = _trunk(x_prompt, mod_ctx, False, None, False, w, 512, 256, 256, 4)

    c0 = state_C[:, l]
    n0 = state_n[:, l].reshape(Bl, 2 * NH_A, DH_A)
    m0 = jnp.broadcast_to(state_m[:, l].reshape(Bl, 2 * NH_A, 1), (Bl, 2 * NH_A, DH_A))
    y_sample, _ = _trunk(x_sample, mod_lat, True, (c0, n0, m0), True, w, 512, 512, 512, 1)

    new_state_C = cn[:, None]
    new_state_n = nn.reshape(Bp, 1, 2, NH_A, DH_A)
    new_state_m = mn[:, :, 0].reshape(Bp, 1, 2, NH_A)
    return (y_prompt, y_sample, new_state_C, new_state_n, new_state_m)
```

```python
import functools

import jax
import jax.numpy as jnp
from jax import lax
from jax.experimental import pallas as pl
from jax.experimental.pallas import tpu as pltpu

D_MODEL = 1024
D_A = 512
NH_A = 4
DH_A = 128
CHUNK = 128
D_B = 512
CONV_W = 31
CONV_HALO = 16
CONV_ROWS = 64
D_FF = 2816
FF_CHUNK = 256
N_FF_CHUNKS = D_FF // FF_CHUNK
GRID_W = 64
N_MOD = 6
GATE_PAD = 128
EPS = 1e-6
Q_SCALE = DH_A ** -0.5
VMEM_LIMIT = 56 * 1024 * 1024

f32 = jnp.float32
bf16 = jnp.bfloat16


def _rms(x, g):
    return x * lax.rsqrt(jnp.mean(x * x, axis=-1, keepdims=True) + EPS) * g


def _sigmoid(x):
    return 1.0 / (1.0 + jnp.exp(-x))


def _log_sigmoid(x):
    return jnp.minimum(x, 0.0) - jnp.log(1.0 + jnp.exp(-jnp.abs(x)))


GELU_C0 = 0.7978845608028654
GELU_C1 = GELU_C0 * 0.044715


def _dot(a, b):
    return jnp.dot(a, b, preferred_element_type=f32)


def _mod_kernel(ct_ref, w_ref, b_ref, o_ref, *, n_rows):
    ct = ct_ref[...]
    st = ct * _sigmoid(ct)
    w = w_ref[...]
    rows = [jnp.sum(w * st[:, r:r + 1], axis=0, keepdims=True) for r in range(n_rows)]
    rows.append(jnp.zeros((8 - n_rows, w.shape[1]), f32))
    o_ref[...] = jnp.concatenate(rows, axis=0) + b_ref[...]


def _adaln_mod(ct, w_ada, b_ada, n_rows):
    n = w_ada.shape[1]
    tn = 512
    return pl.pallas_call(
        functools.partial(_mod_kernel, n_rows=n_rows),
        grid=(n // tn,),
        in_specs=[
            pl.BlockSpec((D_MODEL, 8), lambda j: (0, 0)),
            pl.BlockSpec((D_MODEL, tn), lambda j: (0, j)),
            pl.BlockSpec((1, tn), lambda j: (0, j)),
        ],
        out_specs=pl.BlockSpec((8, tn), lambda j: (0, j)),
        out_shape=jax.ShapeDtypeStruct((8, n), f32),
        compiler_params=pltpu.CompilerParams(dimension_semantics=("arbitrary",), vmem_limit_bytes=VMEM_LIMIT),
        name="adaln_mod",
    )(ct, w_ada, b_ada)


def _inproj_kernel(x_ref, mod_ref, g_ref, wa_ref, ba_ref, wb_ref, bb_ref, wg_ref, bg_ref,
                   q_ref, k_ref, v_ref, so_ref, glu_ref, sga_ref, sgb_ref, gates_ref):
    x = x_ref[0]
    mod = mod_ref[0]
    sh1 = mod[:, 0:D_MODEL]
    sc1 = mod[:, D_MODEL:2 * D_MODEL]
    h = (_rms(x, g_ref[...]) * (1.0 + sc1) + sh1).astype(bf16)

    def proj(w_ref, b_ref, lo, n):
        return _dot(h, w_ref[:, lo:lo + n]) + b_ref[:, lo:lo + n]

    q_ref[0] = (proj(wa_ref, ba_ref, 0, D_A) * Q_SCALE).astype(bf16)
    k_ref[0] = proj(wa_ref, ba_ref, D_A, D_A).astype(bf16)
    v_ref[0] = proj(wa_ref, ba_ref, 2 * D_A, D_A).astype(bf16)
    so_ref[0] = _sigmoid(proj(wa_ref, ba_ref, 3 * D_A, D_A))
    glu_ref[0] = proj(wb_ref, bb_ref, 0, D_B) * _sigmoid(proj(wb_ref, bb_ref, D_B, D_B))
    sga_ref[0] = _sigmoid(proj(wb_ref, bb_ref, 2 * D_B, D_MODEL))
    sgb_ref[0] = _sigmoid(proj(wb_ref, bb_ref, 2 * D_B + D_MODEL, D_MODEL))
    gates_ref[0] = proj(wg_ref, bg_ref, 0, GATE_PAD)


def _in_proj(x, mod3, per_batch_mod, norm_g, wts, tm):
    B0, T0, _ = x.shape
    if tm > T0:
        assert not per_batch_mod and tm % T0 == 0 and B0 % (tm // T0) == 0
        x = x.reshape(B0 * T0 // tm, tm, D_MODEL)
    B, T, _ = x.shape
    nt = T // tm
    tok = lambda n: pl.BlockSpec((1, tm, n), lambda b, t: (b, t, 0))
    full = lambda a: pl.BlockSpec(a.shape, lambda b, t: (0,) * a.ndim)
    mod_map = (lambda b, t: (b, 0, 0)) if per_batch_mod else (lambda b, t: (0, 0, 0))
    sds = lambda n, dt: jax.ShapeDtypeStruct((B, T, n), dt)
    outs = pl.pallas_call(
        _inproj_kernel,
        grid=(B, nt),
        in_specs=[tok(D_MODEL), pl.BlockSpec((1, 1, N_MOD * D_MODEL), mod_map), full(norm_g)]
        + [full(w) for w in wts],
        out_specs=[tok(D_A), tok(D_A), tok(D_A), tok(D_A), tok(D_B), tok(D_MODEL), tok(D_MODEL), tok(GATE_PAD)],
        out_shape=[sds(D_A, bf16), sds(D_A, bf16), sds(D_A, bf16), sds(D_A, f32), sds(D_B, f32),
                   sds(D_MODEL, f32), sds(D_MODEL, f32), sds(GATE_PAD, f32)],
        compiler_params=pltpu.CompilerParams(dimension_semantics=("arbitrary", "arbitrary"),
                                             vmem_limit_bytes=VMEM_LIMIT),
        name="in_proj",
    )(x, mod3, norm_g, *wts)
    return [o.reshape(B0, T0, o.shape[-1]) for o in outs]


NEG_BIG = -1e30
ST_U, ST_INTER, ST_EMJ, ST_WKN, ST_DECAY = 0, 8, 16, 24, 32
ST_ROWS = 40


def _chunk_scan(x, op, fill, prefix, lane, width):
    k = 1
    while k < CHUNK:
        if prefix:
            shifted = jnp.where(lane >= k, pltpu.roll(x, k, axis=1), fill)
        else:
            shifted = jnp.where(lane < CHUNK - k, pltpu.roll(x, width - k, axis=1), fill)
        x = op(x, shifted)
        k *= 2
    return x


def _mlstm_kernel(*refs, nseq, nc, state_in, state_out):
    refs = list(refs)
    q_ref, k_ref, v_ref, g_ref = refs[:4]
    del refs[:4]
    if state_in:
        c0_ref, n0_ref, m0_ref = refs[:3]
        del refs[:3]
    hn_ref = refs.pop(0)
    if state_out:
        cn_ref, nn_ref, mn_ref = refs[:3]
        del refs[:3]
    st_ref, wt_ref, vt_ref, qt_ref, u_ref, cp_ref, npf_ref, npb_ref, cst_ref, nsf_ref, nsb_ref = refs
    L = CHUNK
    nct = nseq * nc
    T = nct * L
    NR = 2 * NH_A

    lane = jnp.bitwise_and(lax.broadcasted_iota(jnp.int32, (NR, T), 1), L - 1)
    is_fwd = lax.broadcasted_iota(jnp.int32, (NR, T), 0) < NH_A
    is_fwd_c = lax.broadcasted_iota(jnp.int32, (NR, L), 0) < NH_A
    i_parts, f_parts = [], []
    for c in range(nct):
        gt = g_ref[0, c * L:(c + 1) * L, :].T
        i_parts.append(gt[0:NR])
        f_parts.append(gt[NR:2 * NR])
    ig = jnp.concatenate(i_parts, axis=1)
    lf = _log_sigmoid(jnp.concatenate(f_parts, axis=1))
    scan = functools.partial(_chunk_scan, lane=lane, width=T)
    ps = scan(lf, jnp.add, 0.0, True)
    ss = scan(lf, jnp.add, 0.0, False)
    b = jnp.where(is_fwd, ps, ss)
    btot = ps + ss - lf
    w = ig - b
    cmw = jnp.where(is_fwd, scan(w, jnp.maximum, -jnp.inf, True), scan(w, jnp.maximum, -jnp.inf, False))
    wk = btot - b + ig
    a = jnp.maximum(scan(wk, jnp.maximum, -jnp.inf, True), scan(wk, jnp.maximum, -jnp.inf, False))
    chunk_lanes = lambda cg: slice(cg * L, (cg + 1) * L)
    pad_rows = jnp.zeros((L - NR, L), f32)
    for s in range(nseq):
        ms = [m0_ref[s] if state_in else jnp.zeros((NR, L), f32)]
        for t in range(nc):
            cf, cb = chunk_lanes(s * nc + t), chunk_lanes(s * nc + nc - 1 - t)
            bt = jnp.where(is_fwd_c, btot[:, cf], btot[:, cb])
            at = jnp.where(is_fwd_c, a[:, cf], a[:, cb])
            ms.append(jnp.maximum(bt + ms[-1], at))
        if state_out:
            mn_ref[s] = ms[nc]
        for c in range(nc):
            cg = s * nc + c
            sl = chunk_lanes(cg)
            m_prev = jnp.where(is_fwd_c, ms[c], ms[nc - 1 - c])
            m_new = jnp.where(is_fwd_c, ms[c + 1], ms[nc - c])
            mx = jnp.maximum(m_prev, cmw[:, sl])
            st_ref[cg, ST_U:ST_U + NR] = -mx
            st_ref[cg, ST_INTER:ST_INTER + NR] = jnp.exp(m_prev - mx)
            st_ref[cg, ST_EMJ:ST_EMJ + NR] = jnp.exp(-mx - b[:, sl])
            st_ref[cg, ST_WKN:ST_WKN + NR] = jnp.exp(wk[:, sl] - m_new)
            st_ref[cg, ST_DECAY:ST_DECAY + NR] = jnp.exp(btot[:, sl] + m_prev - m_new)
            wt_ref[cg] = jnp.concatenate([w[:, sl], pad_rows], axis=0).T

    row = lax.broadcasted_iota(jnp.int32, (L, L), 0)
    col = lax.broadcasted_iota(jnp.int32, (L, L), 1)
    masks = (row <= col, row >= col)
    first_of8 = lax.broadcasted_iota(jnp.int32, (8, L), 0) == 0

    def row_tile(x):
        return jnp.where(first_of8, x, 0.0)

    head_lanes = [slice(h * DH_A, (h + 1) * DH_A) for h in range(NH_A)]

    def increments(j, carry):
        rows = pl.ds(pl.multiple_of(j * L, L), L)
        for h in range(NH_A):
            kh = k_ref[0, rows, head_lanes[h]]
            vt = v_ref[0, rows, head_lanes[h]].astype(f32).T
            vt_ref[j, h] = vt.astype(bf16)
            qt_ref[j, h] = q_ref[0, rows, head_lanes[h]].astype(f32).T.astype(bf16)
            wf = st_ref[j, ST_WKN + h:ST_WKN + h + 1, :]
            wb = st_ref[j, ST_WKN + NH_A + h:ST_WKN + NH_A + h + 1, :]
            lhs = jnp.concatenate([vt * wf, vt * wb, row_tile(wf), row_tile(wb)], axis=0).astype(bf16)
            u_ref[j, h] = _dot(lhs, kh)
        return carry

    lax.fori_loop(0, nct, increments, 0, unroll=4)

    seq_heads = [(s, h) for s in range(nseq) for h in range(NH_A)]
    for s, h in seq_heads:
        i = s * NH_A + h
        if state_in:
            cst_ref[i, 0:DH_A] = c0_ref[s, 0, h].T
            cst_ref[i, DH_A:2 * DH_A] = c0_ref[s, 1, h].T
            nsf_ref[i] = row_tile(n0_ref[s, h:h + 1, :])
            nsb_ref[i] = row_tile(n0_ref[s, NH_A + h:NH_A + h + 1, :])
        else:
            cst_ref[i] = jnp.zeros((2 * DH_A, DH_A), f32)
            nsf_ref[i] = jnp.zeros((8, DH_A), f32)
            nsb_ref[i] = jnp.zeros((8, DH_A), f32)

    def recur(t, carry):
        for s, h in seq_heads:
            i = s * NH_A + h
            jf = s * nc + t
            jb = s * nc + nc - 1 - t
            dec_f = st_ref[jf, ST_DECAY + h:ST_DECAY + h + 1, :]
            dec_b = st_ref[jb, ST_DECAY + NH_A + h:ST_DECAY + NH_A + h + 1, :]
            c_f = cst_ref[i, 0:DH_A]
            c_b = cst_ref[i, DH_A:2 * DH_A]
            cp_ref[jf, h, 0:DH_A] = c_f.astype(bf16)
            cp_ref[jb, h, DH_A:2 * DH_A] = c_b.astype(bf16)
            cst_ref[i, 0:DH_A] = dec_f * c_f + u_ref[jf, h, 0:DH_A]
            cst_ref[i, DH_A:2 * DH_A] = dec_b * c_b + u_ref[jb, h, DH_A:2 * DH_A]
            n_f = nsf_ref[i]
            n_b = nsb_ref[i]
            npf_ref[jf, h] = n_f
            npb_ref[jb, h] = n_b
            nsf_ref[i] = dec_f * n_f + u_ref[jf, h, 2 * DH_A:2 * DH_A + 8]
            nsb_ref[i] = dec_b * n_b + u_ref[jb, h, 2 * DH_A + 8:2 * DH_A + 16]
        return carry

    lax.fori_loop(0, nc, recur, 0)
    for s, h in seq_heads if state_out else ():
        i = s * NH_A + h
        cn_ref[s, 0, h] = cst_ref[i, 0:DH_A].T
        cn_ref[s, 1, h] = cst_ref[i, DH_A:2 * DH_A].T
        nn_ref[s, h:h + 1, :] = nsf_ref[i, 0:1]
        nn_ref[s, NH_A + h:NH_A + h + 1, :] = nsb_ref[i, 0:1]

    def outputs(j, carry):
        rows = pl.ds(pl.multiple_of(j * L, L), L)
        wt = wt_ref[j]
        heads = range(NH_A)
        kqs = []
        for h in heads:
            kh = k_ref[0, rows, head_lanes[h]]
            n_rows = jnp.concatenate([npf_ref[j, h], npb_ref[j, h]], axis=0).astype(bf16)
            kqs.append(_dot(jnp.concatenate([kh, n_rows, cp_ref[j, h]], axis=0), qt_ref[j, h]))
        decays = [[jnp.exp(jnp.where(masks[d], wt[:, NH_A * d + h:NH_A * d + h + 1]
                                     + st_ref[j, ST_U + NH_A * d + h:ST_U + NH_A * d + h + 1, :], NEG_BIG))
                   for d in range(2)] for h in heads]
        s_sums, h_ts = [], []
        for h in heads:
            kq = kqs[h]
            s_sum = None
            h_t = None
            for d in range(2):
                r = NH_A * d + h
                inter = st_ref[j, ST_INTER + r:ST_INTER + r + 1, :]
                emj = st_ref[j, ST_EMJ + r:ST_EMJ + r + 1, :]
                s_t = kq[0:L] * decays[h][d]
                qn = kq[L + 8 * d:L + 8 * d + 1]
                den = inter * qn + jnp.sum(s_t, axis=0, keepdims=True)
                rr = 1.0 / jnp.maximum(jnp.abs(den), emj)
                s_sum = s_t * rr if d == 0 else s_sum + s_t * rr
                part = kq[L + 16 + d * DH_A:L + 16 + (d + 1) * DH_A] * (inter * rr)
                h_t = part if d == 0 else h_t + part
            s_sums.append(s_sum.astype(bf16))
            h_ts.append(h_t)
        h_ts = [h_ts[h] + _dot(vt_ref[j, h], s_sums[h]) for h in heads]
        h_ts = [x * lax.rsqrt(jnp.mean(x * x, axis=0, keepdims=True) + EPS) for x in h_ts]
        for h in heads:
            hn_ref[0, rows, head_lanes[h]] = h_ts[h].T
        return carry

    lax.fori_loop(0, nct, outputs, 0, unroll=4)


def _mlstm(q, k, v, gates, state, nseq, state_out):
    B, T, _ = q.shape
    nc = T // CHUNK
    nct = nseq * nc
    G = B // nseq
    fold = lambda a: a.reshape(G, nseq * T, a.shape[-1])
    seq = lambda n: pl.BlockSpec((1, nseq * T, n), lambda b: (b, 0, 0))
    st_c = pl.BlockSpec((nseq, 2, NH_A, DH_A, DH_A), lambda b: (b, 0, 0, 0, 0))
    st_v = pl.BlockSpec((nseq, 2 * NH_A, DH_A), lambda b: (b, 0, 0))
    state_specs = [st_c, st_v, st_v]
    state_shapes = [jax.ShapeDtypeStruct((B, 2, NH_A, DH_A, DH_A), f32),
                    jax.ShapeDtypeStruct((B, 2 * NH_A, DH_A), f32),
                    jax.ShapeDtypeStruct((B, 2 * NH_A, DH_A), f32)]
    outs = pl.pallas_call(
        functools.partial(_mlstm_kernel, nseq=nseq, nc=nc, state_in=state is not None, state_out=state_out),
        grid=(G,),
        in_specs=[seq(D_A), seq(D_A), seq(D_A), seq(GATE_PAD)] + (state_specs if state is not None else []),
        out_specs=[seq(D_A)] + (state_specs if state_out else []),
        scratch_shapes=[pltpu.VMEM((nct, ST_ROWS, CHUNK), f32), pltpu.VMEM((nct, CHUNK, GATE_PAD), f32),
                        pltpu.VMEM((nct, NH_A, DH_A, CHUNK), bf16), pltpu.VMEM((nct, NH_A, DH_A, CHUNK), bf16),
                        pltpu.VMEM((nct, NH_A, 2 * DH_A + 16, DH_A), f32),
                        pltpu.VMEM((nct, NH_A, 2 * DH_A, DH_A), bf16),
                        pltpu.VMEM((nct, NH_A, 8, DH_A), f32), pltpu.VMEM((nct, NH_A, 8, DH_A), f32),
                        pltpu.VMEM((nseq * NH_A, 2 * DH_A, DH_A), f32),
                        pltpu.VMEM((nseq * NH_A, 8, DH_A), f32), pltpu.VMEM((nseq * NH_A, 8, DH_A), f32)],
        out_shape=[jax.ShapeDtypeStruct((G, nseq * T, D_A), f32)] + (state_shapes if state_out else []),
        compiler_params=pltpu.CompilerParams(dimension_semantics=("arbitrary",), vmem_limit_bytes=VMEM_LIMIT),
        name="mlstm",
    )(fold(q), fold(k), fold(v), fold(gates), *(state if state is not None else ()))
    return (outs[0].reshape(B, T, D_A),) + tuple(outs[1:])


def _mix_kernel(x_ref, mod_ref, hn_ref, so_ref, glu_ref, glu_prev_ref, glu_next_ref, sga_ref, sgb_ref,
                ng_ref, wpa_ref, cw_ref, cb_ref, lng_ref, lnb_ref, wpb_ref, wout_ref,
                x1_ref, ext_ref, zs_ref, *, tm, nt):
    t = pl.program_id(1)
    mod = mod_ref[0]
    g1 = mod[:, 2 * D_MODEL:3 * D_MODEL]

    a_in = (so_ref[0] * (hn_ref[0] * ng_ref[...])).astype(bf16)
    branch_a = _dot(a_in, wpa_ref[...])

    H = CONV_HALO
    SUB = 8
    LANE = 128
    te = tm + 2 * H
    ext_ref[0, 0:H, :] = jnp.where(t > 0, glu_prev_ref[0], 0.0)
    ext_ref[0, H:H + tm, :] = glu_ref[0]
    ext_ref[0, H + tm:te, :] = jnp.where(t < nt - 1, glu_next_ref[0], 0.0)
    for c in range(D_B // LANE):
        lanes = slice(c * LANE, (c + 1) * LANE)
        base = ext_ref[0, :, lanes]
        for k in range(1, SUB):
            ext_ref[k, :, lanes] = pltpu.roll(base, te - k, axis=0)
    off = H - CONV_W // 2
    for r0 in range(0, tm, CONV_ROWS):
        parts = []
        for c in range(D_B // LANE):
            lanes = slice(c * LANE, (c + 1) * LANE)
            acc = jnp.broadcast_to(cb_ref[:, lanes], (CONV_ROWS, LANE))
            for w in range(CONV_W):
                k, a = (off + w) % SUB, (off + w) // SUB * SUB
                acc = acc + ext_ref[k, r0 + a:r0 + a + CONV_ROWS, lanes] * cw_ref[w:w + 1, lanes]
            parts.append(acc)
        z = jnp.concatenate(parts, axis=1)
        mu = jnp.mean(z, axis=-1, keepdims=True)
        zc = z - mu
        var = jnp.mean(zc * zc, axis=-1, keepdims=True)
        zn = zc * lax.rsqrt(var + EPS) * lng_ref[...] + lnb_ref[...]
        zs_ref[r0:r0 + CONV_ROWS, :] = (zn * _sigmoid(zn)).astype(bf16)
    branch_b = _dot(zs_ref[...], wpb_ref[...])

    merged = sga_ref[0] * branch_a + sgb_ref[0] * branch_b
    x1_ref[0] = x_ref[0] + g1 * _dot(merged.astype(bf16), wout_ref[...])


def _mix_out(x, mod3, per_batch_mod, hn, so, glu, sga, sgb, wts, tm):
    B, T, _ = x.shape
    nt = T // tm
    H = CONV_HALO
    r = tm // H
    tok = lambda n: pl.BlockSpec((1, tm, n), lambda b, t: (b, t, 0))
    full = lambda a: pl.BlockSpec(a.shape, lambda b, t: (0,) * a.ndim)
    mod_map = (lambda b, t: (b, 0, 0)) if per_batch_mod else (lambda b, t: (0, 0, 0))
    prev = pl.BlockSpec((1, H, D_B), lambda b, t: (b, jnp.maximum(t * r - 1, 0), 0))
    nxt = pl.BlockSpec((1, H, D_B), lambda b, t: (b, jnp.minimum((t + 1) * r, T // H - 1), 0))
    return pl.pallas_call(
        functools.partial(_mix_kernel, tm=tm, nt=nt),
        grid=(B, nt),
        in_specs=[tok(D_MODEL), pl.BlockSpec((1, 1, N_MOD * D_MODEL), mod_map), tok(D_A), tok(D_A),
                  tok(D_B), prev, nxt, tok(D_MODEL), tok(D_MODEL)] + [full(w) for w in wts],
        out_specs=tok(D_MODEL),
        out_shape=jax.ShapeDtypeStruct((B, T, D_MODEL), f32),
        scratch_shapes=[pltpu.VMEM((8, tm + 2 * H, D_B), f32), pltpu.VMEM((tm, D_B), bf16)],
        compiler_params=pltpu.CompilerParams(dimension_semantics=("arbitrary", "arbitrary"),
                                             vmem_limit_bytes=VMEM_LIMIT),
        name="mix_out",
    )(x, mod3, hn, so, glu, glu, glu, sga, sgb, *wts)


def _ffn_kernel(*refs, tm, nt, on_grid):
    if on_grid:
        (x_ref, xp_ref, xn_ref, mod_ref, g2n_ref, wu_ref, wd_ref, cw_ref, cb_ref, fg_ref,
         y_ref, h_ref, gbuf_ref, vbuf_ref, gl_ref, gr_ref, act_ref, acc_ref) = refs
    else:
        (x_ref, mod_ref, g2n_ref, wu_ref, wd_ref, cw_ref, cb_ref, fg_ref,
         y_ref, h_ref, gbuf_ref, vbuf_ref, gl_ref, gr_ref, act_ref, acc_ref) = refs
    t = pl.program_id(1)
    halo = GRID_W if on_grid else 0
    te = tm + 2 * halo
    mod = mod_ref[0]
    sh2 = mod[:, 3 * D_MODEL:4 * D_MODEL]
    sc2 = mod[:, 4 * D_MODEL:5 * D_MODEL]
    g2 = mod[:, 5 * D_MODEL:6 * D_MODEL]

    def norm_mod(x):
        return _rms(x, g2n_ref[...]) * (1.0 + sc2) + sh2

    h_ref[halo:halo + tm, :] = norm_mod(x_ref[0]).astype(bf16)
    if on_grid:
        h_ref[0:halo, :] = jnp.where(t > 0, norm_mod(xp_ref[0]), 0.0).astype(bf16)
        h_ref[halo + tm:te, :] = jnp.where(t < nt - 1, norm_mod(xn_ref[0]), 0.0).astype(bf16)

    SUB = 8
    LANE = 128
    seg = GRID_W if on_grid else te
    sub_row = lax.broadcasted_iota(jnp.int32, (SUB, LANE), 0)
    first_row = sub_row == 0
    last_row = sub_row == SUB - 1
    row_taps = (0, 1, 2) if on_grid else (1,)

    def ff_cols(fc, base):
        if isinstance(fc, int):
            return slice(base + fc * FF_CHUNK, base + (fc + 1) * FF_CHUNK)
        return pl.ds(pl.multiple_of(base + fc * FF_CHUNK, 128), FF_CHUNK)

    def up(fc, slot):
        gbuf_ref[slot] = _dot(h_ref[...], wu_ref[:, ff_cols(fc, 0)])
        vbuf_ref[slot] = _dot(h_ref[halo:halo + tm, :], wu_ref[:, ff_cols(fc, D_FF)])

    def gate_act(fc, slot):
        cw = cw_ref[:, ff_cols(fc, 0)]
        cb = cb_ref[:, ff_cols(fc, 0)]
        for c in range(FF_CHUNK // LANE):
            lanes = slice(c * LANE, (c + 1) * LANE)
            for s0 in range(0, te, seg):
                gate = gbuf_ref[slot, s0:s0 + seg, lanes]
                g_l = pltpu.roll(gate, 1, axis=0)
                g_r = pltpu.roll(gate, seg - 1, axis=0)
                gl_ref[s0:s0 + SUB, lanes] = jnp.where(first_row, 0.0, g_l[0:SUB])
                gl_ref[s0 + SUB:s0 + seg, lanes] = g_l[SUB:seg]
                gr_ref[s0:s0 + seg - SUB, lanes] = g_r[0:seg - SUB]
                gr_ref[s0 + seg - SUB:s0 + seg, lanes] = jnp.where(last_row, 0.0, g_r[seg - SUB:seg])
            for r0 in range(0, tm, GRID_W):
                conv = jnp.broadcast_to(cb[:, lanes], (GRID_W, LANE))
                for kh in row_taps:
                    lo = r0 + kh * GRID_W if on_grid else r0
                    conv = conv + gl_ref[lo:lo + GRID_W, lanes] * cw[3 * kh:3 * kh + 1, lanes]
                    conv = conv + gbuf_ref[slot, lo:lo + GRID_W, lanes] * cw[3 * kh + 1:3 * kh + 2, lanes]
                    conv = conv + gr_ref[lo:lo + GRID_W, lanes] * cw[3 * kh + 2:3 * kh + 3, lanes]
                gelu = conv * (0.5 + 0.5 * jnp.tanh(conv * (GELU_C0 + GELU_C1 * (conv * conv))))
                act_ref[fc, r0:r0 + GRID_W, lanes] = (gelu * vbuf_ref[slot, r0:r0 + GRID_W, lanes]).astype(bf16)

    def down(fc):
        return _dot(act_ref[fc], wd_ref[fc])

    def pair(p, with_down):
        fc = 2 * p
        if with_down:
            acc_ref[...] += down(fc - 2) + down(fc - 1)
        up(fc + 1, 1)
        gate_act(fc, 0)
        up(fc + 2, 0)
        gate_act(fc + 1, 1)

    n_pairs = (N_FF_CHUNKS - 1) // 2
    up(0, 0)
    acc_ref[...] = jnp.zeros_like(acc_ref)
    pair(0, False)

    def pair_body(p, carry):
        pair(p, True)
        return carry

    lax.fori_loop(1, n_pairs, pair_body, 0)
    last = N_FF_CHUNKS - 1
    gate_act(last, 0)
    ffn_out = acc_ref[...] + (down(last - 2) + down(last - 1) + down(last))
    y_ref[0] = _rms(x_ref[0] + g2 * ffn_out, fg_ref[...])


def _ffn(x1, mod3, per_batch_mod, wts, tm, on_grid):
    B, T, _ = x1.shape
    nt = T // tm
    halo = GRID_W if on_grid else 0
    tok = pl.BlockSpec((1, tm, D_MODEL), lambda b, t: (b, t, 0))
    full = lambda a: pl.BlockSpec(a.shape, lambda b, t: (0,) * a.ndim)
    mod_map = (lambda b, t: (b, 0, 0)) if per_batch_mod else (lambda b, t: (0, 0, 0))
    in_specs = [tok]
    args = [x1]
    if on_grid:
        r = tm // GRID_W
        in_specs += [pl.BlockSpec((1, GRID_W, D_MODEL), lambda b, t: (b, jnp.maximum(t * r - 1, 0), 0)),
                     pl.BlockSpec((1, GRID_W, D_MODEL), lambda b, t: (b, jnp.minimum((t + 1) * r, T // GRID_W - 1), 0))]
        args += [x1, x1]
    in_specs += [pl.BlockSpec((1, 1, N_MOD * D_MODEL), mod_map)] + [full(w) for w in wts]
    args += [mod3] + list(wts)
    return pl.pallas_call(
        functools.partial(_ffn_kernel, tm=tm, nt=nt, on_grid=on_grid),
        grid=(B, nt),
        in_specs=in_specs,
        out_specs=tok,
        out_shape=jax.ShapeDtypeStruct((B, T, D_MODEL), f32),
        scratch_shapes=[pltpu.VMEM((tm + 2 * halo, D_MODEL), bf16), pltpu.VMEM((2, tm + 2 * halo, FF_CHUNK), f32),
                        pltpu.VMEM((2, tm, FF_CHUNK), f32), pltpu.VMEM((tm + 2 * halo, FF_CHUNK), f32),
                        pltpu.VMEM((tm + 2 * halo, FF_CHUNK), f32), pltpu.VMEM((N_FF_CHUNKS, tm, FF_CHUNK), bf16),
                        pltpu.VMEM((tm, D_MODEL), f32)],
        compiler_params=pltpu.CompilerParams(dimension_semantics=("arbitrary", "arbitrary"),
                                             vmem_limit_bytes=VMEM_LIMIT),
        name="ffn_grid" if on_grid else "ffn_ctx",
    )(*args)


def _trunk(x, mod3, per_batch_mod, state, on_grid, w, tm_proj, tm_mix, tm_ffn, mlstm_nseq):
    q, k, v, so, glu, sga, sgb, gates = _in_proj(x, mod3, per_batch_mod, w["norm1_g"], w["in"], tm_proj)
    hn, *new_state = _mlstm(q, k, v, gates, state, mlstm_nseq, state is None)
    x1 = _mix_out(x, mod3, per_batch_mod, hn, so, glu, sga, sgb, w["mix"], tm_mix)
    y = _ffn(x1, mod3, per_batch_mod, w["ffn"], tm_ffn, on_grid)
    return y, new_state


def kernel(x_prompt, x_sample, c, state_C, state_n, state_m, c_ctx, w_ada, b_ada, norm1_g, w_in, b_in,
           mlstm_norm_g, w_proj_a, conv_dw_w, conv_dw_b, conv_ln_g, conv_ln_b, w_proj_b, w_out, norm2_g,
           w_up, ffn_dw_w, ffn_dw_b, w_down, final_norm_g):
    Bp = x_prompt.shape[0]
    Bl = x_sample.shape[0]
    l = 0
    row = lambda a: a.reshape(1, -1)

    ct = jnp.concatenate([c, c_ctx[None, :], jnp.zeros((8 - Bl - 1, D_MODEL), f32)], axis=0).T
    mod = _adaln_mod(ct, w_ada[l], row(b_ada[l]), Bl + 1)
    mod_lat = mod[0:Bl].reshape(Bl, 1, N_MOD * D_MODEL)
    mod_ctx = mod[Bl:Bl + 1].reshape(1, 1, N_MOD * D_MODEL)

    o_g = 4 * D_A
    o_u = o_g + 4 * NH_A
    nh = NH_A

    def split_in(a, dt):
        g = a[..., o_g:o_u]
        pad = jnp.zeros(a.shape[:-1] + (GATE_PAD - 4 * nh,), a.dtype)
        gates = jnp.concatenate([g[..., 0:nh], g[..., 2 * nh:3 * nh], g[..., nh:2 * nh], g[..., 3 * nh:], pad], axis=-1)
        return a[..., :o_g].astype(dt), a[..., o_u:].astype(dt), gates.astype(dt)

    (w_a, w_b, w_g), (b_a, b_b, b_g) = split_in(w_in[l], bf16), split_in(row(b_in[l]), f32)
    w_in_parts = (w_a, b_a, w_b, b_b, w_g, b_g)
    w_mix = (row(mlstm_norm_g[l]), w_proj_a[l].astype(bf16),
             jnp.pad(conv_dw_w[l], ((0, 1), (0, 0))), row(conv_dw_b[l]), row(conv_ln_g[l]), row(conv_ln_b[l]),
             w_proj_b[l].astype(bf16), w_out[l].astype(bf16))
    w_ffn = (row(norm2_g[l]), w_up[l].astype(bf16),
             w_down[l].astype(bf16).reshape(N_FF_CHUNKS, FF_CHUNK, D_MODEL),
             ffn_dw_w[l].reshape(9, D_FF), row(ffn_dw_b[l]), row(final_norm_g))
    w = {"norm1_g": row(norm1_g[l]), "in": w_in_parts, "mix": w_mix, "ffn": w_ffn}

    y_prompt, (cn, nn, mn) = _trunk(x_prompt, mod_ctx, False, None, False, w, 512, 256, 256, 4)

    c0 = state_C[:, l]
    n0 = state_n[:, l].reshape(Bl, 2 * NH_A, DH_A)
    m0 = jnp.broadcast_to(state_m[:, l].reshape(Bl, 2 * NH_A, 1), (Bl, 2 * NH_A, DH_A))
    y_sample, _ = _trunk(x_sample, mod_lat, True, (c0, n0, m0), True, w, 512, 512, 512, 1)

    new_state_C = cn[:, None]
    new_state_n = nn.reshape(Bp, 1, 2, NH_A, DH_A)
    new_state_m = mn[:, :, 0].reshape(Bp, 1, 2, NH_A)
    return (y_prompt, y_sample, new_state_C, new_state_n, new_state_m)
```

```python
import functools

import jax
import jax.numpy as jnp
from jax import lax
from jax.experimental import pallas as pl
from jax.experimental.pallas import tpu as pltpu

D_MODEL = 1024
D_A = 512
NH_A = 4
DH_A = 128
CHUNK = 128
D_B = 512
CONV_W = 31
CONV_HALO = 16
CONV_ROWS = 64
D_FF = 2816
FF_CHUNK = 256
N_FF_CHUNKS = D_FF // FF_CHUNK
GRID_W = 64
N_MOD = 6
GATE_PAD = 128
EPS = 1e-6
Q_SCALE = DH_A ** -0.5
VMEM_LIMIT = 56 * 1024 * 1024

f32 = jnp.float32
bf16 = jnp.bfloat16


def _rms(x, g):
    return x * lax.rsqrt(jnp.mean(x * x, axis=-1, keepdims=True) + EPS) * g


def _sigmoid(x):
    return 1.0 / (1.0 + jnp.exp(-x))


def _log_sigmoid(x):
    return jnp.minimum(x, 0.0) - jnp.log(1.0 + jnp.exp(-jnp.abs(x)))


GELU_C0 = 0.7978845608028654
GELU_C1 = GELU_C0 * 0.044715


def _dot(a, b):
    return jnp.dot(a, b, preferred_element_type=f32)


def _mod_kernel(ct_ref, w_ref, b_ref, o_ref, *, n_rows):
    ct = ct_ref[...]
    st = ct * _sigmoid(ct)
    w = w_ref[...]
    rows = [jnp.sum(w * st[:, r:r + 1], axis=0, keepdims=True) for r in range(n_rows)]
    rows.append(jnp.zeros((8 - n_rows, w.shape[1]), f32))
    o_ref[...] = jnp.concatenate(rows, axis=0) + b_ref[...]


def _adaln_mod(ct, w_ada, b_ada, n_rows):
    n = w_ada.shape[1]
    tn = 512
    return pl.pallas_call(
        functools.partial(_mod_kernel, n_rows=n_rows),
        grid=(n // tn,),
        in_specs=[
            pl.BlockSpec((D_MODEL, 8), lambda j: (0, 0)),
            pl.BlockSpec((D_MODEL, tn), lambda j: (0, j)),
            pl.BlockSpec((1, tn), lambda j: (0, j)),
        ],
        out_specs=pl.BlockSpec((8, tn), lambda j: (0, j)),
        out_shape=jax.ShapeDtypeStruct((8, n), f32),
        compiler_params=pltpu.CompilerParams(dimension_semantics=("arbitrary",), vmem_limit_bytes=VMEM_LIMIT),
        name="adaln_mod",
    )(ct, w_ada, b_ada)


def _inproj_kernel(x_ref, mod_ref, g_ref, wa_ref, ba_ref, wb_ref, bb_ref, wg_ref, bg_ref,
                   q_ref, k_ref, v_ref, so_ref, glu_ref, sga_ref, sgb_ref, gates_ref):
    x = x_ref[0]
    mod = mod_ref[0]
    sh1 = mod[:, 0:D_MODEL]
    sc1 = mod[:, D_MODEL:2 * D_MODEL]
    h = (_rms(x, g_ref[...]) * (1.0 + sc1) + sh1).astype(bf16)

    def proj(w_ref, b_ref, lo, n):
        return _dot(h, w_ref[:, lo:lo + n]) + b_ref[:, lo:lo + n]

    q_ref[0] = (proj(wa_ref, ba_ref, 0, D_A) * Q_SCALE).astype(bf16)
    k_ref[0] = proj(wa_ref, ba_ref, D_A, D_A).astype(bf16)
    v_ref[0] = proj(wa_ref, ba_ref, 2 * D_A, D_A).astype(bf16)
    so_ref[0] = _sigmoid(proj(wa_ref, ba_ref, 3 * D_A, D_A))
    glu_ref[0] = proj(wb_ref, bb_ref, 0, D_B) * _sigmoid(proj(wb_ref, bb_ref, D_B, D_B))
    sga_ref[0] = _sigmoid(proj(wb_ref, bb_ref, 2 * D_B, D_MODEL))
    sgb_ref[0] = _sigmoid(proj(wb_ref, bb_ref, 2 * D_B + D_MODEL, D_MODEL))
    gates_ref[0] = proj(wg_ref, bg_ref, 0, GATE_PAD)


def _in_proj(x, mod3, per_batch_mod, norm_g, wts, tm):
    B0, T0, _ = x.shape
    if tm > T0:
        assert not per_batch_mod and tm % T0 == 0 and B0 % (tm // T0) == 0
        x = x.reshape(B0 * T0 // tm, tm, D_MODEL)
    B, T, _ = x.shape
    nt = T // tm
    tok = lambda n: pl.BlockSpec((1, tm, n), lambda b, t: (b, t, 0))
    full = lambda a: pl.BlockSpec(a.shape, lambda b, t: (0,) * a.ndim)
    mod_map = (lambda b, t: (b, 0, 0)) if per_batch_mod else (lambda b, t: (0, 0, 0))
    sds = lambda n, dt: jax.ShapeDtypeStruct((B, T, n), dt)
    outs = pl.pallas_call(
        _inproj_kernel,
        grid=(B, nt),
        in_specs=[tok(D_MODEL), pl.BlockSpec((1, 1, N_MOD * D_MODEL), mod_map), full(norm_g)]
        + [full(w) for w in wts],
        out_specs=[tok(D_A), tok(D_A), tok(D_A), tok(D_A), tok(D_B), tok(D_MODEL), tok(D_MODEL), tok(GATE_PAD)],
        out_shape=[sds(D_A, bf16), sds(D_A, bf16), sds(D_A, bf16), sds(D_A, f32), sds(D_B, f32),
                   sds(D_MODEL, f32), sds(D_MODEL, f32), sds(GATE_PAD, f32)],
        compiler_params=pltpu.CompilerParams(dimension_semantics=("arbitrary", "arbitrary"),
                                             vmem_limit_bytes=VMEM_LIMIT),
        name="in_proj",
    )(x, mod3, norm_g, *wts)
    return [o.reshape(B0, T0, o.shape[-1]) for o in outs]


NEG_BIG = -1e30
ST_U, ST_INTER, ST_EMJ, ST_WKN, ST_DECAY = 0, 8, 16, 24, 32
ST_ROWS = 40


def _chunk_scan(x, op, fill, prefix, lane, width):
    k = 1
    while k < CHUNK:
        if prefix:
            shifted = jnp.where(lane >= k, pltpu.roll(x, k, axis=1), fill)
        else:
            shifted = jnp.where(lane < CHUNK - k, pltpu.roll(x, width - k, axis=1), fill)
        x = op(x, shifted)
        k *= 2
    return x


def _mlstm_kernel(*refs, nseq, nc, state_in, state_out):
    refs = list(refs)
    q_ref, k_ref, v_ref, g_ref = refs[:4]
    del refs[:4]
    if state_in:
        c0_ref, n0_ref, m0_ref = refs[:3]
        del refs[:3]
    hn_ref = refs.pop(0)
    if state_out:
        cn_ref, nn_ref, mn_ref = refs[:3]
        del refs[:3]
    st_ref, wt_ref, vt_ref, qt_ref, u_ref, cp_ref, npf_ref, npb_ref, cst_ref, nsf_ref, nsb_ref = refs
    L = CHUNK
    nct = nseq * nc
    T = nct * L
    NR = 2 * NH_A

    lane = jnp.bitwise_and(lax.broadcasted_iota(jnp.int32, (NR, T), 1), L - 1)
    is_fwd = lax.broadcasted_iota(jnp.int32, (NR, T), 0) < NH_A
    is_fwd_c = lax.broadcasted_iota(jnp.int32, (NR, L), 0) < NH_A
    i_parts, f_parts = [], []
    for c in range(nct):
        gt = g_ref[0, c * L:(c + 1) * L, :].T
        i_parts.append(gt[0:NR])
        f_parts.append(gt[NR:2 * NR])
    ig = jnp.concatenate(i_parts, axis=1)
    lf = _log_sigmoid(jnp.concatenate(f_parts, axis=1))
    scan = functools.partial(_chunk_scan, lane=lane, width=T)
    ps = scan(lf, jnp.add, 0.0, True)
    ss = scan(lf, jnp.add, 0.0, False)
    b = jnp.where(is_fwd, ps, ss)
    btot = ps + ss - lf
    w = ig - b
    cmw = jnp.where(is_fwd, scan(w, jnp.maximum, -jnp.inf, True), scan(w, jnp.maximum, -jnp.inf, False))
    wk = btot - b + ig
    a = jnp.maximum(scan(wk, jnp.maximum, -jnp.inf, True), scan(wk, jnp.maximum, -jnp.inf, False))
    chunk_lanes = lambda cg: slice(cg * L, (cg + 1) * L)
    pad_rows = jnp.zeros((L - NR, L), f32)
    for s in range(nseq):
        ms = [m0_ref[s] if state_in else jnp.zeros((NR, L), f32)]
        for t in range(nc):
            cf, cb = chunk_lanes(s * nc + t), chunk_lanes(s * nc + nc - 1 - t)
            bt = jnp.where(is_fwd_c, btot[:, cf], btot[:, cb])
            at = jnp.where(is_fwd_c, a[:, cf], a[:, cb])
            ms.append(jnp.maximum(bt + ms[-1], at))
        if state_out:
            mn_ref[s] = ms[nc]
        for c in range(nc):
            cg = s * nc + c
            sl = chunk_lanes(cg)
            m_prev = jnp.where(is_fwd_c, ms[c], ms[nc - 1 - c])
            m_new = jnp.where(is_fwd_c, ms[c + 1], ms[nc - c])
            mx = jnp.maximum(m_prev, cmw[:, sl])
            st_ref[cg, ST_U:ST_U + NR] = -mx
            st_ref[cg, ST_INTER:ST_INTER + NR] = jnp.exp(m_prev - mx)
            st_ref[cg, ST_EMJ:ST_EMJ + NR] = jnp.exp(-mx - b[:, sl])
            st_ref[cg, ST_WKN:ST_WKN + NR] = jnp.exp(wk[:, sl] - m_new)
            st_ref[cg, ST_DECAY:ST_DECAY + NR] = jnp.exp(btot[:, sl] + m_prev - m_new)
            wt_ref[cg] = jnp.concatenate([w[:, sl], pad_rows], axis=0).T

    row = lax.broadcasted_iota(jnp.int32, (L, L), 0)
    col = lax.broadcasted_iota(jnp.int32, (L, L), 1)
    masks = (row <= col, row >= col)
    first_of8 = lax.broadcasted_iota(jnp.int32, (8, L), 0) == 0

    def row_tile(x):
        return jnp.where(first_of8, x, 0.0)

    head_lanes = [slice(h * DH_A, (h + 1) * DH_A) for h in range(NH_A)]

    def increments(j, carry):
        rows = pl.ds(pl.multiple_of(j * L, L), L)
        for h in range(NH_A):
            kh = k_ref[0, rows, head_lanes[h]]
            vt = v_ref[0, rows, head_lanes[h]].astype(f32).T
            vt_ref[j, h] = vt.astype(bf16)
            qt_ref[j, h] = q_ref[0, rows, head_lanes[h]].astype(f32).T.astype(bf16)
            wf = st_ref[j, ST_WKN + h:ST_WKN + h + 1, :]
            wb = st_ref[j, ST_WKN + NH_A + h:ST_WKN + NH_A + h + 1, :]
            lhs = jnp.concatenate([vt * wf, vt * wb, row_tile(wf), row_tile(wb)], axis=0).astype(bf16)
            u_ref[j, h] = _dot(lhs, kh)
        return carry

    lax.fori_loop(0, nct, increments, 0, unroll=4)

    seq_heads = [(s, h) for s in range(nseq) for h in range(NH_A)]
    for s, h in seq_heads:
        i = s * NH_A + h
        if state_in:
            cst_ref[i, 0:DH_A] = c0_ref[s, 0, h].T
            cst_ref[i, DH_A:2 * DH_A] = c0_ref[s, 1, h].T
            nsf_ref[i] = row_tile(n0_ref[s, h:h + 1, :])
            nsb_ref[i] = row_tile(n0_ref[s, NH_A + h:NH_A + h + 1, :])
        else:
            cst_ref[i] = jnp.zeros((2 * DH_A, DH_A), f32)
            nsf_ref[i] = jnp.zeros((8, DH_A), f32)
            nsb_ref[i] = jnp.zeros((8, DH_A), f32)

    def recur(t, carry):
        for s, h in seq_heads:
            i = s * NH_A + h
            jf = s * nc + t
            jb = s * nc + nc - 1 - t
            dec_f = st_ref[jf, ST_DECAY + h:ST_DECAY + h + 1, :]
            dec_b = st_ref[jb, ST_DECAY + NH_A + h:ST_DECAY + NH_A + h + 1, :]
            c_f = cst_ref[i, 0:DH_A]
            c_b = cst_ref[i, DH_A:2 * DH_A]
            cp_ref[jf, h, 0:DH_A] = c_f.astype(bf16)
            cp_ref[jb, h, DH_A:2 * DH_A] = c_b.astype(bf16)
            cst_ref[i, 0:DH_A] = dec_f * c_f + u_ref[jf, h, 0:DH_A]
            cst_ref[i, DH_A:2 * DH_A] = dec_b * c_b + u_ref[jb, h, DH_A:2 * DH_A]
            n_f = nsf_ref[i]
            n_b = nsb_ref[i]
            npf_ref[jf, h] = n_f
            npb_ref[jb, h] = n_b
            nsf_ref[i] = dec_f * n_f + u_ref[jf, h, 2 * DH_A:2 * DH_A + 8]
            nsb_ref[i] = dec_b * n_b + u_ref[jb, h, 2 * DH_A + 8:2 * DH_A + 16]
        return carry

    lax.fori_loop(0, nc, recur, 0)
    for s, h in seq_heads if state_out else ():
        i = s * NH_A + h
        cn_ref[s, 0, h] = cst_ref[i, 0:DH_A].T
        cn_ref[s, 1, h] = cst_ref[i, DH_A:2 * DH_A].T
        nn_ref[s, h:h + 1, :] = nsf_ref[i, 0:1]
        nn_ref[s, NH_A + h:NH_A + h + 1, :] = nsb_ref[i, 0:1]

    def outputs(j, carry):
        rows = pl.ds(pl.multiple_of(j * L, L), L)
        wt = wt_ref[j]
        heads = range(NH_A)
        kqs = []
        for h in heads:
            kh = k_ref[0, rows, head_lanes[h]]
            n_rows = jnp.concatenate([npf_ref[j, h], npb_ref[j, h]], axis=0).astype(bf16)
            kqs.append(_dot(jnp.concatenate([kh, n_rows, cp_ref[j, h]], axis=0), qt_ref[j, h]))
        decays = [[jnp.exp(jnp.where(masks[d], wt[:, NH_A * d + h:NH_A * d + h + 1]
                                     + st_ref[j, ST_U + NH_A * d + h:ST_U + NH_A * d + h + 1, :], NEG_BIG))
                   for d in range(2)] for h in heads]
        s_sums, h_ts = [], []
        for h in heads:
            kq = kqs[h]
            s_sum = None
            h_t = None
            for d in range(2):
                r = NH_A * d + h
                inter = st_ref[j, ST_INTER + r:ST_INTER + r + 1, :]
                emj = st_ref[j, ST_EMJ + r:ST_EMJ + r + 1, :]
                s_t = kq[0:L] * decays[h][d]
                qn = kq[L + 8 * d:L + 8 * d + 1]
                den = inter * qn + jnp.sum(s_t, axis=0, keepdims=True)
                rr = 1.0 / jnp.maximum(jnp.abs(den), emj)
                s_sum = s_t * rr if d == 0 else s_sum + s_t * rr
                part = kq[L + 16 + d * DH_A:L + 16 + (d + 1) * DH_A] * (inter * rr)
                h_t = part if d == 0 else h_t + part
            s_sums.append(s_sum.astype(bf16))
            h_ts.append(h_t)
        h_ts = [h_ts[h] + _dot(vt_ref[j, h], s_sums[h]) for h in heads]
        h_ts = [x * lax.rsqrt(jnp.mean(x * x, axis=0, keepdims=True) + EPS) for x in h_ts]
        for h in heads:
            hn_ref[0, rows, head_lanes[h]] = h_ts[h].T
        return carry

    lax.fori_loop(0, nct, outputs, 0, unroll=4)


def _mlstm(q, k, v, gates, state, nseq, state_out):
    B, T, _ = q.shape
    nc = T // CHUNK
    nct = nseq * nc
    G = B // nseq
    fold = lambda a: a.reshape(G, nseq * T, a.shape[-1])
    seq = lambda n: pl.BlockSpec((1, nseq * T, n), lambda b: (b, 0, 0))
    st_c = pl.BlockSpec((nseq, 2, NH_A, DH_A, DH_A), lambda b: (b, 0, 0, 0, 0))
    st_v = pl.BlockSpec((nseq, 2 * NH_A, DH_A), lambda b: (b, 0, 0))
    state_specs = [st_c, st_v, st_v]
    state_shapes = [jax.ShapeDtypeStruct((B, 2, NH_A, DH_A, DH_A), f32),
                    jax.ShapeDtypeStruct((B, 2 * NH_A, DH_A), f32),
                    jax.ShapeDtypeStruct((B, 2 * NH_A, DH_A), f32)]
    outs = pl.pallas_call(
        functools.partial(_mlstm_kernel, nseq=nseq, nc=nc, state_in=state is not None, state_out=state_out),
        grid=(G,),
        in_specs=[seq(D_A), seq(D_A), seq(D_A), seq(GATE_PAD)] + (state_specs if state is not None else []),
        out_specs=[seq(D_A)] + (state_specs if state_out else []),
        scratch_shapes=[pltpu.VMEM((nct, ST_ROWS, CHUNK), f32), pltpu.VMEM((nct, CHUNK, GATE_PAD), f32),
                        pltpu.VMEM((nct, NH_A, DH_A, CHUNK), bf16), pltpu.VMEM((nct, NH_A, DH_A, CHUNK), bf16),
                        pltpu.VMEM((nct, NH_A, 2 * DH_A + 16, DH_A), f32),
                        pltpu.VMEM((nct, NH_A, 2 * DH_A, DH_A), bf16),
                        pltpu.VMEM((nct, NH_A, 8, DH_A), f32), pltpu.VMEM((nct, NH_A, 8, DH_A), f32),
                        pltpu.VMEM((nseq * NH_A, 2 * DH_A, DH_A), f32),
                        pltpu.VMEM((nseq * NH_A, 8, DH_A), f32), pltpu.VMEM((nseq * NH_A, 8, DH_A), f32)],
        out_shape=[jax.ShapeDtypeStruct((G, nseq * T, D_A), f32)] + (state_shapes if state_out else []),
        compiler_params=pltpu.CompilerParams(dimension_semantics=("arbitrary",), vmem_limit_bytes=VMEM_LIMIT),
        name="mlstm",
    )(fold(q), fold(k), fold(v), fold(gates), *(state if state is not None else ()))
    return (outs[0].reshape(B, T, D_A),) + tuple(outs[1:])


def _mix_kernel(x_ref, mod_ref, hn_ref, so_ref, glu_ref, glu_prev_ref, glu_next_ref, sga_ref, sgb_ref,
                ng_ref, wpa_ref, cw_ref, cb_ref, lng_ref, lnb_ref, wpb_ref, wout_ref,
                x1_ref, ext_ref, zs_ref, *, tm, nt):
    t = pl.program_id(1)
    mod = mod_ref[0]
    g1 = mod[:, 2 * D_MODEL:3 * D_MODEL]

    a_in = (so_ref[0] * (hn_ref[0] * ng_ref[...])).astype(bf16)
    branch_a = _dot(a_in, wpa_ref[...])

    H = CONV_HALO
    SUB = 8
    LANE = 128
    te = tm + 2 * H
    ext_ref[0, 0:H, :] = jnp.where(t > 0, glu_prev_ref[0], 0.0)
    ext_ref[0, H:H + tm, :] = glu_ref[0]
    ext_ref[0, H + tm:te, :] = jnp.where(t < nt - 1, glu_next_ref[0], 0.0)
    for c in range(D_B // LANE):
        lanes = slice(c * LANE, (c + 1) * LANE)
        base = ext_ref[0, :, lanes]
        for k in range(1, SUB):
            ext_ref[k, :, lanes] = pltpu.roll(base, te - k, axis=0)
    off = H - CONV_W // 2
    for r0 in range(0, tm, CONV_ROWS):
        parts = []
        for c in range(D_B // LANE):
            lanes = slice(c * LANE, (c + 1) * LANE)
            acc = jnp.broadcast_to(cb_ref[:, lanes], (CONV_ROWS, LANE))
            for w in range(CONV_W):
                k, a = (off + w) % SUB, (off + w) // SUB * SUB
                acc = acc + ext_ref[k, r0 + a:r0 + a + CONV_ROWS, lanes] * cw_ref[w:w + 1, lanes]
            parts.append(acc)
        z = jnp.concatenate(parts, axis=1)
        mu = jnp.mean(z, axis=-1, keepdims=True)
        zc = z - mu
        var = jnp.mean(zc * zc, axis=-1, keepdims=True)
        zn = zc * lax.rsqrt(var + EPS) * lng_ref[...] + lnb_ref[...]
        zs_ref[r0:r0 + CONV_ROWS, :] = (zn * _sigmoid(zn)).astype(bf16)
    branch_b = _dot(zs_ref[...], wpb_ref[...])

    merged = sga_ref[0] * branch_a + sgb_ref[0] * branch_b
    x1_ref[0] = x_ref[0] + g1 * _dot(merged.astype(bf16), wout_ref[...])


def _mix_out(x, mod3, per_batch_mod, hn, so, glu, sga, sgb, wts, tm):
    B, T, _ = x.shape
    nt = T // tm
    H = CONV_HALO
    r = tm // H
    tok = lambda n: pl.BlockSpec((1, tm, n), lambda b, t: (b, t, 0))
    full = lambda a: pl.BlockSpec(a.shape, lambda b, t: (0,) * a.ndim)
    mod_map = (lambda b, t: (b, 0, 0)) if per_batch_mod else (lambda b, t: (0, 0, 0))
    prev = pl.BlockSpec((1, H, D_B), lambda b, t: (b, jnp.maximum(t * r - 1, 0), 0))
    nxt = pl.BlockSpec((1, H, D_B), lambda b, t: (b, jnp.minimum((t + 1) * r, T // H - 1), 0))
    return pl.pallas_call(
        functools.partial(_mix_kernel, tm=tm, nt=nt),
        grid=(B, nt),
        in_specs=[tok(D_MODEL), pl.BlockSpec((1, 1, N_MOD * D_MODEL), mod_map), tok(D_A), tok(D_A),
                  tok(D_B), prev, nxt, tok(D_MODEL), tok(D_MODEL)] + [full(w) for w in wts],
        out_specs=tok(D_MODEL),
        out_shape=jax.ShapeDtypeStruct((B, T, D_MODEL), f32),
        scratch_shapes=[pltpu.VMEM((8, tm + 2 * H, D_B), f32), pltpu.VMEM((tm, D_B), bf16)],
        compiler_params=pltpu.CompilerParams(dimension_semantics=("arbitrary", "arbitrary"),
                                             vmem_limit_bytes=VMEM_LIMIT),
        name="mix_out",
    )(x, mod3, hn, so, glu, glu, glu, sga, sgb, *wts)


def _ffn_kernel(*refs, tm, nt, on_grid, seq_len):
    if on_grid:
        (x_ref, xp_ref, xn_ref, mod_ref, g2n_ref, wu_ref, wd_ref, cw_ref, cb_ref, fg_ref,
         y_ref, h_ref, g0_ref, g1_ref, v0_ref, v1_ref, gl_ref, gr_ref, a0_ref, a1_ref, a2_ref, a3_ref, acc_ref) = refs
    else:
        (x_ref, mod_ref, g2n_ref, wu_ref, wd_ref, cw_ref, cb_ref, fg_ref,
         y_ref, h_ref, g0_ref, g1_ref, v0_ref, v1_ref, gl_ref, gr_ref, a0_ref, a1_ref, a2_ref, a3_ref, acc_ref) = refs
    gbufs, vbufs, acts = (g0_ref, g1_ref), (v0_ref, v1_ref), (a0_ref, a1_ref, a2_ref, a3_ref)
    t = pl.program_id(1)
    halo = GRID_W if on_grid else 0
    te = tm + 2 * halo
    mod = mod_ref[0]
    sh2 = mod[:, 3 * D_MODEL:4 * D_MODEL]
    sc2 = mod[:, 4 * D_MODEL:5 * D_MODEL]
    g2 = mod[:, 5 * D_MODEL:6 * D_MODEL]

    def norm_mod(x):
        return _rms(x, g2n_ref[...]) * (1.0 + sc2) + sh2

    h_ref[halo:halo + tm, :] = norm_mod(x_ref[0]).astype(bf16)
    if on_grid:
        h_ref[0:halo, :] = jnp.where(t > 0, norm_mod(xp_ref[0]), 0.0).astype(bf16)
        h_ref[halo + tm:te, :] = jnp.where(t < nt - 1, norm_mod(xn_ref[0]), 0.0).astype(bf16)

    SUB = 8
    LANE = 128
    seg = GRID_W if on_grid else seq_len
    sub_row = lax.broadcasted_iota(jnp.int32, (SUB, LANE), 0)
    first_row = sub_row == 0
    last_row = sub_row == SUB - 1
    row_taps = (0, 1, 2) if on_grid else (1,)

    def ff_cols(fc, base):
        if isinstance(fc, int):
            return slice(base + fc * FF_CHUNK, base + (fc + 1) * FF_CHUNK)
        return pl.ds(pl.multiple_of(base + fc * FF_CHUNK, 128), FF_CHUNK)

    def up(fc, slot):
        gbufs[slot][...] = _dot(h_ref[...], wu_ref[:, ff_cols(fc, 0)])
        vbufs[slot][...] = _dot(h_ref[halo:halo + tm, :], wu_ref[:, ff_cols(fc, D_FF)])

    def gate_act(fc, slot, aslot):
        cw = cw_ref[:, ff_cols(fc, 0)]
        cb = cb_ref[:, ff_cols(fc, 0)]
        for c in range(FF_CHUNK // LANE):
            lanes = slice(c * LANE, (c + 1) * LANE)
            for s0 in range(0, te, seg):
                gate = gbufs[slot][s0:s0 + seg, lanes]
                g_l = pltpu.roll(gate, 1, axis=0)
                g_r = pltpu.roll(gate, seg - 1, axis=0)
                gl_ref[s0:s0 + SUB, lanes] = jnp.where(first_row, 0.0, g_l[0:SUB])
                gl_ref[s0 + SUB:s0 + seg, lanes] = g_l[SUB:seg]
                gr_ref[s0:s0 + seg - SUB, lanes] = g_r[0:seg - SUB]
                gr_ref[s0 + seg - SUB:s0 + seg, lanes] = jnp.where(last_row, 0.0, g_r[seg - SUB:seg])
            for r0 in range(0, tm, GRID_W):
                conv = jnp.broadcast_to(cb[:, lanes], (GRID_W, LANE))
                for kh in row_taps:
                    lo = r0 + kh * GRID_W if on_grid else r0
                    conv = conv + gl_ref[lo:lo + GRID_W, lanes] * cw[3 * kh:3 * kh + 1, lanes]
                    conv = conv + gbufs[slot][lo:lo + GRID_W, lanes] * cw[3 * kh + 1:3 * kh + 2, lanes]
                    conv = conv + gr_ref[lo:lo + GRID_W, lanes] * cw[3 * kh + 2:3 * kh + 3, lanes]
                gelu = conv * (0.5 + 0.5 * jnp.tanh(conv * (GELU_C0 + GELU_C1 * (conv * conv))))
                acts[aslot][r0:r0 + GRID_W, lanes] = (gelu * vbufs[slot][r0:r0 + GRID_W, lanes]).astype(bf16)

    def down(fc, aslot):
        return _dot(acts[aslot][...], wd_ref[fc])


    def pair(p, parity, with_down):
        fc = 2 * p
        wr, rd = 2 * parity, 2 * (1 - parity)
        if with_down:
            acc_ref[...] += down(fc - 2, rd) + down(fc - 1, rd + 1)
        up(fc + 1, 1)
        gate_act(fc, 0, wr)
        up(fc + 2, 0)
        gate_act(fc + 1, 1, wr + 1)

    n_pairs = (N_FF_CHUNKS - 1) // 2
    assert N_FF_CHUNKS == 2 * n_pairs + 1 and n_pairs % 2 == 1
    up(0, 0)
    acc_ref[...] = jnp.zeros_like(acc_ref)
    pair(0, 0, False)

    def two_pairs(i, carry):
        p = 2 * i + 1
        pair(p, 1, True)
        pair(p + 1, 0, True)
        return carry

    lax.fori_loop(0, (n_pairs - 1) // 2, two_pairs, 0)
    last = N_FF_CHUNKS - 1
    gate_act(last, 0, 2)
    ffn_out = acc_ref[...] + (down(last - 2, 0) + down(last - 1, 1) + down(last, 2))
    y_ref[0] = _rms(x_ref[0] + g2 * ffn_out, fg_ref[...])


def _ffn(x1, mod3, per_batch_mod, wts, tm, on_grid):
    B0, T0, _ = x1.shape
    if tm > T0:
        assert not per_batch_mod and not on_grid and tm % T0 == 0 and B0 % (tm // T0) == 0
        x1 = x1.reshape(B0 * T0 // tm, tm, D_MODEL)
    B, T, _ = x1.shape
    nt = T // tm
    halo = GRID_W if on_grid else 0
    te = tm + 2 * halo
    tok = pl.BlockSpec((1, tm, D_MODEL), lambda b, t: (b, t, 0))
    full = lambda a: pl.BlockSpec(a.shape, lambda b, t: (0,) * a.ndim)
    mod_map = (lambda b, t: (b, 0, 0)) if per_batch_mod else (lambda b, t: (0, 0, 0))
    in_specs = [tok]
    args = [x1]
    if on_grid:
        r = tm // GRID_W
        in_specs += [pl.BlockSpec((1, GRID_W, D_MODEL), lambda b, t: (b, jnp.maximum(t * r - 1, 0), 0)),
                     pl.BlockSpec((1, GRID_W, D_MODEL), lambda b, t: (b, jnp.minimum((t + 1) * r, T // GRID_W - 1), 0))]
        args += [x1, x1]
    in_specs += [pl.BlockSpec((1, 1, N_MOD * D_MODEL), mod_map)] + [full(w) for w in wts]
    args += [mod3] + list(wts)
    y = pl.pallas_call(
        functools.partial(_ffn_kernel, tm=tm, nt=nt, on_grid=on_grid, seq_len=T0),
        grid=(B, nt),
        in_specs=in_specs,
        out_specs=tok,
        out_shape=jax.ShapeDtypeStruct((B, T, D_MODEL), f32),
        scratch_shapes=[pltpu.VMEM((te, D_MODEL), bf16)]
        + [pltpu.VMEM((te, FF_CHUNK), f32)] * 2 + [pltpu.VMEM((tm, FF_CHUNK), f32)] * 2
        + [pltpu.VMEM((te, FF_CHUNK), f32)] * 2 + [pltpu.VMEM((tm, FF_CHUNK), bf16)] * 4
        + [pltpu.VMEM((tm, D_MODEL), f32)],
        compiler_params=pltpu.CompilerParams(dimension_semantics=("arbitrary", "arbitrary"),
                                             vmem_limit_bytes=VMEM_LIMIT),
        name="ffn_grid" if on_grid else "ffn_ctx",
    )(*args)
    return y.reshape(B0, T0, D_MODEL)


def _trunk(x, mod3, per_batch_mod, state, on_grid, w, tm_proj, tm_mix, tm_ffn, mlstm_nseq):
    q, k, v, so, glu, sga, sgb, gates = _in_proj(x, mod3, per_batch_mod, w["norm1_g"], w["in"], tm_proj)
    hn, *new_state = _mlstm(q, k, v, gates, state, mlstm_nseq, state is None)
    x1 = _mix_out(x, mod3, per_batch_mod, hn, so, glu, sga, sgb, w["mix"], tm_mix)
    y = _ffn(x1, mod3, per_batch_mod, w["ffn"], tm_ffn, on_grid)
    return y, new_state


def kernel(x_prompt, x_sample, c, state_C, state_n, state_m, c_ctx, w_ada, b_ada, norm1_g, w_in, b_in,
           mlstm_norm_g, w_proj_a, conv_dw_w, conv_dw_b, conv_ln_g, conv_ln_b, w_proj_b, w_out, norm2_g,
           w_up, ffn_dw_w, ffn_dw_b, w_down, final_norm_g):
    Bp = x_prompt.shape[0]
    Bl = x_sample.shape[0]
    l = 0
    row = lambda a: a.reshape(1, -1)

    ct = jnp.concatenate([c, c_ctx[None, :], jnp.zeros((8 - Bl - 1, D_MODEL), f32)], axis=0).T
    mod = _adaln_mod(ct, w_ada[l], row(b_ada[l]), Bl + 1)
    mod_lat = mod[0:Bl].reshape(Bl, 1, N_MOD * D_MODEL)
    mod_ctx = mod[Bl:Bl + 1].reshape(1, 1, N_MOD * D_MODEL)

    o_g = 4 * D_A
    o_u = o_g + 4 * NH_A
    nh = NH_A

    def split_in(a, dt):
        g = a[..., o_g:o_u]
        pad = jnp.zeros(a.shape[:-1] + (GATE_PAD - 4 * nh,), a.dtype)
        gates = jnp.concatenate([g[..., 0:nh], g[..., 2 * nh:3 * nh], g[..., nh:2 * nh], g[..., 3 * nh:], pad], axis=-1)
        return a[..., :o_g].astype(dt), a[..., o_u:].astype(dt), gates.astype(dt)

    (w_a, w_b, w_g), (b_a, b_b, b_g) = split_in(w_in[l], bf16), split_in(row(b_in[l]), f32)
    w_in_parts = (w_a, b_a, w_b, b_b, w_g, b_g)
    w_mix = (row(mlstm_norm_g[l]), w_proj_a[l].astype(bf16),
             jnp.pad(conv_dw_w[l], ((0, 1), (0, 0))), row(conv_dw_b[l]), row(conv_ln_g[l]), row(conv_ln_b[l]),
             w_proj_b[l].astype(bf16), w_out[l].astype(bf16))
    w_ffn = (row(norm2_g[l]), w_up[l].astype(bf16),
             w_down[l].astype(bf16).reshape(N_FF_CHUNKS, FF_CHUNK, D_MODEL),
             ffn_dw_w[l].reshape(9, D_FF), row(ffn_dw_b[l]), row(final_norm_g))
    w = {"norm1_g": row(norm1_g[l]), "in": w_in_parts, "mix": w_mix, "ffn": w_ffn}

    y_prompt, (cn, nn, mn) = _trunk(x_prompt, mod_ctx, False, None, False, w, 512, 256, 512, 4)

    c0 = state_C[:, l]
    n0 = state_n[:, l].reshape(Bl, 2 * NH_A, DH_A)
    m0 = jnp.broadcast_to(state_m[:, l].reshape(Bl, 2 * NH_A, 1), (Bl, 2 * NH_A, DH_A))
    y_sample, _ = _trunk(x_sample, mod_lat, True, (c0, n0, m0), True, w, 512, 512, 512, 1)

    new_state_C = cn[:, None]
    new_state_n = nn.reshape(Bp, 1, 2, NH_A, DH_A)
    new_state_m = mn[:, :, 0].reshape(Bp, 1, 2, NH_A)
    return (y_prompt, y_sample, new_state_C, new_state_n, new_state_m)
```

```python
import functools

import jax
import jax.numpy as jnp
from jax import lax
from jax.experimental import pallas as pl
from jax.experimental.pallas import tpu as pltpu

D_MODEL = 1024
D_A = 512
NH_A = 4
DH_A = 128
CHUNK = 128
D_B = 512
CONV_W = 31
CONV_HALO = 16
CONV_ROWS = 64
D_FF = 2816
FF_CHUNK = 256
N_FF_CHUNKS = D_FF // FF_CHUNK
GRID_W = 64
N_MOD = 6
GATE_PAD = 128
EPS = 1e-6
Q_SCALE = DH_A ** -0.5
VMEM_LIMIT = 56 * 1024 * 1024

f32 = jnp.float32
bf16 = jnp.bfloat16


def _rms(x, g):
    return x * lax.rsqrt(jnp.mean(x * x, axis=-1, keepdims=True) + EPS) * g


def _sigmoid(x):
    return 1.0 / (1.0 + jnp.exp(-x))


def _log_sigmoid(x):
    return jnp.minimum(x, 0.0) - jnp.log(1.0 + jnp.exp(-jnp.abs(x)))


GELU_C0 = 0.7978845608028654
GELU_C1 = GELU_C0 * 0.044715


def _dot(a, b):
    return jnp.dot(a, b, preferred_element_type=f32)


def _mod_kernel(ct_ref, w_ref, b_ref, o_ref, *, n_rows):
    ct = ct_ref[...]
    st = ct * _sigmoid(ct)
    w = w_ref[...]
    rows = [jnp.sum(w * st[:, r:r + 1], axis=0, keepdims=True) for r in range(n_rows)]
    rows.append(jnp.zeros((8 - n_rows, w.shape[1]), f32))
    o_ref[...] = jnp.concatenate(rows, axis=0) + b_ref[...]


def _adaln_mod(ct, w_ada, b_ada, n_rows):
    n = w_ada.shape[1]
    tn = 512
    return pl.pallas_call(
        functools.partial(_mod_kernel, n_rows=n_rows),
        grid=(n // tn,),
        in_specs=[
            pl.BlockSpec((D_MODEL, 8), lambda j: (0, 0)),
            pl.BlockSpec((D_MODEL, tn), lambda j: (0, j)),
            pl.BlockSpec((1, tn), lambda j: (0, j)),
        ],
        out_specs=pl.BlockSpec((8, tn), lambda j: (0, j)),
        out_shape=jax.ShapeDtypeStruct((8, n), f32),
        compiler_params=pltpu.CompilerParams(dimension_semantics=("arbitrary",), vmem_limit_bytes=VMEM_LIMIT),
        name="adaln_mod",
    )(ct, w_ada, b_ada)


def _repack_kernel(w_ref, wa_ref, wb_ref, *, o_a, o_b):
    w = w_ref[...]
    wa_ref[...] = w[:, :o_a].astype(bf16)
    wb_ref[...] = w[:, o_b:].astype(bf16)


def _repack_w_in(w, o_a, o_b):
    K, N = w.shape
    tr = 128
    return pl.pallas_call(
        functools.partial(_repack_kernel, o_a=o_a, o_b=o_b),
        grid=(K // tr,),
        in_specs=[pl.BlockSpec((tr, N), lambda i: (i, 0))],
        out_specs=[pl.BlockSpec((tr, o_a), lambda i: (i, 0)), pl.BlockSpec((tr, N - o_b), lambda i: (i, 0))],
        out_shape=[jax.ShapeDtypeStruct((K, o_a), bf16), jax.ShapeDtypeStruct((K, N - o_b), bf16)],
        compiler_params=pltpu.CompilerParams(dimension_semantics=("arbitrary",), vmem_limit_bytes=VMEM_LIMIT),
        name="repack_w_in",
    )(w)


def _inproj_kernel(x_ref, mod_ref, g_ref, wa_ref, ba_ref, wb_ref, bb_ref, wg_ref, bg_ref,
                   q_ref, k_ref, v_ref, so_ref, glu_ref, sga_ref, sgb_ref, gates_ref):
    x = x_ref[0]
    mod = mod_ref[0]
    sh1 = mod[:, 0:D_MODEL]
    sc1 = mod[:, D_MODEL:2 * D_MODEL]
    h = (_rms(x, g_ref[...]) * (1.0 + sc1) + sh1).astype(bf16)

    def proj(w_ref, b_ref, lo, n):
        return _dot(h, w_ref[:, lo:lo + n]) + b_ref[:, lo:lo + n]

    q_ref[0] = (proj(wa_ref, ba_ref, 0, D_A) * Q_SCALE).astype(bf16)
    k_ref[0] = proj(wa_ref, ba_ref, D_A, D_A).astype(bf16)
    v_ref[0] = proj(wa_ref, ba_ref, 2 * D_A, D_A).astype(bf16)
    so_ref[0] = _sigmoid(proj(wa_ref, ba_ref, 3 * D_A, D_A))
    glu_ref[0] = proj(wb_ref, bb_ref, 0, D_B) * _sigmoid(proj(wb_ref, bb_ref, D_B, D_B))
    sga_ref[0] = _sigmoid(proj(wb_ref, bb_ref, 2 * D_B, D_MODEL))
    sgb_ref[0] = _sigmoid(proj(wb_ref, bb_ref, 2 * D_B + D_MODEL, D_MODEL))
    gates_ref[0] = _dot(h, wg_ref[...].astype(bf16)) + bg_ref[...]


def _in_proj(x, mod3, per_batch_mod, norm_g, wts, tm):
    B0, T0, _ = x.shape
    if tm > T0:
        assert not per_batch_mod and tm % T0 == 0 and B0 % (tm // T0) == 0
        x = x.reshape(B0 * T0 // tm, tm, D_MODEL)
    B, T, _ = x.shape
    nt = T // tm
    tok = lambda n: pl.BlockSpec((1, tm, n), lambda b, t: (b, t, 0))
    full = lambda a: pl.BlockSpec(a.shape, lambda b, t: (0,) * a.ndim)
    mod_map = (lambda b, t: (b, 0, 0)) if per_batch_mod else (lambda b, t: (0, 0, 0))
    sds = lambda n, dt: jax.ShapeDtypeStruct((B, T, n), dt)
    outs = pl.pallas_call(
        _inproj_kernel,
        grid=(B, nt),
        in_specs=[tok(D_MODEL), pl.BlockSpec((1, 1, N_MOD * D_MODEL), mod_map), full(norm_g)]
        + [full(w) for w in wts],
        out_specs=[tok(D_A), tok(D_A), tok(D_A), tok(D_A), tok(D_B), tok(D_MODEL), tok(D_MODEL), tok(GATE_PAD)],
        out_shape=[sds(D_A, bf16), sds(D_A, bf16), sds(D_A, bf16), sds(D_A, f32), sds(D_B, f32),
                   sds(D_MODEL, f32), sds(D_MODEL, f32), sds(GATE_PAD, f32)],
        compiler_params=pltpu.CompilerParams(dimension_semantics=("arbitrary", "arbitrary"),
                                             vmem_limit_bytes=VMEM_LIMIT),
        name="in_proj",
    )(x, mod3, norm_g, *wts)
    return [o.reshape(B0, T0, o.shape[-1]) for o in outs]


NEG_BIG = -1e30
ST_U, ST_INTER, ST_EMJ, ST_WKN, ST_DECAY = 0, 8, 16, 24, 32
ST_ROWS = 40


def _chunk_scan(x, op, fill, prefix, lane, width):
    k = 1
    while k < CHUNK:
        if prefix:
            shifted = jnp.where(lane >= k, pltpu.roll(x, k, axis=1), fill)
        else:
            shifted = jnp.where(lane < CHUNK - k, pltpu.roll(x, width - k, axis=1), fill)
        x = op(x, shifted)
        k *= 2
    return x


def _mlstm_kernel(*refs, nseq, nc, state_in, state_out):
    refs = list(refs)
    q_ref, k_ref, v_ref, g_ref = refs[:4]
    del refs[:4]
    if state_in:
        c0_ref, n0_ref, m0_ref = refs[:3]
        del refs[:3]
    hn_ref = refs.pop(0)
    if state_out:
        cn_ref, nn_ref, mn_ref = refs[:3]
        del refs[:3]
    st_ref, wt_ref, vt_ref, qt_ref, u_ref, cp_ref, npf_ref, npb_ref, cst_ref, nsf_ref, nsb_ref = refs
    L = CHUNK
    nct = nseq * nc
    T = nct * L
    NR = 2 * NH_A

    lane = jnp.bitwise_and(lax.broadcasted_iota(jnp.int32, (NR, T), 1), L - 1)
    is_fwd = lax.broadcasted_iota(jnp.int32, (NR, T), 0) < NH_A
    is_fwd_c = lax.broadcasted_iota(jnp.int32, (NR, L), 0) < NH_A
    i_parts, f_parts = [], []
    for c in range(nct):
        gt = g_ref[0, c * L:(c + 1) * L, :].T
        i_parts.append(gt[0:NR])
        f_parts.append(gt[NR:2 * NR])
    ig = jnp.concatenate(i_parts, axis=1)
    lf = _log_sigmoid(jnp.concatenate(f_parts, axis=1))
    scan = functools.partial(_chunk_scan, lane=lane, width=T)
    ps = scan(lf, jnp.add, 0.0, True)
    ss = scan(lf, jnp.add, 0.0, False)
    b = jnp.where(is_fwd, ps, ss)
    btot = ps + ss - lf
    w = ig - b
    cmw = jnp.where(is_fwd, scan(w, jnp.maximum, -jnp.inf, True), scan(w, jnp.maximum, -jnp.inf, False))
    wk = btot - b + ig
    a = jnp.maximum(scan(wk, jnp.maximum, -jnp.inf, True), scan(wk, jnp.maximum, -jnp.inf, False))
    chunk_lanes = lambda cg: slice(cg * L, (cg + 1) * L)
    pad_rows = jnp.zeros((L - NR, L), f32)
    for s in range(nseq):
        ms = [m0_ref[s] if state_in else jnp.zeros((NR, L), f32)]
        for t in range(nc):
            cf, cb = chunk_lanes(s * nc + t), chunk_lanes(s * nc + nc - 1 - t)
            bt = jnp.where(is_fwd_c, btot[:, cf], btot[:, cb])
            at = jnp.where(is_fwd_c, a[:, cf], a[:, cb])
            ms.append(jnp.maximum(bt + ms[-1], at))
        if state_out:
            mn_ref[s] = ms[nc]
        for c in range(nc):
            cg = s * nc + c
            sl = chunk_lanes(cg)
            m_prev = jnp.where(is_fwd_c, ms[c], ms[nc - 1 - c])
            m_new = jnp.where(is_fwd_c, ms[c + 1], ms[nc - c])
            mx = jnp.maximum(m_prev, cmw[:, sl])
            st_ref[cg, ST_U:ST_U + NR] = -mx
            st_ref[cg, ST_INTER:ST_INTER + NR] = jnp.exp(m_prev - mx)
            st_ref[cg, ST_EMJ:ST_EMJ + NR] = jnp.exp(-mx - b[:, sl])
            st_ref[cg, ST_WKN:ST_WKN + NR] = jnp.exp(wk[:, sl] - m_new)
            st_ref[cg, ST_DECAY:ST_DECAY + NR] = jnp.exp(btot[:, sl] + m_prev - m_new)
            wt_ref[cg] = jnp.concatenate([w[:, sl], pad_rows], axis=0).T

    row = lax.broadcasted_iota(jnp.int32, (L, L), 0)
    col = lax.broadcasted_iota(jnp.int32, (L, L), 1)
    masks = (row <= col, row >= col)
    first_of8 = lax.broadcasted_iota(jnp.int32, (8, L), 0) == 0

    def row_tile(x):
        return jnp.where(first_of8, x, 0.0)

    head_lanes = [slice(h * DH_A, (h + 1) * DH_A) for h in range(NH_A)]

    def increments(j, carry):
        rows = pl.ds(pl.multiple_of(j * L, L), L)
        for h in range(NH_A):
            kh = k_ref[0, rows, head_lanes[h]]
            vt = v_ref[0, rows, head_lanes[h]].astype(f32).T
            vt_ref[j, h] = vt.astype(bf16)
            qt_ref[j, h] = q_ref[0, rows, head_lanes[h]].astype(f32).T.astype(bf16)
            wf = st_ref[j, ST_WKN + h:ST_WKN + h + 1, :]
            wb = st_ref[j, ST_WKN + NH_A + h:ST_WKN + NH_A + h + 1, :]
            lhs = jnp.concatenate([vt * wf, vt * wb, row_tile(wf), row_tile(wb)], axis=0).astype(bf16)
            u_ref[j, h] = _dot(lhs, kh)
        return carry

    lax.fori_loop(0, nct, increments, 0, unroll=4)

    seq_heads = [(s, h) for s in range(nseq) for h in range(NH_A)]
    for s, h in seq_heads:
        i = s * NH_A + h
        if state_in:
            cst_ref[i, 0:DH_A] = c0_ref[s, 0, h].T
            cst_ref[i, DH_A:2 * DH_A] = c0_ref[s, 1, h].T
            nsf_ref[i] = row_tile(n0_ref[s, h:h + 1, :])
            nsb_ref[i] = row_tile(n0_ref[s, NH_A + h:NH_A + h + 1, :])
        else:
            cst_ref[i] = jnp.zeros((2 * DH_A, DH_A), f32)
            nsf_ref[i] = jnp.zeros((8, DH_A), f32)
            nsb_ref[i] = jnp.zeros((8, DH_A), f32)

    def recur(t, carry):
        for s, h in seq_heads:
            i = s * NH_A + h
            jf = s * nc + t
            jb = s * nc + nc - 1 - t
            dec_f = st_ref[jf, ST_DECAY + h:ST_DECAY + h + 1, :]
            dec_b = st_ref[jb, ST_DECAY + NH_A + h:ST_DECAY + NH_A + h + 1, :]
            c_f = cst_ref[i, 0:DH_A]
            c_b = cst_ref[i, DH_A:2 * DH_A]
            cp_ref[jf, h, 0:DH_A] = c_f.astype(bf16)
            cp_ref[jb, h, DH_A:2 * DH_A] = c_b.astype(bf16)
            cst_ref[i, 0:DH_A] = dec_f * c_f + u_ref[jf, h, 0:DH_A]
            cst_ref[i, DH_A:2 * DH_A] = dec_b * c_b + u_ref[jb, h, DH_A:2 * DH_A]
            n_f = nsf_ref[i]
            n_b = nsb_ref[i]
            npf_ref[jf, h] = n_f
            npb_ref[jb, h] = n_b
            nsf_ref[i] = dec_f * n_f + u_ref[jf, h, 2 * DH_A:2 * DH_A + 8]
            nsb_ref[i] = dec_b * n_b + u_ref[jb, h, 2 * DH_A + 8:2 * DH_A + 16]
        return carry

    lax.fori_loop(0, nc, recur, 0)
    for s, h in seq_heads if state_out else ():
        i = s * NH_A + h
        cn_ref[s, 0, h] = cst_ref[i, 0:DH_A].T
        cn_ref[s, 1, h] = cst_ref[i, DH_A:2 * DH_A].T
        nn_ref[s, h:h + 1, :] = nsf_ref[i, 0:1]
        nn_ref[s, NH_A + h:NH_A + h + 1, :] = nsb_ref[i, 0:1]

    def outputs(j, carry):
        rows = pl.ds(pl.multiple_of(j * L, L), L)
        wt = wt_ref[j]
        heads = range(NH_A)
        kqs = []
        for h in heads:
            kh = k_ref[0, rows, head_lanes[h]]
            n_rows = jnp.concatenate([npf_ref[j, h], npb_ref[j, h]], axis=0).astype(bf16)
            kqs.append(_dot(jnp.concatenate([kh, n_rows, cp_ref[j, h]], axis=0), qt_ref[j, h]))
        decays = [[jnp.exp(jnp.where(masks[d], wt[:, NH_A * d + h:NH_A * d + h + 1]
                                     + st_ref[j, ST_U + NH_A * d + h:ST_U + NH_A * d + h + 1, :], NEG_BIG))
                   for d in range(2)] for h in heads]
        s_sums, h_ts = [], []
        for h in heads:
            kq = kqs[h]
            s_sum = None
            h_t = None
            for d in range(2):
                r = NH_A * d + h
                inter = st_ref[j, ST_INTER + r:ST_INTER + r + 1, :]
                emj = st_ref[j, ST_EMJ + r:ST_EMJ + r + 1, :]
                s_t = kq[0:L] * decays[h][d]
                qn = kq[L + 8 * d:L + 8 * d + 1]
                den = inter * qn + jnp.sum(s_t, axis=0, keepdims=True)
                rr = 1.0 / jnp.maximum(jnp.abs(den), emj)
                s_sum = s_t * rr if d == 0 else s_sum + s_t * rr
                part = kq[L + 16 + d * DH_A:L + 16 + (d + 1) * DH_A] * (inter * rr)
                h_t = part if d == 0 else h_t + part
            s_sums.append(s_sum.astype(bf16))
            h_ts.append(h_t)
        h_ts = [h_ts[h] + _dot(vt_ref[j, h], s_sums[h]) for h in heads]
        h_ts = [x * lax.rsqrt(jnp.mean(x * x, axis=0, keepdims=True) + EPS) for x in h_ts]
        for h in heads:
            hn_ref[0, rows, head_lanes[h]] = h_ts[h].T
        return carry

    lax.fori_loop(0, nct, outputs, 0, unroll=4)


def _mlstm(q, k, v, gates, state, nseq, state_out):
    B, T, _ = q.shape
    nc = T // CHUNK
    nct = nseq * nc
    G = B // nseq
    fold = lambda a: a.reshape(G, nseq * T, a.shape[-1])
    seq = lambda n: pl.BlockSpec((1, nseq * T, n), lambda b: (b, 0, 0))
    st_c = pl.BlockSpec((nseq, 2, NH_A, DH_A, DH_A), lambda b: (b, 0, 0, 0, 0))
    st_v = pl.BlockSpec((nseq, 2 * NH_A, DH_A), lambda b: (b, 0, 0))
    state_specs = [st_c, st_v, st_v]
    state_shapes = [jax.ShapeDtypeStruct((B, 2, NH_A, DH_A, DH_A), f32),
                    jax.ShapeDtypeStruct((B, 2 * NH_A, DH_A), f32),
                    jax.ShapeDtypeStruct((B, 2 * NH_A, DH_A), f32)]
    outs = pl.pallas_call(
        functools.partial(_mlstm_kernel, nseq=nseq, nc=nc, state_in=state is not None, state_out=state_out),
        grid=(G,),
        in_specs=[seq(D_A), seq(D_A), seq(D_A), seq(GATE_PAD)] + (state_specs if state is not None else []),
        out_specs=[seq(D_A)] + (state_specs if state_out else []),
        scratch_shapes=[pltpu.VMEM((nct, ST_ROWS, CHUNK), f32), pltpu.VMEM((nct, CHUNK, GATE_PAD), f32),
                        pltpu.VMEM((nct, NH_A, DH_A, CHUNK), bf16), pltpu.VMEM((nct, NH_A, DH_A, CHUNK), bf16),
                        pltpu.VMEM((nct, NH_A, 2 * DH_A + 16, DH_A), f32),
                        pltpu.VMEM((nct, NH_A, 2 * DH_A, DH_A), bf16),
                        pltpu.VMEM((nct, NH_A, 8, DH_A), f32), pltpu.VMEM((nct, NH_A, 8, DH_A), f32),
                        pltpu.VMEM((nseq * NH_A, 2 * DH_A, DH_A), f32),
                        pltpu.VMEM((nseq * NH_A, 8, DH_A), f32), pltpu.VMEM((nseq * NH_A, 8, DH_A), f32)],
        out_shape=[jax.ShapeDtypeStruct((G, nseq * T, D_A), f32)] + (state_shapes if state_out else []),
        compiler_params=pltpu.CompilerParams(dimension_semantics=("arbitrary",), vmem_limit_bytes=VMEM_LIMIT),
        name="mlstm",
    )(fold(q), fold(k), fold(v), fold(gates), *(state if state is not None else ()))
    return (outs[0].reshape(B, T, D_A),) + tuple(outs[1:])


def _mix_kernel(x_ref, mod_ref, hn_ref, so_ref, glu_ref, glu_prev_ref, glu_next_ref, sga_ref, sgb_ref,
                ng_ref, wpa_ref, cw_ref, cb_ref, lng_ref, lnb_ref, wpb_ref, wout_ref,
                x1_ref, ext_ref, zs_ref, *, tm, nt):
    t = pl.program_id(1)
    mod = mod_ref[0]
    g1 = mod[:, 2 * D_MODEL:3 * D_MODEL]

    a_in = (so_ref[0] * (hn_ref[0] * ng_ref[...])).astype(bf16)
    branch_a = _dot(a_in, wpa_ref[...])

    H = CONV_HALO
    SUB = 8
    LANE = 128
    te = tm + 2 * H
    ext_ref[0, 0:H, :] = jnp.where(t > 0, glu_prev_ref[0], 0.0)
    ext_ref[0, H:H + tm, :] = glu_ref[0]
    ext_ref[0, H + tm:te, :] = jnp.where(t < nt - 1, glu_next_ref[0], 0.0)
    for c in range(D_B // LANE):
        lanes = slice(c * LANE, (c + 1) * LANE)
        base = ext_ref[0, :, lanes]
        for k in range(1, SUB):
            ext_ref[k, :, lanes] = pltpu.roll(base, te - k, axis=0)
    off = H - CONV_W // 2
    for r0 in range(0, tm, CONV_ROWS):
        parts = []
        for c in range(D_B // LANE):
            lanes = slice(c * LANE, (c + 1) * LANE)
            acc = jnp.broadcast_to(cb_ref[:, lanes], (CONV_ROWS, LANE))
            for w in range(CONV_W):
                k, a = (off + w) % SUB, (off + w) // SUB * SUB
                acc = acc + ext_ref[k, r0 + a:r0 + a + CONV_ROWS, lanes] * cw_ref[w:w + 1, lanes]
            parts.append(acc)
        z = jnp.concatenate(parts, axis=1)
        mu = jnp.mean(z, axis=-1, keepdims=True)
        zc = z - mu
        var = jnp.mean(zc * zc, axis=-1, keepdims=True)
        zn = zc * lax.rsqrt(var + EPS) * lng_ref[...] + lnb_ref[...]
        zs_ref[r0:r0 + CONV_ROWS, :] = (zn * _sigmoid(zn)).astype(bf16)
    branch_b = _dot(zs_ref[...], wpb_ref[...])

    merged = sga_ref[0] * branch_a + sgb_ref[0] * branch_b
    x1_ref[0] = x_ref[0] + g1 * _dot(merged.astype(bf16), wout_ref[...])


def _mix_out(x, mod3, per_batch_mod, hn, so, glu, sga, sgb, wts, tm):
    B, T, _ = x.shape
    nt = T // tm
    H = CONV_HALO
    r = tm // H
    tok = lambda n: pl.BlockSpec((1, tm, n), lambda b, t: (b, t, 0))
    full = lambda a: pl.BlockSpec(a.shape, lambda b, t: (0,) * a.ndim)
    mod_map = (lambda b, t: (b, 0, 0)) if per_batch_mod else (lambda b, t: (0, 0, 0))
    prev = pl.BlockSpec((1, H, D_B), lambda b, t: (b, jnp.maximum(t * r - 1, 0), 0))
    nxt = pl.BlockSpec((1, H, D_B), lambda b, t: (b, jnp.minimum((t + 1) * r, T // H - 1), 0))
    return pl.pallas_call(
        functools.partial(_mix_kernel, tm=tm, nt=nt),
        grid=(B, nt),
        in_specs=[tok(D_MODEL), pl.BlockSpec((1, 1, N_MOD * D_MODEL), mod_map), tok(D_A), tok(D_A),
                  tok(D_B), prev, nxt, tok(D_MODEL), tok(D_MODEL)] + [full(w) for w in wts],
        out_specs=tok(D_MODEL),
        out_shape=jax.ShapeDtypeStruct((B, T, D_MODEL), f32),
        scratch_shapes=[pltpu.VMEM((8, tm + 2 * H, D_B), f32), pltpu.VMEM((tm, D_B), bf16)],
        compiler_params=pltpu.CompilerParams(dimension_semantics=("arbitrary", "arbitrary"),
                                             vmem_limit_bytes=VMEM_LIMIT),
        name="mix_out",
    )(x, mod3, hn, so, glu, glu, glu, sga, sgb, *wts)


def _ffn_kernel(*refs, tm, nt, on_grid, seq_len):
    if on_grid:
        (x_ref, xp_ref, xn_ref, mod_ref, g2n_ref, wu_ref, wd_ref, cw_ref, cb_ref, fg_ref,
         y_ref, h_ref, g0_ref, g1_ref, v0_ref, v1_ref, gl_ref, gr_ref, a0_ref, a1_ref, a2_ref, a3_ref, acc_ref) = refs
    else:
        (x_ref, mod_ref, g2n_ref, wu_ref, wd_ref, cw_ref, cb_ref, fg_ref,
         y_ref, h_ref, g0_ref, g1_ref, v0_ref, v1_ref, gl_ref, gr_ref, a0_ref, a1_ref, a2_ref, a3_ref, acc_ref) = refs
    gbufs, vbufs, acts = (g0_ref, g1_ref), (v0_ref, v1_ref), (a0_ref, a1_ref, a2_ref, a3_ref)
    t = pl.program_id(1)
    halo = GRID_W if on_grid else 0
    te = tm + 2 * halo
    mod = mod_ref[0]
    sh2 = mod[:, 3 * D_MODEL:4 * D_MODEL]
    sc2 = mod[:, 4 * D_MODEL:5 * D_MODEL]
    g2 = mod[:, 5 * D_MODEL:6 * D_MODEL]

    def norm_mod(x):
        return _rms(x, g2n_ref[...]) * (1.0 + sc2) + sh2

    h_ref[halo:halo + tm, :] = norm_mod(x_ref[0]).astype(bf16)
    if on_grid:
        h_ref[0:halo, :] = jnp.where(t > 0, norm_mod(xp_ref[0]), 0.0).astype(bf16)
        h_ref[halo + tm:te, :] = jnp.where(t < nt - 1, norm_mod(xn_ref[0]), 0.0).astype(bf16)

    SUB = 8
    LANE = 128
    seg = GRID_W if on_grid else seq_len
    sub_row = lax.broadcasted_iota(jnp.int32, (SUB, LANE), 0)
    first_row = sub_row == 0
    last_row = sub_row == SUB - 1
    row_taps = (0, 1, 2) if on_grid else (1,)

    def ff_cols(fc, base):
        if isinstance(fc, int):
            return slice(base + fc * FF_CHUNK, base + (fc + 1) * FF_CHUNK)
        return pl.ds(pl.multiple_of(base + fc * FF_CHUNK, 128), FF_CHUNK)

    def up(fc, slot):
        gbufs[slot][...] = _dot(h_ref[...], wu_ref[:, ff_cols(fc, 0)])
        vbufs[slot][...] = _dot(h_ref[halo:halo + tm, :], wu_ref[:, ff_cols(fc, D_FF)])

    def gate_act(fc, slot, aslot):
        cw = cw_ref[:, ff_cols(fc, 0)]
        cb = cb_ref[:, ff_cols(fc, 0)]
        for c in range(FF_CHUNK // LANE):
            lanes = slice(c * LANE, (c + 1) * LANE)
            for s0 in range(0, te, seg):
                gate = gbufs[slot][s0:s0 + seg, lanes]
                g_l = pltpu.roll(gate, 1, axis=0)
                g_r = pltpu.roll(gate, seg - 1, axis=0)
                gl_ref[s0:s0 + SUB, lanes] = jnp.where(first_row, 0.0, g_l[0:SUB])
                gl_ref[s0 + SUB:s0 + seg, lanes] = g_l[SUB:seg]
                gr_ref[s0:s0 + seg - SUB, lanes] = g_r[0:seg - SUB]
                gr_ref[s0 + seg - SUB:s0 + seg, lanes] = jnp.where(last_row, 0.0, g_r[seg - SUB:seg])
            for r0 in range(0, tm, GRID_W):
                conv = jnp.broadcast_to(cb[:, lanes], (GRID_W, LANE))
                for kh in row_taps:
                    lo = r0 + kh * GRID_W if on_grid else r0
                    conv = conv + gl_ref[lo:lo + GRID_W, lanes] * cw[3 * kh:3 * kh + 1, lanes]
                    conv = conv + gbufs[slot][lo:lo + GRID_W, lanes] * cw[3 * kh + 1:3 * kh + 2, lanes]
                    conv = conv + gr_ref[lo:lo + GRID_W, lanes] * cw[3 * kh + 2:3 * kh + 3, lanes]
                gelu = conv * (0.5 + 0.5 * jnp.tanh(conv * (GELU_C0 + GELU_C1 * (conv * conv))))
                acts[aslot][r0:r0 + GRID_W, lanes] = (gelu * vbufs[slot][r0:r0 + GRID_W, lanes]).astype(bf16)

    def down(fc, aslot):
        return _dot(acts[aslot][...], wd_ref[fc])


    def pair(p, parity, with_down):
        fc = 2 * p
        wr, rd = 2 * parity, 2 * (1 - parity)
        if with_down:
            acc_ref[...] += down(fc - 2, rd) + down(fc - 1, rd + 1)
        up(fc + 1, 1)
        gate_act(fc, 0, wr)
        up(fc + 2, 0)
        gate_act(fc + 1, 1, wr + 1)

    n_pairs = (N_FF_CHUNKS - 1) // 2
    assert N_FF_CHUNKS == 2 * n_pairs + 1 and n_pairs % 2 == 1
    up(0, 0)
    acc_ref[...] = jnp.zeros_like(acc_ref)
    pair(0, 0, False)

    def two_pairs(i, carry):
        p = 2 * i + 1
        pair(p, 1, True)
        pair(p + 1, 0, True)
        return carry

    lax.fori_loop(0, (n_pairs - 1) // 2, two_pairs, 0)
    last = N_FF_CHUNKS - 1
    gate_act(last, 0, 2)
    ffn_out = acc_ref[...] + (down(last - 2, 0) + down(last - 1, 1) + down(last, 2))
    y_ref[0] = _rms(x_ref[0] + g2 * ffn_out, fg_ref[...])


def _ffn(x1, mod3, per_batch_mod, wts, tm, on_grid):
    B0, T0, _ = x1.shape
    if tm > T0:
        assert not per_batch_mod and not on_grid and tm % T0 == 0 and B0 % (tm // T0) == 0
        x1 = x1.reshape(B0 * T0 // tm, tm, D_MODEL)
    B, T, _ = x1.shape
    nt = T // tm
    halo = GRID_W if on_grid else 0
    te = tm + 2 * halo
    tok = pl.BlockSpec((1, tm, D_MODEL), lambda b, t: (b, t, 0))
    full = lambda a: pl.BlockSpec(a.shape, lambda b, t: (0,) * a.ndim)
    mod_map = (lambda b, t: (b, 0, 0)) if per_batch_mod else (lambda b, t: (0, 0, 0))
    in_specs = [tok]
    args = [x1]
    if on_grid:
        r = tm // GRID_W
        in_specs += [pl.BlockSpec((1, GRID_W, D_MODEL), lambda b, t: (b, jnp.maximum(t * r - 1, 0), 0)),
                     pl.BlockSpec((1, GRID_W, D_MODEL), lambda b, t: (b, jnp.minimum((t + 1) * r, T // GRID_W - 1), 0))]
        args += [x1, x1]
    in_specs += [pl.BlockSpec((1, 1, N_MOD * D_MODEL), mod_map)] + [full(w) for w in wts]
    args += [mod3] + list(wts)
    y = pl.pallas_call(
        functools.partial(_ffn_kernel, tm=tm, nt=nt, on_grid=on_grid, seq_len=T0),
        grid=(B, nt),
        in_specs=in_specs,
        out_specs=tok,
        out_shape=jax.ShapeDtypeStruct((B, T, D_MODEL), f32),
        scratch_shapes=[pltpu.VMEM((te, D_MODEL), bf16)]
        + [pltpu.VMEM((te, FF_CHUNK), f32)] * 2 + [pltpu.VMEM((tm, FF_CHUNK), f32)] * 2
        + [pltpu.VMEM((te, FF_CHUNK), f32)] * 2 + [pltpu.VMEM((tm, FF_CHUNK), bf16)] * 4
        + [pltpu.VMEM((tm, D_MODEL), f32)],
        compiler_params=pltpu.CompilerParams(dimension_semantics=("arbitrary", "arbitrary"),
                                             vmem_limit_bytes=VMEM_LIMIT),
        name="ffn_grid" if on_grid else "ffn_ctx",
    )(*args)
    return y.reshape(B0, T0, D_MODEL)


def _trunk(x, mod3, per_batch_mod, state, on_grid, w, tm_proj, tm_mix, tm_ffn, mlstm_nseq):
    q, k, v, so, glu, sga, sgb, gates = _in_proj(x, mod3, per_batch_mod, w["norm1_g"], w["in"], tm_proj)
    hn, *new_state = _mlstm(q, k, v, gates, state, mlstm_nseq, state is None)
    x1 = _mix_out(x, mod3, per_batch_mod, hn, so, glu, sga, sgb, w["mix"], tm_mix)
    y = _ffn(x1, mod3, per_batch_mod, w["ffn"], tm_ffn, on_grid)
    return y, new_state


def kernel(x_prompt, x_sample, c, state_C, state_n, state_m, c_ctx, w_ada, b_ada, norm1_g, w_in, b_in,
           mlstm_norm_g, w_proj_a, conv_dw_w, conv_dw_b, conv_ln_g, conv_ln_b, w_proj_b, w_out, norm2_g,
           w_up, ffn_dw_w, ffn_dw_b, w_down, final_norm_g):
    Bp = x_prompt.shape[0]
    Bl = x_sample.shape[0]
    l = 0
    row = lambda a: a.reshape(1, -1)

    ct = jnp.concatenate([c, c_ctx[None, :], jnp.zeros((8 - Bl - 1, D_MODEL), f32)], axis=0).T
    mod = _adaln_mod(ct, w_ada[l], row(b_ada[l]), Bl + 1)
    mod_lat = mod[0:Bl].reshape(Bl, 1, N_MOD * D_MODEL)
    mod_ctx = mod[Bl:Bl + 1].reshape(1, 1, N_MOD * D_MODEL)

    o_g = 4 * D_A
    o_u = o_g + 4 * NH_A
    nh = NH_A

    def gate_cols(a):
        g = a[..., o_g:o_u]
        pad = jnp.zeros(a.shape[:-1] + (GATE_PAD - 4 * nh,), a.dtype)
        return jnp.concatenate([g[..., 0:nh], g[..., 2 * nh:3 * nh], g[..., nh:2 * nh], g[..., 3 * nh:], pad], axis=-1)

    w_a, w_b = _repack_w_in(w_in[l], o_g, o_u)
    bi = row(b_in[l])
    w_in_parts = (w_a, bi[:, :o_g], w_b, bi[:, o_u:], gate_cols(w_in[l]), gate_cols(bi))
    w_mix = (row(mlstm_norm_g[l]), w_proj_a[l].astype(bf16),
             jnp.pad(conv_dw_w[l], ((0, 1), (0, 0))), row(conv_dw_b[l]), row(conv_ln_g[l]), row(conv_ln_b[l]),
             w_proj_b[l].astype(bf16), w_out[l].astype(bf16))
    w_ffn = (row(norm2_g[l]), w_up[l].astype(bf16),
             w_down[l].astype(bf16).reshape(N_FF_CHUNKS, FF_CHUNK, D_MODEL),
             ffn_dw_w[l].reshape(9, D_FF), row(ffn_dw_b[l]), row(final_norm_g))
    w = {"norm1_g": row(norm1_g[l]), "in": w_in_parts, "mix": w_mix, "ffn": w_ffn}

    y_prompt, (cn, nn, mn) = _trunk(x_prompt, mod_ctx, False, None, False, w, 512, 256, 512, 4)

    c0 = state_C[:, l]
    n0 = state_n[:, l].reshape(Bl, 2 * NH_A, DH_A)
    m0 = jnp.broadcast_to(state_m[:, l].reshape(Bl, 2 * NH_A, 1), (Bl, 2 * NH_A, DH_A))
    y_sample, _ = _trunk(x_sample, mod_lat, True, (c0, n0, m0), True, w, 512, 512, 512, 1)

    new_state_C = cn[:, None]
    new_state_n = nn.reshape(Bp, 1, 2, NH_A, DH_A)
    new_state_m = mn[:, :, 0].reshape(Bp, 1, 2, NH_A)
    return (y_prompt, y_sample, new_state_C, new_state_n, new_state_m)
```

```python
import functools

import jax
import jax.numpy as jnp
from jax import lax
from jax.experimental import pallas as pl
from jax.experimental.pallas import tpu as pltpu

D_MODEL = 1024
D_A = 512
NH_A = 4
DH_A = 128
CHUNK = 128
D_B = 512
CONV_W = 31
CONV_HALO = 16
CONV_ROWS = 64
D_FF = 2816
FF_CHUNK = 256
N_FF_CHUNKS = D_FF // FF_CHUNK
GRID_W = 64
N_MOD = 6
GATE_PAD = 128
EPS = 1e-6
Q_SCALE = DH_A ** -0.5
VMEM_LIMIT = 56 * 1024 * 1024
SUBLANES = 8
LANES = 128

f32 = jnp.float32
bf16 = jnp.bfloat16


def _rms(x, g):
    return x * lax.rsqrt(jnp.mean(x * x, axis=-1, keepdims=True) + EPS) * g


def _sigmoid(x):
    return 1.0 / (1.0 + jnp.exp(-x))


def _log_sigmoid(x):
    return jnp.minimum(x, 0.0) - jnp.log(1.0 + jnp.exp(-jnp.abs(x)))


GELU_C0 = 0.7978845608028654
GELU_C1 = GELU_C0 * 0.044715


def _dot(a, b):
    return jnp.dot(a, b, preferred_element_type=f32)


def _mod_kernel(ct_ref, w_ref, b_ref, o_ref, *, n_rows):
    ct = ct_ref[...]
    st = ct * _sigmoid(ct)
    w = w_ref[...]
    rows = []
    for r in range(n_rows):
        s_col = jnp.broadcast_to(st[:, r:r + 1], (D_MODEL, LANES))
        rows.append(jnp.concatenate(
            [jnp.sum(w[:, c:c + LANES] * s_col, axis=0, keepdims=True) for c in range(0, w.shape[1], LANES)], axis=1))
    rows.append(jnp.zeros((8 - n_rows, w.shape[1]), f32))
    o_ref[...] = jnp.concatenate(rows, axis=0) + b_ref[...]


def _adaln_mod(ct, w_ada, b_ada, n_rows):
    n = w_ada.shape[1]
    tn = 512
    return pl.pallas_call(
        functools.partial(_mod_kernel, n_rows=n_rows),
        grid=(n // tn,),
        in_specs=[
            pl.BlockSpec((D_MODEL, 8), lambda j: (0, 0)),
            pl.BlockSpec((D_MODEL, tn), lambda j: (0, j)),
            pl.BlockSpec((1, tn), lambda j: (0, j)),
        ],
        out_specs=pl.BlockSpec((8, tn), lambda j: (0, j)),
        out_shape=jax.ShapeDtypeStruct((8, n), f32),
        compiler_params=pltpu.CompilerParams(dimension_semantics=("arbitrary",), vmem_limit_bytes=VMEM_LIMIT),
        name="adaln_mod",
    )(ct, w_ada, b_ada)


def _inproj_kernel(x_ref, mod_ref, g_ref, wa_ref, ba_ref, wb_ref, bb_ref, wg_ref, bg_ref,
                   q_ref, k_ref, v_ref, so_ref, glu_ref, sga_ref, sgb_ref, gates_ref):
    x = x_ref[0]
    mod = mod_ref[0]
    sh1 = mod[:, 0:D_MODEL]
    sc1 = mod[:, D_MODEL:2 * D_MODEL]
    h = (_rms(x, g_ref[...]) * (1.0 + sc1) + sh1).astype(bf16)

    def proj(w_ref, b_ref, lo, n):
        return _dot(h, w_ref[:, lo:lo + n]) + b_ref[:, lo:lo + n]

    q_ref[0] = (proj(wa_ref, ba_ref, 0, D_A) * Q_SCALE).astype(bf16)
    k_ref[0] = proj(wa_ref, ba_ref, D_A, D_A).astype(bf16)
    v_ref[0] = proj(wa_ref, ba_ref, 2 * D_A, D_A).astype(bf16)
    so_ref[0] = _sigmoid(proj(wa_ref, ba_ref, 3 * D_A, D_A))
    glu_ref[0] = proj(wb_ref, bb_ref, 0, D_B) * _sigmoid(proj(wb_ref, bb_ref, D_B, D_B))
    sga_ref[0] = _sigmoid(proj(wb_ref, bb_ref, 2 * D_B, D_MODEL))
    sgb_ref[0] = _sigmoid(proj(wb_ref, bb_ref, 2 * D_B + D_MODEL, D_MODEL))
    gates_ref[0] = proj(wg_ref, bg_ref, 0, GATE_PAD)


def _in_proj(x, mod3, per_batch_mod, norm_g, wts, tm):
    B0, T0, _ = x.shape
    if tm > T0:
        assert not per_batch_mod and tm % T0 == 0 and B0 % (tm // T0) == 0
        x = x.reshape(B0 * T0 // tm, tm, D_MODEL)
    B, T, _ = x.shape
    nt = T // tm
    tok = lambda n: pl.BlockSpec((1, tm, n), lambda b, t: (b, t, 0))
    full = lambda a: pl.BlockSpec(a.shape, lambda b, t: (0,) * a.ndim)
    mod_map = (lambda b, t: (b, 0, 0)) if per_batch_mod else (lambda b, t: (0, 0, 0))
    sds = lambda n, dt: jax.ShapeDtypeStruct((B, T, n), dt)
    outs = pl.pallas_call(
        _inproj_kernel,
        grid=(B, nt),
        in_specs=[tok(D_MODEL), pl.BlockSpec((1, 1, N_MOD * D_MODEL), mod_map), full(norm_g)]
        + [full(w) for w in wts],
        out_specs=[tok(D_A), tok(D_A), tok(D_A), tok(D_A), tok(D_B), tok(D_MODEL), tok(D_MODEL), tok(GATE_PAD)],
        out_shape=[sds(D_A, bf16), sds(D_A, bf16), sds(D_A, bf16), sds(D_A, f32), sds(D_B, f32),
                   sds(D_MODEL, f32), sds(D_MODEL, f32), sds(GATE_PAD, f32)],
        compiler_params=pltpu.CompilerParams(dimension_semantics=("arbitrary", "arbitrary"),
                                             vmem_limit_bytes=VMEM_LIMIT),
        name="in_proj",
    )(x, mod3, norm_g, *wts)
    return [o.reshape(B0, T0, o.shape[-1]) for o in outs]


NEG_BIG = -1e30
ST_U, ST_INTER, ST_EMJ, ST_WKN, ST_DECAY = 0, 8, 16, 24, 32
ST_ROWS = 40


def _chunk_scan(x, op, fill, prefix, lane, width):
    k = 1
    while k < CHUNK:
        if prefix:
            shifted = jnp.where(lane >= k, pltpu.roll(x, k, axis=1), fill)
        else:
            shifted = jnp.where(lane < CHUNK - k, pltpu.roll(x, width - k, axis=1), fill)
        x = op(x, shifted)
        k *= 2
    return x


def _mlstm_kernel(*refs, nseq, nc, state_in, state_out):
    refs = list(refs)
    q_ref, k_ref, v_ref, g_ref = refs[:4]
    del refs[:4]
    if state_in:
        c0_ref, n0_ref, m0_ref = refs[:3]
        del refs[:3]
    hn_ref = refs.pop(0)
    if state_out:
        cn_ref, nn_ref, mn_ref = refs[:3]
        del refs[:3]
    st_ref, wt_ref, vt_ref, qt_ref, u_ref, cp_ref, npf_ref, npb_ref, cst_ref, nsf_ref, nsb_ref = refs
    L = CHUNK
    nct = nseq * nc
    T = nct * L
    NR = 2 * NH_A

    lane = jnp.bitwise_and(lax.broadcasted_iota(jnp.int32, (NR, T), 1), L - 1)
    is_fwd = lax.broadcasted_iota(jnp.int32, (NR, T), 0) < NH_A
    is_fwd_c = lax.broadcasted_iota(jnp.int32, (NR, L), 0) < NH_A
    i_parts, f_parts = [], []
    for c in range(nct):
        gt = g_ref[0, c * L:(c + 1) * L, :].T
        i_parts.append(gt[0:NR])
        f_parts.append(gt[NR:2 * NR])
    ig = jnp.concatenate(i_parts, axis=1)
    lf = _log_sigmoid(jnp.concatenate(f_parts, axis=1))
    scan = functools.partial(_chunk_scan, lane=lane, width=T)
    ps = scan(lf, jnp.add, 0.0, True)
    ss = scan(lf, jnp.add, 0.0, False)
    b = jnp.where(is_fwd, ps, ss)
    btot = ps + ss - lf
    w = ig - b
    cmw = jnp.where(is_fwd, scan(w, jnp.maximum, -jnp.inf, True), scan(w, jnp.maximum, -jnp.inf, False))
    wk = btot - b + ig
    a = jnp.maximum(scan(wk, jnp.maximum, -jnp.inf, True), scan(wk, jnp.maximum, -jnp.inf, False))
    chunk_lanes = lambda cg: slice(cg * L, (cg + 1) * L)
    pad_rows = jnp.zeros((L - NR, L), f32)
    for s in range(nseq):
        ms = [m0_ref[s] if state_in else jnp.zeros((NR, L), f32)]
        for t in range(nc):
            cf, cb = chunk_lanes(s * nc + t), chunk_lanes(s * nc + nc - 1 - t)
            bt = jnp.where(is_fwd_c, btot[:, cf], btot[:, cb])
            at = jnp.where(is_fwd_c, a[:, cf], a[:, cb])
            ms.append(jnp.maximum(bt + ms[-1], at))
        if state_out:
            mn_ref[s] = ms[nc]
        for c in range(nc):
            cg = s * nc + c
            sl = chunk_lanes(cg)
            m_prev = jnp.where(is_fwd_c, ms[c], ms[nc - 1 - c])
            m_new = jnp.where(is_fwd_c, ms[c + 1], ms[nc - c])
            mx = jnp.maximum(m_prev, cmw[:, sl])
            st_ref[cg, ST_U:ST_U + NR] = -mx
            st_ref[cg, ST_INTER:ST_INTER + NR] = jnp.exp(m_prev - mx)
            st_ref[cg, ST_EMJ:ST_EMJ + NR] = jnp.exp(-mx - b[:, sl])
            st_ref[cg, ST_WKN:ST_WKN + NR] = jnp.exp(wk[:, sl] - m_new)
            st_ref[cg, ST_DECAY:ST_DECAY + NR] = jnp.exp(btot[:, sl] + m_prev - m_new)
            wt_ref[cg] = jnp.concatenate([w[:, sl], pad_rows], axis=0).T

    row = lax.broadcasted_iota(jnp.int32, (L, L), 0)
    col = lax.broadcasted_iota(jnp.int32, (L, L), 1)
    masks = (row <= col, row >= col)
    first_of8 = lax.broadcasted_iota(jnp.int32, (8, L), 0) == 0

    def row_tile(x):
        return jnp.where(first_of8, x, 0.0)

    head_lanes = [slice(h * DH_A, (h + 1) * DH_A) for h in range(NH_A)]

    def increments(j, carry):
        rows = pl.ds(pl.multiple_of(j * L, L), L)
        for h in range(NH_A):
            kh = k_ref[0, rows, head_lanes[h]]
            vt = v_ref[0, rows, head_lanes[h]].astype(f32).T
            vt_ref[j, h] = vt.astype(bf16)
            qt_ref[j, h] = q_ref[0, rows, head_lanes[h]].astype(f32).T.astype(bf16)
            wf = st_ref[j, ST_WKN + h:ST_WKN + h + 1, :]
            wb = st_ref[j, ST_WKN + NH_A + h:ST_WKN + NH_A + h + 1, :]
            lhs = jnp.concatenate([vt * wf, vt * wb, row_tile(wf), row_tile(wb)], axis=0).astype(bf16)
            u_ref[j, h] = _dot(lhs, kh)
        return carry

    lax.fori_loop(0, nct, increments, 0, unroll=4)

    seq_heads = [(s, h) for s in range(nseq) for h in range(NH_A)]
    for s, h in seq_heads:
        i = s * NH_A + h
        if state_in:
            cst_ref[i, 0:DH_A] = c0_ref[s, 0, h].T
            cst_ref[i, DH_A:2 * DH_A] = c0_ref[s, 1, h].T
            nsf_ref[i] = row_tile(n0_ref[s, h:h + 1, :])
            nsb_ref[i] = row_tile(n0_ref[s, NH_A + h:NH_A + h + 1, :])
        else:
            cst_ref[i] = jnp.zeros((2 * DH_A, DH_A), f32)
            nsf_ref[i] = jnp.zeros((8, DH_A), f32)
            nsb_ref[i] = jnp.zeros((8, DH_A), f32)

    def recur(t, carry):
        for s, h in seq_heads:
            i = s * NH_A + h
            jf = s * nc + t
            jb = s * nc + nc - 1 - t
            dec_f = st_ref[jf, ST_DECAY + h:ST_DECAY + h + 1, :]
            dec_b = st_ref[jb, ST_DECAY + NH_A + h:ST_DECAY + NH_A + h + 1, :]
            c_f = cst_ref[i, 0:DH_A]
            c_b = cst_ref[i, DH_A:2 * DH_A]
            cp_ref[jf, h, 0:DH_A] = c_f.astype(bf16)
            cp_ref[jb, h, DH_A:2 * DH_A] = c_b.astype(bf16)
            cst_ref[i, 0:DH_A] = dec_f * c_f + u_ref[jf, h, 0:DH_A]
            cst_ref[i, DH_A:2 * DH_A] = dec_b * c_b + u_ref[jb, h, DH_A:2 * DH_A]
            n_f = nsf_ref[i]
            n_b = nsb_ref[i]
            npf_ref[jf, h] = n_f
            npb_ref[jb, h] = n_b
            nsf_ref[i] = dec_f * n_f + u_ref[jf, h, 2 * DH_A:2 * DH_A + 8]
            nsb_ref[i] = dec_b * n_b + u_ref[jb, h, 2 * DH_A + 8:2 * DH_A + 16]
        return carry

    lax.fori_loop(0, nc, recur, 0)
    for s, h in seq_heads if state_out else ():
        i = s * NH_A + h
        cn_ref[s, 0, h] = cst_ref[i, 0:DH_A].T
        cn_ref[s, 1, h] = cst_ref[i, DH_A:2 * DH_A].T
        nn_ref[s, h:h + 1, :] = nsf_ref[i, 0:1]
        nn_ref[s, NH_A + h:NH_A + h + 1, :] = nsb_ref[i, 0:1]

    def outputs(j, carry):
        rows = pl.ds(pl.multiple_of(j * L, L), L)
        wt = wt_ref[j]
        heads = range(NH_A)
        kqs = []
        for h in heads:
            kh = k_ref[0, rows, head_lanes[h]]
            n_rows = jnp.concatenate([npf_ref[j, h], npb_ref[j, h]], axis=0).astype(bf16)
            kqs.append(_dot(jnp.concatenate([kh, n_rows, cp_ref[j, h]], axis=0), qt_ref[j, h]))
        decays = [[jnp.exp(jnp.where(masks[d], wt[:, NH_A * d + h:NH_A * d + h + 1]
                                     + st_ref[j, ST_U + NH_A * d + h:ST_U + NH_A * d + h + 1, :], NEG_BIG))
                   for d in range(2)] for h in heads]
        s_sums, h_ts = [], []
        for h in heads:
            kq = kqs[h]
            s_sum = None
            h_t = None
            for d in range(2):
                r = NH_A * d + h
                inter = st_ref[j, ST_INTER + r:ST_INTER + r + 1, :]
                emj = st_ref[j, ST_EMJ + r:ST_EMJ + r + 1, :]
                s_t = kq[0:L] * decays[h][d]
                qn = kq[L + 8 * d:L + 8 * d + 1]
                den = inter * qn + jnp.sum(s_t, axis=0, keepdims=True)
                rr = 1.0 / jnp.maximum(jnp.abs(den), emj)
                s_sum = s_t * rr if d == 0 else s_sum + s_t * rr
                part = kq[L + 16 + d * DH_A:L + 16 + (d + 1) * DH_A] * (inter * rr)
                h_t = part if d == 0 else h_t + part
            s_sums.append(s_sum.astype(bf16))
            h_ts.append(h_t)
        h_ts = [h_ts[h] + _dot(vt_ref[j, h], s_sums[h]) for h in heads]
        h_ts = [x * lax.rsqrt(jnp.mean(x * x, axis=0, keepdims=True) + EPS) for x in h_ts]
        for h in heads:
            hn_ref[0, rows, head_lanes[h]] = h_ts[h].T
        return carry

    lax.fori_loop(0, nct, outputs, 0, unroll=4)


def _mlstm(q, k, v, gates, state, nseq, state_out):
    B, T, _ = q.shape
    nc = T // CHUNK
    nct = nseq * nc
    G = B // nseq
    fold = lambda a: a.reshape(G, nseq * T, a.shape[-1])
    seq = lambda n: pl.BlockSpec((1, nseq * T, n), lambda b: (b, 0, 0))
    st_c = pl.BlockSpec((nseq, 2, NH_A, DH_A, DH_A), lambda b: (b, 0, 0, 0, 0))
    st_v = pl.BlockSpec((nseq, 2 * NH_A, DH_A), lambda b: (b, 0, 0))
    state_specs = [st_c, st_v, st_v]
    state_shapes = [jax.ShapeDtypeStruct((B, 2, NH_A, DH_A, DH_A), f32),
                    jax.ShapeDtypeStruct((B, 2 * NH_A, DH_A), f32),
                    jax.ShapeDtypeStruct((B, 2 * NH_A, DH_A), f32)]
    outs = pl.pallas_call(
        functools.partial(_mlstm_kernel, nseq=nseq, nc=nc, state_in=state is not None, state_out=state_out),
        grid=(G,),
        in_specs=[seq(D_A), seq(D_A), seq(D_A), seq(GATE_PAD)] + (state_specs if state is not None else []),
        out_specs=[seq(D_A)] + (state_specs if state_out else []),
        scratch_shapes=[pltpu.VMEM((nct, ST_ROWS, CHUNK), f32), pltpu.VMEM((nct, CHUNK, GATE_PAD), f32),
                        pltpu.VMEM((nct, NH_A, DH_A, CHUNK), bf16), pltpu.VMEM((nct, NH_A, DH_A, CHUNK), bf16),
                        pltpu.VMEM((nct, NH_A, 2 * DH_A + 16, DH_A), f32),
                        pltpu.VMEM((nct, NH_A, 2 * DH_A, DH_A), bf16),
                        pltpu.VMEM((nct, NH_A, 8, DH_A), f32), pltpu.VMEM((nct, NH_A, 8, DH_A), f32),
                        pltpu.VMEM((nseq * NH_A, 2 * DH_A, DH_A), f32),
                        pltpu.VMEM((nseq * NH_A, 8, DH_A), f32), pltpu.VMEM((nseq * NH_A, 8, DH_A), f32)],
        out_shape=[jax.ShapeDtypeStruct((G, nseq * T, D_A), f32)] + (state_shapes if state_out else []),
        compiler_params=pltpu.CompilerParams(dimension_semantics=("arbitrary",), vmem_limit_bytes=VMEM_LIMIT),
        name="mlstm",
    )(fold(q), fold(k), fold(v), fold(gates), *(state if state is not None else ()))
    return (outs[0].reshape(B, T, D_A),) + tuple(outs[1:])


def _mix_kernel(x_ref, mod_ref, hn_ref, so_ref, glu_ref, glu_prev_ref, glu_next_ref, sga_ref, sgb_ref,
                ng_ref, wpa_ref, cw_ref, cb_ref, lng_ref, lnb_ref, wpb_ref, wout_ref,
                x1_ref, ext_ref, zs_ref, *, tm, nt, nsub):
    t = pl.program_id(1)
    mod = mod_ref[0]
    g1 = mod[:, 2 * D_MODEL:3 * D_MODEL]

    a_in = (so_ref[0] * (hn_ref[0] * ng_ref[...])).astype(bf16)
    branch_a = _dot(a_in, wpa_ref[...])

    H = CONV_HALO
    SUB, LANE = SUBLANES, LANES
    tsub = tm // nsub
    te = tsub + 2 * H
    for s in range(nsub):
        inner = nsub == 1
        ext_ref[0, s, 0:H, :] = jnp.where(t > 0, glu_prev_ref[0], 0.0) if inner else jnp.zeros((H, D_B), f32)
        ext_ref[0, s, H:H + tsub, :] = glu_ref[0, s * tsub:(s + 1) * tsub, :]
        ext_ref[0, s, H + tsub:te, :] = jnp.where(t < nt - 1, glu_next_ref[0], 0.0) if inner else jnp.zeros((H, D_B), f32)
    for s in range(nsub):
        for c in range(D_B // LANE):
            lanes = slice(c * LANE, (c + 1) * LANE)
            base = ext_ref[0, s, :, lanes]
            for k in range(1, SUB):
                ext_ref[k, s, :, lanes] = pltpu.roll(base, te - k, axis=0)
    off = H - CONV_W // 2
    for s in range(nsub):
        for r0 in range(0, tsub, CONV_ROWS):
            parts = []
            for c in range(D_B // LANE):
                lanes = slice(c * LANE, (c + 1) * LANE)
                acc = jnp.broadcast_to(cb_ref[:, lanes], (CONV_ROWS, LANE))
                for w in range(CONV_W):
                    k, a = (off + w) % SUB, (off + w) // SUB * SUB
                    acc = acc + ext_ref[k, s, r0 + a:r0 + a + CONV_ROWS, lanes] * cw_ref[w:w + 1, lanes]
                parts.append(acc)
            z = jnp.concatenate(parts, axis=1)
            mu = jnp.mean(z, axis=-1, keepdims=True)
            zc = z - mu
            var = jnp.mean(zc * zc, axis=-1, keepdims=True)
            zn = zc * lax.rsqrt(var + EPS) * lng_ref[...] + lnb_ref[...]
            zs_ref[s * tsub + r0:s * tsub + r0 + CONV_ROWS, :] = (zn * _sigmoid(zn)).astype(bf16)
    branch_b = _dot(zs_ref[...], wpb_ref[...])

    merged = sga_ref[0] * branch_a + sgb_ref[0] * branch_b
    x1_ref[0] = x_ref[0] + g1 * _dot(merged.astype(bf16), wout_ref[...])


def _mix_out(x, mod3, per_batch_mod, hn, so, glu, sga, sgb, wts, tm):
    B0, T0, _ = x.shape
    nsub = 1
    if tm > T0:
        assert not per_batch_mod and tm % T0 == 0 and B0 % (tm // T0) == 0
        nsub = tm // T0
        x, hn, so, glu, sga, sgb = (a.reshape(B0 // nsub, tm, a.shape[-1]) for a in (x, hn, so, glu, sga, sgb))
    B, T, _ = x.shape
    nt = T // tm
    H = CONV_HALO
    r = tm // H
    tok = lambda n: pl.BlockSpec((1, tm, n), lambda b, t: (b, t, 0))
    full = lambda a: pl.BlockSpec(a.shape, lambda b, t: (0,) * a.ndim)
    mod_map = (lambda b, t: (b, 0, 0)) if per_batch_mod else (lambda b, t: (0, 0, 0))
    prev = pl.BlockSpec((1, H, D_B), lambda b, t: (b, jnp.maximum(t * r - 1, 0), 0))
    nxt = pl.BlockSpec((1, H, D_B), lambda b, t: (b, jnp.minimum((t + 1) * r, T // H - 1), 0))
    x1 = pl.pallas_call(
        functools.partial(_mix_kernel, tm=tm, nt=nt, nsub=nsub),
        grid=(B, nt),
        in_specs=[tok(D_MODEL), pl.BlockSpec((1, 1, N_MOD * D_MODEL), mod_map), tok(D_A), tok(D_A),
                  tok(D_B), prev, nxt, tok(D_MODEL), tok(D_MODEL)] + [full(w) for w in wts],
        out_specs=tok(D_MODEL),
        out_shape=jax.ShapeDtypeStruct((B, T, D_MODEL), f32),
        scratch_shapes=[pltpu.VMEM((SUBLANES, nsub, tm // nsub + 2 * H, D_B), f32), pltpu.VMEM((tm, D_B), bf16)],
        compiler_params=pltpu.CompilerParams(dimension_semantics=("arbitrary", "arbitrary"),
                                             vmem_limit_bytes=VMEM_LIMIT),
        name="mix_out",
    )(x, mod3, hn, so, glu, glu, glu, sga, sgb, *wts)
    return x1.reshape(B0, T0, D_MODEL)


def _ffn_kernel(*refs, tm, nt, on_grid, seq_len):
    if on_grid:
        (x_ref, xp_ref, xn_ref, mod_ref, g2n_ref, wu_ref, wd_ref, cw_ref, cb_ref, fg_ref,
         y_ref, h_ref, g0_ref, g1_ref, v0_ref, v1_ref, gl_ref, gr_ref, a0_ref, a1_ref, a2_ref, a3_ref, acc_ref) = refs
    else:
        (x_ref, mod_ref, g2n_ref, wu_ref, wd_ref, cw_ref, cb_ref, fg_ref,
         y_ref, h_ref, g0_ref, g1_ref, v0_ref, v1_ref, gl_ref, gr_ref, a0_ref, a1_ref, a2_ref, a3_ref, acc_ref) = refs
    gbufs, vbufs, acts = (g0_ref, g1_ref), (v0_ref, v1_ref), (a0_ref, a1_ref, a2_ref, a3_ref)
    t = pl.program_id(1)
    halo = GRID_W if on_grid else 0
    te = tm + 2 * halo
    mod = mod_ref[0]
    sh2 = mod[:, 3 * D_MODEL:4 * D_MODEL]
    sc2 = mod[:, 4 * D_MODEL:5 * D_MODEL]
    g2 = mod[:, 5 * D_MODEL:6 * D_MODEL]

    def norm_mod(x):
        return _rms(x, g2n_ref[...]) * (1.0 + sc2) + sh2

    h_ref[halo:halo + tm, :] = norm_mod(x_ref[0]).astype(bf16)
    if on_grid:
        h_ref[0:halo, :] = jnp.where(t > 0, norm_mod(xp_ref[0]), 0.0).astype(bf16)
        h_ref[halo + tm:te, :] = jnp.where(t < nt - 1, norm_mod(xn_ref[0]), 0.0).astype(bf16)

    SUB, LANE = SUBLANES, LANES
    seg = GRID_W if on_grid else seq_len
    sub_row = lax.broadcasted_iota(jnp.int32, (SUB, LANE), 0)
    first_row = sub_row == 0
    last_row = sub_row == SUB - 1
    row_taps = (0, 1, 2) if on_grid else (1,)

    def ff_cols(fc, base):
        if isinstance(fc, int):
            return slice(base + fc * FF_CHUNK, base + (fc + 1) * FF_CHUNK)
        return pl.ds(pl.multiple_of(base + fc * FF_CHUNK, 128), FF_CHUNK)

    def up(fc, slot):
        gbufs[slot][...] = _dot(h_ref[...], wu_ref[:, ff_cols(fc, 0)])
        vbufs[slot][...] = _dot(h_ref[halo:halo + tm, :], wu_ref[:, ff_cols(fc, D_FF)])

    def gate_act(fc, slot, aslot):
        cw = cw_ref[:, ff_cols(fc, 0)]
        cb = cb_ref[:, ff_cols(fc, 0)]
        for c in range(FF_CHUNK // LANE):
            lanes = slice(c * LANE, (c + 1) * LANE)
            for s0 in range(0, te, seg):
                gate = gbufs[slot][s0:s0 + seg, lanes]
                g_l = pltpu.roll(gate, 1, axis=0)
                g_r = pltpu.roll(gate, seg - 1, axis=0)
                gl_ref[s0:s0 + SUB, lanes] = jnp.where(first_row, 0.0, g_l[0:SUB])
                gl_ref[s0 + SUB:s0 + seg, lanes] = g_l[SUB:seg]
                gr_ref[s0:s0 + seg - SUB, lanes] = g_r[0:seg - SUB]
                gr_ref[s0 + seg - SUB:s0 + seg, lanes] = jnp.where(last_row, 0.0, g_r[seg - SUB:seg])
            for r0 in range(0, tm, GRID_W):
                conv = jnp.broadcast_to(cb[:, lanes], (GRID_W, LANE))
                for kh in row_taps:
                    lo = r0 + kh * GRID_W if on_grid else r0
                    conv = conv + gl_ref[lo:lo + GRID_W, lanes] * cw[3 * kh:3 * kh + 1, lanes]
                    conv = conv + gbufs[slot][lo:lo + GRID_W, lanes] * cw[3 * kh + 1:3 * kh + 2, lanes]
                    conv = conv + gr_ref[lo:lo + GRID_W, lanes] * cw[3 * kh + 2:3 * kh + 3, lanes]
                gelu = conv * (0.5 + 0.5 * jnp.tanh(conv * (GELU_C0 + GELU_C1 * (conv * conv))))
                acts[aslot][r0:r0 + GRID_W, lanes] = (gelu * vbufs[slot][r0:r0 + GRID_W, lanes]).astype(bf16)

    def down(fc, aslot):
        return _dot(acts[aslot][...], wd_ref[fc])


    def pair(p, parity, with_down):
        fc = 2 * p
        wr, rd = 2 * parity, 2 * (1 - parity)
        if with_down:
            acc_ref[...] += down(fc - 2, rd) + down(fc - 1, rd + 1)
        up(fc + 1, 1)
        gate_act(fc, 0, wr)
        up(fc + 2, 0)
        gate_act(fc + 1, 1, wr + 1)

    n_pairs = (N_FF_CHUNKS - 1) // 2
    assert N_FF_CHUNKS == 2 * n_pairs + 1 and n_pairs % 2 == 1
    up(0, 0)
    acc_ref[...] = jnp.zeros_like(acc_ref)
    pair(0, 0, False)

    def two_pairs(i, carry):
        p = 2 * i + 1
        pair(p, 1, True)
        pair(p + 1, 0, True)
        return carry

    lax.fori_loop(0, (n_pairs - 1) // 2, two_pairs, 0)
    last = N_FF_CHUNKS - 1
    gate_act(last, 0, 2)
    ffn_out = acc_ref[...] + (down(last - 2, 0) + down(last - 1, 1) + down(last, 2))
    y_ref[0] = _rms(x_ref[0] + g2 * ffn_out, fg_ref[...])


def _ffn(x1, mod3, per_batch_mod, wts, tm, on_grid):
    B0, T0, _ = x1.shape
    if tm > T0:
        assert not per_batch_mod and not on_grid and tm % T0 == 0 and B0 % (tm // T0) == 0
        x1 = x1.reshape(B0 * T0 // tm, tm, D_MODEL)
    B, T, _ = x1.shape
    nt = T // tm
    halo = GRID_W if on_grid else 0
    te = tm + 2 * halo
    tok = pl.BlockSpec((1, tm, D_MODEL), lambda b, t: (b, t, 0))
    full = lambda a: pl.BlockSpec(a.shape, lambda b, t: (0,) * a.ndim)
    mod_map = (lambda b, t: (b, 0, 0)) if per_batch_mod else (lambda b, t: (0, 0, 0))
    in_specs = [tok]
    args = [x1]
    if on_grid:
        r = tm // GRID_W
        in_specs += [pl.BlockSpec((1, GRID_W, D_MODEL), lambda b, t: (b, jnp.maximum(t * r - 1, 0), 0)),
                     pl.BlockSpec((1, GRID_W, D_MODEL), lambda b, t: (b, jnp.minimum((t + 1) * r, T // GRID_W - 1), 0))]
        args += [x1, x1]
    in_specs += [pl.BlockSpec((1, 1, N_MOD * D_MODEL), mod_map)] + [full(w) for w in wts]
    args += [mod3] + list(wts)
    y = pl.pallas_call(
        functools.partial(_ffn_kernel, tm=tm, nt=nt, on_grid=on_grid, seq_len=T0),
        grid=(B, nt),
        in_specs=in_specs,
        out_specs=tok,
        out_shape=jax.ShapeDtypeStruct((B, T, D_MODEL), f32),
        scratch_shapes=[pltpu.VMEM((te, D_MODEL), bf16)]
        + [pltpu.VMEM((te, FF_CHUNK), f32)] * 2 + [pltpu.VMEM((tm, FF_CHUNK), f32)] * 2
        + [pltpu.VMEM((te, FF_CHUNK), f32)] * 2 + [pltpu.VMEM((tm, FF_CHUNK), bf16)] * 4
        + [pltpu.VMEM((tm, D_MODEL), f32)],
        compiler_params=pltpu.CompilerParams(dimension_semantics=("arbitrary", "arbitrary"),
                                             vmem_limit_bytes=VMEM_LIMIT),
        name="ffn_grid" if on_grid else "ffn_ctx",
    )(*args)
    return y.reshape(B0, T0, D_MODEL)


TOKEN_TILE = 512
MLSTM_TOKENS = 1024


def _tiles(T, per_batch_mod):
    span = TOKEN_TILE if not per_batch_mod else min(TOKEN_TILE, T)
    return span, span, span, max(1, MLSTM_TOKENS // T)


def _trunk(x, mod3, per_batch_mod, state, on_grid, w):
    tm_proj, tm_mix, tm_ffn, mlstm_nseq = _tiles(x.shape[1], per_batch_mod)
    q, k, v, so, glu, sga, sgb, gates = _in_proj(x, mod3, per_batch_mod, w["norm1_g"], w["in"], tm_proj)
    hn, *new_state = _mlstm(q, k, v, gates, state, mlstm_nseq, state is None)
    x1 = _mix_out(x, mod3, per_batch_mod, hn, so, glu, sga, sgb, w["mix"], tm_mix)
    y = _ffn(x1, mod3, per_batch_mod, w["ffn"], tm_ffn, on_grid)
    return y, new_state


def kernel(x_prompt, x_sample, c, state_C, state_n, state_m, c_ctx, w_ada, b_ada, norm1_g, w_in, b_in,
           mlstm_norm_g, w_proj_a, conv_dw_w, conv_dw_b, conv_ln_g, conv_ln_b, w_proj_b, w_out, norm2_g,
           w_up, ffn_dw_w, ffn_dw_b, w_down, final_norm_g):
    Bp = x_prompt.shape[0]
    Bl = x_sample.shape[0]
    l = 0
    row = lambda a: a.reshape(1, -1)

    ct = jnp.concatenate([c, c_ctx[None, :], jnp.zeros((8 - Bl - 1, D_MODEL), f32)], axis=0).T
    mod = _adaln_mod(ct, w_ada[l], row(b_ada[l]), Bl + 1)
    mod_lat = mod[0:Bl].reshape(Bl, 1, N_MOD * D_MODEL)
    mod_ctx = mod[Bl:Bl + 1].reshape(1, 1, N_MOD * D_MODEL)

    o_g = 4 * D_A
    o_u = o_g + 4 * NH_A
    nh = NH_A

    def split_in(a, dt):
        g = a[..., o_g:o_u]
        pad = jnp.zeros(a.shape[:-1] + (GATE_PAD - 4 * nh,), a.dtype)
        gates = jnp.concatenate([g[..., 0:nh], g[..., 2 * nh:3 * nh], g[..., nh:2 * nh], g[..., 3 * nh:], pad], axis=-1)
        return a[..., :o_g].astype(dt), a[..., o_u:].astype(dt), gates.astype(dt)

    (w_a, w_b, w_g), (b_a, b_b, b_g) = split_in(w_in[l], bf16), split_in(row(b_in[l]), f32)
    w_in_parts = (w_a, b_a, w_b, b_b, w_g, b_g)
    w_mix = (row(mlstm_norm_g[l]), w_proj_a[l].astype(bf16),
             jnp.pad(conv_dw_w[l], ((0, 1), (0, 0))), row(conv_dw_b[l]), row(conv_ln_g[l]), row(conv_ln_b[l]),
             w_proj_b[l].astype(bf16), w_out[l].astype(bf16))
    w_ffn = (row(norm2_g[l]), w_up[l].astype(bf16),
             w_down[l].astype(bf16).reshape(N_FF_CHUNKS, FF_CHUNK, D_MODEL),
             ffn_dw_w[l].reshape(9, D_FF), row(ffn_dw_b[l]), row(final_norm_g))
    w = {"norm1_g": row(norm1_g[l]), "in": w_in_parts, "mix": w_mix, "ffn": w_ffn}

    y_prompt, (cn, nn, mn) = _trunk(x_prompt, mod_ctx, False, None, False, w)

    c0 = state_C[:, l]
    n0 = state_n[:, l].reshape(Bl, 2 * NH_A, DH_A)
    m0 = jnp.broadcast_to(state_m[:, l].reshape(Bl, 2 * NH_A, 1), (Bl, 2 * NH_A, DH_A))
    y_sample, _ = _trunk(x_sample, mod_lat, True, (c0, n0, m0), True, w)

    new_state_C = cn[:, None]
    new_state_n = nn.reshape(Bp, 1, 2, NH_A, DH_A)
    new_state_m = mn[:, :, 0].reshape(Bp, 1, 2, NH_A)
    return (y_prompt, y_sample, new_state_C, new_state_n, new_state_m)
```

```python
import functools

import jax
import jax.numpy as jnp
from jax import lax
from jax.experimental import pallas as pl
from jax.experimental.pallas import tpu as pltpu

D_MODEL = 1024
D_A = 512
NH_A = 4
DH_A = 128
CHUNK = 128
D_B = 512
CONV_W = 31
CONV_HALO = 16
CONV_ROWS = 64
D_FF = 2816
FF_CHUNK = 256
N_FF_CHUNKS = D_FF // FF_CHUNK
GRID_W = 64
N_MOD = 6
GATE_PAD = 128
EPS = 1e-6
Q_SCALE = DH_A ** -0.5
VMEM_LIMIT = 56 * 1024 * 1024
SUBLANES = 8
LANES = 128

f32 = jnp.float32
bf16 = jnp.bfloat16


def _rms(x, g):
    return x * lax.rsqrt(jnp.mean(x * x, axis=-1, keepdims=True) + EPS) * g


def _sigmoid(x):
    return 1.0 / (1.0 + jnp.exp(-x))


def _log_sigmoid(x):
    return jnp.minimum(x, 0.0) - jnp.log(1.0 + jnp.exp(-jnp.abs(x)))


GELU_C0 = 0.7978845608028654
GELU_C1 = GELU_C0 * 0.044715


def _dot(a, b):
    return jnp.dot(a, b, preferred_element_type=f32)


def _mod_kernel(ct_ref, w_ref, b_ref, o_ref, *, n_rows):
    ct = ct_ref[...]
    st = ct * _sigmoid(ct)
    w = w_ref[...]
    rows = []
    for r in range(n_rows):
        s_col = jnp.broadcast_to(st[:, r:r + 1], (D_MODEL, LANES))
        rows.append(jnp.concatenate(
            [jnp.sum(w[:, c:c + LANES] * s_col, axis=0, keepdims=True) for c in range(0, w.shape[1], LANES)], axis=1))
    rows.append(jnp.zeros((8 - n_rows, w.shape[1]), f32))
    o_ref[...] = jnp.concatenate(rows, axis=0) + b_ref[...]


def _adaln_mod(ct, w_ada, b_ada, n_rows):
    n = w_ada.shape[1]
    tn = 512
    return pl.pallas_call(
        functools.partial(_mod_kernel, n_rows=n_rows),
        grid=(n // tn,),
        in_specs=[
            pl.BlockSpec((D_MODEL, 8), lambda j: (0, 0)),
            pl.BlockSpec((D_MODEL, tn), lambda j: (0, j)),
            pl.BlockSpec((1, tn), lambda j: (0, j)),
        ],
        out_specs=pl.BlockSpec((8, tn), lambda j: (0, j)),
        out_shape=jax.ShapeDtypeStruct((8, n), f32),
        compiler_params=pltpu.CompilerParams(dimension_semantics=("arbitrary",), vmem_limit_bytes=VMEM_LIMIT),
        name="adaln_mod",
    )(ct, w_ada, b_ada)


def _inproj_kernel(x_ref, mod_ref, g_ref, wa_ref, ba_ref, wb_ref, bb_ref, wg_ref, bg_ref,
                   q_ref, k_ref, v_ref, so_ref, glu_ref, sga_ref, sgb_ref, gates_ref):
    x = x_ref[0]
    mod = mod_ref[0]
    sh1 = mod[:, 0:D_MODEL]
    sc1 = mod[:, D_MODEL:2 * D_MODEL]
    h = (_rms(x, g_ref[...]) * (1.0 + sc1) + sh1).astype(bf16)

    def proj(w_ref, b_ref, lo, n):
        return _dot(h, w_ref[:, lo:lo + n]) + b_ref[:, lo:lo + n]

    q_ref[0] = (proj(wa_ref, ba_ref, 0, D_A) * Q_SCALE).astype(bf16)
    k_ref[0] = proj(wa_ref, ba_ref, D_A, D_A).astype(bf16)
    v_ref[0] = proj(wa_ref, ba_ref, 2 * D_A, D_A).astype(bf16)
    so_ref[0] = _sigmoid(proj(wa_ref, ba_ref, 3 * D_A, D_A)).astype(bf16)
    glu_ref[0] = proj(wb_ref, bb_ref, 0, D_B) * _sigmoid(proj(wb_ref, bb_ref, D_B, D_B))
    sga_ref[0] = _sigmoid(proj(wb_ref, bb_ref, 2 * D_B, D_MODEL)).astype(bf16)
    sgb_ref[0] = _sigmoid(proj(wb_ref, bb_ref, 2 * D_B + D_MODEL, D_MODEL)).astype(bf16)
    gates_ref[0] = proj(wg_ref, bg_ref, 0, GATE_PAD)


def _in_proj(x, mod3, per_batch_mod, norm_g, wts, tm):
    B0, T0, _ = x.shape
    if tm > T0:
        assert not per_batch_mod and tm % T0 == 0 and B0 % (tm // T0) == 0
        x = x.reshape(B0 * T0 // tm, tm, D_MODEL)
    B, T, _ = x.shape
    nt = T // tm
    tok = lambda n: pl.BlockSpec((1, tm, n), lambda b, t: (b, t, 0))
    full = lambda a: pl.BlockSpec(a.shape, lambda b, t: (0,) * a.ndim)
    mod_map = (lambda b, t: (b, 0, 0)) if per_batch_mod else (lambda b, t: (0, 0, 0))
    sds = lambda n, dt: jax.ShapeDtypeStruct((B, T, n), dt)
    outs = pl.pallas_call(
        _inproj_kernel,
        grid=(B, nt),
        in_specs=[tok(D_MODEL), pl.BlockSpec((1, 1, N_MOD * D_MODEL), mod_map), full(norm_g)]
        + [full(w) for w in wts],
        out_specs=[tok(D_A), tok(D_A), tok(D_A), tok(D_A), tok(D_B), tok(D_MODEL), tok(D_MODEL), tok(GATE_PAD)],
        out_shape=[sds(D_A, bf16), sds(D_A, bf16), sds(D_A, bf16), sds(D_A, bf16), sds(D_B, f32),
                   sds(D_MODEL, bf16), sds(D_MODEL, bf16), sds(GATE_PAD, f32)],
        compiler_params=pltpu.CompilerParams(dimension_semantics=("arbitrary", "arbitrary"),
                                             vmem_limit_bytes=VMEM_LIMIT),
        name="in_proj",
    )(x, mod3, norm_g, *wts)
    return [o.reshape(B0, T0, o.shape[-1]) for o in outs]


NEG_BIG = -1e30
ST_U, ST_INTER, ST_EMJ, ST_WKN, ST_DECAY = 0, 8, 16, 24, 32
ST_ROWS = 40


def _chunk_scan(x, op, fill, prefix, lane, width):
    k = 1
    while k < CHUNK:
        if prefix:
            shifted = jnp.where(lane >= k, pltpu.roll(x, k, axis=1), fill)
        else:
            shifted = jnp.where(lane < CHUNK - k, pltpu.roll(x, width - k, axis=1), fill)
        x = op(x, shifted)
        k *= 2
    return x


def _mlstm_kernel(*refs, nseq, nc, state_in, state_out):
    refs = list(refs)
    q_ref, k_ref, v_ref, g_ref = refs[:4]
    del refs[:4]
    if state_in:
        c0_ref, n0_ref, m0_ref = refs[:3]
        del refs[:3]
    hn_ref = refs.pop(0)
    if state_out:
        cn_ref, nn_ref, mn_ref = refs[:3]
        del refs[:3]
    st_ref, wt_ref, vt_ref, qt_ref, u_ref, cp_ref, npf_ref, npb_ref, cst_ref, nsf_ref, nsb_ref = refs
    L = CHUNK
    nct = nseq * nc
    T = nct * L
    NR = 2 * NH_A

    lane = jnp.bitwise_and(lax.broadcasted_iota(jnp.int32, (NR, T), 1), L - 1)
    is_fwd = lax.broadcasted_iota(jnp.int32, (NR, T), 0) < NH_A
    is_fwd_c = lax.broadcasted_iota(jnp.int32, (NR, L), 0) < NH_A
    i_parts, f_parts = [], []
    for c in range(nct):
        gt = g_ref[0, c * L:(c + 1) * L, :].T
        i_parts.append(gt[0:NR])
        f_parts.append(gt[NR:2 * NR])
    ig = jnp.concatenate(i_parts, axis=1)
    lf = _log_sigmoid(jnp.concatenate(f_parts, axis=1))
    scan = functools.partial(_chunk_scan, lane=lane, width=T)
    ps = scan(lf, jnp.add, 0.0, True)
    ss = scan(lf, jnp.add, 0.0, False)
    b = jnp.where(is_fwd, ps, ss)
    btot = ps + ss - lf
    w = ig - b
    cmw = jnp.where(is_fwd, scan(w, jnp.maximum, -jnp.inf, True), scan(w, jnp.maximum, -jnp.inf, False))
    wk = btot - b + ig
    a = jnp.maximum(scan(wk, jnp.maximum, -jnp.inf, True), scan(wk, jnp.maximum, -jnp.inf, False))
    chunk_lanes = lambda cg: slice(cg * L, (cg + 1) * L)
    pad_rows = jnp.zeros((L - NR, L), f32)
    for s in range(nseq):
        ms = [m0_ref[s] if state_in else jnp.zeros((NR, L), f32)]
        for t in range(nc):
            cf, cb = chunk_lanes(s * nc + t), chunk_lanes(s * nc + nc - 1 - t)
            bt = jnp.where(is_fwd_c, btot[:, cf], btot[:, cb])
            at = jnp.where(is_fwd_c, a[:, cf], a[:, cb])
            ms.append(jnp.maximum(bt + ms[-1], at))
        if state_out:
            mn_ref[s] = ms[nc]
        for c in range(nc):
            cg = s * nc + c
            sl = chunk_lanes(cg)
            m_prev = jnp.where(is_fwd_c, ms[c], ms[nc - 1 - c])
            m_new = jnp.where(is_fwd_c, ms[c + 1], ms[nc - c])
            mx = jnp.maximum(m_prev, cmw[:, sl])
            st_ref[cg, ST_U:ST_U + NR] = -mx
            st_ref[cg, ST_INTER:ST_INTER + NR] = jnp.exp(m_prev - mx)
            st_ref[cg, ST_EMJ:ST_EMJ + NR] = jnp.exp(-mx - b[:, sl])
            st_ref[cg, ST_WKN:ST_WKN + NR] = jnp.exp(wk[:, sl] - m_new)
            st_ref[cg, ST_DECAY:ST_DECAY + NR] = jnp.exp(btot[:, sl] + m_prev - m_new)
            wt_ref[cg] = jnp.concatenate([w[:, sl], pad_rows], axis=0).T

    row = lax.broadcasted_iota(jnp.int32, (L, L), 0)
    col = lax.broadcasted_iota(jnp.int32, (L, L), 1)
    masks = (row <= col, row >= col)
    first_of8 = lax.broadcasted_iota(jnp.int32, (8, L), 0) == 0

    def row_tile(x):
        return jnp.where(first_of8, x, 0.0)

    head_lanes = [slice(h * DH_A, (h + 1) * DH_A) for h in range(NH_A)]

    def increments(j, carry):
        rows = pl.ds(pl.multiple_of(j * L, L), L)
        for h in range(NH_A):
            kh = k_ref[0, rows, head_lanes[h]]
            vt = v_ref[0, rows, head_lanes[h]].astype(f32).T
            vt_ref[j, h] = vt.astype(bf16)
            qt_ref[j, h] = q_ref[0, rows, head_lanes[h]].astype(f32).T.astype(bf16)
            wf = st_ref[j, ST_WKN + h:ST_WKN + h + 1, :]
            wb = st_ref[j, ST_WKN + NH_A + h:ST_WKN + NH_A + h + 1, :]
            lhs = jnp.concatenate([vt * wf, vt * wb, row_tile(wf), row_tile(wb)], axis=0).astype(bf16)
            u_ref[j, h] = _dot(lhs, kh)
        return carry

    lax.fori_loop(0, nct, increments, 0, unroll=4)

    seq_heads = [(s, h) for s in range(nseq) for h in range(NH_A)]
    for s, h in seq_heads:
        i = s * NH_A + h
        if state_in:
            cst_ref[i, 0:DH_A] = c0_ref[s, 0, h].T
            cst_ref[i, DH_A:2 * DH_A] = c0_ref[s, 1, h].T
            nsf_ref[i] = row_tile(n0_ref[s, h:h + 1, :])
            nsb_ref[i] = row_tile(n0_ref[s, NH_A + h:NH_A + h + 1, :])
        else:
            cst_ref[i] = jnp.zeros((2 * DH_A, DH_A), f32)
            nsf_ref[i] = jnp.zeros((8, DH_A), f32)
            nsb_ref[i] = jnp.zeros((8, DH_A), f32)

    def recur(t, carry):
        for s, h in seq_heads:
            i = s * NH_A + h
            jf = s * nc + t
            jb = s * nc + nc - 1 - t
            dec_f = st_ref[jf, ST_DECAY + h:ST_DECAY + h + 1, :]
            dec_b = st_ref[jb, ST_DECAY + NH_A + h:ST_DECAY + NH_A + h + 1, :]
            c_f = cst_ref[i, 0:DH_A]
            c_b = cst_ref[i, DH_A:2 * DH_A]
            cp_ref[jf, h, 0:DH_A] = c_f.astype(bf16)
            cp_ref[jb, h, DH_A:2 * DH_A] = c_b.astype(bf16)
            cst_ref[i, 0:DH_A] = dec_f * c_f + u_ref[jf, h, 0:DH_A]
            cst_ref[i, DH_A:2 * DH_A] = dec_b * c_b + u_ref[jb, h, DH_A:2 * DH_A]
            n_f = nsf_ref[i]
            n_b = nsb_ref[i]
            npf_ref[jf, h] = n_f
            npb_ref[jb, h] = n_b
            nsf_ref[i] = dec_f * n_f + u_ref[jf, h, 2 * DH_A:2 * DH_A + 8]
            nsb_ref[i] = dec_b * n_b + u_ref[jb, h, 2 * DH_A + 8:2 * DH_A + 16]
        return carry

    lax.fori_loop(0, nc, recur, 0)
    for s, h in seq_heads if state_out else ():
        i = s * NH_A + h
        cn_ref[s, 0, h] = cst_ref[i, 0:DH_A].T
        cn_ref[s, 1, h] = cst_ref[i, DH_A:2 * DH_A].T
        nn_ref[s, h:h + 1, :] = nsf_ref[i, 0:1]
        nn_ref[s, NH_A + h:NH_A + h + 1, :] = nsb_ref[i, 0:1]

    def outputs(j, carry):
        rows = pl.ds(pl.multiple_of(j * L, L), L)
        wt = wt_ref[j]
        heads = range(NH_A)
        kqs = []
        for h in heads:
            kh = k_ref[0, rows, head_lanes[h]]
            n_rows = jnp.concatenate([npf_ref[j, h], npb_ref[j, h]], axis=0).astype(bf16)
            kqs.append(_dot(jnp.concatenate([kh, n_rows, cp_ref[j, h]], axis=0), qt_ref[j, h]))
        decays = [[jnp.exp(jnp.where(masks[d], wt[:, NH_A * d + h:NH_A * d + h + 1]
                                     + st_ref[j, ST_U + NH_A * d + h:ST_U + NH_A * d + h + 1, :], NEG_BIG))
                   for d in range(2)] for h in heads]
        s_sums, h_ts = [], []
        for h in heads:
            kq = kqs[h]
            s_sum = None
            h_t = None
            for d in range(2):
                r = NH_A * d + h
                inter = st_ref[j, ST_INTER + r:ST_INTER + r + 1, :]
                emj = st_ref[j, ST_EMJ + r:ST_EMJ + r + 1, :]
                s_t = kq[0:L] * decays[h][d]
                qn = kq[L + 8 * d:L + 8 * d + 1]
                den = inter * qn + jnp.sum(s_t, axis=0, keepdims=True)
                rr = 1.0 / jnp.maximum(jnp.abs(den), emj)
                s_sum = s_t * rr if d == 0 else s_sum + s_t * rr
                part = kq[L + 16 + d * DH_A:L + 16 + (d + 1) * DH_A] * (inter * rr)
                h_t = part if d == 0 else h_t + part
            s_sums.append(s_sum.astype(bf16))
            h_ts.append(h_t)
        h_ts = [h_ts[h] + _dot(vt_ref[j, h], s_sums[h]) for h in heads]
        h_ts = [x * lax.rsqrt(jnp.mean(x * x, axis=0, keepdims=True) + EPS) for x in h_ts]
        for h in heads:
            hn_ref[0, rows, head_lanes[h]] = h_ts[h].T.astype(bf16)
        return carry

    lax.fori_loop(0, nct, outputs, 0, unroll=4)


def _mlstm(q, k, v, gates, state, nseq, state_out):
    B, T, _ = q.shape
    nc = T // CHUNK
    nct = nseq * nc
    G = B // nseq
    fold = lambda a: a.reshape(G, nseq * T, a.shape[-1])
    seq = lambda n: pl.BlockSpec((1, nseq * T, n), lambda b: (b, 0, 0))
    st_c = pl.BlockSpec((nseq, 2, NH_A, DH_A, DH_A), lambda b: (b, 0, 0, 0, 0))
    st_v = pl.BlockSpec((nseq, 2 * NH_A, DH_A), lambda b: (b, 0, 0))
    state_specs = [st_c, st_v, st_v]
    state_shapes = [jax.ShapeDtypeStruct((B, 2, NH_A, DH_A, DH_A), f32),
                    jax.ShapeDtypeStruct((B, 2 * NH_A, DH_A), f32),
                    jax.ShapeDtypeStruct((B, 2 * NH_A, DH_A), f32)]
    outs = pl.pallas_call(
        functools.partial(_mlstm_kernel, nseq=nseq, nc=nc, state_in=state is not None, state_out=state_out),
        grid=(G,),
        in_specs=[seq(D_A), seq(D_A), seq(D_A), seq(GATE_PAD)] + (state_specs if state is not None else []),
        out_specs=[seq(D_A)] + (state_specs if state_out else []),
        scratch_shapes=[pltpu.VMEM((nct, ST_ROWS, CHUNK), f32), pltpu.VMEM((nct, CHUNK, GATE_PAD), f32),
                        pltpu.VMEM((nct, NH_A, DH_A, CHUNK), bf16), pltpu.VMEM((nct, NH_A, DH_A, CHUNK), bf16),
                        pltpu.VMEM((nct, NH_A, 2 * DH_A + 16, DH_A), f32),
                        pltpu.VMEM((nct, NH_A, 2 * DH_A, DH_A), bf16),
                        pltpu.VMEM((nct, NH_A, 8, DH_A), f32), pltpu.VMEM((nct, NH_A, 8, DH_A), f32),
                        pltpu.VMEM((nseq * NH_A, 2 * DH_A, DH_A), f32),
                        pltpu.VMEM((nseq * NH_A, 8, DH_A), f32), pltpu.VMEM((nseq * NH_A, 8, DH_A), f32)],
        out_shape=[jax.ShapeDtypeStruct((G, nseq * T, D_A), bf16)] + (state_shapes if state_out else []),
        compiler_params=pltpu.CompilerParams(dimension_semantics=("arbitrary",), vmem_limit_bytes=VMEM_LIMIT),
        name="mlstm",
    )(fold(q), fold(k), fold(v), fold(gates), *(state if state is not None else ()))
    return (outs[0].reshape(B, T, D_A),) + tuple(outs[1:])


def _mix_kernel(x_ref, mod_ref, hn_ref, so_ref, glu_ref, glu_prev_ref, glu_next_ref, sga_ref, sgb_ref,
                ng_ref, wpa_ref, cw_ref, cb_ref, lng_ref, lnb_ref, wpb_ref, wout_ref,
                x1_ref, ext_ref, zs_ref, *, tm, nt, nsub):
    t = pl.program_id(1)
    mod = mod_ref[0]
    g1 = mod[:, 2 * D_MODEL:3 * D_MODEL]

    a_in = (so_ref[0].astype(f32) * (hn_ref[0].astype(f32) * ng_ref[...])).astype(bf16)
    branch_a = _dot(a_in, wpa_ref[...])

    H = CONV_HALO
    SUB, LANE = SUBLANES, LANES
    tsub = tm // nsub
    te = tsub + 2 * H
    for s in range(nsub):
        inner = nsub == 1
        ext_ref[0, s, 0:H, :] = jnp.where(t > 0, glu_prev_ref[0], 0.0) if inner else jnp.zeros((H, D_B), f32)
        ext_ref[0, s, H:H + tsub, :] = glu_ref[0, s * tsub:(s + 1) * tsub, :]
        ext_ref[0, s, H + tsub:te, :] = jnp.where(t < nt - 1, glu_next_ref[0], 0.0) if inner else jnp.zeros((H, D_B), f32)
    for s in range(nsub):
        for c in range(D_B // LANE):
            lanes = slice(c * LANE, (c + 1) * LANE)
            base = ext_ref[0, s, :, lanes]
            for k in range(1, SUB):
                ext_ref[k, s, :, lanes] = pltpu.roll(base, te - k, axis=0)
    off = H - CONV_W // 2
    for s in range(nsub):
        for r0 in range(0, tsub, CONV_ROWS):
            parts = []
            for c in range(D_B // LANE):
                lanes = slice(c * LANE, (c + 1) * LANE)
                acc = jnp.broadcast_to(cb_ref[:, lanes], (CONV_ROWS, LANE))
                for w in range(CONV_W):
                    k, a = (off + w) % SUB, (off + w) // SUB * SUB
                    acc = acc + ext_ref[k, s, r0 + a:r0 + a + CONV_ROWS, lanes] * cw_ref[w:w + 1, lanes]
                parts.append(acc)
            z = jnp.concatenate(parts, axis=1)
            mu = jnp.mean(z, axis=-1, keepdims=True)
            zc = z - mu
            var = jnp.mean(zc * zc, axis=-1, keepdims=True)
            zn = zc * lax.rsqrt(var + EPS) * lng_ref[...] + lnb_ref[...]
            zs_ref[s * tsub + r0:s * tsub + r0 + CONV_ROWS, :] = (zn * _sigmoid(zn)).astype(bf16)
    branch_b = _dot(zs_ref[...], wpb_ref[...])

    merged = sga_ref[0].astype(f32) * branch_a + sgb_ref[0].astype(f32) * branch_b
    x1_ref[0] = x_ref[0] + g1 * _dot(merged.astype(bf16), wout_ref[...])


def _mix_out(x, mod3, per_batch_mod, hn, so, glu, sga, sgb, wts, tm):
    B0, T0, _ = x.shape
    nsub = 1
    if tm > T0:
        assert not per_batch_mod and tm % T0 == 0 and B0 % (tm // T0) == 0
        nsub = tm // T0
        x, hn, so, glu, sga, sgb = (a.reshape(B0 // nsub, tm, a.shape[-1]) for a in (x, hn, so, glu, sga, sgb))
    B, T, _ = x.shape
    nt = T // tm
    H = CONV_HALO
    r = tm // H
    tok = lambda n: pl.BlockSpec((1, tm, n), lambda b, t: (b, t, 0))
    full = lambda a: pl.BlockSpec(a.shape, lambda b, t: (0,) * a.ndim)
    mod_map = (lambda b, t: (b, 0, 0)) if per_batch_mod else (lambda b, t: (0, 0, 0))
    prev = pl.BlockSpec((1, H, D_B), lambda b, t: (b, jnp.maximum(t * r - 1, 0), 0))
    nxt = pl.BlockSpec((1, H, D_B), lambda b, t: (b, jnp.minimum((t + 1) * r, T // H - 1), 0))
    x1 = pl.pallas_call(
        functools.partial(_mix_kernel, tm=tm, nt=nt, nsub=nsub),
        grid=(B, nt),
        in_specs=[tok(D_MODEL), pl.BlockSpec((1, 1, N_MOD * D_MODEL), mod_map), tok(D_A), tok(D_A),
                  tok(D_B), prev, nxt, tok(D_MODEL), tok(D_MODEL)] + [full(w) for w in wts],
        out_specs=tok(D_MODEL),
        out_shape=jax.ShapeDtypeStruct((B, T, D_MODEL), f32),
        scratch_shapes=[pltpu.VMEM((SUBLANES, nsub, tm // nsub + 2 * H, D_B), f32), pltpu.VMEM((tm, D_B), bf16)],
        compiler_params=pltpu.CompilerParams(dimension_semantics=("arbitrary", "arbitrary"),
                                             vmem_limit_bytes=VMEM_LIMIT),
        name="mix_out",
    )(x, mod3, hn, so, glu, glu, glu, sga, sgb, *wts)
    return x1.reshape(B0, T0, D_MODEL)


def _ffn_kernel(*refs, tm, nt, on_grid, seq_len):
    if on_grid:
        (x_ref, xp_ref, xn_ref, mod_ref, g2n_ref, wu_ref, wd_ref, cw_ref, cb_ref, fg_ref,
         y_ref, h_ref, g0_ref, g1_ref, v0_ref, v1_ref, gl_ref, gr_ref, a0_ref, a1_ref, a2_ref, a3_ref, acc_ref) = refs
    else:
        (x_ref, mod_ref, g2n_ref, wu_ref, wd_ref, cw_ref, cb_ref, fg_ref,
         y_ref, h_ref, g0_ref, g1_ref, v0_ref, v1_ref, gl_ref, gr_ref, a0_ref, a1_ref, a2_ref, a3_ref, acc_ref) = refs
    gbufs, vbufs, acts = (g0_ref, g1_ref), (v0_ref, v1_ref), (a0_ref, a1_ref, a2_ref, a3_ref)
    t = pl.program_id(1)
    halo = GRID_W if on_grid else 0
    te = tm + 2 * halo
    mod = mod_ref[0]
    sh2 = mod[:, 3 * D_MODEL:4 * D_MODEL]
    sc2 = mod[:, 4 * D_MODEL:5 * D_MODEL]
    g2 = mod[:, 5 * D_MODEL:6 * D_MODEL]

    def norm_mod(x):
        return _rms(x, g2n_ref[...]) * (1.0 + sc2) + sh2

    h_ref[halo:halo + tm, :] = norm_mod(x_ref[0]).astype(bf16)
    if on_grid:
        h_ref[0:halo, :] = jnp.where(t > 0, norm_mod(xp_ref[0]), 0.0).astype(bf16)
        h_ref[halo + tm:te, :] = jnp.where(t < nt - 1, norm_mod(xn_ref[0]), 0.0).astype(bf16)

    SUB, LANE = SUBLANES, LANES
    seg = GRID_W if on_grid else seq_len
    sub_row = lax.broadcasted_iota(jnp.int32, (SUB, LANE), 0)
    first_row = sub_row == 0
    last_row = sub_row == SUB - 1
    row_taps = (0, 1, 2) if on_grid else (1,)

    def ff_cols(fc, base):
        if isinstance(fc, int):
            return slice(base + fc * FF_CHUNK, base + (fc + 1) * FF_CHUNK)
        return pl.ds(pl.multiple_of(base + fc * FF_CHUNK, 128), FF_CHUNK)

    def up(fc, slot):
        gbufs[slot][...] = _dot(h_ref[...], wu_ref[:, ff_cols(fc, 0)])
        vbufs[slot][...] = _dot(h_ref[halo:halo + tm, :], wu_ref[:, ff_cols(fc, D_FF)])

    def gate_act(fc, slot, aslot):
        cw = cw_ref[:, ff_cols(fc, 0)]
        cb = cb_ref[:, ff_cols(fc, 0)]
        for c in range(FF_CHUNK // LANE):
            lanes = slice(c * LANE, (c + 1) * LANE)
            for s0 in range(0, te, seg):
                gate = gbufs[slot][s0:s0 + seg, lanes]
                g_l = pltpu.roll(gate, 1, axis=0)
                g_r = pltpu.roll(gate, seg - 1, axis=0)
                gl_ref[s0:s0 + SUB, lanes] = jnp.where(first_row, 0.0, g_l[0:SUB])
                gl_ref[s0 + SUB:s0 + seg, lanes] = g_l[SUB:seg]
                gr_ref[s0:s0 + seg - SUB, lanes] = g_r[0:seg - SUB]
                gr_ref[s0 + seg - SUB:s0 + seg, lanes] = jnp.where(last_row, 0.0, g_r[seg - SUB:seg])
            for r0 in range(0, tm, GRID_W):
                conv = jnp.broadcast_to(cb[:, lanes], (GRID_W, LANE))
                for kh in row_taps:
                    lo = r0 + kh * GRID_W if on_grid else r0
                    conv = conv + gl_ref[lo:lo + GRID_W, lanes] * cw[3 * kh:3 * kh + 1, lanes]
                    conv = conv + gbufs[slot][lo:lo + GRID_W, lanes] * cw[3 * kh + 1:3 * kh + 2, lanes]
                    conv = conv + gr_ref[lo:lo + GRID_W, lanes] * cw[3 * kh + 2:3 * kh + 3, lanes]
                gelu = conv * (0.5 + 0.5 * jnp.tanh(conv * (GELU_C0 + GELU_C1 * (conv * conv))))
                acts[aslot][r0:r0 + GRID_W, lanes] = (gelu * vbufs[slot][r0:r0 + GRID_W, lanes]).astype(bf16)

    def down(fc, aslot):
        return _dot(acts[aslot][...], wd_ref[fc])


    def pair(p, parity, with_down):
        fc = 2 * p
        wr, rd = 2 * parity, 2 * (1 - parity)
        if with_down:
            acc_ref[...] += down(fc - 2, rd) + down(fc - 1, rd + 1)
        up(fc + 1, 1)
        gate_act(fc, 0, wr)
        up(fc + 2, 0)
        gate_act(fc + 1, 1, wr + 1)

    n_pairs = (N_FF_CHUNKS - 1) // 2
    assert N_FF_CHUNKS == 2 * n_pairs + 1 and n_pairs % 2 == 1
    up(0, 0)
    acc_ref[...] = jnp.zeros_like(acc_ref)
    pair(0, 0, False)

    def two_pairs(i, carry):
        p = 2 * i + 1
        pair(p, 1, True)
        pair(p + 1, 0, True)
        return carry

    lax.fori_loop(0, (n_pairs - 1) // 2, two_pairs, 0)
    last = N_FF_CHUNKS - 1
    gate_act(last, 0, 2)
    ffn_out = acc_ref[...] + (down(last - 2, 0) + down(last - 1, 1) + down(last, 2))
    y_ref[0] = _rms(x_ref[0] + g2 * ffn_out, fg_ref[...])


def _ffn(x1, mod3, per_batch_mod, wts, tm, on_grid):
    B0, T0, _ = x1.shape
    if tm > T0:
        assert not per_batch_mod and not on_grid and tm % T0 == 0 and B0 % (tm // T0) == 0
        x1 = x1.reshape(B0 * T0 // tm, tm, D_MODEL)
    B, T, _ = x1.shape
    nt = T // tm
    halo = GRID_W if on_grid else 0
    te = tm + 2 * halo
    tok = pl.BlockSpec((1, tm, D_MODEL), lambda b, t: (b, t, 0))
    full = lambda a: pl.BlockSpec(a.shape, lambda b, t: (0,) * a.ndim)
    mod_map = (lambda b, t: (b, 0, 0)) if per_batch_mod else (lambda b, t: (0, 0, 0))
    in_specs = [tok]
    args = [x1]
    if on_grid:
        r = tm // GRID_W
        in_specs += [pl.BlockSpec((1, GRID_W, D_MODEL), lambda b, t: (b, jnp.maximum(t * r - 1, 0), 0)),
                     pl.BlockSpec((1, GRID_W, D_MODEL), lambda b, t: (b, jnp.minimum((t + 1) * r, T // GRID_W - 1), 0))]
        args += [x1, x1]
    in_specs += [pl.BlockSpec((1, 1, N_MOD * D_MODEL), mod_map)] + [full(w) for w in wts]
    args += [mod3] + list(wts)
    y = pl.pallas_call(
        functools.partial(_ffn_kernel, tm=tm, nt=nt, on_grid=on_grid, seq_len=T0),
        grid=(B, nt),
        in_specs=in_specs,
        out_specs=tok,
        out_shape=jax.ShapeDtypeStruct((B, T, D_MODEL), f32),
        scratch_shapes=[pltpu.VMEM((te, D_MODEL), bf16)]
        + [pltpu.VMEM((te, FF_CHUNK), f32)] * 2 + [pltpu.VMEM((tm, FF_CHUNK), f32)] * 2
        + [pltpu.VMEM((te, FF_CHUNK), f32)] * 2 + [pltpu.VMEM((tm, FF_CHUNK), bf16)] * 4
        + [pltpu.VMEM((tm, D_MODEL), f32)],
        compiler_params=pltpu.CompilerParams(dimension_semantics=("arbitrary", "arbitrary"),
                                             vmem_limit_bytes=VMEM_LIMIT),
        name="ffn_grid" if on_grid else "ffn_ctx",
    )(*args)
    return y.reshape(B0, T0, D_MODEL)


TOKEN_TILE = 512
MLSTM_TOKENS = 1024


def _tiles(T, per_batch_mod):
    span = TOKEN_TILE if not per_batch_mod else min(TOKEN_TILE, T)
    return span, span, span, max(1, MLSTM_TOKENS // T)


def _trunk(x, mod3, per_batch_mod, state, on_grid, w):
    tm_proj, tm_mix, tm_ffn, mlstm_nseq = _tiles(x.shape[1], per_batch_mod)
    q, k, v, so, glu, sga, sgb, gates = _in_proj(x, mod3, per_batch_mod, w["norm1_g"], w["in"], tm_proj)
    hn, *new_state = _mlstm(q, k, v, gates, state, mlstm_nseq, state is None)
    x1 = _mix_out(x, mod3, per_batch_mod, hn, so, glu, sga, sgb, w["mix"], tm_mix)
    y = _ffn(x1, mod3, per_batch_mod, w["ffn"], tm_ffn, on_grid)
    return y, new_state


def kernel(x_prompt, x_sample, c, state_C, state_n, state_m, c_ctx, w_ada, b_ada, norm1_g, w_in, b_in,
           mlstm_norm_g, w_proj_a, conv_dw_w, conv_dw_b, conv_ln_g, conv_ln_b, w_proj_b, w_out, norm2_g,
           w_up, ffn_dw_w, ffn_dw_b, w_down, final_norm_g):
    Bp = x_prompt.shape[0]
    Bl = x_sample.shape[0]
    l = 0
    row = lambda a: a.reshape(1, -1)

    ct = jnp.concatenate([c, c_ctx[None, :], jnp.zeros((8 - Bl - 1, D_MODEL), f32)], axis=0).T
    mod = _adaln_mod(ct, w_ada[l], row(b_ada[l]), Bl + 1)
    mod_lat = mod[0:Bl].reshape(Bl, 1, N_MOD * D_MODEL)
    mod_ctx = mod[Bl:Bl + 1].reshape(1, 1, N_MOD * D_MODEL)

    o_g = 4 * D_A
    o_u = o_g + 4 * NH_A
    nh = NH_A

    def split_in(a, dt):
        g = a[..., o_g:o_u]
        pad = jnp.zeros(a.shape[:-1] + (GATE_PAD - 4 * nh,), a.dtype)
        gates = jnp.concatenate([g[..., 0:nh], g[..., 2 * nh:3 * nh], g[..., nh:2 * nh], g[..., 3 * nh:], pad], axis=-1)
        return a[..., :o_g].astype(dt), a[..., o_u:].astype(dt), gates.astype(dt)

    (w_a, w_b, w_g), (b_a, b_b, b_g) = split_in(w_in[l], bf16), split_in(row(b_in[l]), f32)
    w_in_parts = (w_a, b_a, w_b, b_b, w_g, b_g)
    w_mix = (row(mlstm_norm_g[l]), w_proj_a[l].astype(bf16),
             jnp.pad(conv_dw_w[l], ((0, 1), (0, 0))), row(conv_dw_b[l]), row(conv_ln_g[l]), row(conv_ln_b[l]),
             w_proj_b[l].astype(bf16), w_out[l].astype(bf16))
    w_ffn = (row(norm2_g[l]), w_up[l].astype(bf16),
             w_down[l].astype(bf16).reshape(N_FF_CHUNKS, FF_CHUNK, D_MODEL),
             ffn_dw_w[l].reshape(9, D_FF), row(ffn_dw_b[l]), row(final_norm_g))
    w = {"norm1_g": row(norm1_g[l]), "in": w_in_parts, "mix": w_mix, "ffn": w_ffn}

    y_prompt, (cn, nn, mn) = _trunk(x_prompt, mod_ctx, False, None, False, w)

    c0 = state_C[:, l]
    n0 = state_n[:, l].reshape(Bl, 2 * NH_A, DH_A)
    m0 = jnp.broadcast_to(state_m[:, l].reshape(Bl, 2 * NH_A, 1), (Bl, 2 * NH_A, DH_A))
    y_sample, _ = _trunk(x_sample, mod_lat, True, (c0, n0, m0), True, w)

    new_state_C = cn[:, None]
    new_state_n = nn.reshape(Bp, 1, 2, NH_A, DH_A)
    new_state_m = mn[:, :, 0].reshape(Bp, 1, 2, NH_A)
    return (y_prompt, y_sample, new_state_C, new_state_n, new_state_m)
```

```python
import functools

import jax
import jax.numpy as jnp
from jax import lax
from jax.experimental import pallas as pl
from jax.experimental.pallas import tpu as pltpu

D_MODEL = 1024
D_A = 512
NH_A = 4
DH_A = 128
CHUNK = 128
D_B = 512
CONV_W = 31
CONV_HALO = 16
CONV_ROWS = 64
D_FF = 2816
FF_CHUNK = 256
N_FF_CHUNKS = D_FF // FF_CHUNK
GRID_W = 64
N_MOD = 6
GATE_PAD = 128
EPS = 1e-6
Q_SCALE = DH_A ** -0.5
VMEM_LIMIT = 56 * 1024 * 1024
SUBLANES = 8
LANES = 128

f32 = jnp.float32
bf16 = jnp.bfloat16


def _rms(x, g):
    return x * lax.rsqrt(jnp.mean(x * x, axis=-1, keepdims=True) + EPS) * g


def _sigmoid(x):
    return 1.0 / (1.0 + jnp.exp(-x))


def _log_sigmoid(x):
    return jnp.minimum(x, 0.0) - jnp.log(1.0 + jnp.exp(-jnp.abs(x)))


GELU_C0 = 0.7978845608028654
GELU_C1 = GELU_C0 * 0.044715


def _dot(a, b):
    return jnp.dot(a, b, preferred_element_type=f32)


def _mod_kernel(*refs, n_rows, n_cast):
    ct_ref, w_ref, b_ref = refs[:3]
    src_refs = refs[3:3 + n_cast]
    o_ref = refs[3 + n_cast]
    dst_refs = refs[4 + n_cast:]
    ct = ct_ref[...]
    st = ct * _sigmoid(ct)
    w = w_ref[...]
    rows = []
    for r in range(n_rows):
        s_col = jnp.broadcast_to(st[:, r:r + 1], (D_MODEL, LANES))
        rows.append(jnp.concatenate(
            [jnp.sum(w[:, c:c + LANES] * s_col, axis=0, keepdims=True) for c in range(0, w.shape[1], LANES)], axis=1))
    rows.append(jnp.zeros((8 - n_rows, w.shape[1]), f32))
    o_ref[...] = jnp.concatenate(rows, axis=0) + b_ref[...]
    for src, dst in zip(src_refs, dst_refs):
        dst[...] = src[...].astype(bf16)


def _adaln_mod(ct, w_ada, b_ada, n_rows, cast_weights):
    n = w_ada.shape[1]
    tn = 512
    steps = n // tn

    def walk(a, axis, block):
        nblk = a.shape[axis] // block
        assert nblk * block == a.shape[axis] and nblk <= steps
        shape = tuple(block if d == axis else a.shape[d] for d in range(a.ndim))
        return pl.BlockSpec(shape, lambda j: tuple(jnp.minimum(j, nblk - 1) if d == axis else 0 for d in range(a.ndim)))

    cast_specs = [walk(*cw) for cw in cast_weights]
    outs = pl.pallas_call(
        functools.partial(_mod_kernel, n_rows=n_rows, n_cast=len(cast_weights)),
        grid=(steps,),
        in_specs=[
            pl.BlockSpec((D_MODEL, 8), lambda j: (0, 0)),
            pl.BlockSpec((D_MODEL, tn), lambda j: (0, j)),
            pl.BlockSpec((1, tn), lambda j: (0, j)),
        ] + cast_specs,
        out_specs=[pl.BlockSpec((8, tn), lambda j: (0, j))] + cast_specs,
        out_shape=[jax.ShapeDtypeStruct((8, n), f32)]
        + [jax.ShapeDtypeStruct(cw[0].shape, bf16) for cw in cast_weights],
        compiler_params=pltpu.CompilerParams(dimension_semantics=("arbitrary",), vmem_limit_bytes=VMEM_LIMIT),
        name="adaln_mod",
    )(ct, w_ada, b_ada, *[cw[0] for cw in cast_weights])
    return outs[0], outs[1:]


def _inproj_kernel(x_ref, mod_ref, g_ref, wa_ref, ba_ref, wb_ref, bb_ref, wg_ref, bg_ref,
                   q_ref, k_ref, v_ref, so_ref, glu_ref, sga_ref, sgb_ref, gates_ref):
    x = x_ref[0]
    mod = mod_ref[0]
    sh1 = mod[:, 0:D_MODEL]
    sc1 = mod[:, D_MODEL:2 * D_MODEL]
    h = (_rms(x, g_ref[...]) * (1.0 + sc1) + sh1).astype(bf16)

    def proj(w_ref, b_ref, lo, n):
        return _dot(h, w_ref[:, lo:lo + n]) + b_ref[:, lo:lo + n]

    q_ref[0] = (proj(wa_ref, ba_ref, 0, D_A) * Q_SCALE).astype(bf16)
    k_ref[0] = proj(wa_ref, ba_ref, D_A, D_A).astype(bf16)
    v_ref[0] = proj(wa_ref, ba_ref, 2 * D_A, D_A).astype(bf16)
    so_ref[0] = _sigmoid(proj(wa_ref, ba_ref, 3 * D_A, D_A)).astype(bf16)
    glu_ref[0] = proj(wb_ref, bb_ref, 0, D_B) * _sigmoid(proj(wb_ref, bb_ref, D_B, D_B))
    sga_ref[0] = _sigmoid(proj(wb_ref, bb_ref, 2 * D_B, D_MODEL)).astype(bf16)
    sgb_ref[0] = _sigmoid(proj(wb_ref, bb_ref, 2 * D_B + D_MODEL, D_MODEL)).astype(bf16)
    gates_ref[0] = proj(wg_ref, bg_ref, 0, GATE_PAD)


def _in_proj(x, mod3, per_batch_mod, norm_g, wts, tm):
    B0, T0, _ = x.shape
    if tm > T0:
        assert not per_batch_mod and tm % T0 == 0 and B0 % (tm // T0) == 0
        x = x.reshape(B0 * T0 // tm, tm, D_MODEL)
    B, T, _ = x.shape
    nt = T // tm
    tok = lambda n: pl.BlockSpec((1, tm, n), lambda b, t: (b, t, 0))
    full = lambda a: pl.BlockSpec(a.shape, lambda b, t: (0,) * a.ndim)
    mod_map = (lambda b, t: (b, 0, 0)) if per_batch_mod else (lambda b, t: (0, 0, 0))
    sds = lambda n, dt: jax.ShapeDtypeStruct((B, T, n), dt)
    outs = pl.pallas_call(
        _inproj_kernel,
        grid=(B, nt),
        in_specs=[tok(D_MODEL), pl.BlockSpec((1, 1, N_MOD * D_MODEL), mod_map), full(norm_g)]
        + [full(w) for w in wts],
        out_specs=[tok(D_A), tok(D_A), tok(D_A), tok(D_A), tok(D_B), tok(D_MODEL), tok(D_MODEL), tok(GATE_PAD)],
        out_shape=[sds(D_A, bf16), sds(D_A, bf16), sds(D_A, bf16), sds(D_A, bf16), sds(D_B, f32),
                   sds(D_MODEL, bf16), sds(D_MODEL, bf16), sds(GATE_PAD, f32)],
        compiler_params=pltpu.CompilerParams(dimension_semantics=("arbitrary", "arbitrary"),
                                             vmem_limit_bytes=VMEM_LIMIT),
        name="in_proj",
    )(x, mod3, norm_g, *wts)
    return [o.reshape(B0, T0, o.shape[-1]) for o in outs]


NEG_BIG = -1e30
ST_U, ST_INTER, ST_EMJ, ST_WKN, ST_DECAY = 0, 8, 16, 24, 32
ST_ROWS = 40


def _chunk_scan(x, op, fill, prefix, lane, width):
    k = 1
    while k < CHUNK:
        if prefix:
            shifted = jnp.where(lane >= k, pltpu.roll(x, k, axis=1), fill)
        else:
            shifted = jnp.where(lane < CHUNK - k, pltpu.roll(x, width - k, axis=1), fill)
        x = op(x, shifted)
        k *= 2
    return x


def _mlstm_kernel(*refs, nseq, nc, state_in, state_out):
    refs = list(refs)
    q_ref, k_ref, v_ref, g_ref = refs[:4]
    del refs[:4]
    if state_in:
        c0_ref, n0_ref, m0_ref = refs[:3]
        del refs[:3]
    hn_ref = refs.pop(0)
    if state_out:
        cn_ref, nn_ref, mn_ref = refs[:3]
        del refs[:3]
    st_ref, wt_ref, vt_ref, qt_ref, u_ref, cp_ref, npf_ref, npb_ref, cst_ref, nsf_ref, nsb_ref = refs
    L = CHUNK
    nct = nseq * nc
    T = nct * L
    NR = 2 * NH_A

    lane = jnp.bitwise_and(lax.broadcasted_iota(jnp.int32, (NR, T), 1), L - 1)
    is_fwd = lax.broadcasted_iota(jnp.int32, (NR, T), 0) < NH_A
    is_fwd_c = lax.broadcasted_iota(jnp.int32, (NR, L), 0) < NH_A
    i_parts, f_parts = [], []
    for c in range(nct):
        gt = g_ref[0, c * L:(c + 1) * L, :].T
        i_parts.append(gt[0:NR])
        f_parts.append(gt[NR:2 * NR])
    ig = jnp.concatenate(i_parts, axis=1)
    lf = _log_sigmoid(jnp.concatenate(f_parts, axis=1))
    scan = functools.partial(_chunk_scan, lane=lane, width=T)
    ps = scan(lf, jnp.add, 0.0, True)
    ss = scan(lf, jnp.add, 0.0, False)
    b = jnp.where(is_fwd, ps, ss)
    btot = ps + ss - lf
    w = ig - b
    cmw = jnp.where(is_fwd, scan(w, jnp.maximum, -jnp.inf, True), scan(w, jnp.maximum, -jnp.inf, False))
    wk = btot - b + ig
    a = jnp.maximum(scan(wk, jnp.maximum, -jnp.inf, True), scan(wk, jnp.maximum, -jnp.inf, False))
    chunk_lanes = lambda cg: slice(cg * L, (cg + 1) * L)
    pad_rows = jnp.zeros((L - NR, L), f32)
    for s in range(nseq):
        ms = [m0_ref[s] if state_in else jnp.zeros((NR, L), f32)]
        for t in range(nc):
            cf, cb = chunk_lanes(s * nc + t), chunk_lanes(s * nc + nc - 1 - t)
            bt = jnp.where(is_fwd_c, btot[:, cf], btot[:, cb])
            at = jnp.where(is_fwd_c, a[:, cf], a[:, cb])
            ms.append(jnp.maximum(bt + ms[-1], at))
        if state_out:
            mn_ref[s] = ms[nc]
        for c in range(nc):
            cg = s * nc + c
            sl = chunk_lanes(cg)
            m_prev = jnp.where(is_fwd_c, ms[c], ms[nc - 1 - c])
            m_new = jnp.where(is_fwd_c, ms[c + 1], ms[nc - c])
            mx = jnp.maximum(m_prev, cmw[:, sl])
            st_ref[cg, ST_U:ST_U + NR] = -mx
            st_ref[cg, ST_INTER:ST_INTER + NR] = jnp.exp(m_prev - mx)
            st_ref[cg, ST_EMJ:ST_EMJ + NR] = jnp.exp(-mx - b[:, sl])
            st_ref[cg, ST_WKN:ST_WKN + NR] = jnp.exp(wk[:, sl] - m_new)
            st_ref[cg, ST_DECAY:ST_DECAY + NR] = jnp.exp(btot[:, sl] + m_prev - m_new)
            wt_ref[cg] = jnp.concatenate([w[:, sl], pad_rows], axis=0).T

    row = lax.broadcasted_iota(jnp.int32, (L, L), 0)
    col = lax.broadcasted_iota(jnp.int32, (L, L), 1)
    masks = (row <= col, row >= col)
    first_of8 = lax.broadcasted_iota(jnp.int32, (8, L), 0) == 0

    def row_tile(x):
        return jnp.where(first_of8, x, 0.0)

    head_lanes = [slice(h * DH_A, (h + 1) * DH_A) for h in range(NH_A)]

    def increments(j, carry):
        rows = pl.ds(pl.multiple_of(j * L, L), L)
        for h in range(NH_A):
            kh = k_ref[0, rows, head_lanes[h]]
            vt = v_ref[0, rows, head_lanes[h]].astype(f32).T
            vt_ref[j, h] = vt.astype(bf16)
            qt_ref[j, h] = q_ref[0, rows, head_lanes[h]].astype(f32).T.astype(bf16)
            wf = st_ref[j, ST_WKN + h:ST_WKN + h + 1, :]
            wb = st_ref[j, ST_WKN + NH_A + h:ST_WKN + NH_A + h + 1, :]
            lhs = jnp.concatenate([vt * wf, vt * wb, row_tile(wf), row_tile(wb)], axis=0).astype(bf16)
            u_ref[j, h] = _dot(lhs, kh)
        return carry

    lax.fori_loop(0, nct, increments, 0, unroll=4)

    seq_heads = [(s, h) for s in range(nseq) for h in range(NH_A)]
    for s, h in seq_heads:
        i = s * NH_A + h
        if state_in:
            cst_ref[i, 0:DH_A] = c0_ref[s, 0, h].T
            cst_ref[i, DH_A:2 * DH_A] = c0_ref[s, 1, h].T
            nsf_ref[i] = row_tile(n0_ref[s, h:h + 1, :])
            nsb_ref[i] = row_tile(n0_ref[s, NH_A + h:NH_A + h + 1, :])
        else:
            cst_ref[i] = jnp.zeros((2 * DH_A, DH_A), f32)
            nsf_ref[i] = jnp.zeros((8, DH_A), f32)
            nsb_ref[i] = jnp.zeros((8, DH_A), f32)

    def recur(t, carry):
        for s, h in seq_heads:
            i = s * NH_A + h
            jf = s * nc + t
            jb = s * nc + nc - 1 - t
            dec_f = st_ref[jf, ST_DECAY + h:ST_DECAY + h + 1, :]
            dec_b = st_ref[jb, ST_DECAY + NH_A + h:ST_DECAY + NH_A + h + 1, :]
            c_f = cst_ref[i, 0:DH_A]
            c_b = cst_ref[i, DH_A:2 * DH_A]
            cp_ref[jf, h, 0:DH_A] = c_f.astype(bf16)
            cp_ref[jb, h, DH_A:2 * DH_A] = c_b.astype(bf16)
            cst_ref[i, 0:DH_A] = dec_f * c_f + u_ref[jf, h, 0:DH_A]
            cst_ref[i, DH_A:2 * DH_A] = dec_b * c_b + u_ref[jb, h, DH_A:2 * DH_A]
            n_f = nsf_ref[i]
            n_b = nsb_ref[i]
            npf_ref[jf, h] = n_f
            npb_ref[jb, h] = n_b
            nsf_ref[i] = dec_f * n_f + u_ref[jf, h, 2 * DH_A:2 * DH_A + 8]
            nsb_ref[i] = dec_b * n_b + u_ref[jb, h, 2 * DH_A + 8:2 * DH_A + 16]
        return carry

    lax.fori_loop(0, nc, recur, 0)
    for s, h in seq_heads if state_out else ():
        i = s * NH_A + h
        cn_ref[s, 0, h] = cst_ref[i, 0:DH_A].T
        cn_ref[s, 1, h] = cst_ref[i, DH_A:2 * DH_A].T
        nn_ref[s, h:h + 1, :] = nsf_ref[i, 0:1]
        nn_ref[s, NH_A + h:NH_A + h + 1, :] = nsb_ref[i, 0:1]

    def outputs(j, carry):
        rows = pl.ds(pl.multiple_of(j * L, L), L)
        wt = wt_ref[j]
        heads = range(NH_A)
        kqs = []
        for h in heads:
            kh = k_ref[0, rows, head_lanes[h]]
            n_rows = jnp.concatenate([npf_ref[j, h], npb_ref[j, h]], axis=0).astype(bf16)
            kqs.append(_dot(jnp.concatenate([kh, n_rows, cp_ref[j, h]], axis=0), qt_ref[j, h]))
        decays = [[jnp.exp(jnp.where(masks[d], wt[:, NH_A * d + h:NH_A * d + h + 1]
                                     + st_ref[j, ST_U + NH_A * d + h:ST_U + NH_A * d + h + 1, :], NEG_BIG))
                   for d in range(2)] for h in heads]
        s_sums, h_ts = [], []
        for h in heads:
            kq = kqs[h]
            s_sum = None
            h_t = None
            for d in range(2):
                r = NH_A * d + h
                inter = st_ref[j, ST_INTER + r:ST_INTER + r + 1, :]
                emj = st_ref[j, ST_EMJ + r:ST_EMJ + r + 1, :]
                s_t = kq[0:L] * decays[h][d]
                qn = kq[L + 8 * d:L + 8 * d + 1]
                den = inter * qn + jnp.sum(s_t, axis=0, keepdims=True)
                rr = 1.0 / jnp.maximum(jnp.abs(den), emj)
                s_sum = s_t * rr if d == 0 else s_sum + s_t * rr
                part = kq[L + 16 + d * DH_A:L + 16 + (d + 1) * DH_A] * (inter * rr)
                h_t = part if d == 0 else h_t + part
            s_sums.append(s_sum.astype(bf16))
            h_ts.append(h_t)
        h_ts = [h_ts[h] + _dot(vt_ref[j, h], s_sums[h]) for h in heads]
        h_ts = [x * lax.rsqrt(jnp.mean(x * x, axis=0, keepdims=True) + EPS) for x in h_ts]
        for h in heads:
            hn_ref[0, rows, head_lanes[h]] = h_ts[h].T.astype(bf16)
        return carry

    lax.fori_loop(0, nct, outputs, 0, unroll=4)


def _mlstm(q, k, v, gates, state, nseq, state_out):
    B, T, _ = q.shape
    nc = T // CHUNK
    nct = nseq * nc
    G = B // nseq
    fold = lambda a: a.reshape(G, nseq * T, a.shape[-1])
    seq = lambda n: pl.BlockSpec((1, nseq * T, n), lambda b: (b, 0, 0))
    st_c = pl.BlockSpec((nseq, 2, NH_A, DH_A, DH_A), lambda b: (b, 0, 0, 0, 0))
    st_v = pl.BlockSpec((nseq, 2 * NH_A, DH_A), lambda b: (b, 0, 0))
    state_specs = [st_c, st_v, st_v]
    state_shapes = [jax.ShapeDtypeStruct((B, 2, NH_A, DH_A, DH_A), f32),
                    jax.ShapeDtypeStruct((B, 2 * NH_A, DH_A), f32),
                    jax.ShapeDtypeStruct((B, 2 * NH_A, DH_A), f32)]
    outs = pl.pallas_call(
        functools.partial(_mlstm_kernel, nseq=nseq, nc=nc, state_in=state is not None, state_out=state_out),
        grid=(G,),
        in_specs=[seq(D_A), seq(D_A), seq(D_A), seq(GATE_PAD)] + (state_specs if state is not None else []),
        out_specs=[seq(D_A)] + (state_specs if state_out else []),
        scratch_shapes=[pltpu.VMEM((nct, ST_ROWS, CHUNK), f32), pltpu.VMEM((nct, CHUNK, GATE_PAD), f32),
                        pltpu.VMEM((nct, NH_A, DH_A, CHUNK), bf16), pltpu.VMEM((nct, NH_A, DH_A, CHUNK), bf16),
                        pltpu.VMEM((nct, NH_A, 2 * DH_A + 16, DH_A), f32),
                        pltpu.VMEM((nct, NH_A, 2 * DH_A, DH_A), bf16),
                        pltpu.VMEM((nct, NH_A, 8, DH_A), f32), pltpu.VMEM((nct, NH_A, 8, DH_A), f32),
                        pltpu.VMEM((nseq * NH_A, 2 * DH_A, DH_A), f32),
                        pltpu.VMEM((nseq * NH_A, 8, DH_A), f32), pltpu.VMEM((nseq * NH_A, 8, DH_A), f32)],
        out_shape=[jax.ShapeDtypeStruct((G, nseq * T, D_A), bf16)] + (state_shapes if state_out else []),
        compiler_params=pltpu.CompilerParams(dimension_semantics=("arbitrary",), vmem_limit_bytes=VMEM_LIMIT),
        name="mlstm",
    )(fold(q), fold(k), fold(v), fold(gates), *(state if state is not None else ()))
    return (outs[0].reshape(B, T, D_A),) + tuple(outs[1:])


def _mix_kernel(x_ref, mod_ref, hn_ref, so_ref, glu_ref, glu_prev_ref, glu_next_ref, sga_ref, sgb_ref,
                ng_ref, wpa_ref, cw_ref, cb_ref, lng_ref, lnb_ref, wpb_ref, wout_ref,
                x1_ref, ext_ref, zs_ref, *, tm, nt, nsub):
    t = pl.program_id(1)
    mod = mod_ref[0]
    g1 = mod[:, 2 * D_MODEL:3 * D_MODEL]

    a_in = (so_ref[0].astype(f32) * (hn_ref[0].astype(f32) * ng_ref[...])).astype(bf16)
    branch_a = _dot(a_in, wpa_ref[...])

    H = CONV_HALO
    SUB, LANE = SUBLANES, LANES
    tsub = tm // nsub
    te = tsub + 2 * H
    for s in range(nsub):
        inner = nsub == 1
        ext_ref[0, s, 0:H, :] = jnp.where(t > 0, glu_prev_ref[0], 0.0) if inner else jnp.zeros((H, D_B), f32)
        ext_ref[0, s, H:H + tsub, :] = glu_ref[0, s * tsub:(s + 1) * tsub, :]
        ext_ref[0, s, H + tsub:te, :] = jnp.where(t < nt - 1, glu_next_ref[0], 0.0) if inner else jnp.zeros((H, D_B), f32)
    for s in range(nsub):
        for c in range(D_B // LANE):
            lanes = slice(c * LANE, (c + 1) * LANE)
            base = ext_ref[0, s, :, lanes]
            for k in range(1, SUB):
                ext_ref[k, s, :, lanes] = pltpu.roll(base, te - k, axis=0)
    off = H - CONV_W // 2
    for s in range(nsub):
        for r0 in range(0, tsub, CONV_ROWS):
            parts = []
            for c in range(D_B // LANE):
                lanes = slice(c * LANE, (c + 1) * LANE)
                acc = jnp.broadcast_to(cb_ref[:, lanes], (CONV_ROWS, LANE))
                for w in range(CONV_W):
                    k, a = (off + w) % SUB, (off + w) // SUB * SUB
                    acc = acc + ext_ref[k, s, r0 + a:r0 + a + CONV_ROWS, lanes] * cw_ref[w:w + 1, lanes]
                parts.append(acc)
            z = jnp.concatenate(parts, axis=1)
            mu = jnp.mean(z, axis=-1, keepdims=True)
            zc = z - mu
            var = jnp.mean(zc * zc, axis=-1, keepdims=True)
            zn = zc * lax.rsqrt(var + EPS) * lng_ref[...] + lnb_ref[...]
            zs_ref[s * tsub + r0:s * tsub + r0 + CONV_ROWS, :] = (zn * _sigmoid(zn)).astype(bf16)
    branch_b = _dot(zs_ref[...], wpb_ref[...])

    merged = sga_ref[0].astype(f32) * branch_a + sgb_ref[0].astype(f32) * branch_b
    x1_ref[0] = x_ref[0] + g1 * _dot(merged.astype(bf16), wout_ref[...])


def _mix_out(x, mod3, per_batch_mod, hn, so, glu, sga, sgb, wts, tm):
    B0, T0, _ = x.shape
    nsub = 1
    if tm > T0:
        assert not per_batch_mod and tm % T0 == 0 and B0 % (tm // T0) == 0
        nsub = tm // T0
        x, hn, so, glu, sga, sgb = (a.reshape(B0 // nsub, tm, a.shape[-1]) for a in (x, hn, so, glu, sga, sgb))
    B, T, _ = x.shape
    nt = T // tm
    H = CONV_HALO
    r = tm // H
    tok = lambda n: pl.BlockSpec((1, tm, n), lambda b, t: (b, t, 0))
    full = lambda a: pl.BlockSpec(a.shape, lambda b, t: (0,) * a.ndim)
    mod_map = (lambda b, t: (b, 0, 0)) if per_batch_mod else (lambda b, t: (0, 0, 0))
    prev = pl.BlockSpec((1, H, D_B), lambda b, t: (b, jnp.maximum(t * r - 1, 0), 0))
    nxt = pl.BlockSpec((1, H, D_B), lambda b, t: (b, jnp.minimum((t + 1) * r, T // H - 1), 0))
    x1 = pl.pallas_call(
        functools.partial(_mix_kernel, tm=tm, nt=nt, nsub=nsub),
        grid=(B, nt),
        in_specs=[tok(D_MODEL), pl.BlockSpec((1, 1, N_MOD * D_MODEL), mod_map), tok(D_A), tok(D_A),
                  tok(D_B), prev, nxt, tok(D_MODEL), tok(D_MODEL)] + [full(w) for w in wts],
        out_specs=tok(D_MODEL),
        out_shape=jax.ShapeDtypeStruct((B, T, D_MODEL), f32),
        scratch_shapes=[pltpu.VMEM((SUBLANES, nsub, tm // nsub + 2 * H, D_B), f32), pltpu.VMEM((tm, D_B), bf16)],
        compiler_params=pltpu.CompilerParams(dimension_semantics=("arbitrary", "arbitrary"),
                                             vmem_limit_bytes=VMEM_LIMIT),
        name="mix_out",
    )(x, mod3, hn, so, glu, glu, glu, sga, sgb, *wts)
    return x1.reshape(B0, T0, D_MODEL)


def _ffn_kernel(*refs, tm, nt, on_grid, seq_len):
    if on_grid:
        (x_ref, xp_ref, xn_ref, mod_ref, g2n_ref, wu_ref, wd_ref, cw_ref, cb_ref, fg_ref,
         y_ref, h_ref, g0_ref, g1_ref, v0_ref, v1_ref, gl_ref, gr_ref, a0_ref, a1_ref, a2_ref, a3_ref, acc_ref) = refs
    else:
        (x_ref, mod_ref, g2n_ref, wu_ref, wd_ref, cw_ref, cb_ref, fg_ref,
         y_ref, h_ref, g0_ref, g1_ref, v0_ref, v1_ref, gl_ref, gr_ref, a0_ref, a1_ref, a2_ref, a3_ref, acc_ref) = refs
    gbufs, vbufs, acts = (g0_ref, g1_ref), (v0_ref, v1_ref), (a0_ref, a1_ref, a2_ref, a3_ref)
    t = pl.program_id(1)
    halo = GRID_W if on_grid else 0
    te = tm + 2 * halo
    mod = mod_ref[0]
    sh2 = mod[:, 3 * D_MODEL:4 * D_MODEL]
    sc2 = mod[:, 4 * D_MODEL:5 * D_MODEL]
    g2 = mod[:, 5 * D_MODEL:6 * D_MODEL]

    def norm_mod(x):
        return _rms(x, g2n_ref[...]) * (1.0 + sc2) + sh2

    h_ref[halo:halo + tm, :] = norm_mod(x_ref[0]).astype(bf16)
    if on_grid:
        h_ref[0:halo, :] = jnp.where(t > 0, norm_mod(xp_ref[0]), 0.0).astype(bf16)
        h_ref[halo + tm:te, :] = jnp.where(t < nt - 1, norm_mod(xn_ref[0]), 0.0).astype(bf16)

    SUB, LANE = SUBLANES, LANES
    seg = GRID_W if on_grid else seq_len
    sub_row = lax.broadcasted_iota(jnp.int32, (SUB, LANE), 0)
    first_row = sub_row == 0
    last_row = sub_row == SUB - 1
    row_taps = (0, 1, 2) if on_grid else (1,)

    def ff_cols(fc, base):
        if isinstance(fc, int):
            return slice(base + fc * FF_CHUNK, base + (fc + 1) * FF_CHUNK)
        return pl.ds(pl.multiple_of(base + fc * FF_CHUNK, 128), FF_CHUNK)

    def up(fc, slot):
        gbufs[slot][...] = _dot(h_ref[...], wu_ref[:, ff_cols(fc, 0)])
        vbufs[slot][...] = _dot(h_ref[halo:halo + tm, :], wu_ref[:, ff_cols(fc, D_FF)])

    def gate_act(fc, slot, aslot):
        cw = cw_ref[:, ff_cols(fc, 0)]
        cb = cb_ref[:, ff_cols(fc, 0)]
        for c in range(FF_CHUNK // LANE):
            lanes = slice(c * LANE, (c + 1) * LANE)
            for s0 in range(0, te, seg):
                gate = gbufs[slot][s0:s0 + seg, lanes]
                g_l = pltpu.roll(gate, 1, axis=0)
                g_r = pltpu.roll(gate, seg - 1, axis=0)
                gl_ref[s0:s0 + SUB, lanes] = jnp.where(first_row, 0.0, g_l[0:SUB])
                gl_ref[s0 + SUB:s0 + seg, lanes] = g_l[SUB:seg]
                gr_ref[s0:s0 + seg - SUB, lanes] = g_r[0:seg - SUB]
                gr_ref[s0 + seg - SUB:s0 + seg, lanes] = jnp.where(last_row, 0.0, g_r[seg - SUB:seg])
            for r0 in range(0, tm, GRID_W):
                conv = jnp.broadcast_to(cb[:, lanes], (GRID_W, LANE))
                for kh in row_taps:
                    lo = r0 + kh * GRID_W if on_grid else r0
                    conv = conv + gl_ref[lo:lo + GRID_W, lanes] * cw[3 * kh:3 * kh + 1, lanes]
                    conv = conv + gbufs[slot][lo:lo + GRID_W, lanes] * cw[3 * kh + 1:3 * kh + 2, lanes]
                    conv = conv + gr_ref[lo:lo + GRID_W, lanes] * cw[3 * kh + 2:3 * kh + 3, lanes]
                gelu = conv * (0.5 + 0.5 * jnp.tanh(conv * (GELU_C0 + GELU_C1 * (conv * conv))))
                acts[aslot][r0:r0 + GRID_W, lanes] = (gelu * vbufs[slot][r0:r0 + GRID_W, lanes]).astype(bf16)

    def down(fc, aslot):
        return _dot(acts[aslot][...], wd_ref[fc])


    def pair(p, parity, with_down):
        fc = 2 * p
        wr, rd = 2 * parity, 2 * (1 - parity)
        if with_down:
            acc_ref[...] += down(fc - 2, rd) + down(fc - 1, rd + 1)
        up(fc + 1, 1)
        gate_act(fc, 0, wr)
        up(fc + 2, 0)
        gate_act(fc + 1, 1, wr + 1)

    n_pairs = (N_FF_CHUNKS - 1) // 2
    assert N_FF_CHUNKS == 2 * n_pairs + 1 and n_pairs % 2 == 1
    up(0, 0)
    acc_ref[...] = jnp.zeros_like(acc_ref)
    pair(0, 0, False)

    def two_pairs(i, carry):
        p = 2 * i + 1
        pair(p, 1, True)
        pair(p + 1, 0, True)
        return carry

    lax.fori_loop(0, (n_pairs - 1) // 2, two_pairs, 0)
    last = N_FF_CHUNKS - 1
    gate_act(last, 0, 2)
    ffn_out = acc_ref[...] + (down(last - 2, 0) + down(last - 1, 1) + down(last, 2))
    y_ref[0] = _rms(x_ref[0] + g2 * ffn_out, fg_ref[...])


def _ffn(x1, mod3, per_batch_mod, wts, tm, on_grid):
    B0, T0, _ = x1.shape
    if tm > T0:
        assert not per_batch_mod and not on_grid and tm % T0 == 0 and B0 % (tm // T0) == 0
        x1 = x1.reshape(B0 * T0 // tm, tm, D_MODEL)
    B, T, _ = x1.shape
    nt = T // tm
    halo = GRID_W if on_grid else 0
    te = tm + 2 * halo
    tok = pl.BlockSpec((1, tm, D_MODEL), lambda b, t: (b, t, 0))
    full = lambda a: pl.BlockSpec(a.shape, lambda b, t: (0,) * a.ndim)
    mod_map = (lambda b, t: (b, 0, 0)) if per_batch_mod else (lambda b, t: (0, 0, 0))
    in_specs = [tok]
    args = [x1]
    if on_grid:
        r = tm // GRID_W
        in_specs += [pl.BlockSpec((1, GRID_W, D_MODEL), lambda b, t: (b, jnp.maximum(t * r - 1, 0), 0)),
                     pl.BlockSpec((1, GRID_W, D_MODEL), lambda b, t: (b, jnp.minimum((t + 1) * r, T // GRID_W - 1), 0))]
        args += [x1, x1]
    in_specs += [pl.BlockSpec((1, 1, N_MOD * D_MODEL), mod_map)] + [full(w) for w in wts]
    args += [mod3] + list(wts)
    y = pl.pallas_call(
        functools.partial(_ffn_kernel, tm=tm, nt=nt, on_grid=on_grid, seq_len=T0),
        grid=(B, nt),
        in_specs=in_specs,
        out_specs=tok,
        out_shape=jax.ShapeDtypeStruct((B, T, D_MODEL), f32),
        scratch_shapes=[pltpu.VMEM((te, D_MODEL), bf16)]
        + [pltpu.VMEM((te, FF_CHUNK), f32)] * 2 + [pltpu.VMEM((tm, FF_CHUNK), f32)] * 2
        + [pltpu.VMEM((te, FF_CHUNK), f32)] * 2 + [pltpu.VMEM((tm, FF_CHUNK), bf16)] * 4
        + [pltpu.VMEM((tm, D_MODEL), f32)],
        compiler_params=pltpu.CompilerParams(dimension_semantics=("arbitrary", "arbitrary"),
                                             vmem_limit_bytes=VMEM_LIMIT),
        name="ffn_grid" if on_grid else "ffn_ctx",
    )(*args)
    return y.reshape(B0, T0, D_MODEL)


TOKEN_TILE = 512
MLSTM_TOKENS = 1024


def _tiles(T, per_batch_mod):
    span = TOKEN_TILE if not per_batch_mod else min(TOKEN_TILE, T)
    return span, span, span, max(1, MLSTM_TOKENS // T)


def _trunk(x, mod3, per_batch_mod, state, on_grid, w):
    tm_proj, tm_mix, tm_ffn, mlstm_nseq = _tiles(x.shape[1], per_batch_mod)
    q, k, v, so, glu, sga, sgb, gates = _in_proj(x, mod3, per_batch_mod, w["norm1_g"], w["in"], tm_proj)
    hn, *new_state = _mlstm(q, k, v, gates, state, mlstm_nseq, state is None)
    x1 = _mix_out(x, mod3, per_batch_mod, hn, so, glu, sga, sgb, w["mix"], tm_mix)
    y = _ffn(x1, mod3, per_batch_mod, w["ffn"], tm_ffn, on_grid)
    return y, new_state


def kernel(x_prompt, x_sample, c, state_C, state_n, state_m, c_ctx, w_ada, b_ada, norm1_g, w_in, b_in,
           mlstm_norm_g, w_proj_a, conv_dw_w, conv_dw_b, conv_ln_g, conv_ln_b, w_proj_b, w_out, norm2_g,
           w_up, ffn_dw_w, ffn_dw_b, w_down, final_norm_g):
    Bp = x_prompt.shape[0]
    Bl = x_sample.shape[0]
    l = 0
    row = lambda a: a.reshape(1, -1)

    ct = jnp.concatenate([c, c_ctx[None, :], jnp.zeros((8 - Bl - 1, D_MODEL), f32)], axis=0).T
    mod, (w_up_b, w_down_b, w_out_b, w_pa_b, w_pb_b) = _adaln_mod(
        ct, w_ada[l], row(b_ada[l]), Bl + 1,
        [(w_up[l], 1, 512), (w_down[l], 0, FF_CHUNK), (w_out[l], 0, 128), (w_proj_a[l], 0, 64), (w_proj_b[l], 0, 64)])
    mod_lat = mod[0:Bl].reshape(Bl, 1, N_MOD * D_MODEL)
    mod_ctx = mod[Bl:Bl + 1].reshape(1, 1, N_MOD * D_MODEL)

    o_g = 4 * D_A
    o_u = o_g + 4 * NH_A
    nh = NH_A

    def split_in(a, dt):
        g = a[..., o_g:o_u]
        pad = jnp.zeros(a.shape[:-1] + (GATE_PAD - 4 * nh,), a.dtype)
        gates = jnp.concatenate([g[..., 0:nh], g[..., 2 * nh:3 * nh], g[..., nh:2 * nh], g[..., 3 * nh:], pad], axis=-1)
        return a[..., :o_g].astype(dt), a[..., o_u:].astype(dt), gates.astype(dt)

    (w_a, w_b, w_g), (b_a, b_b, b_g) = split_in(w_in[l], bf16), split_in(row(b_in[l]), f32)
    w_in_parts = (w_a, b_a, w_b, b_b, w_g, b_g)
    w_mix = (row(mlstm_norm_g[l]), w_pa_b,
             jnp.pad(conv_dw_w[l], ((0, 1), (0, 0))), row(conv_dw_b[l]), row(conv_ln_g[l]), row(conv_ln_b[l]),
             w_pb_b, w_out_b)
    w_ffn = (row(norm2_g[l]), w_up_b,
             w_down_b.reshape(N_FF_CHUNKS, FF_CHUNK, D_MODEL),
             ffn_dw_w[l].reshape(9, D_FF), row(ffn_dw_b[l]), row(final_norm_g))
    w = {"norm1_g": row(norm1_g[l]), "in": w_in_parts, "mix": w_mix, "ffn": w_ffn}

    y_prompt, (cn, nn, mn) = _trunk(x_prompt, mod_ctx, False, None, False, w)

    c0 = state_C[:, l]
    n0 = state_n[:, l].reshape(Bl, 2 * NH_A, DH_A)
    m0 = jnp.broadcast_to(state_m[:, l].reshape(Bl, 2 * NH_A, 1), (Bl, 2 * NH_A, DH_A))
    y_sample, _ = _trunk(x_sample, mod_lat, True, (c0, n0, m0), True, w)

    new_state_C = cn[:, None]
    new_state_n = nn.reshape(Bp, 1, 2, NH_A, DH_A)
    new_state_m = mn[:, :, 0].reshape(Bp, 1, 2, NH_A)
    return (y_prompt, y_sample, new_state_C, new_state_n, new_state_m)
```

```python
import functools

import jax
import jax.numpy as jnp
from jax import lax
from jax.experimental import pallas as pl
from jax.experimental.pallas import tpu as pltpu

D_MODEL = 1024
D_A = 512
NH_A = 4
DH_A = 128
CHUNK = 128
D_B = 512
CONV_W = 31
CONV_HALO = 16
CONV_ROWS = 64
D_FF = 2816
FF_CHUNK = 256
N_FF_CHUNKS = D_FF // FF_CHUNK
GRID_W = 64
N_MOD = 6
GATE_PAD = 128
EPS = 1e-6
Q_SCALE = DH_A ** -0.5
VMEM_LIMIT = 56 * 1024 * 1024
SUBLANES = 8
LANES = 128

f32 = jnp.float32
bf16 = jnp.bfloat16


def _rms(x, g):
    return x * lax.rsqrt(jnp.mean(x * x, axis=-1, keepdims=True) + EPS) * g


def _sigmoid(x):
    return 1.0 / (1.0 + jnp.exp(-x))


def _log_sigmoid(x):
    return jnp.minimum(x, 0.0) - jnp.log(1.0 + jnp.exp(-jnp.abs(x)))


GELU_C0 = 0.7978845608028654
GELU_C1 = GELU_C0 * 0.044715


def _dot(a, b):
    return jnp.dot(a, b, preferred_element_type=f32)


def _mod_kernel(*refs, n_rows, n_cast):
    ct_ref, w_ref, b_ref = refs[:3]
    src_refs = refs[3:3 + n_cast]
    o_ref = refs[3 + n_cast]
    dst_refs = refs[4 + n_cast:]
    ct = ct_ref[...]
    st = ct * _sigmoid(ct)
    w = w_ref[...]
    rows = []
    for r in range(n_rows):
        s_col = jnp.broadcast_to(st[:, r:r + 1], (D_MODEL, LANES))
        rows.append(jnp.concatenate(
            [jnp.sum(w[:, c:c + LANES] * s_col, axis=0, keepdims=True) for c in range(0, w.shape[1], LANES)], axis=1))
    rows.append(jnp.zeros((8 - n_rows, w.shape[1]), f32))
    o_ref[...] = jnp.concatenate(rows, axis=0) + b_ref[...]
    for src, dst in zip(src_refs, dst_refs):
        dst[...] = src[...].astype(bf16)


def _adaln_mod(ct, w_ada, b_ada, n_rows, cast_weights):
    n = w_ada.shape[1]
    tn = 512
    steps = n // tn

    def walk(a, axis, block):
        nblk = a.shape[axis] // block
        assert nblk * block == a.shape[axis] and nblk <= steps
        shape = tuple(block if d == axis else a.shape[d] for d in range(a.ndim))
        return pl.BlockSpec(shape, lambda j: tuple(jnp.minimum(j, nblk - 1) if d == axis else 0 for d in range(a.ndim)))

    cast_specs = [walk(*cw) for cw in cast_weights]
    outs = pl.pallas_call(
        functools.partial(_mod_kernel, n_rows=n_rows, n_cast=len(cast_weights)),
        grid=(steps,),
        in_specs=[
            pl.BlockSpec((D_MODEL, 8), lambda j: (0, 0)),
            pl.BlockSpec((D_MODEL, tn), lambda j: (0, j)),
            pl.BlockSpec((1, tn), lambda j: (0, j)),
        ] + cast_specs,
        out_specs=[pl.BlockSpec((8, tn), lambda j: (0, j))] + cast_specs,
        out_shape=[jax.ShapeDtypeStruct((8, n), f32)]
        + [jax.ShapeDtypeStruct(cw[0].shape, bf16) for cw in cast_weights],
        compiler_params=pltpu.CompilerParams(dimension_semantics=("arbitrary",), vmem_limit_bytes=VMEM_LIMIT),
        name="adaln_mod",
    )(ct, w_ada, b_ada, *[cw[0] for cw in cast_weights])
    return outs[0], outs[1:]


def _inproj_kernel(x_ref, mod_ref, g_ref, w_ref, ba_ref, bb_ref, wg_ref, bg_ref,
                   q_ref, k_ref, v_ref, so_ref, glu_ref, sga_ref, sgb_ref, gates_ref, wa_ref, wb_ref, *, o_a, o_b):
    @pl.when(jnp.logical_and(pl.program_id(0) == 0, pl.program_id(1) == 0))
    def _():
        wa_ref[...] = w_ref[:, 0:o_a]
        wb_ref[...] = w_ref[:, o_b:]

    x = x_ref[0]
    mod = mod_ref[0]
    sh1 = mod[:, 0:D_MODEL]
    sc1 = mod[:, D_MODEL:2 * D_MODEL]
    h = (_rms(x, g_ref[...]) * (1.0 + sc1) + sh1).astype(bf16)

    def proj(w_ref, b_ref, lo, n):
        return _dot(h, w_ref[:, lo:lo + n]) + b_ref[:, lo:lo + n]

    q_ref[0] = (proj(wa_ref, ba_ref, 0, D_A) * Q_SCALE).astype(bf16)
    k_ref[0] = proj(wa_ref, ba_ref, D_A, D_A).astype(bf16)
    v_ref[0] = proj(wa_ref, ba_ref, 2 * D_A, D_A).astype(bf16)
    so_ref[0] = _sigmoid(proj(wa_ref, ba_ref, 3 * D_A, D_A)).astype(bf16)
    glu_ref[0] = proj(wb_ref, bb_ref, 0, D_B) * _sigmoid(proj(wb_ref, bb_ref, D_B, D_B))
    sga_ref[0] = _sigmoid(proj(wb_ref, bb_ref, 2 * D_B, D_MODEL)).astype(bf16)
    sgb_ref[0] = _sigmoid(proj(wb_ref, bb_ref, 2 * D_B + D_MODEL, D_MODEL)).astype(bf16)
    gates_ref[0] = proj(wg_ref, bg_ref, 0, GATE_PAD)


def _in_proj(x, mod3, per_batch_mod, norm_g, wts, tm, o_a, o_b):
    B0, T0, _ = x.shape
    if tm > T0:
        assert not per_batch_mod and tm % T0 == 0 and B0 % (tm // T0) == 0
        x = x.reshape(B0 * T0 // tm, tm, D_MODEL)
    B, T, _ = x.shape
    nt = T // tm
    tok = lambda n: pl.BlockSpec((1, tm, n), lambda b, t: (b, t, 0))
    full = lambda a: pl.BlockSpec(a.shape, lambda b, t: (0,) * a.ndim)
    mod_map = (lambda b, t: (b, 0, 0)) if per_batch_mod else (lambda b, t: (0, 0, 0))
    sds = lambda n, dt: jax.ShapeDtypeStruct((B, T, n), dt)
    n_b = wts[0].shape[1] - o_b
    outs = pl.pallas_call(
        functools.partial(_inproj_kernel, o_a=o_a, o_b=o_b),
        grid=(B, nt),
        in_specs=[tok(D_MODEL), pl.BlockSpec((1, 1, N_MOD * D_MODEL), mod_map), full(norm_g)]
        + [full(w) for w in wts],
        out_specs=[tok(D_A), tok(D_A), tok(D_A), tok(D_A), tok(D_B), tok(D_MODEL), tok(D_MODEL), tok(GATE_PAD)],
        out_shape=[sds(D_A, bf16), sds(D_A, bf16), sds(D_A, bf16), sds(D_A, bf16), sds(D_B, f32),
                   sds(D_MODEL, bf16), sds(D_MODEL, bf16), sds(GATE_PAD, f32)],
        scratch_shapes=[pltpu.VMEM((D_MODEL, o_a), bf16), pltpu.VMEM((D_MODEL, n_b), bf16)],
        compiler_params=pltpu.CompilerParams(dimension_semantics=("arbitrary", "arbitrary"),
                                             vmem_limit_bytes=VMEM_LIMIT),
        name="in_proj",
    )(x, mod3, norm_g, *wts)
    return [o.reshape(B0, T0, o.shape[-1]) for o in outs]


NEG_BIG = -1e30
ST_U, ST_INTER, ST_EMJ, ST_WKN, ST_DECAY = 0, 8, 16, 24, 32
ST_ROWS = 40


def _chunk_scan(x, op, fill, prefix, lane, width):
    k = 1
    while k < CHUNK:
        if prefix:
            shifted = jnp.where(lane >= k, pltpu.roll(x, k, axis=1), fill)
        else:
            shifted = jnp.where(lane < CHUNK - k, pltpu.roll(x, width - k, axis=1), fill)
        x = op(x, shifted)
        k *= 2
    return x


def _mlstm_kernel(*refs, nseq, nc, state_in, state_out):
    refs = list(refs)
    q_ref, k_ref, v_ref, g_ref = refs[:4]
    del refs[:4]
    if state_in:
        c0_ref, n0_ref, m0_ref = refs[:3]
        del refs[:3]
    hn_ref = refs.pop(0)
    if state_out:
        cn_ref, nn_ref, mn_ref = refs[:3]
        del refs[:3]
    st_ref, wt_ref, vt_ref, qt_ref, u_ref, cp_ref, npf_ref, npb_ref, cst_ref, nsf_ref, nsb_ref = refs
    L = CHUNK
    nct = nseq * nc
    T = nct * L
    NR = 2 * NH_A

    lane = jnp.bitwise_and(lax.broadcasted_iota(jnp.int32, (NR, T), 1), L - 1)
    is_fwd = lax.broadcasted_iota(jnp.int32, (NR, T), 0) < NH_A
    is_fwd_c = lax.broadcasted_iota(jnp.int32, (NR, L), 0) < NH_A
    i_parts, f_parts = [], []
    for c in range(nct):
        gt = g_ref[0, c * L:(c + 1) * L, :].T
        i_parts.append(gt[0:NR])
        f_parts.append(gt[NR:2 * NR])
    ig = jnp.concatenate(i_parts, axis=1)
    lf = _log_sigmoid(jnp.concatenate(f_parts, axis=1))
    scan = functools.partial(_chunk_scan, lane=lane, width=T)
    ps = scan(lf, jnp.add, 0.0, True)
    ss = scan(lf, jnp.add, 0.0, False)
    b = jnp.where(is_fwd, ps, ss)
    btot = ps + ss - lf
    w = ig - b
    cmw = jnp.where(is_fwd, scan(w, jnp.maximum, -jnp.inf, True), scan(w, jnp.maximum, -jnp.inf, False))
    wk = btot - b + ig
    a = jnp.maximum(scan(wk, jnp.maximum, -jnp.inf, True), scan(wk, jnp.maximum, -jnp.inf, False))
    chunk_lanes = lambda cg: slice(cg * L, (cg + 1) * L)
    pad_rows = jnp.zeros((L - NR, L), f32)
    for s in range(nseq):
        ms = [m0_ref[s] if state_in else jnp.zeros((NR, L), f32)]
        for t in range(nc):
            cf, cb = chunk_lanes(s * nc + t), chunk_lanes(s * nc + nc - 1 - t)
            bt = jnp.where(is_fwd_c, btot[:, cf], btot[:, cb])
            at = jnp.where(is_fwd_c, a[:, cf], a[:, cb])
            ms.append(jnp.maximum(bt + ms[-1], at))
        if state_out:
            mn_ref[s] = ms[nc]
        for c in range(nc):
            cg = s * nc + c
            sl = chunk_lanes(cg)
            m_prev = jnp.where(is_fwd_c, ms[c], ms[nc - 1 - c])
            m_new = jnp.where(is_fwd_c, ms[c + 1], ms[nc - c])
            mx = jnp.maximum(m_prev, cmw[:, sl])
            st_ref[cg, ST_U:ST_U + NR] = -mx
            st_ref[cg, ST_INTER:ST_INTER + NR] = jnp.exp(m_prev - mx)
            st_ref[cg, ST_EMJ:ST_EMJ + NR] = jnp.exp(-mx - b[:, sl])
            st_ref[cg, ST_WKN:ST_WKN + NR] = jnp.exp(wk[:, sl] - m_new)
            st_ref[cg, ST_DECAY:ST_DECAY + NR] = jnp.exp(btot[:, sl] + m_prev - m_new)
            wt_ref[cg] = jnp.concatenate([w[:, sl], pad_rows], axis=0).T

    row = lax.broadcasted_iota(jnp.int32, (L, L), 0)
    col = lax.broadcasted_iota(jnp.int32, (L, L), 1)
    masks = (row <= col, row >= col)
    first_of8 = lax.broadcasted_iota(jnp.int32, (8, L), 0) == 0

    def row_tile(x):
        return jnp.where(first_of8, x, 0.0)

    head_lanes = [slice(h * DH_A, (h + 1) * DH_A) for h in range(NH_A)]

    def increments(j, carry):
        rows = pl.ds(pl.multiple_of(j * L, L), L)
        for h in range(NH_A):
            kh = k_ref[0, rows, head_lanes[h]]
            vt = v_ref[0, rows, head_lanes[h]].astype(f32).T
            vt_ref[j, h] = vt.astype(bf16)
            qt_ref[j, h] = q_ref[0, rows, head_lanes[h]].astype(f32).T.astype(bf16)
            wf = st_ref[j, ST_WKN + h:ST_WKN + h + 1, :]
            wb = st_ref[j, ST_WKN + NH_A + h:ST_WKN + NH_A + h + 1, :]
            lhs = jnp.concatenate([vt * wf, vt * wb, row_tile(wf), row_tile(wb)], axis=0).astype(bf16)
            u_ref[j, h] = _dot(lhs, kh)
        return carry

    lax.fori_loop(0, nct, increments, 0, unroll=4)

    seq_heads = [(s, h) for s in range(nseq) for h in range(NH_A)]
    for s, h in seq_heads:
        i = s * NH_A + h
        if state_in:
            cst_ref[i, 0:DH_A] = c0_ref[s, 0, h].T
            cst_ref[i, DH_A:2 * DH_A] = c0_ref[s, 1, h].T
            nsf_ref[i] = row_tile(n0_ref[s, h:h + 1, :])
            nsb_ref[i] = row_tile(n0_ref[s, NH_A + h:NH_A + h + 1, :])
        else:
            cst_ref[i] = jnp.zeros((2 * DH_A, DH_A), f32)
            nsf_ref[i] = jnp.zeros((8, DH_A), f32)
            nsb_ref[i] = jnp.zeros((8, DH_A), f32)

    def recur(t, carry):
        for s, h in seq_heads:
            i = s * NH_A + h
            jf = s * nc + t
            jb = s * nc + nc - 1 - t
            dec_f = st_ref[jf, ST_DECAY + h:ST_DECAY + h + 1, :]
            dec_b = st_ref[jb, ST_DECAY + NH_A + h:ST_DECAY + NH_A + h + 1, :]
            c_f = cst_ref[i, 0:DH_A]
            c_b = cst_ref[i, DH_A:2 * DH_A]
            cp_ref[jf, h, 0:DH_A] = c_f.astype(bf16)
            cp_ref[jb, h, DH_A:2 * DH_A] = c_b.astype(bf16)
            cst_ref[i, 0:DH_A] = dec_f * c_f + u_ref[jf, h, 0:DH_A]
            cst_ref[i, DH_A:2 * DH_A] = dec_b * c_b + u_ref[jb, h, DH_A:2 * DH_A]
            n_f = nsf_ref[i]
            n_b = nsb_ref[i]
            npf_ref[jf, h] = n_f
            npb_ref[jb, h] = n_b
            nsf_ref[i] = dec_f * n_f + u_ref[jf, h, 2 * DH_A:2 * DH_A + 8]
            nsb_ref[i] = dec_b * n_b + u_ref[jb, h, 2 * DH_A + 8:2 * DH_A + 16]
        return carry

    lax.fori_loop(0, nc, recur, 0)
    for s, h in seq_heads if state_out else ():
        i = s * NH_A + h
        cn_ref[s, 0, h] = cst_ref[i, 0:DH_A].T
        cn_ref[s, 1, h] = cst_ref[i, DH_A:2 * DH_A].T
        nn_ref[s, h:h + 1, :] = nsf_ref[i, 0:1]
        nn_ref[s, NH_A + h:NH_A + h + 1, :] = nsb_ref[i, 0:1]

    def outputs(j, carry):
        rows = pl.ds(pl.multiple_of(j * L, L), L)
        wt = wt_ref[j]
        heads = range(NH_A)
        kqs = []
        for h in heads:
            kh = k_ref[0, rows, head_lanes[h]]
            n_rows = jnp.concatenate([npf_ref[j, h], npb_ref[j, h]], axis=0).astype(bf16)
            kqs.append(_dot(jnp.concatenate([kh, n_rows, cp_ref[j, h]], axis=0), qt_ref[j, h]))
        decays = [[jnp.exp(jnp.where(masks[d], wt[:, NH_A * d + h:NH_A * d + h + 1]
                                     + st_ref[j, ST_U + NH_A * d + h:ST_U + NH_A * d + h + 1, :], NEG_BIG))
                   for d in range(2)] for h in heads]
        s_sums, h_ts = [], []
        for h in heads:
            kq = kqs[h]
            s_sum = None
            h_t = None
            for d in range(2):
                r = NH_A * d + h
                inter = st_ref[j, ST_INTER + r:ST_INTER + r + 1, :]
                emj = st_ref[j, ST_EMJ + r:ST_EMJ + r + 1, :]
                s_t = kq[0:L] * decays[h][d]
                qn = kq[L + 8 * d:L + 8 * d + 1]
                den = inter * qn + jnp.sum(s_t, axis=0, keepdims=True)
                rr = 1.0 / jnp.maximum(jnp.abs(den), emj)
                s_sum = s_t * rr if d == 0 else s_sum + s_t * rr
                part = kq[L + 16 + d * DH_A:L + 16 + (d + 1) * DH_A] * (inter * rr)
                h_t = part if d == 0 else h_t + part
            s_sums.append(s_sum.astype(bf16))
            h_ts.append(h_t)
        h_ts = [h_ts[h] + _dot(vt_ref[j, h], s_sums[h]) for h in heads]
        h_ts = [x * lax.rsqrt(jnp.mean(x * x, axis=0, keepdims=True) + EPS) for x in h_ts]
        for h in heads:
            hn_ref[0, rows, head_lanes[h]] = h_ts[h].T.astype(bf16)
        return carry

    lax.fori_loop(0, nct, outputs, 0, unroll=4)


def _mlstm(q, k, v, gates, state, nseq, state_out):
    B, T, _ = q.shape
    nc = T // CHUNK
    nct = nseq * nc
    G = B // nseq
    fold = lambda a: a.reshape(G, nseq * T, a.shape[-1])
    seq = lambda n: pl.BlockSpec((1, nseq * T, n), lambda b: (b, 0, 0))
    st_c = pl.BlockSpec((nseq, 2, NH_A, DH_A, DH_A), lambda b: (b, 0, 0, 0, 0))
    st_v = pl.BlockSpec((nseq, 2 * NH_A, DH_A), lambda b: (b, 0, 0))
    state_specs = [st_c, st_v, st_v]
    state_shapes = [jax.ShapeDtypeStruct((B, 2, NH_A, DH_A, DH_A), f32),
                    jax.ShapeDtypeStruct((B, 2 * NH_A, DH_A), f32),
                    jax.ShapeDtypeStruct((B, 2 * NH_A, DH_A), f32)]
    outs = pl.pallas_call(
        functools.partial(_mlstm_kernel, nseq=nseq, nc=nc, state_in=state is not None, state_out=state_out),
        grid=(G,),
        in_specs=[seq(D_A), seq(D_A), seq(D_A), seq(GATE_PAD)] + (state_specs if state is not None else []),
        out_specs=[seq(D_A)] + (state_specs if state_out else []),
        scratch_shapes=[pltpu.VMEM((nct, ST_ROWS, CHUNK), f32), pltpu.VMEM((nct, CHUNK, GATE_PAD), f32),
                        pltpu.VMEM((nct, NH_A, DH_A, CHUNK), bf16), pltpu.VMEM((nct, NH_A, DH_A, CHUNK), bf16),
                        pltpu.VMEM((nct, NH_A, 2 * DH_A + 16, DH_A), f32),
                        pltpu.VMEM((nct, NH_A, 2 * DH_A, DH_A), bf16),
                        pltpu.VMEM((nct, NH_A, 8, DH_A), f32), pltpu.VMEM((nct, NH_A, 8, DH_A), f32),
                        pltpu.VMEM((nseq * NH_A, 2 * DH_A, DH_A), f32),
                        pltpu.VMEM((nseq * NH_A, 8, DH_A), f32), pltpu.VMEM((nseq * NH_A, 8, DH_A), f32)],
        out_shape=[jax.ShapeDtypeStruct((G, nseq * T, D_A), bf16)] + (state_shapes if state_out else []),
        compiler_params=pltpu.CompilerParams(dimension_semantics=("arbitrary",), vmem_limit_bytes=VMEM_LIMIT),
        name="mlstm",
    )(fold(q), fold(k), fold(v), fold(gates), *(state if state is not None else ()))
    return (outs[0].reshape(B, T, D_A),) + tuple(outs[1:])


def _mix_kernel(x_ref, mod_ref, hn_ref, so_ref, glu_ref, glu_prev_ref, glu_next_ref, sga_ref, sgb_ref,
                ng_ref, wpa_ref, cw_ref, cb_ref, lng_ref, lnb_ref, wpb_ref, wout_ref,
                x1_ref, ext_ref, zs_ref, *, tm, nt, nsub):
    t = pl.program_id(1)
    mod = mod_ref[0]
    g1 = mod[:, 2 * D_MODEL:3 * D_MODEL]

    a_in = (so_ref[0].astype(f32) * (hn_ref[0].astype(f32) * ng_ref[...])).astype(bf16)
    branch_a = _dot(a_in, wpa_ref[...])

    H = CONV_HALO
    SUB, LANE = SUBLANES, LANES
    tsub = tm // nsub
    te = tsub + 2 * H
    for s in range(nsub):
        inner = nsub == 1
        ext_ref[0, s, 0:H, :] = jnp.where(t > 0, glu_prev_ref[0], 0.0) if inner else jnp.zeros((H, D_B), f32)
        ext_ref[0, s, H:H + tsub, :] = glu_ref[0, s * tsub:(s + 1) * tsub, :]
        ext_ref[0, s, H + tsub:te, :] = jnp.where(t < nt - 1, glu_next_ref[0], 0.0) if inner else jnp.zeros((H, D_B), f32)
    for s in range(nsub):
        for c in range(D_B // LANE):
            lanes = slice(c * LANE, (c + 1) * LANE)
            base = ext_ref[0, s, :, lanes]
            for k in range(1, SUB):
                ext_ref[k, s, :, lanes] = pltpu.roll(base, te - k, axis=0)
    off = H - CONV_W // 2
    for s in range(nsub):
        for r0 in range(0, tsub, CONV_ROWS):
            parts = []
            for c in range(D_B // LANE):
                lanes = slice(c * LANE, (c + 1) * LANE)
                acc = jnp.broadcast_to(cb_ref[:, lanes], (CONV_ROWS, LANE))
                for w in range(CONV_W):
                    k, a = (off + w) % SUB, (off + w) // SUB * SUB
                    acc = acc + ext_ref[k, s, r0 + a:r0 + a + CONV_ROWS, lanes] * cw_ref[w:w + 1, lanes]
                parts.append(acc)
            z = jnp.concatenate(parts, axis=1)
            mu = jnp.mean(z, axis=-1, keepdims=True)
            zc = z - mu
            var = jnp.mean(zc * zc, axis=-1, keepdims=True)
            zn = zc * lax.rsqrt(var + EPS) * lng_ref[...] + lnb_ref[...]
            zs_ref[s * tsub + r0:s * tsub + r0 + CONV_ROWS, :] = (zn * _sigmoid(zn)).astype(bf16)
    branch_b = _dot(zs_ref[...], wpb_ref[...])

    merged = sga_ref[0].astype(f32) * branch_a + sgb_ref[0].astype(f32) * branch_b
    x1_ref[0] = x_ref[0] + g1 * _dot(merged.astype(bf16), wout_ref[...])


def _mix_out(x, mod3, per_batch_mod, hn, so, glu, sga, sgb, wts, tm):
    B0, T0, _ = x.shape
    nsub = 1
    if tm > T0:
        assert not per_batch_mod and tm % T0 == 0 and B0 % (tm // T0) == 0
        nsub = tm // T0
        x, hn, so, glu, sga, sgb = (a.reshape(B0 // nsub, tm, a.shape[-1]) for a in (x, hn, so, glu, sga, sgb))
    B, T, _ = x.shape
    nt = T // tm
    H = CONV_HALO
    r = tm // H
    tok = lambda n: pl.BlockSpec((1, tm, n), lambda b, t: (b, t, 0))
    full = lambda a: pl.BlockSpec(a.shape, lambda b, t: (0,) * a.ndim)
    mod_map = (lambda b, t: (b, 0, 0)) if per_batch_mod else (lambda b, t: (0, 0, 0))
    prev = pl.BlockSpec((1, H, D_B), lambda b, t: (b, jnp.maximum(t * r - 1, 0), 0))
    nxt = pl.BlockSpec((1, H, D_B), lambda b, t: (b, jnp.minimum((t + 1) * r, T // H - 1), 0))
    x1 = pl.pallas_call(
        functools.partial(_mix_kernel, tm=tm, nt=nt, nsub=nsub),
        grid=(B, nt),
        in_specs=[tok(D_MODEL), pl.BlockSpec((1, 1, N_MOD * D_MODEL), mod_map), tok(D_A), tok(D_A),
                  tok(D_B), prev, nxt, tok(D_MODEL), tok(D_MODEL)] + [full(w) for w in wts],
        out_specs=tok(D_MODEL),
        out_shape=jax.ShapeDtypeStruct((B, T, D_MODEL), f32),
        scratch_shapes=[pltpu.VMEM((SUBLANES, nsub, tm // nsub + 2 * H, D_B), f32), pltpu.VMEM((tm, D_B), bf16)],
        compiler_params=pltpu.CompilerParams(dimension_semantics=("arbitrary", "arbitrary"),
                                             vmem_limit_bytes=VMEM_LIMIT),
        name="mix_out",
    )(x, mod3, hn, so, glu, glu, glu, sga, sgb, *wts)
    return x1.reshape(B0, T0, D_MODEL)


def _ffn_kernel(*refs, tm, nt, on_grid, seq_len):
    if on_grid:
        (x_ref, xp_ref, xn_ref, mod_ref, g2n_ref, wu_ref, wd_ref, cw_ref, cb_ref, fg_ref,
         y_ref, h_ref, g0_ref, g1_ref, v0_ref, v1_ref, gl_ref, gr_ref, a0_ref, a1_ref, a2_ref, a3_ref, acc_ref) = refs
    else:
        (x_ref, mod_ref, g2n_ref, wu_ref, wd_ref, cw_ref, cb_ref, fg_ref,
         y_ref, h_ref, g0_ref, g1_ref, v0_ref, v1_ref, gl_ref, gr_ref, a0_ref, a1_ref, a2_ref, a3_ref, acc_ref) = refs
    gbufs, vbufs, acts = (g0_ref, g1_ref), (v0_ref, v1_ref), (a0_ref, a1_ref, a2_ref, a3_ref)
    t = pl.program_id(1)
    halo = GRID_W if on_grid else 0
    te = tm + 2 * halo
    mod = mod_ref[0]
    sh2 = mod[:, 3 * D_MODEL:4 * D_MODEL]
    sc2 = mod[:, 4 * D_MODEL:5 * D_MODEL]
    g2 = mod[:, 5 * D_MODEL:6 * D_MODEL]

    def norm_mod(x):
        return _rms(x, g2n_ref[...]) * (1.0 + sc2) + sh2

    h_ref[halo:halo + tm, :] = norm_mod(x_ref[0]).astype(bf16)
    if on_grid:
        h_ref[0:halo, :] = jnp.where(t > 0, norm_mod(xp_ref[0]), 0.0).astype(bf16)
        h_ref[halo + tm:te, :] = jnp.where(t < nt - 1, norm_mod(xn_ref[0]), 0.0).astype(bf16)

    SUB, LANE = SUBLANES, LANES
    seg = GRID_W if on_grid else seq_len
    sub_row = lax.broadcasted_iota(jnp.int32, (SUB, LANE), 0)
    first_row = sub_row == 0
    last_row = sub_row == SUB - 1
    row_taps = (0, 1, 2) if on_grid else (1,)

    def ff_cols(fc, base):
        if isinstance(fc, int):
            return slice(base + fc * FF_CHUNK, base + (fc + 1) * FF_CHUNK)
        return pl.ds(pl.multiple_of(base + fc * FF_CHUNK, 128), FF_CHUNK)

    def up(fc, slot):
        gbufs[slot][...] = _dot(h_ref[...], wu_ref[:, ff_cols(fc, 0)])
        vbufs[slot][...] = _dot(h_ref[halo:halo + tm, :], wu_ref[:, ff_cols(fc, D_FF)])

    def gate_act(fc, slot, aslot):
        cw = cw_ref[:, ff_cols(fc, 0)]
        cb = cb_ref[:, ff_cols(fc, 0)]
        for c in range(FF_CHUNK // LANE):
            lanes = slice(c * LANE, (c + 1) * LANE)
            for s0 in range(0, te, seg):
                gate = gbufs[slot][s0:s0 + seg, lanes]
                g_l = pltpu.roll(gate, 1, axis=0)
                g_r = pltpu.roll(gate, seg - 1, axis=0)
                gl_ref[s0:s0 + SUB, lanes] = jnp.where(first_row, 0.0, g_l[0:SUB])
                gl_ref[s0 + SUB:s0 + seg, lanes] = g_l[SUB:seg]
                gr_ref[s0:s0 + seg - SUB, lanes] = g_r[0:seg - SUB]
                gr_ref[s0 + seg - SUB:s0 + seg, lanes] = jnp.where(last_row, 0.0, g_r[seg - SUB:seg])
            for r0 in range(0, tm, GRID_W):
                conv = jnp.broadcast_to(cb[:, lanes], (GRID_W, LANE))
                for kh in row_taps:
                    lo = r0 + kh * GRID_W if on_grid else r0
                    conv = conv + gl_ref[lo:lo + GRID_W, lanes] * cw[3 * kh:3 * kh + 1, lanes]
                    conv = conv + gbufs[slot][lo:lo + GRID_W, lanes] * cw[3 * kh + 1:3 * kh + 2, lanes]
                    conv = conv + gr_ref[lo:lo + GRID_W, lanes] * cw[3 * kh + 2:3 * kh + 3, lanes]
                gelu = conv * (0.5 + 0.5 * jnp.tanh(conv * (GELU_C0 + GELU_C1 * (conv * conv))))
                acts[aslot][r0:r0 + GRID_W, lanes] = (gelu * vbufs[slot][r0:r0 + GRID_W, lanes]).astype(bf16)

    def down(fc, aslot):
        return _dot(acts[aslot][...], wd_ref[fc])


    def pair(p, parity, with_down):
        fc = 2 * p
        wr, rd = 2 * parity, 2 * (1 - parity)
        if with_down:
            acc_ref[...] += down(fc - 2, rd) + down(fc - 1, rd + 1)
        up(fc + 1, 1)
        gate_act(fc, 0, wr)
        up(fc + 2, 0)
        gate_act(fc + 1, 1, wr + 1)

    n_pairs = (N_FF_CHUNKS - 1) // 2
    assert N_FF_CHUNKS == 2 * n_pairs + 1 and n_pairs % 2 == 1
    up(0, 0)
    acc_ref[...] = jnp.zeros_like(acc_ref)
    pair(0, 0, False)

    def two_pairs(i, carry):
        p = 2 * i + 1
        pair(p, 1, True)
        pair(p + 1, 0, True)
        return carry

    lax.fori_loop(0, (n_pairs - 1) // 2, two_pairs, 0)
    last = N_FF_CHUNKS - 1
    gate_act(last, 0, 2)
    ffn_out = acc_ref[...] + (down(last - 2, 0) + down(last - 1, 1) + down(last, 2))
    y_ref[0] = _rms(x_ref[0] + g2 * ffn_out, fg_ref[...])


def _ffn(x1, mod3, per_batch_mod, wts, tm, on_grid):
    B0, T0, _ = x1.shape
    if tm > T0:
        assert not per_batch_mod and not on_grid and tm % T0 == 0 and B0 % (tm // T0) == 0
        x1 = x1.reshape(B0 * T0 // tm, tm, D_MODEL)
    B, T, _ = x1.shape
    nt = T // tm
    halo = GRID_W if on_grid else 0
    te = tm + 2 * halo
    tok = pl.BlockSpec((1, tm, D_MODEL), lambda b, t: (b, t, 0))
    full = lambda a: pl.BlockSpec(a.shape, lambda b, t: (0,) * a.ndim)
    mod_map = (lambda b, t: (b, 0, 0)) if per_batch_mod else (lambda b, t: (0, 0, 0))
    in_specs = [tok]
    args = [x1]
    if on_grid:
        r = tm // GRID_W
        in_specs += [pl.BlockSpec((1, GRID_W, D_MODEL), lambda b, t: (b, jnp.maximum(t * r - 1, 0), 0)),
                     pl.BlockSpec((1, GRID_W, D_MODEL), lambda b, t: (b, jnp.minimum((t + 1) * r, T // GRID_W - 1), 0))]
        args += [x1, x1]
    in_specs += [pl.BlockSpec((1, 1, N_MOD * D_MODEL), mod_map)] + [full(w) for w in wts]
    args += [mod3] + list(wts)
    y = pl.pallas_call(
        functools.partial(_ffn_kernel, tm=tm, nt=nt, on_grid=on_grid, seq_len=T0),
        grid=(B, nt),
        in_specs=in_specs,
        out_specs=tok,
        out_shape=jax.ShapeDtypeStruct((B, T, D_MODEL), f32),
        scratch_shapes=[pltpu.VMEM((te, D_MODEL), bf16)]
        + [pltpu.VMEM((te, FF_CHUNK), f32)] * 2 + [pltpu.VMEM((tm, FF_CHUNK), f32)] * 2
        + [pltpu.VMEM((te, FF_CHUNK), f32)] * 2 + [pltpu.VMEM((tm, FF_CHUNK), bf16)] * 4
        + [pltpu.VMEM((tm, D_MODEL), f32)],
        compiler_params=pltpu.CompilerParams(dimension_semantics=("arbitrary", "arbitrary"),
                                             vmem_limit_bytes=VMEM_LIMIT),
        name="ffn_grid" if on_grid else "ffn_ctx",
    )(*args)
    return y.reshape(B0, T0, D_MODEL)


TOKEN_TILE = 512
MLSTM_TOKENS = 1024


def _tiles(T, per_batch_mod):
    span = TOKEN_TILE if not per_batch_mod else min(TOKEN_TILE, T)
    return span, span, span, max(1, MLSTM_TOKENS // T)


def _trunk(x, mod3, per_batch_mod, state, on_grid, w):
    tm_proj, tm_mix, tm_ffn, mlstm_nseq = _tiles(x.shape[1], per_batch_mod)
    q, k, v, so, glu, sga, sgb, gates = _in_proj(x, mod3, per_batch_mod, w["norm1_g"], w["in"], tm_proj, *w["in_groups"])
    hn, *new_state = _mlstm(q, k, v, gates, state, mlstm_nseq, state is None)
    x1 = _mix_out(x, mod3, per_batch_mod, hn, so, glu, sga, sgb, w["mix"], tm_mix)
    y = _ffn(x1, mod3, per_batch_mod, w["ffn"], tm_ffn, on_grid)
    return y, new_state


def kernel(x_prompt, x_sample, c, state_C, state_n, state_m, c_ctx, w_ada, b_ada, norm1_g, w_in, b_in,
           mlstm_norm_g, w_proj_a, conv_dw_w, conv_dw_b, conv_ln_g, conv_ln_b, w_proj_b, w_out, norm2_g,
           w_up, ffn_dw_w, ffn_dw_b, w_down, final_norm_g):
    Bp = x_prompt.shape[0]
    Bl = x_sample.shape[0]
    l = 0
    row = lambda a: a.reshape(1, -1)

    ct = jnp.concatenate([c, c_ctx[None, :], jnp.zeros((8 - Bl - 1, D_MODEL), f32)], axis=0).T
    mod, (w_up_b, w_down_b, w_out_b, w_pa_b, w_pb_b) = _adaln_mod(
        ct, w_ada[l], row(b_ada[l]), Bl + 1,
        [(w_up[l], 1, 512), (w_down[l], 0, FF_CHUNK), (w_out[l], 0, 128), (w_proj_a[l], 0, 64), (w_proj_b[l], 0, 64)])
    mod_lat = mod[0:Bl].reshape(Bl, 1, N_MOD * D_MODEL)
    mod_ctx = mod[Bl:Bl + 1].reshape(1, 1, N_MOD * D_MODEL)

    o_g = 4 * D_A
    o_u = o_g + 4 * NH_A
    nh = NH_A

    def gate_cols(a):
        g = a[..., o_g:o_u]
        pad = jnp.zeros(a.shape[:-1] + (GATE_PAD - 4 * nh,), a.dtype)
        return jnp.concatenate([g[..., 0:nh], g[..., 2 * nh:3 * nh], g[..., nh:2 * nh], g[..., 3 * nh:], pad], axis=-1)

    w_in_b = w_in[l].astype(bf16)
    bi = row(b_in[l])
    w_in_parts = (w_in_b, bi[:, :o_g], bi[:, o_u:], gate_cols(w_in_b), gate_cols(bi))
    w_mix = (row(mlstm_norm_g[l]), w_pa_b,
             jnp.pad(conv_dw_w[l], ((0, 1), (0, 0))), row(conv_dw_b[l]), row(conv_ln_g[l]), row(conv_ln_b[l]),
             w_pb_b, w_out_b)
    w_ffn = (row(norm2_g[l]), w_up_b,
             w_down_b.reshape(N_FF_CHUNKS, FF_CHUNK, D_MODEL),
             ffn_dw_w[l].reshape(9, D_FF), row(ffn_dw_b[l]), row(final_norm_g))
    w = {"norm1_g": row(norm1_g[l]), "in": w_in_parts, "in_groups": (o_g, o_u), "mix": w_mix, "ffn": w_ffn}

    y_prompt, (cn, nn, mn) = _trunk(x_prompt, mod_ctx, False, None, False, w)

    c0 = state_C[:, l]
    n0 = state_n[:, l].reshape(Bl, 2 * NH_A, DH_A)
    m0 = jnp.broadcast_to(state_m[:, l].reshape(Bl, 2 * NH_A, 1), (Bl, 2 * NH_A, DH_A))
    y_sample, _ = _trunk(x_sample, mod_lat, True, (c0, n0, m0), True, w)

    new_state_C = cn[:, None]
    new_state_n = nn.reshape(Bp, 1, 2, NH_A, DH_A)
    new_state_m = mn[:, :, 0].reshape(Bp, 1, 2, NH_A)
    return (y_prompt, y_sample, new_state_C, new_state_n, new_state_m)
```

```python
import functools

import jax
import jax.numpy as jnp
from jax import lax
from jax.experimental import pallas as pl
from jax.experimental.pallas import tpu as pltpu

D_MODEL = 1024
D_A = 512
NH_A = 4
DH_A = 128
CHUNK = 128
D_B = 512
CONV_W = 31
CONV_HALO = 16
CONV_ROWS = 64
D_FF = 2816
FF_CHUNK = 256
N_FF_CHUNKS = D_FF // FF_CHUNK
GRID_W = 64
N_MOD = 6
GATE_PAD = 128
EPS = 1e-6
Q_SCALE = DH_A ** -0.5
VMEM_LIMIT = 56 * 1024 * 1024
SUBLANES = 8
LANES = 128

f32 = jnp.float32
bf16 = jnp.bfloat16


def _rms(x, g):
    return x * lax.rsqrt(jnp.mean(x * x, axis=-1, keepdims=True) + EPS) * g


def _sigmoid(x):
    return 1.0 / (1.0 + jnp.exp(-x))


def _log_sigmoid(x):
    return jnp.minimum(x, 0.0) - jnp.log(1.0 + jnp.exp(-jnp.abs(x)))


GELU_C0 = 0.7978845608028654
GELU_C1 = GELU_C0 * 0.044715


def _dot(a, b):
    return jnp.dot(a, b, preferred_element_type=f32)


def _mod_kernel(ct_ref, w_ref, b_ref, o_ref, *, n_rows):
    ct = ct_ref[...]
    st = ct * _sigmoid(ct)
    w = w_ref[...]
    rows = []
    for r in range(n_rows):
        s_col = jnp.broadcast_to(st[:, r:r + 1], (D_MODEL, LANES))
        rows.append(jnp.concatenate(
            [jnp.sum(w[:, c:c + LANES] * s_col, axis=0, keepdims=True) for c in range(0, w.shape[1], LANES)], axis=1))
    rows.append(jnp.zeros((8 - n_rows, w.shape[1]), f32))
    o_ref[...] = jnp.concatenate(rows, axis=0) + b_ref[...]


def _adaln_mod(ct, w_ada, b_ada, n_rows):
    n = w_ada.shape[1]
    tn = 512
    return pl.pallas_call(
        functools.partial(_mod_kernel, n_rows=n_rows),
        grid=(n // tn,),
        in_specs=[
            pl.BlockSpec((D_MODEL, 8), lambda j: (0, 0)),
            pl.BlockSpec((D_MODEL, tn), lambda j: (0, j)),
            pl.BlockSpec((1, tn), lambda j: (0, j)),
        ],
        out_specs=pl.BlockSpec((8, tn), lambda j: (0, j)),
        out_shape=jax.ShapeDtypeStruct((8, n), f32),
        compiler_params=pltpu.CompilerParams(dimension_semantics=("arbitrary",), vmem_limit_bytes=VMEM_LIMIT),
        name="adaln_mod",
    )(ct, w_ada, b_ada)


def _inproj_kernel(*refs, o_a, o_b, n_cast):
    x_ref, mod_ref, g_ref, w_ref, ba_ref, bb_ref, wg_ref, bg_ref = refs[:8]
    cast_src = refs[8:8 + n_cast]
    q_ref, k_ref, v_ref, so_ref, glu_ref, sga_ref, sgb_ref, gates_ref = refs[8 + n_cast:16 + n_cast]
    cast_dst = refs[16 + n_cast:16 + 2 * n_cast]
    wa_ref, wb_ref = refs[16 + 2 * n_cast:]
    for src, dst in zip(cast_src, cast_dst):
        dst[...] = src[...].astype(bf16)

    @pl.when(jnp.logical_and(pl.program_id(0) == 0, pl.program_id(1) == 0))
    def _():
        wa_ref[...] = w_ref[:, 0:o_a]
        wb_ref[...] = w_ref[:, o_b:]

    x = x_ref[0]
    mod = mod_ref[0]
    sh1 = mod[:, 0:D_MODEL]
    sc1 = mod[:, D_MODEL:2 * D_MODEL]
    h = (_rms(x, g_ref[...]) * (1.0 + sc1) + sh1).astype(bf16)

    def proj(w_ref, b_ref, lo, n):
        return _dot(h, w_ref[:, lo:lo + n]) + b_ref[:, lo:lo + n]

    q_ref[0] = (proj(wa_ref, ba_ref, 0, D_A) * Q_SCALE).astype(bf16)
    k_ref[0] = proj(wa_ref, ba_ref, D_A, D_A).astype(bf16)
    v_ref[0] = proj(wa_ref, ba_ref, 2 * D_A, D_A).astype(bf16)
    so_ref[0] = _sigmoid(proj(wa_ref, ba_ref, 3 * D_A, D_A)).astype(bf16)
    glu_ref[0] = proj(wb_ref, bb_ref, 0, D_B) * _sigmoid(proj(wb_ref, bb_ref, D_B, D_B))
    sga_ref[0] = _sigmoid(proj(wb_ref, bb_ref, 2 * D_B, D_MODEL)).astype(bf16)
    sgb_ref[0] = _sigmoid(proj(wb_ref, bb_ref, 2 * D_B + D_MODEL, D_MODEL)).astype(bf16)
    gates_ref[0] = proj(wg_ref, bg_ref, 0, GATE_PAD)


def _in_proj(x, mod3, per_batch_mod, norm_g, wts, tm, o_a, o_b, cast_weights=()):
    B0, T0, _ = x.shape
    if tm > T0:
        assert not per_batch_mod and tm % T0 == 0 and B0 % (tm // T0) == 0
        x = x.reshape(B0 * T0 // tm, tm, D_MODEL)
    B, T, _ = x.shape
    nt = T // tm
    tok = lambda n: pl.BlockSpec((1, tm, n), lambda b, t: (b, t, 0))
    full = lambda a: pl.BlockSpec(a.shape, lambda b, t: (0,) * a.ndim)
    mod_map = (lambda b, t: (b, 0, 0)) if per_batch_mod else (lambda b, t: (0, 0, 0))
    sds = lambda n, dt: jax.ShapeDtypeStruct((B, T, n), dt)
    n_b = wts[0].shape[1] - o_b

    def walk(a, block):
        nblk = a.shape[0] // block
        assert nblk * block == a.shape[0] and nblk <= B * nt
        return pl.BlockSpec((block, a.shape[1]), lambda b, t: (jnp.minimum(b * nt + t, nblk - 1), 0))

    cast_specs = [walk(a, blk) for a, blk in cast_weights]
    outs = pl.pallas_call(
        functools.partial(_inproj_kernel, o_a=o_a, o_b=o_b, n_cast=len(cast_weights)),
        grid=(B, nt),
        in_specs=[tok(D_MODEL), pl.BlockSpec((1, 1, N_MOD * D_MODEL), mod_map), full(norm_g)]
        + [full(w) for w in wts] + cast_specs,
        out_specs=[tok(D_A), tok(D_A), tok(D_A), tok(D_A), tok(D_B), tok(D_MODEL), tok(D_MODEL), tok(GATE_PAD)]
        + cast_specs,
        out_shape=[sds(D_A, bf16), sds(D_A, bf16), sds(D_A, bf16), sds(D_A, bf16), sds(D_B, f32),
                   sds(D_MODEL, bf16), sds(D_MODEL, bf16), sds(GATE_PAD, f32)]
        + [jax.ShapeDtypeStruct(a.shape, bf16) for a, _ in cast_weights],
        scratch_shapes=[pltpu.VMEM((D_MODEL, o_a), bf16), pltpu.VMEM((D_MODEL, n_b), bf16)],
        compiler_params=pltpu.CompilerParams(dimension_semantics=("arbitrary", "arbitrary"),
                                             vmem_limit_bytes=VMEM_LIMIT),
        name="in_proj",
    )(x, mod3, norm_g, *wts, *[a for a, _ in cast_weights])
    return [o.reshape(B0, T0, o.shape[-1]) for o in outs[:8]], outs[8:]


NEG_BIG = -1e30
ST_U, ST_INTER, ST_EMJ, ST_WKN, ST_DECAY = 0, 8, 16, 24, 32
ST_ROWS = 40


def _chunk_scan(x, op, fill, prefix, lane, width):
    k = 1
    while k < CHUNK:
        if prefix:
            shifted = jnp.where(lane >= k, pltpu.roll(x, k, axis=1), fill)
        else:
            shifted = jnp.where(lane < CHUNK - k, pltpu.roll(x, width - k, axis=1), fill)
        x = op(x, shifted)
        k *= 2
    return x


def _mlstm_kernel(*refs, nseq, nc, state_in, state_out):
    refs = list(refs)
    q_ref, k_ref, v_ref, g_ref = refs[:4]
    del refs[:4]
    if state_in:
        c0_ref, n0_ref, m0_ref = refs[:3]
        del refs[:3]
    hn_ref = refs.pop(0)
    if state_out:
        cn_ref, nn_ref, mn_ref = refs[:3]
        del refs[:3]
    st_ref, wt_ref, vt_ref, qt_ref, u_ref, cp_ref, npf_ref, npb_ref, cst_ref, nsf_ref, nsb_ref = refs
    L = CHUNK
    nct = nseq * nc
    T = nct * L
    NR = 2 * NH_A

    lane = jnp.bitwise_and(lax.broadcasted_iota(jnp.int32, (NR, T), 1), L - 1)
    is_fwd = lax.broadcasted_iota(jnp.int32, (NR, T), 0) < NH_A
    is_fwd_c = lax.broadcasted_iota(jnp.int32, (NR, L), 0) < NH_A
    i_parts, f_parts = [], []
    for c in range(nct):
        gt = g_ref[0, c * L:(c + 1) * L, :].T
        i_parts.append(gt[0:NR])
        f_parts.append(gt[NR:2 * NR])
    ig = jnp.concatenate(i_parts, axis=1)
    lf = _log_sigmoid(jnp.concatenate(f_parts, axis=1))
    scan = functools.partial(_chunk_scan, lane=lane, width=T)
    ps = scan(lf, jnp.add, 0.0, True)
    ss = scan(lf, jnp.add, 0.0, False)
    b = jnp.where(is_fwd, ps, ss)
    btot = ps + ss - lf
    w = ig - b
    cmw = jnp.where(is_fwd, scan(w, jnp.maximum, -jnp.inf, True), scan(w, jnp.maximum, -jnp.inf, False))
    wk = btot - b + ig
    a = jnp.maximum(scan(wk, jnp.maximum, -jnp.inf, True), scan(wk, jnp.maximum, -jnp.inf, False))
    chunk_lanes = lambda cg: slice(cg * L, (cg + 1) * L)
    pad_rows = jnp.zeros((L - NR, L), f32)
    for s in range(nseq):
        ms = [m0_ref[s] if state_in else jnp.zeros((NR, L), f32)]
        for t in range(nc):
            cf, cb = chunk_lanes(s * nc + t), chunk_lanes(s * nc + nc - 1 - t)
            bt = jnp.where(is_fwd_c, btot[:, cf], btot[:, cb])
            at = jnp.where(is_fwd_c, a[:, cf], a[:, cb])
            ms.append(jnp.maximum(bt + ms[-1], at))
        if state_out:
            mn_ref[s] = ms[nc]
        for c in range(nc):
            cg = s * nc + c
            sl = chunk_lanes(cg)
            m_prev = jnp.where(is_fwd_c, ms[c], ms[nc - 1 - c])
            m_new = jnp.where(is_fwd_c, ms[c + 1], ms[nc - c])
            mx = jnp.maximum(m_prev, cmw[:, sl])
            st_ref[cg, ST_U:ST_U + NR] = -mx
            st_ref[cg, ST_INTER:ST_INTER + NR] = jnp.exp(m_prev - mx)
            st_ref[cg, ST_EMJ:ST_EMJ + NR] = jnp.exp(-mx - b[:, sl])
            st_ref[cg, ST_WKN:ST_WKN + NR] = jnp.exp(wk[:, sl] - m_new)
            st_ref[cg, ST_DECAY:ST_DECAY + NR] = jnp.exp(btot[:, sl] + m_prev - m_new)
            wt_ref[cg] = jnp.concatenate([w[:, sl], pad_rows], axis=0).T

    row = lax.broadcasted_iota(jnp.int32, (L, L), 0)
    col = lax.broadcasted_iota(jnp.int32, (L, L), 1)
    masks = (row <= col, row >= col)
    first_of8 = lax.broadcasted_iota(jnp.int32, (8, L), 0) == 0

    def row_tile(x):
        return jnp.where(first_of8, x, 0.0)

    head_lanes = [slice(h * DH_A, (h + 1) * DH_A) for h in range(NH_A)]

    def increments(j, carry):
        rows = pl.ds(pl.multiple_of(j * L, L), L)
        for h in range(NH_A):
            kh = k_ref[0, rows, head_lanes[h]]
            vt = v_ref[0, rows, head_lanes[h]].astype(f32).T
            vt_ref[j, h] = vt.astype(bf16)
            qt_ref[j, h] = q_ref[0, rows, head_lanes[h]].astype(f32).T.astype(bf16)
            wf = st_ref[j, ST_WKN + h:ST_WKN + h + 1, :]
            wb = st_ref[j, ST_WKN + NH_A + h:ST_WKN + NH_A + h + 1, :]
            lhs = jnp.concatenate([vt * wf, vt * wb, row_tile(wf), row_tile(wb)], axis=0).astype(bf16)
            u_ref[j, h] = _dot(lhs, kh)
        return carry

    lax.fori_loop(0, nct, increments, 0, unroll=4)

    seq_heads = [(s, h) for s in range(nseq) for h in range(NH_A)]
    for s, h in seq_heads:
        i = s * NH_A + h
        if state_in:
            cst_ref[i, 0:DH_A] = c0_ref[s, 0, h].T
            cst_ref[i, DH_A:2 * DH_A] = c0_ref[s, 1, h].T
            nsf_ref[i] = row_tile(n0_ref[s, h:h + 1, :])
            nsb_ref[i] = row_tile(n0_ref[s, NH_A + h:NH_A + h + 1, :])
        else:
            cst_ref[i] = jnp.zeros((2 * DH_A, DH_A), f32)
            nsf_ref[i] = jnp.zeros((8, DH_A), f32)
            nsb_ref[i] = jnp.zeros((8, DH_A), f32)

    def recur(t, carry):
        for s, h in seq_heads:
            i = s * NH_A + h
            jf = s * nc + t
            jb = s * nc + nc - 1 - t
            dec_f = st_ref[jf, ST_DECAY + h:ST_DECAY + h + 1, :]
            dec_b = st_ref[jb, ST_DECAY + NH_A + h:ST_DECAY + NH_A + h + 1, :]
            c_f = cst_ref[i, 0:DH_A]
            c_b = cst_ref[i, DH_A:2 * DH_A]
            cp_ref[jf, h, 0:DH_A] = c_f.astype(bf16)
            cp_ref[jb, h, DH_A:2 * DH_A] = c_b.astype(bf16)
            cst_ref[i, 0:DH_A] = dec_f * c_f + u_ref[jf, h, 0:DH_A]
            cst_ref[i, DH_A:2 * DH_A] = dec_b * c_b + u_ref[jb, h, DH_A:2 * DH_A]
            n_f = nsf_ref[i]
            n_b = nsb_ref[i]
            npf_ref[jf, h] = n_f
            npb_ref[jb, h] = n_b
            nsf_ref[i] = dec_f * n_f + u_ref[jf, h, 2 * DH_A:2 * DH_A + 8]
            nsb_ref[i] = dec_b * n_b + u_ref[jb, h, 2 * DH_A + 8:2 * DH_A + 16]
        return carry

    lax.fori_loop(0, nc, recur, 0)
    for s, h in seq_heads if state_out else ():
        i = s * NH_A + h
        cn_ref[s, 0, h] = cst_ref[i, 0:DH_A].T
        cn_ref[s, 1, h] = cst_ref[i, DH_A:2 * DH_A].T
        nn_ref[s, h:h + 1, :] = nsf_ref[i, 0:1]
        nn_ref[s, NH_A + h:NH_A + h + 1, :] = nsb_ref[i, 0:1]

    def outputs(j, carry):
        rows = pl.ds(pl.multiple_of(j * L, L), L)
        wt = wt_ref[j]
        heads = range(NH_A)
        kqs = []
        for h in heads:
            kh = k_ref[0, rows, head_lanes[h]]
            n_rows = jnp.concatenate([npf_ref[j, h], npb_ref[j, h]], axis=0).astype(bf16)
            kqs.append(_dot(jnp.concatenate([kh, n_rows, cp_ref[j, h]], axis=0), qt_ref[j, h]))
        decays = [[jnp.exp(jnp.where(masks[d], wt[:, NH_A * d + h:NH_A * d + h + 1]
                                     + st_ref[j, ST_U + NH_A * d + h:ST_U + NH_A * d + h + 1, :], NEG_BIG))
                   for d in range(2)] for h in heads]
        s_sums, h_ts = [], []
        for h in heads:
            kq = kqs[h]
            s_sum = None
            h_t = None
            for d in range(2):
                r = NH_A * d + h
                inter = st_ref[j, ST_INTER + r:ST_INTER + r + 1, :]
                emj = st_ref[j, ST_EMJ + r:ST_EMJ + r + 1, :]
                s_t = kq[0:L] * decays[h][d]
                qn = kq[L + 8 * d:L + 8 * d + 1]
                den = inter * qn + jnp.sum(s_t, axis=0, keepdims=True)
                rr = 1.0 / jnp.maximum(jnp.abs(den), emj)
                s_sum = s_t * rr if d == 0 else s_sum + s_t * rr
                part = kq[L + 16 + d * DH_A:L + 16 + (d + 1) * DH_A] * (inter * rr)
                h_t = part if d == 0 else h_t + part
            s_sums.append(s_sum.astype(bf16))
            h_ts.append(h_t)
        h_ts = [h_ts[h] + _dot(vt_ref[j, h], s_sums[h]) for h in heads]
        h_ts = [x * lax.rsqrt(jnp.mean(x * x, axis=0, keepdims=True) + EPS) for x in h_ts]
        for h in heads:
            hn_ref[0, rows, head_lanes[h]] = h_ts[h].T.astype(bf16)
        return carry

    lax.fori_loop(0, nct, outputs, 0, unroll=4)


def _mlstm(q, k, v, gates, state, nseq, state_out):
    B, T, _ = q.shape
    nc = T // CHUNK
    nct = nseq * nc
    G = B // nseq
    fold = lambda a: a.reshape(G, nseq * T, a.shape[-1])
    seq = lambda n: pl.BlockSpec((1, nseq * T, n), lambda b: (b, 0, 0))
    st_c = pl.BlockSpec((nseq, 2, NH_A, DH_A, DH_A), lambda b: (b, 0, 0, 0, 0))
    st_v = pl.BlockSpec((nseq, 2 * NH_A, DH_A), lambda b: (b, 0, 0))
    state_specs = [st_c, st_v, st_v]
    state_shapes = [jax.ShapeDtypeStruct((B, 2, NH_A, DH_A, DH_A), f32),
                    jax.ShapeDtypeStruct((B, 2 * NH_A, DH_A), f32),
                    jax.ShapeDtypeStruct((B, 2 * NH_A, DH_A), f32)]
    outs = pl.pallas_call(
        functools.partial(_mlstm_kernel, nseq=nseq, nc=nc, state_in=state is not None, state_out=state_out),
        grid=(G,),
        in_specs=[seq(D_A), seq(D_A), seq(D_A), seq(GATE_PAD)] + (state_specs if state is not None else []),
        out_specs=[seq(D_A)] + (state_specs if state_out else []),
        scratch_shapes=[pltpu.VMEM((nct, ST_ROWS, CHUNK), f32), pltpu.VMEM((nct, CHUNK, GATE_PAD), f32),
                        pltpu.VMEM((nct, NH_A, DH_A, CHUNK), bf16), pltpu.VMEM((nct, NH_A, DH_A, CHUNK), bf16),
                        pltpu.VMEM((nct, NH_A, 2 * DH_A + 16, DH_A), f32),
                        pltpu.VMEM((nct, NH_A, 2 * DH_A, DH_A), bf16),
                        pltpu.VMEM((nct, NH_A, 8, DH_A), f32), pltpu.VMEM((nct, NH_A, 8, DH_A), f32),
                        pltpu.VMEM((nseq * NH_A, 2 * DH_A, DH_A), f32),
                        pltpu.VMEM((nseq * NH_A, 8, DH_A), f32), pltpu.VMEM((nseq * NH_A, 8, DH_A), f32)],
        out_shape=[jax.ShapeDtypeStruct((G, nseq * T, D_A), bf16)] + (state_shapes if state_out else []),
        compiler_params=pltpu.CompilerParams(dimension_semantics=("arbitrary",), vmem_limit_bytes=VMEM_LIMIT),
        name="mlstm",
    )(fold(q), fold(k), fold(v), fold(gates), *(state if state is not None else ()))
    return (outs[0].reshape(B, T, D_A),) + tuple(outs[1:])


def _mix_kernel(x_ref, mod_ref, hn_ref, so_ref, glu_ref, glu_prev_ref, glu_next_ref, sga_ref, sgb_ref,
                ng_ref, wpa_ref, cw_ref, cb_ref, lng_ref, lnb_ref, wpb_ref, wout_ref,
                x1_ref, ext_ref, zs_ref, *, tm, nt, nsub):
    t = pl.program_id(1)
    mod = mod_ref[0]
    g1 = mod[:, 2 * D_MODEL:3 * D_MODEL]

    a_in = (so_ref[0].astype(f32) * (hn_ref[0].astype(f32) * ng_ref[...])).astype(bf16)
    branch_a = _dot(a_in, wpa_ref[...])

    H = CONV_HALO
    SUB, LANE = SUBLANES, LANES
    tsub = tm // nsub
    te = tsub + 2 * H
    for s in range(nsub):
        inner = nsub == 1
        ext_ref[0, s, 0:H, :] = jnp.where(t > 0, glu_prev_ref[0], 0.0) if inner else jnp.zeros((H, D_B), f32)
        ext_ref[0, s, H:H + tsub, :] = glu_ref[0, s * tsub:(s + 1) * tsub, :]
        ext_ref[0, s, H + tsub:te, :] = jnp.where(t < nt - 1, glu_next_ref[0], 0.0) if inner else jnp.zeros((H, D_B), f32)
    for s in range(nsub):
        for c in range(D_B // LANE):
            lanes = slice(c * LANE, (c + 1) * LANE)
            base = ext_ref[0, s, :, lanes]
            for k in range(1, SUB):
                ext_ref[k, s, :, lanes] = pltpu.roll(base, te - k, axis=0)
    off = H - CONV_W // 2
    for s in range(nsub):
        for r0 in range(0, tsub, CONV_ROWS):
            parts = []
            for c in range(D_B // LANE):
                lanes = slice(c * LANE, (c + 1) * LANE)
                acc = jnp.broadcast_to(cb_ref[:, lanes], (CONV_ROWS, LANE))
                for w in range(CONV_W):
                    k, a = (off + w) % SUB, (off + w) // SUB * SUB
                    acc = acc + ext_ref[k, s, r0 + a:r0 + a + CONV_ROWS, lanes] * cw_ref[w:w + 1, lanes]
                parts.append(acc)
            z = jnp.concatenate(parts, axis=1)
            mu = jnp.mean(z, axis=-1, keepdims=True)
            zc = z - mu
            var = jnp.mean(zc * zc, axis=-1, keepdims=True)
            zn = zc * lax.rsqrt(var + EPS) * lng_ref[...] + lnb_ref[...]
            zs_ref[s * tsub + r0:s * tsub + r0 + CONV_ROWS, :] = (zn * _sigmoid(zn)).astype(bf16)
    branch_b = _dot(zs_ref[...], wpb_ref[...])

    merged = sga_ref[0].astype(f32) * branch_a + sgb_ref[0].astype(f32) * branch_b
    x1_ref[0] = x_ref[0] + g1 * _dot(merged.astype(bf16), wout_ref[...])


def _mix_out(x, mod3, per_batch_mod, hn, so, glu, sga, sgb, wts, tm):
    B0, T0, _ = x.shape
    nsub = 1
    if tm > T0:
        assert not per_batch_mod and tm % T0 == 0 and B0 % (tm // T0) == 0
        nsub = tm // T0
        x, hn, so, glu, sga, sgb = (a.reshape(B0 // nsub, tm, a.shape[-1]) for a in (x, hn, so, glu, sga, sgb))
    B, T, _ = x.shape
    nt = T // tm
    H = CONV_HALO
    r = tm // H
    tok = lambda n: pl.BlockSpec((1, tm, n), lambda b, t: (b, t, 0))
    full = lambda a: pl.BlockSpec(a.shape, lambda b, t: (0,) * a.ndim)
    mod_map = (lambda b, t: (b, 0, 0)) if per_batch_mod else (lambda b, t: (0, 0, 0))
    prev = pl.BlockSpec((1, H, D_B), lambda b, t: (b, jnp.maximum(t * r - 1, 0), 0))
    nxt = pl.BlockSpec((1, H, D_B), lambda b, t: (b, jnp.minimum((t + 1) * r, T // H - 1), 0))
    x1 = pl.pallas_call(
        functools.partial(_mix_kernel, tm=tm, nt=nt, nsub=nsub),
        grid=(B, nt),
        in_specs=[tok(D_MODEL), pl.BlockSpec((1, 1, N_MOD * D_MODEL), mod_map), tok(D_A), tok(D_A),
                  tok(D_B), prev, nxt, tok(D_MODEL), tok(D_MODEL)] + [full(w) for w in wts],
        out_specs=tok(D_MODEL),
        out_shape=jax.ShapeDtypeStruct((B, T, D_MODEL), f32),
        scratch_shapes=[pltpu.VMEM((SUBLANES, nsub, tm // nsub + 2 * H, D_B), f32), pltpu.VMEM((tm, D_B), bf16)],
        compiler_params=pltpu.CompilerParams(dimension_semantics=("arbitrary", "arbitrary"),
                                             vmem_limit_bytes=VMEM_LIMIT),
        name="mix_out",
    )(x, mod3, hn, so, glu, glu, glu, sga, sgb, *wts)
    return x1.reshape(B0, T0, D_MODEL)


def _ffn_kernel(*refs, tm, nt, on_grid, seq_len):
    if on_grid:
        (x_ref, xp_ref, xn_ref, mod_ref, g2n_ref, wu_ref, wd_ref, cw_ref, cb_ref, fg_ref,
         y_ref, h_ref, g0_ref, g1_ref, v0_ref, v1_ref, gl_ref, gr_ref, a0_ref, a1_ref, a2_ref, a3_ref, acc_ref) = refs
    else:
        (x_ref, mod_ref, g2n_ref, wu_ref, wd_ref, cw_ref, cb_ref, fg_ref,
         y_ref, h_ref, g0_ref, g1_ref, v0_ref, v1_ref, gl_ref, gr_ref, a0_ref, a1_ref, a2_ref, a3_ref, acc_ref) = refs
    gbufs, vbufs, acts = (g0_ref, g1_ref), (v0_ref, v1_ref), (a0_ref, a1_ref, a2_ref, a3_ref)
    t = pl.program_id(1)
    halo = GRID_W if on_grid else 0
    te = tm + 2 * halo
    mod = mod_ref[0]
    sh2 = mod[:, 3 * D_MODEL:4 * D_MODEL]
    sc2 = mod[:, 4 * D_MODEL:5 * D_MODEL]
    g2 = mod[:, 5 * D_MODEL:6 * D_MODEL]

    def norm_mod(x):
        return _rms(x, g2n_ref[...]) * (1.0 + sc2) + sh2

    h_ref[halo:halo + tm, :] = norm_mod(x_ref[0]).astype(bf16)
    if on_grid:
        h_ref[0:halo, :] = jnp.where(t > 0, norm_mod(xp_ref[0]), 0.0).astype(bf16)
        h_ref[halo + tm:te, :] = jnp.where(t < nt - 1, norm_mod(xn_ref[0]), 0.0).astype(bf16)

    SUB, LANE = SUBLANES, LANES
    seg = GRID_W if on_grid else seq_len
    sub_row = lax.broadcasted_iota(jnp.int32, (SUB, LANE), 0)
    first_row = sub_row == 0
    last_row = sub_row == SUB - 1
    row_taps = (0, 1, 2) if on_grid else (1,)

    def ff_cols(fc, base):
        if isinstance(fc, int):
            return slice(base + fc * FF_CHUNK, base + (fc + 1) * FF_CHUNK)
        return pl.ds(pl.multiple_of(base + fc * FF_CHUNK, 128), FF_CHUNK)

    def up(fc, slot):
        gbufs[slot][...] = _dot(h_ref[...], wu_ref[:, ff_cols(fc, 0)])
        vbufs[slot][...] = _dot(h_ref[halo:halo + tm, :], wu_ref[:, ff_cols(fc, D_FF)])

    def gate_act(fc, slot, aslot):
        cw = cw_ref[:, ff_cols(fc, 0)]
        cb = cb_ref[:, ff_cols(fc, 0)]
        for c in range(FF_CHUNK // LANE):
            lanes = slice(c * LANE, (c + 1) * LANE)
            for s0 in range(0, te, seg):
                gate = gbufs[slot][s0:s0 + seg, lanes]
                g_l = pltpu.roll(gate, 1, axis=0)
                g_r = pltpu.roll(gate, seg - 1, axis=0)
                gl_ref[s0:s0 + SUB, lanes] = jnp.where(first_row, 0.0, g_l[0:SUB])
                gl_ref[s0 + SUB:s0 + seg, lanes] = g_l[SUB:seg]
                gr_ref[s0:s0 + seg - SUB, lanes] = g_r[0:seg - SUB]
                gr_ref[s0 + seg - SUB:s0 + seg, lanes] = jnp.where(last_row, 0.0, g_r[seg - SUB:seg])
            for r0 in range(0, tm, GRID_W):
                conv = jnp.broadcast_to(cb[:, lanes], (GRID_W, LANE))
                for kh in row_taps:
                    lo = r0 + kh * GRID_W if on_grid else r0
                    conv = conv + gl_ref[lo:lo + GRID_W, lanes] * cw[3 * kh:3 * kh + 1, lanes]
                    conv = conv + gbufs[slot][lo:lo + GRID_W, lanes] * cw[3 * kh + 1:3 * kh + 2, lanes]
                    conv = conv + gr_ref[lo:lo + GRID_W, lanes] * cw[3 * kh + 2:3 * kh + 3, lanes]
                gelu = conv * (0.5 + 0.5 * jnp.tanh(conv * (GELU_C0 + GELU_C1 * (conv * conv))))
                acts[aslot][r0:r0 + GRID_W, lanes] = (gelu * vbufs[slot][r0:r0 + GRID_W, lanes]).astype(bf16)

    def down(fc, aslot):
        return _dot(acts[aslot][...], wd_ref[fc])


    def pair(p, parity, with_down):
        fc = 2 * p
        wr, rd = 2 * parity, 2 * (1 - parity)
        if with_down:
            acc_ref[...] += down(fc - 2, rd) + down(fc - 1, rd + 1)
        up(fc + 1, 1)
        gate_act(fc, 0, wr)
        up(fc + 2, 0)
        gate_act(fc + 1, 1, wr + 1)

    n_pairs = (N_FF_CHUNKS - 1) // 2
    assert N_FF_CHUNKS == 2 * n_pairs + 1 and n_pairs % 2 == 1
    up(0, 0)
    acc_ref[...] = jnp.zeros_like(acc_ref)
    pair(0, 0, False)

    def two_pairs(i, carry):
        p = 2 * i + 1
        pair(p, 1, True)
        pair(p + 1, 0, True)
        return carry

    lax.fori_loop(0, (n_pairs - 1) // 2, two_pairs, 0)
    last = N_FF_CHUNKS - 1
    gate_act(last, 0, 2)
    ffn_out = acc_ref[...] + (down(last - 2, 0) + down(last - 1, 1) + down(last, 2))
    y_ref[0] = _rms(x_ref[0] + g2 * ffn_out, fg_ref[...])


def _ffn(x1, mod3, per_batch_mod, wts, tm, on_grid):
    B0, T0, _ = x1.shape
    if tm > T0:
        assert not per_batch_mod and not on_grid and tm % T0 == 0 and B0 % (tm // T0) == 0
        x1 = x1.reshape(B0 * T0 // tm, tm, D_MODEL)
    B, T, _ = x1.shape
    nt = T // tm
    halo = GRID_W if on_grid else 0
    te = tm + 2 * halo
    tok = pl.BlockSpec((1, tm, D_MODEL), lambda b, t: (b, t, 0))
    full = lambda a: pl.BlockSpec(a.shape, lambda b, t: (0,) * a.ndim)
    mod_map = (lambda b, t: (b, 0, 0)) if per_batch_mod else (lambda b, t: (0, 0, 0))
    in_specs = [tok]
    args = [x1]
    if on_grid:
        r = tm // GRID_W
        in_specs += [pl.BlockSpec((1, GRID_W, D_MODEL), lambda b, t: (b, jnp.maximum(t * r - 1, 0), 0)),
                     pl.BlockSpec((1, GRID_W, D_MODEL), lambda b, t: (b, jnp.minimum((t + 1) * r, T // GRID_W - 1), 0))]
        args += [x1, x1]
    in_specs += [pl.BlockSpec((1, 1, N_MOD * D_MODEL), mod_map)] + [full(w) for w in wts]
    args += [mod3] + list(wts)
    y = pl.pallas_call(
        functools.partial(_ffn_kernel, tm=tm, nt=nt, on_grid=on_grid, seq_len=T0),
        grid=(B, nt),
        in_specs=in_specs,
        out_specs=tok,
        out_shape=jax.ShapeDtypeStruct((B, T, D_MODEL), f32),
        scratch_shapes=[pltpu.VMEM((te, D_MODEL), bf16)]
        + [pltpu.VMEM((te, FF_CHUNK), f32)] * 2 + [pltpu.VMEM((tm, FF_CHUNK), f32)] * 2
        + [pltpu.VMEM((te, FF_CHUNK), f32)] * 2 + [pltpu.VMEM((tm, FF_CHUNK), bf16)] * 4
        + [pltpu.VMEM((tm, D_MODEL), f32)],
        compiler_params=pltpu.CompilerParams(dimension_semantics=("arbitrary", "arbitrary"),
                                             vmem_limit_bytes=VMEM_LIMIT),
        name="ffn_grid" if on_grid else "ffn_ctx",
    )(*args)
    return y.reshape(B0, T0, D_MODEL)


TOKEN_TILE = 512
MLSTM_TOKENS = 1024


def _tiles(T, per_batch_mod):
    span = TOKEN_TILE if not per_batch_mod else min(TOKEN_TILE, T)
    return span, span, span, max(1, MLSTM_TOKENS // T)


def _trunk(x, mod3, per_batch_mod, state, on_grid, w, cast_weights=()):
    tm_proj, tm_mix, tm_ffn, mlstm_nseq = _tiles(x.shape[1], per_batch_mod)
    (q, k, v, so, glu, sga, sgb, gates), casts = _in_proj(x, mod3, per_batch_mod, w["norm1_g"], w["in"], tm_proj,
                                                          *w["in_groups"], cast_weights=cast_weights)
    if cast_weights:
        w = dict(w, mix=w["mix"](casts), ffn=w["ffn"](casts))
    hn, *new_state = _mlstm(q, k, v, gates, state, mlstm_nseq, state is None)
    x1 = _mix_out(x, mod3, per_batch_mod, hn, so, glu, sga, sgb, w["mix"], tm_mix)
    y = _ffn(x1, mod3, per_batch_mod, w["ffn"], tm_ffn, on_grid)
    return y, new_state, w


def kernel(x_prompt, x_sample, c, state_C, state_n, state_m, c_ctx, w_ada, b_ada, norm1_g, w_in, b_in,
           mlstm_norm_g, w_proj_a, conv_dw_w, conv_dw_b, conv_ln_g, conv_ln_b, w_proj_b, w_out, norm2_g,
           w_up, ffn_dw_w, ffn_dw_b, w_down, final_norm_g):
    Bp = x_prompt.shape[0]
    Bl = x_sample.shape[0]
    l = 0
    row = lambda a: a.reshape(1, -1)

    ct = jnp.concatenate([c, c_ctx[None, :], jnp.zeros((8 - Bl - 1, D_MODEL), f32)], axis=0).T
    mod = _adaln_mod(ct, w_ada[l], row(b_ada[l]), Bl + 1)
    mod_lat = mod[0:Bl].reshape(Bl, 1, N_MOD * D_MODEL)
    mod_ctx = mod[Bl:Bl + 1].reshape(1, 1, N_MOD * D_MODEL)

    o_g = 4 * D_A
    o_u = o_g + 4 * NH_A
    nh = NH_A

    def gate_cols(a):
        g = a[..., o_g:o_u]
        pad = jnp.zeros(a.shape[:-1] + (GATE_PAD - 4 * nh,), a.dtype)
        return jnp.concatenate([g[..., 0:nh], g[..., 2 * nh:3 * nh], g[..., nh:2 * nh], g[..., 3 * nh:], pad], axis=-1)

    w_in_b = w_in[l].astype(bf16)
    bi = row(b_in[l])
    w_in_parts = (w_in_b, bi[:, :o_g], bi[:, o_u:], gate_cols(w_in_b), gate_cols(bi))
    cast_weights = [(w_up[l], D_MODEL // 8), (w_down[l], D_FF // 8), (w_out[l], D_MODEL // 8),
                    (w_proj_a[l], D_A // 8), (w_proj_b[l], D_B // 8)]

    def w_mix(casts):
        return (row(mlstm_norm_g[l]), casts[3],
                jnp.pad(conv_dw_w[l], ((0, 1), (0, 0))), row(conv_dw_b[l]), row(conv_ln_g[l]), row(conv_ln_b[l]),
                casts[4], casts[2])

    def w_ffn(casts):
        return (row(norm2_g[l]), casts[0], casts[1].reshape(N_FF_CHUNKS, FF_CHUNK, D_MODEL),
                ffn_dw_w[l].reshape(9, D_FF), row(ffn_dw_b[l]), row(final_norm_g))

    w = {"norm1_g": row(norm1_g[l]), "in": w_in_parts, "in_groups": (o_g, o_u), "mix": w_mix, "ffn": w_ffn}

    y_prompt, (cn, nn, mn), w = _trunk(x_prompt, mod_ctx, False, None, False, w, cast_weights)

    c0 = state_C[:, l]
    n0 = state_n[:, l].reshape(Bl, 2 * NH_A, DH_A)
    m0 = jnp.broadcast_to(state_m[:, l].reshape(Bl, 2 * NH_A, 1), (Bl, 2 * NH_A, DH_A))
    y_sample, _, _ = _trunk(x_sample, mod_lat, True, (c0, n0, m0), True, w)

    new_state_C = cn[:, None]
    new_state_n = nn.reshape(Bp, 1, 2, NH_A, DH_A)
    new_state_m = mn[:, :, 0].reshape(Bp, 1, 2, NH_A)
    return (y_prompt, y_sample, new_state_C, new_state_n, new_state_m)
```

```python
import functools

import jax
import jax.numpy as jnp
from jax import lax
from jax.experimental import pallas as pl
from jax.experimental.pallas import tpu as pltpu

D_MODEL = 1024
D_A = 512
NH_A = 4
DH_A = 128
CHUNK = 128
D_B = 512
CONV_W = 31
CONV_HALO = 16
CONV_ROWS = 64
D_FF = 2816
FF_CHUNK = 256
N_FF_CHUNKS = D_FF // FF_CHUNK
GRID_W = 64
N_MOD = 6
GATE_PAD = 128
EPS = 1e-6
Q_SCALE = DH_A ** -0.5
VMEM_LIMIT = 56 * 1024 * 1024
SUBLANES = 8
LANES = 128

f32 = jnp.float32
bf16 = jnp.bfloat16


def _rms(x, g):
    return x * lax.rsqrt(jnp.mean(x * x, axis=-1, keepdims=True) + EPS) * g


def _sigmoid(x):
    return 1.0 / (1.0 + jnp.exp(-x))


def _log_sigmoid(x):
    return jnp.minimum(x, 0.0) - jnp.log(1.0 + jnp.exp(-jnp.abs(x)))


GELU_C0 = 0.7978845608028654
GELU_C1 = GELU_C0 * 0.044715


def _dot(a, b):
    return jnp.dot(a, b, preferred_element_type=f32)


def _mod_kernel(ct_ref, w_ref, b_ref, o_ref, *, n_rows):
    ct = ct_ref[...]
    st = ct * _sigmoid(ct)
    w = w_ref[...]
    rows = []
    for r in range(n_rows):
        s_col = jnp.broadcast_to(st[:, r:r + 1], (D_MODEL, LANES))
        rows.append(jnp.concatenate(
            [jnp.sum(w[:, c:c + LANES] * s_col, axis=0, keepdims=True) for c in range(0, w.shape[1], LANES)], axis=1))
    rows.append(jnp.zeros((8 - n_rows, w.shape[1]), f32))
    o_ref[...] = jnp.concatenate(rows, axis=0) + b_ref[...]


def _adaln_mod(ct, w_ada, b_ada, n_rows):
    n = w_ada.shape[1]
    tn = 512
    return pl.pallas_call(
        functools.partial(_mod_kernel, n_rows=n_rows),
        grid=(n // tn,),
        in_specs=[
            pl.BlockSpec((D_MODEL, 8), lambda j: (0, 0)),
            pl.BlockSpec((D_MODEL, tn), lambda j: (0, j)),
            pl.BlockSpec((1, tn), lambda j: (0, j)),
        ],
        out_specs=pl.BlockSpec((8, tn), lambda j: (0, j)),
        out_shape=jax.ShapeDtypeStruct((8, n), f32),
        compiler_params=pltpu.CompilerParams(dimension_semantics=("arbitrary",), vmem_limit_bytes=VMEM_LIMIT),
        name="adaln_mod",
    )(ct, w_ada, b_ada)


def _inproj_kernel(*refs, o_a, o_b, n_cast):
    x_ref, mod_ref, g_ref, w_ref, ba_ref, bb_ref, wg_ref, bg_ref = refs[:8]
    cast_src = refs[8:8 + n_cast]
    q_ref, k_ref, v_ref, so_ref, glu_ref, sga_ref, sgb_ref, gates_ref = refs[8 + n_cast:16 + n_cast]
    cast_dst = refs[16 + n_cast:16 + 2 * n_cast]
    wa_ref, wb_ref = refs[16 + 2 * n_cast:]
    for src, dst in zip(cast_src, cast_dst):
        dst[...] = src[...].astype(bf16)

    @pl.when(jnp.logical_and(pl.program_id(0) == 0, pl.program_id(1) == 0))
    def _():
        wa_ref[...] = w_ref[:, 0:o_a]
        wb_ref[...] = w_ref[:, o_b:]

    x = x_ref[0]
    mod = mod_ref[0]
    sh1 = mod[:, 0:D_MODEL]
    sc1 = mod[:, D_MODEL:2 * D_MODEL]
    h = (_rms(x, g_ref[...]) * (1.0 + sc1) + sh1).astype(bf16)

    def proj(w_ref, b_ref, lo, n):
        return _dot(h, w_ref[:, lo:lo + n]) + b_ref[:, lo:lo + n]

    q_ref[0] = (proj(wa_ref, ba_ref, 0, D_A) * Q_SCALE).astype(bf16)
    k_ref[0] = proj(wa_ref, ba_ref, D_A, D_A).astype(bf16)
    v_ref[0] = proj(wa_ref, ba_ref, 2 * D_A, D_A).astype(bf16)
    so_ref[0] = _sigmoid(proj(wa_ref, ba_ref, 3 * D_A, D_A)).astype(bf16)
    glu_ref[0] = proj(wb_ref, bb_ref, 0, D_B) * _sigmoid(proj(wb_ref, bb_ref, D_B, D_B))
    sga_ref[0] = _sigmoid(proj(wb_ref, bb_ref, 2 * D_B, D_MODEL)).astype(bf16)
    sgb_ref[0] = _sigmoid(proj(wb_ref, bb_ref, 2 * D_B + D_MODEL, D_MODEL)).astype(bf16)
    gates_ref[0] = proj(wg_ref, bg_ref, 0, GATE_PAD)


def _in_proj(x, mod3, per_batch_mod, norm_g, wts, tm, o_a, o_b, cast_weights=()):
    B0, T0, _ = x.shape
    if tm > T0:
        assert not per_batch_mod and tm % T0 == 0 and B0 % (tm // T0) == 0
        x = x.reshape(B0 * T0 // tm, tm, D_MODEL)
    B, T, _ = x.shape
    nt = T // tm
    tok = lambda n: pl.BlockSpec((1, tm, n), lambda b, t: (b, t, 0))
    full = lambda a: pl.BlockSpec(a.shape, lambda b, t: (0,) * a.ndim)
    mod_map = (lambda b, t: (b, 0, 0)) if per_batch_mod else (lambda b, t: (0, 0, 0))
    sds = lambda n, dt: jax.ShapeDtypeStruct((B, T, n), dt)
    n_b = wts[0].shape[1] - o_b

    def walk(a):
        block = a.shape[0] // (B * nt)
        assert block * B * nt == a.shape[0] and block % 16 == 0
        return pl.BlockSpec((block, a.shape[1]), lambda b, t: (b * nt + t, 0))

    cast_specs = [walk(a) for a in cast_weights]
    outs = pl.pallas_call(
        functools.partial(_inproj_kernel, o_a=o_a, o_b=o_b, n_cast=len(cast_weights)),
        grid=(B, nt),
        in_specs=[tok(D_MODEL), pl.BlockSpec((1, 1, N_MOD * D_MODEL), mod_map), full(norm_g)]
        + [full(w) for w in wts] + cast_specs,
        out_specs=[tok(D_A), tok(D_A), tok(D_A), tok(D_A), tok(D_B), tok(D_MODEL), tok(D_MODEL), tok(GATE_PAD)]
        + cast_specs,
        out_shape=[sds(D_A, bf16), sds(D_A, bf16), sds(D_A, bf16), sds(D_A, bf16), sds(D_B, f32),
                   sds(D_MODEL, bf16), sds(D_MODEL, bf16), sds(GATE_PAD, f32)]
        + [jax.ShapeDtypeStruct(a.shape, bf16) for a in cast_weights],
        scratch_shapes=[pltpu.VMEM((D_MODEL, o_a), bf16), pltpu.VMEM((D_MODEL, n_b), bf16)],
        compiler_params=pltpu.CompilerParams(dimension_semantics=("arbitrary", "arbitrary"),
                                             vmem_limit_bytes=VMEM_LIMIT),
        name="in_proj",
    )(x, mod3, norm_g, *wts, *cast_weights)
    return [o.reshape(B0, T0, o.shape[-1]) for o in outs[:8]], outs[8:]


NEG_BIG = -1e30
ST_U, ST_INTER, ST_EMJ, ST_WKN, ST_DECAY = 0, 8, 16, 24, 32
ST_ROWS = 40


def _chunk_scan(x, op, fill, prefix, lane, width):
    k = 1
    while k < CHUNK:
        if prefix:
            shifted = jnp.where(lane >= k, pltpu.roll(x, k, axis=1), fill)
        else:
            shifted = jnp.where(lane < CHUNK - k, pltpu.roll(x, width - k, axis=1), fill)
        x = op(x, shifted)
        k *= 2
    return x


def _mlstm_kernel(*refs, nseq, nc, state_in, state_out):
    refs = list(refs)
    q_ref, k_ref, v_ref, g_ref = refs[:4]
    del refs[:4]
    if state_in:
        c0_ref, n0_ref, m0_ref = refs[:3]
        del refs[:3]
    hn_ref = refs.pop(0)
    if state_out:
        cn_ref, nn_ref, mn_ref = refs[:3]
        del refs[:3]
    st_ref, wt_ref, vt_ref, qt_ref, u_ref, cp_ref, npf_ref, npb_ref, cst_ref, nsf_ref, nsb_ref = refs
    L = CHUNK
    nct = nseq * nc
    T = nct * L
    NR = 2 * NH_A

    lane = jnp.bitwise_and(lax.broadcasted_iota(jnp.int32, (NR, T), 1), L - 1)
    is_fwd = lax.broadcasted_iota(jnp.int32, (NR, T), 0) < NH_A
    is_fwd_c = lax.broadcasted_iota(jnp.int32, (NR, L), 0) < NH_A
    i_parts, f_parts = [], []
    for c in range(nct):
        gt = g_ref[0, c * L:(c + 1) * L, :].T
        i_parts.append(gt[0:NR])
        f_parts.append(gt[NR:2 * NR])
    ig = jnp.concatenate(i_parts, axis=1)
    lf = _log_sigmoid(jnp.concatenate(f_parts, axis=1))
    scan = functools.partial(_chunk_scan, lane=lane, width=T)
    ps = scan(lf, jnp.add, 0.0, True)
    ss = scan(lf, jnp.add, 0.0, False)
    b = jnp.where(is_fwd, ps, ss)
    btot = ps + ss - lf
    w = ig - b
    cmw = jnp.where(is_fwd, scan(w, jnp.maximum, -jnp.inf, True), scan(w, jnp.maximum, -jnp.inf, False))
    wk = btot - b + ig
    a = jnp.maximum(scan(wk, jnp.maximum, -jnp.inf, True), scan(wk, jnp.maximum, -jnp.inf, False))
    chunk_lanes = lambda cg: slice(cg * L, (cg + 1) * L)
    pad_rows = jnp.zeros((L - NR, L), f32)
    for s in range(nseq):
        ms = [m0_ref[s] if state_in else jnp.zeros((NR, L), f32)]
        for t in range(nc):
            cf, cb = chunk_lanes(s * nc + t), chunk_lanes(s * nc + nc - 1 - t)
            bt = jnp.where(is_fwd_c, btot[:, cf], btot[:, cb])
            at = jnp.where(is_fwd_c, a[:, cf], a[:, cb])
            ms.append(jnp.maximum(bt + ms[-1], at))
        if state_out:
            mn_ref[s] = ms[nc]
        for c in range(nc):
            cg = s * nc + c
            sl = chunk_lanes(cg)
            m_prev = jnp.where(is_fwd_c, ms[c], ms[nc - 1 - c])
            m_new = jnp.where(is_fwd_c, ms[c + 1], ms[nc - c])
            mx = jnp.maximum(m_prev, cmw[:, sl])
            st_ref[cg, ST_U:ST_U + NR] = -mx
            st_ref[cg, ST_INTER:ST_INTER + NR] = jnp.exp(m_prev - mx)
            st_ref[cg, ST_EMJ:ST_EMJ + NR] = jnp.exp(-mx - b[:, sl])
            st_ref[cg, ST_WKN:ST_WKN + NR] = jnp.exp(wk[:, sl] - m_new)
            st_ref[cg, ST_DECAY:ST_DECAY + NR] = jnp.exp(btot[:, sl] + m_prev - m_new)
            wt_ref[cg] = jnp.concatenate([w[:, sl], pad_rows], axis=0).T

    row = lax.broadcasted_iota(jnp.int32, (L, L), 0)
    col = lax.broadcasted_iota(jnp.int32, (L, L), 1)
    masks = (row <= col, row >= col)
    first_of8 = lax.broadcasted_iota(jnp.int32, (8, L), 0) == 0

    def row_tile(x):
        return jnp.where(first_of8, x, 0.0)

    head_lanes = [slice(h * DH_A, (h + 1) * DH_A) for h in range(NH_A)]

    def increments(j, carry):
        rows = pl.ds(pl.multiple_of(j * L, L), L)
        for h in range(NH_A):
            kh = k_ref[0, rows, head_lanes[h]]
            vt = v_ref[0, rows, head_lanes[h]].astype(f32).T
            vt_ref[j, h] = vt.astype(bf16)
            qt_ref[j, h] = q_ref[0, rows, head_lanes[h]].astype(f32).T.astype(bf16)
            wf = st_ref[j, ST_WKN + h:ST_WKN + h + 1, :]
            wb = st_ref[j, ST_WKN + NH_A + h:ST_WKN + NH_A + h + 1, :]
            lhs = jnp.concatenate([vt * wf, vt * wb, row_tile(wf), row_tile(wb)], axis=0).astype(bf16)
            u_ref[j, h] = _dot(lhs, kh)
        return carry

    lax.fori_loop(0, nct, increments, 0, unroll=4)

    seq_heads = [(s, h) for s in range(nseq) for h in range(NH_A)]
    for s, h in seq_heads:
        i = s * NH_A + h
        if state_in:
            cst_ref[i, 0:DH_A] = c0_ref[s, 0, h].T
            cst_ref[i, DH_A:2 * DH_A] = c0_ref[s, 1, h].T
            nsf_ref[i] = row_tile(n0_ref[s, h:h + 1, :])
            nsb_ref[i] = row_tile(n0_ref[s, NH_A + h:NH_A + h + 1, :])
        else:
            cst_ref[i] = jnp.zeros((2 * DH_A, DH_A), f32)
            nsf_ref[i] = jnp.zeros((8, DH_A), f32)
            nsb_ref[i] = jnp.zeros((8, DH_A), f32)

    def recur(t, carry):
        for s, h in seq_heads:
            i = s * NH_A + h
            jf = s * nc + t
            jb = s * nc + nc - 1 - t
            dec_f = st_ref[jf, ST_DECAY + h:ST_DECAY + h + 1, :]
            dec_b = st_ref[jb, ST_DECAY + NH_A + h:ST_DECAY + NH_A + h + 1, :]
            c_f = cst_ref[i, 0:DH_A]
            c_b = cst_ref[i, DH_A:2 * DH_A]
            cp_ref[jf, h, 0:DH_A] = c_f.astype(bf16)
            cp_ref[jb, h, DH_A:2 * DH_A] = c_b.astype(bf16)
            cst_ref[i, 0:DH_A] = dec_f * c_f + u_ref[jf, h, 0:DH_A]
            cst_ref[i, DH_A:2 * DH_A] = dec_b * c_b + u_ref[jb, h, DH_A:2 * DH_A]
            n_f = nsf_ref[i]
            n_b = nsb_ref[i]
            npf_ref[jf, h] = n_f
            npb_ref[jb, h] = n_b
            nsf_ref[i] = dec_f * n_f + u_ref[jf, h, 2 * DH_A:2 * DH_A + 8]
            nsb_ref[i] = dec_b * n_b + u_ref[jb, h, 2 * DH_A + 8:2 * DH_A + 16]
        return carry

    lax.fori_loop(0, nc, recur, 0)
    for s, h in seq_heads if state_out else ():
        i = s * NH_A + h
        cn_ref[s, 0, h] = cst_ref[i, 0:DH_A].T
        cn_ref[s, 1, h] = cst_ref[i, DH_A:2 * DH_A].T
        nn_ref[s, h:h + 1, :] = nsf_ref[i, 0:1]
        nn_ref[s, NH_A + h:NH_A + h + 1, :] = nsb_ref[i, 0:1]

    def outputs(j, carry):
        rows = pl.ds(pl.multiple_of(j * L, L), L)
        wt = wt_ref[j]
        heads = range(NH_A)
        kqs = []
        for h in heads:
            kh = k_ref[0, rows, head_lanes[h]]
            n_rows = jnp.concatenate([npf_ref[j, h], npb_ref[j, h]], axis=0).astype(bf16)
            kqs.append(_dot(jnp.concatenate([kh, n_rows, cp_ref[j, h]], axis=0), qt_ref[j, h]))
        decays = [[jnp.exp(jnp.where(masks[d], wt[:, NH_A * d + h:NH_A * d + h + 1]
                                     + st_ref[j, ST_U + NH_A * d + h:ST_U + NH_A * d + h + 1, :], NEG_BIG))
                   for d in range(2)] for h in heads]
        s_sums, h_ts = [], []
        for h in heads:
            kq = kqs[h]
            s_sum = None
            h_t = None
            for d in range(2):
                r = NH_A * d + h
                inter = st_ref[j, ST_INTER + r:ST_INTER + r + 1, :]
                emj = st_ref[j, ST_EMJ + r:ST_EMJ + r + 1, :]
                s_t = kq[0:L] * decays[h][d]
                qn = kq[L + 8 * d:L + 8 * d + 1]
                den = inter * qn + jnp.sum(s_t, axis=0, keepdims=True)
                rr = 1.0 / jnp.maximum(jnp.abs(den), emj)
                s_sum = s_t * rr if d == 0 else s_sum + s_t * rr
                part = kq[L + 16 + d * DH_A:L + 16 + (d + 1) * DH_A] * (inter * rr)
                h_t = part if d == 0 else h_t + part
            s_sums.append(s_sum.astype(bf16))
            h_ts.append(h_t)
        h_ts = [h_ts[h] + _dot(vt_ref[j, h], s_sums[h]) for h in heads]
        h_ts = [x * lax.rsqrt(jnp.mean(x * x, axis=0, keepdims=True) + EPS) for x in h_ts]
        for h in heads:
            hn_ref[0, rows, head_lanes[h]] = h_ts[h].T.astype(bf16)
        return carry

    lax.fori_loop(0, nct, outputs, 0, unroll=4)


def _mlstm(q, k, v, gates, state, nseq, state_out):
    B, T, _ = q.shape
    nc = T // CHUNK
    nct = nseq * nc
    G = B // nseq
    fold = lambda a: a.reshape(G, nseq * T, a.shape[-1])
    seq = lambda n: pl.BlockSpec((1, nseq * T, n), lambda b: (b, 0, 0))
    st_c = pl.BlockSpec((nseq, 2, NH_A, DH_A, DH_A), lambda b: (b, 0, 0, 0, 0))
    st_v = pl.BlockSpec((nseq, 2 * NH_A, DH_A), lambda b: (b, 0, 0))
    state_specs = [st_c, st_v, st_v]
    state_shapes = [jax.ShapeDtypeStruct((B, 2, NH_A, DH_A, DH_A), f32),
                    jax.ShapeDtypeStruct((B, 2 * NH_A, DH_A), f32),
                    jax.ShapeDtypeStruct((B, 2 * NH_A, DH_A), f32)]
    outs = pl.pallas_call(
        functools.partial(_mlstm_kernel, nseq=nseq, nc=nc, state_in=state is not None, state_out=state_out),
        grid=(G,),
        in_specs=[seq(D_A), seq(D_A), seq(D_A), seq(GATE_PAD)] + (state_specs if state is not None else []),
        out_specs=[seq(D_A)] + (state_specs if state_out else []),
        scratch_shapes=[pltpu.VMEM((nct, ST_ROWS, CHUNK), f32), pltpu.VMEM((nct, CHUNK, GATE_PAD), f32),
                        pltpu.VMEM((nct, NH_A, DH_A, CHUNK), bf16), pltpu.VMEM((nct, NH_A, DH_A, CHUNK), bf16),
                        pltpu.VMEM((nct, NH_A, 2 * DH_A + 16, DH_A), f32),
                        pltpu.VMEM((nct, NH_A, 2 * DH_A, DH_A), bf16),
                        pltpu.VMEM((nct, NH_A, 8, DH_A), f32), pltpu.VMEM((nct, NH_A, 8, DH_A), f32),
                        pltpu.VMEM((nseq * NH_A, 2 * DH_A, DH_A), f32),
                        pltpu.VMEM((nseq * NH_A, 8, DH_A), f32), pltpu.VMEM((nseq * NH_A, 8, DH_A), f32)],
        out_shape=[jax.ShapeDtypeStruct((G, nseq * T, D_A), bf16)] + (state_shapes if state_out else []),
        compiler_params=pltpu.CompilerParams(dimension_semantics=("arbitrary",), vmem_limit_bytes=VMEM_LIMIT),
        name="mlstm",
    )(fold(q), fold(k), fold(v), fold(gates), *(state if state is not None else ()))
    return (outs[0].reshape(B, T, D_A),) + tuple(outs[1:])


def _mix_kernel(x_ref, mod_ref, hn_ref, so_ref, glu_ref, glu_prev_ref, glu_next_ref, sga_ref, sgb_ref,
                ng_ref, wpa_ref, cw_ref, cb_ref, lng_ref, lnb_ref, wpb_ref, wout_ref,
                x1_ref, ext_ref, zs_ref, *, tm, nt, nsub):
    t = pl.program_id(1)
    mod = mod_ref[0]
    g1 = mod[:, 2 * D_MODEL:3 * D_MODEL]

    a_in = (so_ref[0].astype(f32) * (hn_ref[0].astype(f32) * ng_ref[...])).astype(bf16)
    branch_a = _dot(a_in, wpa_ref[...])

    H = CONV_HALO
    SUB, LANE = SUBLANES, LANES
    tsub = tm // nsub
    te = tsub + 2 * H
    for s in range(nsub):
        inner = nsub == 1
        ext_ref[0, s, 0:H, :] = jnp.where(t > 0, glu_prev_ref[0], 0.0) if inner else jnp.zeros((H, D_B), f32)
        ext_ref[0, s, H:H + tsub, :] = glu_ref[0, s * tsub:(s + 1) * tsub, :]
        ext_ref[0, s, H + tsub:te, :] = jnp.where(t < nt - 1, glu_next_ref[0], 0.0) if inner else jnp.zeros((H, D_B), f32)
    for s in range(nsub):
        for c in range(D_B // LANE):
            lanes = slice(c * LANE, (c + 1) * LANE)
            base = ext_ref[0, s, :, lanes]
            for k in range(1, SUB):
                ext_ref[k, s, :, lanes] = pltpu.roll(base, te - k, axis=0)
    off = H - CONV_W // 2
    for s in range(nsub):
        for r0 in range(0, tsub, CONV_ROWS):
            parts = []
            for c in range(D_B // LANE):
                lanes = slice(c * LANE, (c + 1) * LANE)
                acc = jnp.broadcast_to(cb_ref[:, lanes], (CONV_ROWS, LANE))
                for w in range(CONV_W):
                    k, a = (off + w) % SUB, (off + w) // SUB * SUB
                    acc = acc + ext_ref[k, s, r0 + a:r0 + a + CONV_ROWS, lanes] * cw_ref[w:w + 1, lanes]
                parts.append(acc)
            z = jnp.concatenate(parts, axis=1)
            mu = jnp.mean(z, axis=-1, keepdims=True)
            zc = z - mu
            var = jnp.mean(zc * zc, axis=-1, keepdims=True)
            zn = zc * lax.rsqrt(var + EPS) * lng_ref[...] + lnb_ref[...]
            zs_ref[s * tsub + r0:s * tsub + r0 + CONV_ROWS, :] = (zn * _sigmoid(zn)).astype(bf16)
    branch_b = _dot(zs_ref[...], wpb_ref[...])

    merged = sga_ref[0].astype(f32) * branch_a + sgb_ref[0].astype(f32) * branch_b
    x1_ref[0] = x_ref[0] + g1 * _dot(merged.astype(bf16), wout_ref[...])


def _mix_out(x, mod3, per_batch_mod, hn, so, glu, sga, sgb, wts, tm):
    B0, T0, _ = x.shape
    nsub = 1
    if tm > T0:
        assert not per_batch_mod and tm % T0 == 0 and B0 % (tm // T0) == 0
        nsub = tm // T0
        x, hn, so, glu, sga, sgb = (a.reshape(B0 // nsub, tm, a.shape[-1]) for a in (x, hn, so, glu, sga, sgb))
    B, T, _ = x.shape
    nt = T // tm
    H = CONV_HALO
    r = tm // H
    tok = lambda n: pl.BlockSpec((1, tm, n), lambda b, t: (b, t, 0))
    full = lambda a: pl.BlockSpec(a.shape, lambda b, t: (0,) * a.ndim)
    mod_map = (lambda b, t: (b, 0, 0)) if per_batch_mod else (lambda b, t: (0, 0, 0))
    prev = pl.BlockSpec((1, H, D_B), lambda b, t: (b, jnp.maximum(t * r - 1, 0), 0))
    nxt = pl.BlockSpec((1, H, D_B), lambda b, t: (b, jnp.minimum((t + 1) * r, T // H - 1), 0))
    x1 = pl.pallas_call(
        functools.partial(_mix_kernel, tm=tm, nt=nt, nsub=nsub),
        grid=(B, nt),
        in_specs=[tok(D_MODEL), pl.BlockSpec((1, 1, N_MOD * D_MODEL), mod_map), tok(D_A), tok(D_A),
                  tok(D_B), prev, nxt, tok(D_MODEL), tok(D_MODEL)] + [full(w) for w in wts],
        out_specs=tok(D_MODEL),
        out_shape=jax.ShapeDtypeStruct((B, T, D_MODEL), f32),
        scratch_shapes=[pltpu.VMEM((SUBLANES, nsub, tm // nsub + 2 * H, D_B), f32), pltpu.VMEM((tm, D_B), bf16)],
        compiler_params=pltpu.CompilerParams(dimension_semantics=("arbitrary", "arbitrary"),
                                             vmem_limit_bytes=VMEM_LIMIT),
        name="mix_out",
    )(x, mod3, hn, so, glu, glu, glu, sga, sgb, *wts)
    return x1.reshape(B0, T0, D_MODEL)


def _ffn_kernel(*refs, tm, nt, on_grid, seq_len):
    if on_grid:
        (x_ref, xp_ref, xn_ref, mod_ref, g2n_ref, wu_ref, wd_ref, cw_ref, cb_ref, fg_ref,
         y_ref, h_ref, g0_ref, g1_ref, v0_ref, v1_ref, gl_ref, gr_ref, a0_ref, a1_ref, a2_ref, a3_ref, acc_ref) = refs
    else:
        (x_ref, mod_ref, g2n_ref, wu_ref, wd_ref, cw_ref, cb_ref, fg_ref,
         y_ref, h_ref, g0_ref, g1_ref, v0_ref, v1_ref, gl_ref, gr_ref, a0_ref, a1_ref, a2_ref, a3_ref, acc_ref) = refs
    gbufs, vbufs, acts = (g0_ref, g1_ref), (v0_ref, v1_ref), (a0_ref, a1_ref, a2_ref, a3_ref)
    t = pl.program_id(1)
    halo = GRID_W if on_grid else 0
    te = tm + 2 * halo
    mod = mod_ref[0]
    sh2 = mod[:, 3 * D_MODEL:4 * D_MODEL]
    sc2 = mod[:, 4 * D_MODEL:5 * D_MODEL]
    g2 = mod[:, 5 * D_MODEL:6 * D_MODEL]

    def norm_mod(x):
        return _rms(x, g2n_ref[...]) * (1.0 + sc2) + sh2

    h_ref[halo:halo + tm, :] = norm_mod(x_ref[0]).astype(bf16)
    if on_grid:
        h_ref[0:halo, :] = jnp.where(t > 0, norm_mod(xp_ref[0]), 0.0).astype(bf16)
        h_ref[halo + tm:te, :] = jnp.where(t < nt - 1, norm_mod(xn_ref[0]), 0.0).astype(bf16)

    SUB, LANE = SUBLANES, LANES
    seg = GRID_W if on_grid else seq_len
    sub_row = lax.broadcasted_iota(jnp.int32, (SUB, LANE), 0)
    first_row = sub_row == 0
    last_row = sub_row == SUB - 1
    row_taps = (0, 1, 2) if on_grid else (1,)

    def ff_cols(fc, base):
        if isinstance(fc, int):
            return slice(base + fc * FF_CHUNK, base + (fc + 1) * FF_CHUNK)
        return pl.ds(pl.multiple_of(base + fc * FF_CHUNK, 128), FF_CHUNK)

    def up(fc, slot):
        gbufs[slot][...] = _dot(h_ref[...], wu_ref[:, ff_cols(fc, 0)])
        vbufs[slot][...] = _dot(h_ref[halo:halo + tm, :], wu_ref[:, ff_cols(fc, D_FF)])

    def gate_act(fc, slot, aslot):
        cw = cw_ref[:, ff_cols(fc, 0)]
        cb = cb_ref[:, ff_cols(fc, 0)]
        for c in range(FF_CHUNK // LANE):
            lanes = slice(c * LANE, (c + 1) * LANE)
            for s0 in range(0, te, seg):
                gate = gbufs[slot][s0:s0 + seg, lanes]
                g_l = pltpu.roll(gate, 1, axis=0)
                g_r = pltpu.roll(gate, seg - 1, axis=0)
                gl_ref[s0:s0 + SUB, lanes] = jnp.where(first_row, 0.0, g_l[0:SUB])
                gl_ref[s0 + SUB:s0 + seg, lanes] = g_l[SUB:seg]
                gr_ref[s0:s0 + seg - SUB, lanes] = g_r[0:seg - SUB]
                gr_ref[s0 + seg - SUB:s0 + seg, lanes] = jnp.where(last_row, 0.0, g_r[seg - SUB:seg])
            for r0 in range(0, tm, GRID_W):
                conv = jnp.broadcast_to(cb[:, lanes], (GRID_W, LANE))
                for kh in row_taps:
                    lo = r0 + kh * GRID_W if on_grid else r0
                    conv = conv + gl_ref[lo:lo + GRID_W, lanes] * cw[3 * kh:3 * kh + 1, lanes]
                    conv = conv + gbufs[slot][lo:lo + GRID_W, lanes] * cw[3 * kh + 1:3 * kh + 2, lanes]
                    conv = conv + gr_ref[lo:lo + GRID_W, lanes] * cw[3 * kh + 2:3 * kh + 3, lanes]
                gelu = conv * (0.5 + 0.5 * jnp.tanh(conv * (GELU_C0 + GELU_C1 * (conv * conv))))
                acts[aslot][r0:r0 + GRID_W, lanes] = (gelu * vbufs[slot][r0:r0 + GRID_W, lanes]).astype(bf16)

    def down(fc, aslot):
        return _dot(acts[aslot][...], wd_ref[fc])


    def pair(p, parity, with_down):
        fc = 2 * p
        wr, rd = 2 * parity, 2 * (1 - parity)
        if with_down:
            acc_ref[...] += down(fc - 2, rd) + down(fc - 1, rd + 1)
        up(fc + 1, 1)
        gate_act(fc, 0, wr)
        up(fc + 2, 0)
        gate_act(fc + 1, 1, wr + 1)

    n_pairs = (N_FF_CHUNKS - 1) // 2
    assert N_FF_CHUNKS == 2 * n_pairs + 1 and n_pairs % 2 == 1
    up(0, 0)
    acc_ref[...] = jnp.zeros_like(acc_ref)
    pair(0, 0, False)

    def two_pairs(i, carry):
        p = 2 * i + 1
        pair(p, 1, True)
        pair(p + 1, 0, True)
        return carry

    lax.fori_loop(0, (n_pairs - 1) // 2, two_pairs, 0)
    last = N_FF_CHUNKS - 1
    gate_act(last, 0, 2)
    ffn_out = acc_ref[...] + (down(last - 2, 0) + down(last - 1, 1) + down(last, 2))
    y_ref[0] = _rms(x_ref[0] + g2 * ffn_out, fg_ref[...])


def _ffn(x1, mod3, per_batch_mod, wts, tm, on_grid):
    B0, T0, _ = x1.shape
    if tm > T0:
        assert not per_batch_mod and not on_grid and tm % T0 == 0 and B0 % (tm // T0) == 0
        x1 = x1.reshape(B0 * T0 // tm, tm, D_MODEL)
    B, T, _ = x1.shape
    nt = T // tm
    halo = GRID_W if on_grid else 0
    te = tm + 2 * halo
    tok = pl.BlockSpec((1, tm, D_MODEL), lambda b, t: (b, t, 0))
    full = lambda a: pl.BlockSpec(a.shape, lambda b, t: (0,) * a.ndim)
    mod_map = (lambda b, t: (b, 0, 0)) if per_batch_mod else (lambda b, t: (0, 0, 0))
    in_specs = [tok]
    args = [x1]
    if on_grid:
        r = tm // GRID_W
        in_specs += [pl.BlockSpec((1, GRID_W, D_MODEL), lambda b, t: (b, jnp.maximum(t * r - 1, 0), 0)),
                     pl.BlockSpec((1, GRID_W, D_MODEL), lambda b, t: (b, jnp.minimum((t + 1) * r, T // GRID_W - 1), 0))]
        args += [x1, x1]
    in_specs += [pl.BlockSpec((1, 1, N_MOD * D_MODEL), mod_map)] + [full(w) for w in wts]
    args += [mod3] + list(wts)
    y = pl.pallas_call(
        functools.partial(_ffn_kernel, tm=tm, nt=nt, on_grid=on_grid, seq_len=T0),
        grid=(B, nt),
        in_specs=in_specs,
        out_specs=tok,
        out_shape=jax.ShapeDtypeStruct((B, T, D_MODEL), f32),
        scratch_shapes=[pltpu.VMEM((te, D_MODEL), bf16)]
        + [pltpu.VMEM((te, FF_CHUNK), f32)] * 2 + [pltpu.VMEM((tm, FF_CHUNK), f32)] * 2
        + [pltpu.VMEM((te, FF_CHUNK), f32)] * 2 + [pltpu.VMEM((tm, FF_CHUNK), bf16)] * 4
        + [pltpu.VMEM((tm, D_MODEL), f32)],
        compiler_params=pltpu.CompilerParams(dimension_semantics=("arbitrary", "arbitrary"),
                                             vmem_limit_bytes=VMEM_LIMIT),
        name="ffn_grid" if on_grid else "ffn_ctx",
    )(*args)
    return y.reshape(B0, T0, D_MODEL)


TOKEN_TILE = 512
MLSTM_TOKENS = 1024


def _tiles(T, per_batch_mod):
    span = TOKEN_TILE if not per_batch_mod else min(TOKEN_TILE, T)
    return span, span, span, max(1, MLSTM_TOKENS // T)


def _trunk(x, mod3, per_batch_mod, state, on_grid, w, cast_weights=()):
    tm_proj, tm_mix, tm_ffn, mlstm_nseq = _tiles(x.shape[1], per_batch_mod)
    (q, k, v, so, glu, sga, sgb, gates), casts = _in_proj(x, mod3, per_batch_mod, w["norm1_g"], w["in"], tm_proj,
                                                          *w["in_groups"], cast_weights=cast_weights)
    if cast_weights:
        w = dict(w, mix=w["mix"](casts), ffn=w["ffn"](casts))
    hn, *new_state = _mlstm(q, k, v, gates, state, mlstm_nseq, state is None)
    x1 = _mix_out(x, mod3, per_batch_mod, hn, so, glu, sga, sgb, w["mix"], tm_mix)
    y = _ffn(x1, mod3, per_batch_mod, w["ffn"], tm_ffn, on_grid)
    return y, new_state, w


def kernel(x_prompt, x_sample, c, state_C, state_n, state_m, c_ctx, w_ada, b_ada, norm1_g, w_in, b_in,
           mlstm_norm_g, w_proj_a, conv_dw_w, conv_dw_b, conv_ln_g, conv_ln_b, w_proj_b, w_out, norm2_g,
           w_up, ffn_dw_w, ffn_dw_b, w_down, final_norm_g):
    Bp = x_prompt.shape[0]
    Bl = x_sample.shape[0]
    l = 0
    row = lambda a: a.reshape(1, -1)

    ct = jnp.concatenate([c, c_ctx[None, :], jnp.zeros((8 - Bl - 1, D_MODEL), f32)], axis=0).T
    mod = _adaln_mod(ct, w_ada[l], row(b_ada[l]), Bl + 1)
    mod_lat = mod[0:Bl].reshape(Bl, 1, N_MOD * D_MODEL)
    mod_ctx = mod[Bl:Bl + 1].reshape(1, 1, N_MOD * D_MODEL)

    o_g = 4 * D_A
    o_u = o_g + 4 * NH_A
    nh = NH_A

    def gate_cols(a):
        g = a[..., o_g:o_u]
        pad = jnp.zeros(a.shape[:-1] + (GATE_PAD - 4 * nh,), a.dtype)
        return jnp.concatenate([g[..., 0:nh], g[..., 2 * nh:3 * nh], g[..., nh:2 * nh], g[..., 3 * nh:], pad], axis=-1)

    w_in_b = w_in[l].astype(bf16)
    bi = row(b_in[l])
    w_in_parts = (w_in_b, bi[:, :o_g], bi[:, o_u:], gate_cols(w_in_b), gate_cols(bi))
    cast_weights = [w_up[l], w_down[l], w_out[l], w_proj_a[l], w_proj_b[l]]

    def w_mix(casts):
        return (row(mlstm_norm_g[l]), casts[3],
                jnp.pad(conv_dw_w[l], ((0, 1), (0, 0))), row(conv_dw_b[l]), row(conv_ln_g[l]), row(conv_ln_b[l]),
                casts[4], casts[2])

    def w_ffn(casts):
        return (row(norm2_g[l]), casts[0], casts[1].reshape(N_FF_CHUNKS, FF_CHUNK, D_MODEL),
                ffn_dw_w[l].reshape(9, D_FF), row(ffn_dw_b[l]), row(final_norm_g))

    w = {"norm1_g": row(norm1_g[l]), "in": w_in_parts, "in_groups": (o_g, o_u), "mix": w_mix, "ffn": w_ffn}

    c0 = state_C[:, l]
    n0 = state_n[:, l].reshape(Bl, 2 * NH_A, DH_A)
    m0 = jnp.broadcast_to(state_m[:, l].reshape(Bl, 2 * NH_A, 1), (Bl, 2 * NH_A, DH_A))
    y_sample, _, w = _trunk(x_sample, mod_lat, True, (c0, n0, m0), True, w, cast_weights)

    y_prompt, (cn, nn, mn), _ = _trunk(x_prompt, mod_ctx, False, None, False, w)

    new_state_C = cn[:, None]
    new_state_n = nn.reshape(Bp, 1, 2, NH_A, DH_A)
    new_state_m = mn[:, :, 0].reshape(Bp, 1, 2, NH_A)
    return (y_prompt, y_sample, new_state_C, new_state_n, new_state_m)
```

```python
import functools

import jax
import jax.numpy as jnp
from jax import lax
from jax.experimental import pallas as pl
from jax.experimental.pallas import tpu as pltpu

D_MODEL = 1024
D_A = 512
NH_A = 4
DH_A = 128
CHUNK = 128
D_B = 512
CONV_W = 31
CONV_HALO = 16
CONV_ROWS = 64
D_FF = 2816
FF_CHUNK = 256
N_FF_CHUNKS = D_FF // FF_CHUNK
GRID_W = 64
N_MOD = 6
GATE_PAD = 128
EPS = 1e-6
Q_SCALE = DH_A ** -0.5
VMEM_LIMIT = 56 * 1024 * 1024
SUBLANES = 8
LANES = 128

f32 = jnp.float32
bf16 = jnp.bfloat16


def _rms(x, g):
    return x * lax.rsqrt(jnp.mean(x * x, axis=-1, keepdims=True) + EPS) * g


def _sigmoid(x):
    return 1.0 / (1.0 + jnp.exp(-x))


def _log_sigmoid(x):
    return jnp.minimum(x, 0.0) - jnp.log(1.0 + jnp.exp(-jnp.abs(x)))


GELU_C0 = 0.7978845608028654
GELU_C1 = GELU_C0 * 0.044715


def _dot(a, b):
    return jnp.dot(a, b, preferred_element_type=f32)


ADA_SPLIT = 2


def _mod_kernel(ct_ref, *refs, n_rows):
    w_refs = refs[:ADA_SPLIT]
    b_ref, o_ref = refs[ADA_SPLIT:]
    ct = ct_ref[...]
    st = ct * _sigmoid(ct)
    ws = [w_ref[...] for w_ref in w_refs]
    h = D_MODEL // ADA_SPLIT
    tn = o_ref.shape[1]
    rows = []
    for r in range(n_rows):
        s_col = jnp.broadcast_to(st[:, r:r + 1], (D_MODEL, LANES))
        rows.append(jnp.concatenate(
            [sum(jnp.sum(w[:, c:c + LANES] * s_col[k * h:(k + 1) * h], axis=0, keepdims=True)
                 for k, w in enumerate(ws)) for c in range(0, tn, LANES)], axis=1))
    rows.append(jnp.zeros((8 - n_rows, tn), f32))
    o_ref[...] = jnp.concatenate(rows, axis=0) + b_ref[...]


def _adaln_mod(ct, w_ada, b_ada, n_rows):
    n = w_ada.shape[1]
    tn = 512
    h = D_MODEL // ADA_SPLIT
    return pl.pallas_call(
        functools.partial(_mod_kernel, n_rows=n_rows),
        grid=(n // tn,),
        in_specs=[pl.BlockSpec((D_MODEL, 8), lambda j: (0, 0))]
        + [pl.BlockSpec((h, tn), lambda j, k=k: (k, j)) for k in range(ADA_SPLIT)]
        + [pl.BlockSpec((1, tn), lambda j: (0, j))],
        out_specs=pl.BlockSpec((8, tn), lambda j: (0, j)),
        out_shape=jax.ShapeDtypeStruct((8, n), f32),
        compiler_params=pltpu.CompilerParams(dimension_semantics=("arbitrary",), vmem_limit_bytes=VMEM_LIMIT),
        name="adaln_mod",
    )(ct, *([w_ada] * ADA_SPLIT), b_ada)


def _inproj_kernel(*refs, o_a, o_b, n_cast):
    x_ref, mod_ref, g_ref, w_ref, ba_ref, bb_ref, wg_ref, bg_ref = refs[:8]
    cast_src = refs[8:8 + n_cast]
    q_ref, k_ref, v_ref, so_ref, glu_ref, sga_ref, sgb_ref, gates_ref = refs[8 + n_cast:16 + n_cast]
    cast_dst = refs[16 + n_cast:16 + 2 * n_cast]
    wa_ref, wb_ref = refs[16 + 2 * n_cast:]
    for src, dst in zip(cast_src, cast_dst):
        dst[...] = src[...].astype(bf16)

    @pl.when(jnp.logical_and(pl.program_id(0) == 0, pl.program_id(1) == 0))
    def _():
        wa_ref[...] = w_ref[:, 0:o_a]
        wb_ref[...] = w_ref[:, o_b:]

    x = x_ref[0]
    mod = mod_ref[0]
    sh1 = mod[:, 0:D_MODEL]
    sc1 = mod[:, D_MODEL:2 * D_MODEL]
    h = (_rms(x, g_ref[...]) * (1.0 + sc1) + sh1).astype(bf16)

    def proj(w_ref, b_ref, lo, n):
        return _dot(h, w_ref[:, lo:lo + n]) + b_ref[:, lo:lo + n]

    q_ref[0] = (proj(wa_ref, ba_ref, 0, D_A) * Q_SCALE).astype(bf16)
    k_ref[0] = proj(wa_ref, ba_ref, D_A, D_A).astype(bf16)
    v_ref[0] = proj(wa_ref, ba_ref, 2 * D_A, D_A).astype(bf16)
    so_ref[0] = _sigmoid(proj(wa_ref, ba_ref, 3 * D_A, D_A)).astype(bf16)
    glu_ref[0] = proj(wb_ref, bb_ref, 0, D_B) * _sigmoid(proj(wb_ref, bb_ref, D_B, D_B))
    sga_ref[0] = _sigmoid(proj(wb_ref, bb_ref, 2 * D_B, D_MODEL)).astype(bf16)
    sgb_ref[0] = _sigmoid(proj(wb_ref, bb_ref, 2 * D_B + D_MODEL, D_MODEL)).astype(bf16)
    gates_ref[0] = proj(wg_ref, bg_ref, 0, GATE_PAD)


def _in_proj(x, mod3, per_batch_mod, norm_g, wts, tm, o_a, o_b, cast_weights=()):
    B0, T0, _ = x.shape
    if tm > T0:
        assert not per_batch_mod and tm % T0 == 0 and B0 % (tm // T0) == 0
        x = x.reshape(B0 * T0 // tm, tm, D_MODEL)
    B, T, _ = x.shape
    nt = T // tm
    tok = lambda n: pl.BlockSpec((1, tm, n), lambda b, t: (b, t, 0))
    full = lambda a: pl.BlockSpec(a.shape, lambda b, t: (0,) * a.ndim)
    mod_map = (lambda b, t: (b, 0, 0)) if per_batch_mod else (lambda b, t: (0, 0, 0))
    sds = lambda n, dt: jax.ShapeDtypeStruct((B, T, n), dt)
    n_b = wts[0].shape[1] - o_b

    def walk(a):
        block = a.shape[0] // (B * nt)
        assert block * B * nt == a.shape[0] and block % 16 == 0
        return pl.BlockSpec((block, a.shape[1]), lambda b, t: (b * nt + t, 0))

    cast_specs = [walk(a) for a in cast_weights]
    outs = pl.pallas_call(
        functools.partial(_inproj_kernel, o_a=o_a, o_b=o_b, n_cast=len(cast_weights)),
        grid=(B, nt),
        in_specs=[tok(D_MODEL), pl.BlockSpec((1, 1, N_MOD * D_MODEL), mod_map), full(norm_g)]
        + [full(w) for w in wts] + cast_specs,
        out_specs=[tok(D_A), tok(D_A), tok(D_A), tok(D_A), tok(D_B), tok(D_MODEL), tok(D_MODEL), tok(GATE_PAD)]
        + cast_specs,
        out_shape=[sds(D_A, bf16), sds(D_A, bf16), sds(D_A, bf16), sds(D_A, bf16), sds(D_B, f32),
                   sds(D_MODEL, bf16), sds(D_MODEL, bf16), sds(GATE_PAD, f32)]
        + [jax.ShapeDtypeStruct(a.shape, bf16) for a in cast_weights],
        scratch_shapes=[pltpu.VMEM((D_MODEL, o_a), bf16), pltpu.VMEM((D_MODEL, n_b), bf16)],
        compiler_params=pltpu.CompilerParams(dimension_semantics=("arbitrary", "arbitrary"),
                                             vmem_limit_bytes=VMEM_LIMIT),
        name="in_proj",
    )(x, mod3, norm_g, *wts, *cast_weights)
    return [o.reshape(B0, T0, o.shape[-1]) for o in outs[:8]], outs[8:]


NEG_BIG = -1e30
ST_U, ST_INTER, ST_EMJ, ST_WKN, ST_DECAY = 0, 8, 16, 24, 32
ST_ROWS = 40


def _chunk_scan(x, op, fill, prefix, lane, width):
    k = 1
    while k < CHUNK:
        if prefix:
            shifted = jnp.where(lane >= k, pltpu.roll(x, k, axis=1), fill)
        else:
            shifted = jnp.where(lane < CHUNK - k, pltpu.roll(x, width - k, axis=1), fill)
        x = op(x, shifted)
        k *= 2
    return x


def _mlstm_kernel(*refs, nseq, nc, state_in, state_out):
    refs = list(refs)
    q_ref, k_ref, v_ref, g_ref = refs[:4]
    del refs[:4]
    if state_in:
        c0_ref, n0_ref, m0_ref = refs[:3]
        del refs[:3]
    hn_ref = refs.pop(0)
    if state_out:
        cn_ref, nn_ref, mn_ref = refs[:3]
        del refs[:3]
    st_ref, wt_ref, vt_ref, qt_ref, u_ref, cp_ref, npf_ref, npb_ref, cst_ref, nsf_ref, nsb_ref = refs
    L = CHUNK
    nct = nseq * nc
    T = nct * L
    NR = 2 * NH_A

    lane = jnp.bitwise_and(lax.broadcasted_iota(jnp.int32, (NR, T), 1), L - 1)
    is_fwd = lax.broadcasted_iota(jnp.int32, (NR, T), 0) < NH_A
    is_fwd_c = lax.broadcasted_iota(jnp.int32, (NR, L), 0) < NH_A
    i_parts, f_parts = [], []
    for c in range(nct):
        gt = g_ref[0, c * L:(c + 1) * L, :].T
        i_parts.append(gt[0:NR])
        f_parts.append(gt[NR:2 * NR])
    ig = jnp.concatenate(i_parts, axis=1)
    lf = _log_sigmoid(jnp.concatenate(f_parts, axis=1))
    scan = functools.partial(_chunk_scan, lane=lane, width=T)
    ps = scan(lf, jnp.add, 0.0, True)
    ss = scan(lf, jnp.add, 0.0, False)
    b = jnp.where(is_fwd, ps, ss)
    btot = ps + ss - lf
    w = ig - b
    cmw = jnp.where(is_fwd, scan(w, jnp.maximum, -jnp.inf, True), scan(w, jnp.maximum, -jnp.inf, False))
    wk = btot - b + ig
    a = jnp.maximum(scan(wk, jnp.maximum, -jnp.inf, True), scan(wk, jnp.maximum, -jnp.inf, False))
    chunk_lanes = lambda cg: slice(cg * L, (cg + 1) * L)
    pad_rows = jnp.zeros((L - NR, L), f32)
    for s in range(nseq):
        ms = [m0_ref[s] if state_in else jnp.zeros((NR, L), f32)]
        for t in range(nc):
            cf, cb = chunk_lanes(s * nc + t), chunk_lanes(s * nc + nc - 1 - t)
            bt = jnp.where(is_fwd_c, btot[:, cf], btot[:, cb])
            at = jnp.where(is_fwd_c, a[:, cf], a[:, cb])
            ms.append(jnp.maximum(bt + ms[-1], at))
        if state_out:
            mn_ref[s] = ms[nc]
        for c in range(nc):
            cg = s * nc + c
            sl = chunk_lanes(cg)
            m_prev = jnp.where(is_fwd_c, ms[c], ms[nc - 1 - c])
            m_new = jnp.where(is_fwd_c, ms[c + 1], ms[nc - c])
            mx = jnp.maximum(m_prev, cmw[:, sl])
            st_ref[cg, ST_U:ST_U + NR] = -mx
            st_ref[cg, ST_INTER:ST_INTER + NR] = jnp.exp(m_prev - mx)
            st_ref[cg, ST_EMJ:ST_EMJ + NR] = jnp.exp(-mx - b[:, sl])
            st_ref[cg, ST_WKN:ST_WKN + NR] = jnp.exp(wk[:, sl] - m_new)
            st_ref[cg, ST_DECAY:ST_DECAY + NR] = jnp.exp(btot[:, sl] + m_prev - m_new)
            wt_ref[cg] = jnp.concatenate([w[:, sl], pad_rows], axis=0).T

    row = lax.broadcasted_iota(jnp.int32, (L, L), 0)
    col = lax.broadcasted_iota(jnp.int32, (L, L), 1)
    masks = (row <= col, row >= col)
    first_of8 = lax.broadcasted_iota(jnp.int32, (8, L), 0) == 0

    def row_tile(x):
        return jnp.where(first_of8, x, 0.0)

    head_lanes = [slice(h * DH_A, (h + 1) * DH_A) for h in range(NH_A)]

    def increments(j, carry):
        rows = pl.ds(pl.multiple_of(j * L, L), L)
        for h in range(NH_A):
            kh = k_ref[0, rows, head_lanes[h]]
            vt = v_ref[0, rows, head_lanes[h]].astype(f32).T
            vt_ref[j, h] = vt.astype(bf16)
            qt_ref[j, h] = q_ref[0, rows, head_lanes[h]].astype(f32).T.astype(bf16)
            wf = st_ref[j, ST_WKN + h:ST_WKN + h + 1, :]
            wb = st_ref[j, ST_WKN + NH_A + h:ST_WKN + NH_A + h + 1, :]
            lhs = jnp.concatenate([vt * wf, vt * wb, row_tile(wf), row_tile(wb)], axis=0).astype(bf16)
            u_ref[j, h] = _dot(lhs, kh)
        return carry

    lax.fori_loop(0, nct, increments, 0, unroll=4)

    seq_heads = [(s, h) for s in range(nseq) for h in range(NH_A)]
    for s, h in seq_heads:
        i = s * NH_A + h
        if state_in:
            cst_ref[i, 0:DH_A] = c0_ref[s, 0, h].T
            cst_ref[i, DH_A:2 * DH_A] = c0_ref[s, 1, h].T
            nsf_ref[i] = row_tile(n0_ref[s, h:h + 1, :])
            nsb_ref[i] = row_tile(n0_ref[s, NH_A + h:NH_A + h + 1, :])
        else:
            cst_ref[i] = jnp.zeros((2 * DH_A, DH_A), f32)
            nsf_ref[i] = jnp.zeros((8, DH_A), f32)
            nsb_ref[i] = jnp.zeros((8, DH_A), f32)

    def recur(t, carry):
        for s, h in seq_heads:
            i = s * NH_A + h
            jf = s * nc + t
            jb = s * nc + nc - 1 - t
            dec_f = st_ref[jf, ST_DECAY + h:ST_DECAY + h + 1, :]
            dec_b = st_ref[jb, ST_DECAY + NH_A + h:ST_DECAY + NH_A + h + 1, :]
            c_f = cst_ref[i, 0:DH_A]
            c_b = cst_ref[i, DH_A:2 * DH_A]
            cp_ref[jf, h, 0:DH_A] = c_f.astype(bf16)
            cp_ref[jb, h, DH_A:2 * DH_A] = c_b.astype(bf16)
            cst_ref[i, 0:DH_A] = dec_f * c_f + u_ref[jf, h, 0:DH_A]
            cst_ref[i, DH_A:2 * DH_A] = dec_b * c_b + u_ref[jb, h, DH_A:2 * DH_A]
            n_f = nsf_ref[i]
            n_b = nsb_ref[i]
            npf_ref[jf, h] = n_f
            npb_ref[jb, h] = n_b
            nsf_ref[i] = dec_f * n_f + u_ref[jf, h, 2 * DH_A:2 * DH_A + 8]
            nsb_ref[i] = dec_b * n_b + u_ref[jb, h, 2 * DH_A + 8:2 * DH_A + 16]
        return carry

    lax.fori_loop(0, nc, recur, 0)
    for s, h in seq_heads if state_out else ():
        i = s * NH_A + h
        cn_ref[s, 0, h] = cst_ref[i, 0:DH_A].T
        cn_ref[s, 1, h] = cst_ref[i, DH_A:2 * DH_A].T
        nn_ref[s, h:h + 1, :] = nsf_ref[i, 0:1]
        nn_ref[s, NH_A + h:NH_A + h + 1, :] = nsb_ref[i, 0:1]

    def outputs(j, carry):
        rows = pl.ds(pl.multiple_of(j * L, L), L)
        wt = wt_ref[j]
        heads = range(NH_A)
        kqs = []
        for h in heads:
            kh = k_ref[0, rows, head_lanes[h]]
            n_rows = jnp.concatenate([npf_ref[j, h], npb_ref[j, h]], axis=0).astype(bf16)
            kqs.append(_dot(jnp.concatenate([kh, n_rows, cp_ref[j, h]], axis=0), qt_ref[j, h]))
        decays = [[jnp.exp(jnp.where(masks[d], wt[:, NH_A * d + h:NH_A * d + h + 1]
                                     + st_ref[j, ST_U + NH_A * d + h:ST_U + NH_A * d + h + 1, :], NEG_BIG))
                   for d in range(2)] for h in heads]
        s_sums, h_ts = [], []
        for h in heads:
            kq = kqs[h]
            s_sum = None
            h_t = None
            for d in range(2):
                r = NH_A * d + h
                inter = st_ref[j, ST_INTER + r:ST_INTER + r + 1, :]
                emj = st_ref[j, ST_EMJ + r:ST_EMJ + r + 1, :]
                s_t = kq[0:L] * decays[h][d]
                qn = kq[L + 8 * d:L + 8 * d + 1]
                den = inter * qn + jnp.sum(s_t, axis=0, keepdims=True)
                rr = 1.0 / jnp.maximum(jnp.abs(den), emj)
                s_sum = s_t * rr if d == 0 else s_sum + s_t * rr
                part = kq[L + 16 + d * DH_A:L + 16 + (d + 1) * DH_A] * (inter * rr)
                h_t = part if d == 0 else h_t + part
            s_sums.append(s_sum.astype(bf16))
            h_ts.append(h_t)
        h_ts = [h_ts[h] + _dot(vt_ref[j, h], s_sums[h]) for h in heads]
        h_ts = [x * lax.rsqrt(jnp.mean(x * x, axis=0, keepdims=True) + EPS) for x in h_ts]
        for h in heads:
            hn_ref[0, rows, head_lanes[h]] = h_ts[h].T.astype(bf16)
        return carry

    lax.fori_loop(0, nct, outputs, 0, unroll=4)


def _mlstm(q, k, v, gates, state, nseq, state_out):
    B, T, _ = q.shape
    nc = T // CHUNK
    nct = nseq * nc
    G = B // nseq
    fold = lambda a: a.reshape(G, nseq * T, a.shape[-1])
    seq = lambda n: pl.BlockSpec((1, nseq * T, n), lambda b: (b, 0, 0))
    st_c = pl.BlockSpec((nseq, 2, NH_A, DH_A, DH_A), lambda b: (b, 0, 0, 0, 0))
    st_v = pl.BlockSpec((nseq, 2 * NH_A, DH_A), lambda b: (b, 0, 0))
    state_specs = [st_c, st_v, st_v]
    state_shapes = [jax.ShapeDtypeStruct((B, 2, NH_A, DH_A, DH_A), f32),
                    jax.ShapeDtypeStruct((B, 2 * NH_A, DH_A), f32),
                    jax.ShapeDtypeStruct((B, 2 * NH_A, DH_A), f32)]
    outs = pl.pallas_call(
        functools.partial(_mlstm_kernel, nseq=nseq, nc=nc, state_in=state is not None, state_out=state_out),
        grid=(G,),
        in_specs=[seq(D_A), seq(D_A), seq(D_A), seq(GATE_PAD)] + (state_specs if state is not None else []),
        out_specs=[seq(D_A)] + (state_specs if state_out else []),
        scratch_shapes=[pltpu.VMEM((nct, ST_ROWS, CHUNK), f32), pltpu.VMEM((nct, CHUNK, GATE_PAD), f32),
                        pltpu.VMEM((nct, NH_A, DH_A, CHUNK), bf16), pltpu.VMEM((nct, NH_A, DH_A, CHUNK), bf16),
                        pltpu.VMEM((nct, NH_A, 2 * DH_A + 16, DH_A), f32),
                        pltpu.VMEM((nct, NH_A, 2 * DH_A, DH_A), bf16),
                        pltpu.VMEM((nct, NH_A, 8, DH_A), f32), pltpu.VMEM((nct, NH_A, 8, DH_A), f32),
                        pltpu.VMEM((nseq * NH_A, 2 * DH_A, DH_A), f32),
                        pltpu.VMEM((nseq * NH_A, 8, DH_A), f32), pltpu.VMEM((nseq * NH_A, 8, DH_A), f32)],
        out_shape=[jax.ShapeDtypeStruct((G, nseq * T, D_A), bf16)] + (state_shapes if state_out else []),
        compiler_params=pltpu.CompilerParams(dimension_semantics=("arbitrary",), vmem_limit_bytes=VMEM_LIMIT),
        name="mlstm",
    )(fold(q), fold(k), fold(v), fold(gates), *(state if state is not None else ()))
    return (outs[0].reshape(B, T, D_A),) + tuple(outs[1:])


def _mix_kernel(x_ref, mod_ref, hn_ref, so_ref, glu_ref, glu_prev_ref, glu_next_ref, sga_ref, sgb_ref,
                ng_ref, wpa_ref, cw_ref, cb_ref, lng_ref, lnb_ref, wpb_ref, wout_ref,
                x1_ref, ext_ref, zs_ref, *, tm, nt, nsub):
    t = pl.program_id(1)
    mod = mod_ref[0]
    g1 = mod[:, 2 * D_MODEL:3 * D_MODEL]

    a_in = (so_ref[0].astype(f32) * (hn_ref[0].astype(f32) * ng_ref[...])).astype(bf16)
    branch_a = _dot(a_in, wpa_ref[...])

    H = CONV_HALO
    SUB, LANE = SUBLANES, LANES
    tsub = tm // nsub
    te = tsub + 2 * H
    for s in range(nsub):
        inner = nsub == 1
        ext_ref[0, s, 0:H, :] = jnp.where(t > 0, glu_prev_ref[0], 0.0) if inner else jnp.zeros((H, D_B), f32)
        ext_ref[0, s, H:H + tsub, :] = glu_ref[0, s * tsub:(s + 1) * tsub, :]
        ext_ref[0, s, H + tsub:te, :] = jnp.where(t < nt - 1, glu_next_ref[0], 0.0) if inner else jnp.zeros((H, D_B), f32)
    for s in range(nsub):
        for c in range(D_B // LANE):
            lanes = slice(c * LANE, (c + 1) * LANE)
            base = ext_ref[0, s, :, lanes]
            for k in range(1, SUB):
                ext_ref[k, s, :, lanes] = pltpu.roll(base, te - k, axis=0)
    off = H - CONV_W // 2
    for s in range(nsub):
        for r0 in range(0, tsub, CONV_ROWS):
            parts = []
            for c in range(D_B // LANE):
                lanes = slice(c * LANE, (c + 1) * LANE)
                acc = jnp.broadcast_to(cb_ref[:, lanes], (CONV_ROWS, LANE))
                for w in range(CONV_W):
                    k, a = (off + w) % SUB, (off + w) // SUB * SUB
                    acc = acc + ext_ref[k, s, r0 + a:r0 + a + CONV_ROWS, lanes] * cw_ref[w:w + 1, lanes]
                parts.append(acc)
            z = jnp.concatenate(parts, axis=1)
            mu = jnp.mean(z, axis=-1, keepdims=True)
            zc = z - mu
            var = jnp.mean(zc * zc, axis=-1, keepdims=True)
            zn = zc * lax.rsqrt(var + EPS) * lng_ref[...] + lnb_ref[...]
            zs_ref[s * tsub + r0:s * tsub + r0 + CONV_ROWS, :] = (zn * _sigmoid(zn)).astype(bf16)
    branch_b = _dot(zs_ref[...], wpb_ref[...])

    merged = sga_ref[0].astype(f32) * branch_a + sgb_ref[0].astype(f32) * branch_b
    x1_ref[0] = x_ref[0] + g1 * _dot(merged.astype(bf16), wout_ref[...])


def _mix_out(x, mod3, per_batch_mod, hn, so, glu, sga, sgb, wts, tm):
    B0, T0, _ = x.shape
    nsub = 1
    if tm > T0:
        assert not per_batch_mod and tm % T0 == 0 and B0 % (tm // T0) == 0
        nsub = tm // T0
        x, hn, so, glu, sga, sgb = (a.reshape(B0 // nsub, tm, a.shape[-1]) for a in (x, hn, so, glu, sga, sgb))
    B, T, _ = x.shape
    nt = T // tm
    H = CONV_HALO
    r = tm // H
    tok = lambda n: pl.BlockSpec((1, tm, n), lambda b, t: (b, t, 0))
    full = lambda a: pl.BlockSpec(a.shape, lambda b, t: (0,) * a.ndim)
    mod_map = (lambda b, t: (b, 0, 0)) if per_batch_mod else (lambda b, t: (0, 0, 0))
    prev = pl.BlockSpec((1, H, D_B), lambda b, t: (b, jnp.maximum(t * r - 1, 0), 0))
    nxt = pl.BlockSpec((1, H, D_B), lambda b, t: (b, jnp.minimum((t + 1) * r, T // H - 1), 0))
    x1 = pl.pallas_call(
        functools.partial(_mix_kernel, tm=tm, nt=nt, nsub=nsub),
        grid=(B, nt),
        in_specs=[tok(D_MODEL), pl.BlockSpec((1, 1, N_MOD * D_MODEL), mod_map), tok(D_A), tok(D_A),
                  tok(D_B), prev, nxt, tok(D_MODEL), tok(D_MODEL)] + [full(w) for w in wts],
        out_specs=tok(D_MODEL),
        out_shape=jax.ShapeDtypeStruct((B, T, D_MODEL), f32),
        scratch_shapes=[pltpu.VMEM((SUBLANES, nsub, tm // nsub + 2 * H, D_B), f32), pltpu.VMEM((tm, D_B), bf16)],
        compiler_params=pltpu.CompilerParams(dimension_semantics=("arbitrary", "arbitrary"),
                                             vmem_limit_bytes=VMEM_LIMIT),
        name="mix_out",
    )(x, mod3, hn, so, glu, glu, glu, sga, sgb, *wts)
    return x1.reshape(B0, T0, D_MODEL)


def _ffn_kernel(*refs, tm, nt, on_grid, seq_len):
    if on_grid:
        (x_ref, xp_ref, xn_ref, mod_ref, g2n_ref, wu_ref, wd_ref, cw_ref, cb_ref, fg_ref,
         y_ref, h_ref, g0_ref, g1_ref, v0_ref, v1_ref, gl_ref, gr_ref, a0_ref, a1_ref, a2_ref, a3_ref, acc_ref) = refs
    else:
        (x_ref, mod_ref, g2n_ref, wu_ref, wd_ref, cw_ref, cb_ref, fg_ref,
         y_ref, h_ref, g0_ref, g1_ref, v0_ref, v1_ref, gl_ref, gr_ref, a0_ref, a1_ref, a2_ref, a3_ref, acc_ref) = refs
    gbufs, vbufs, acts = (g0_ref, g1_ref), (v0_ref, v1_ref), (a0_ref, a1_ref, a2_ref, a3_ref)
    t = pl.program_id(1)
    halo = GRID_W if on_grid else 0
    te = tm + 2 * halo
    mod = mod_ref[0]
    sh2 = mod[:, 3 * D_MODEL:4 * D_MODEL]
    sc2 = mod[:, 4 * D_MODEL:5 * D_MODEL]
    g2 = mod[:, 5 * D_MODEL:6 * D_MODEL]

    def norm_mod(x):
        return _rms(x, g2n_ref[...]) * (1.0 + sc2) + sh2

    h_ref[halo:halo + tm, :] = norm_mod(x_ref[0]).astype(bf16)
    if on_grid:
        h_ref[0:halo, :] = jnp.where(t > 0, norm_mod(xp_ref[0]), 0.0).astype(bf16)
        h_ref[halo + tm:te, :] = jnp.where(t < nt - 1, norm_mod(xn_ref[0]), 0.0).astype(bf16)

    SUB, LANE = SUBLANES, LANES
    seg = GRID_W if on_grid else seq_len
    sub_row = lax.broadcasted_iota(jnp.int32, (SUB, LANE), 0)
    first_row = sub_row == 0
    last_row = sub_row == SUB - 1
    row_taps = (0, 1, 2) if on_grid else (1,)

    def ff_cols(fc, base):
        if isinstance(fc, int):
            return slice(base + fc * FF_CHUNK, base + (fc + 1) * FF_CHUNK)
        return pl.ds(pl.multiple_of(base + fc * FF_CHUNK, 128), FF_CHUNK)

    def up(fc, slot):
        gbufs[slot][...] = _dot(h_ref[...], wu_ref[:, ff_cols(fc, 0)])
        vbufs[slot][...] = _dot(h_ref[halo:halo + tm, :], wu_ref[:, ff_cols(fc, D_FF)])

    def gate_act(fc, slot, aslot):
        cw = cw_ref[:, ff_cols(fc, 0)]
        cb = cb_ref[:, ff_cols(fc, 0)]
        for c in range(FF_CHUNK // LANE):
            lanes = slice(c * LANE, (c + 1) * LANE)
            for s0 in range(0, te, seg):
                gate = gbufs[slot][s0:s0 + seg, lanes]
                g_l = pltpu.roll(gate, 1, axis=0)
                g_r = pltpu.roll(gate, seg - 1, axis=0)
                gl_ref[s0:s0 + SUB, lanes] = jnp.where(first_row, 0.0, g_l[0:SUB])
                gl_ref[s0 + SUB:s0 + seg, lanes] = g_l[SUB:seg]
                gr_ref[s0:s0 + seg - SUB, lanes] = g_r[0:seg - SUB]
                gr_ref[s0 + seg - SUB:s0 + seg, lanes] = jnp.where(last_row, 0.0, g_r[seg - SUB:seg])
            for r0 in range(0, tm, GRID_W):
                conv = jnp.broadcast_to(cb[:, lanes], (GRID_W, LANE))
                for kh in row_taps:
                    lo = r0 + kh * GRID_W if on_grid else r0
                    conv = conv + gl_ref[lo:lo + GRID_W, lanes] * cw[3 * kh:3 * kh + 1, lanes]
                    conv = conv + gbufs[slot][lo:lo + GRID_W, lanes] * cw[3 * kh + 1:3 * kh + 2, lanes]
                    conv = conv + gr_ref[lo:lo + GRID_W, lanes] * cw[3 * kh + 2:3 * kh + 3, lanes]
                gelu = conv * (0.5 + 0.5 * jnp.tanh(conv * (GELU_C0 + GELU_C1 * (conv * conv))))
                acts[aslot][r0:r0 + GRID_W, lanes] = (gelu * vbufs[slot][r0:r0 + GRID_W, lanes]).astype(bf16)

    def down(fc, aslot):
        return _dot(acts[aslot][...], wd_ref[fc])


    def pair(p, parity, with_down):
        fc = 2 * p
        wr, rd = 2 * parity, 2 * (1 - parity)
        if with_down:
            acc_ref[...] += down(fc - 2, rd) + down(fc - 1, rd + 1)
        up(fc + 1, 1)
        gate_act(fc, 0, wr)
        up(fc + 2, 0)
        gate_act(fc + 1, 1, wr + 1)

    n_pairs = (N_FF_CHUNKS - 1) // 2
    assert N_FF_CHUNKS == 2 * n_pairs + 1 and n_pairs % 2 == 1
    up(0, 0)
    acc_ref[...] = jnp.zeros_like(acc_ref)
    pair(0, 0, False)

    def two_pairs(i, carry):
        p = 2 * i + 1
        pair(p, 1, True)
        pair(p + 1, 0, True)
        return carry

    lax.fori_loop(0, (n_pairs - 1) // 2, two_pairs, 0)
    last = N_FF_CHUNKS - 1
    gate_act(last, 0, 2)
    ffn_out = acc_ref[...] + (down(last - 2, 0) + down(last - 1, 1) + down(last, 2))
    y_ref[0] = _rms(x_ref[0] + g2 * ffn_out, fg_ref[...])


def _ffn(x1, mod3, per_batch_mod, wts, tm, on_grid):
    B0, T0, _ = x1.shape
    if tm > T0:
        assert not per_batch_mod and not on_grid and tm % T0 == 0 and B0 % (tm // T0) == 0
        x1 = x1.reshape(B0 * T0 // tm, tm, D_MODEL)
    B, T, _ = x1.shape
    nt = T // tm
    halo = GRID_W if on_grid else 0
    te = tm + 2 * halo
    tok = pl.BlockSpec((1, tm, D_MODEL), lambda b, t: (b, t, 0))
    full = lambda a: pl.BlockSpec(a.shape, lambda b, t: (0,) * a.ndim)
    mod_map = (lambda b, t: (b, 0, 0)) if per_batch_mod else (lambda b, t: (0, 0, 0))
    in_specs = [tok]
    args = [x1]
    if on_grid:
        r = tm // GRID_W
        in_specs += [pl.BlockSpec((1, GRID_W, D_MODEL), lambda b, t: (b, jnp.maximum(t * r - 1, 0), 0)),
                     pl.BlockSpec((1, GRID_W, D_MODEL), lambda b, t: (b, jnp.minimum((t + 1) * r, T // GRID_W - 1), 0))]
        args += [x1, x1]
    in_specs += [pl.BlockSpec((1, 1, N_MOD * D_MODEL), mod_map)] + [full(w) for w in wts]
    args += [mod3] + list(wts)
    y = pl.pallas_call(
        functools.partial(_ffn_kernel, tm=tm, nt=nt, on_grid=on_grid, seq_len=T0),
        grid=(B, nt),
        in_specs=in_specs,
        out_specs=tok,
        out_shape=jax.ShapeDtypeStruct((B, T, D_MODEL), f32),
        scratch_shapes=[pltpu.VMEM((te, D_MODEL), bf16)]
        + [pltpu.VMEM((te, FF_CHUNK), f32)] * 2 + [pltpu.VMEM((tm, FF_CHUNK), f32)] * 2
        + [pltpu.VMEM((te, FF_CHUNK), f32)] * 2 + [pltpu.VMEM((tm, FF_CHUNK), bf16)] * 4
        + [pltpu.VMEM((tm, D_MODEL), f32)],
        compiler_params=pltpu.CompilerParams(dimension_semantics=("arbitrary", "arbitrary"),
                                             vmem_limit_bytes=VMEM_LIMIT),
        name="ffn_grid" if on_grid else "ffn_ctx",
    )(*args)
    return y.reshape(B0, T0, D_MODEL)


TOKEN_TILE = 512
MLSTM_TOKENS = 1024


def _tiles(T, per_batch_mod):
    span = TOKEN_TILE if not per_batch_mod else min(TOKEN_TILE, T)
    return span, span, span, max(1, MLSTM_TOKENS // T)


def _trunk(x, mod3, per_batch_mod, state, on_grid, w, cast_weights=()):
    tm_proj, tm_mix, tm_ffn, mlstm_nseq = _tiles(x.shape[1], per_batch_mod)
    (q, k, v, so, glu, sga, sgb, gates), casts = _in_proj(x, mod3, per_batch_mod, w["norm1_g"], w["in"], tm_proj,
                                                          *w["in_groups"], cast_weights=cast_weights)
    if cast_weights:
        w = dict(w, mix=w["mix"](casts), ffn=w["ffn"](casts))
    hn, *new_state = _mlstm(q, k, v, gates, state, mlstm_nseq, state is None)
    x1 = _mix_out(x, mod3, per_batch_mod, hn, so, glu, sga, sgb, w["mix"], tm_mix)
    y = _ffn(x1, mod3, per_batch_mod, w["ffn"], tm_ffn, on_grid)
    return y, new_state, w


def kernel(x_prompt, x_sample, c, state_C, state_n, state_m, c_ctx, w_ada, b_ada, norm1_g, w_in, b_in,
           mlstm_norm_g, w_proj_a, conv_dw_w, conv_dw_b, conv_ln_g, conv_ln_b, w_proj_b, w_out, norm2_g,
           w_up, ffn_dw_w, ffn_dw_b, w_down, final_norm_g):
    Bp = x_prompt.shape[0]
    Bl = x_sample.shape[0]
    l = 0
    row = lambda a: a.reshape(1, -1)

    ct = jnp.concatenate([c, c_ctx[None, :], jnp.zeros((8 - Bl - 1, D_MODEL), f32)], axis=0).T
    mod = _adaln_mod(ct, w_ada[l], row(b_ada[l]), Bl + 1)
    mod_lat = mod[0:Bl].reshape(Bl, 1, N_MOD * D_MODEL)
    mod_ctx = mod[Bl:Bl + 1].reshape(1, 1, N_MOD * D_MODEL)

    o_g = 4 * D_A
    o_u = o_g + 4 * NH_A
    nh = NH_A

    def gate_cols(a):
        g = a[..., o_g:o_u]
        pad = jnp.zeros(a.shape[:-1] + (GATE_PAD - 4 * nh,), a.dtype)
        return jnp.concatenate([g[..., 0:nh], g[..., 2 * nh:3 * nh], g[..., nh:2 * nh], g[..., 3 * nh:], pad], axis=-1)

    w_in_b = w_in[l].astype(bf16)
    bi = row(b_in[l])
    w_in_parts = (w_in_b, bi[:, :o_g], bi[:, o_u:], gate_cols(w_in_b), gate_cols(bi))
    cast_weights = [w_up[l], w_down[l], w_out[l], w_proj_a[l], w_proj_b[l]]

    def w_mix(casts):
        return (row(mlstm_norm_g[l]), casts[3],
                jnp.pad(conv_dw_w[l], ((0, 1), (0, 0))), row(conv_dw_b[l]), row(conv_ln_g[l]), row(conv_ln_b[l]),
                casts[4], casts[2])

    def w_ffn(casts):
        return (row(norm2_g[l]), casts[0], casts[1].reshape(N_FF_CHUNKS, FF_CHUNK, D_MODEL),
                ffn_dw_w[l].reshape(9, D_FF), row(ffn_dw_b[l]), row(final_norm_g))

    w = {"norm1_g": row(norm1_g[l]), "in": w_in_parts, "in_groups": (o_g, o_u), "mix": w_mix, "ffn": w_ffn}

    c0 = state_C[:, l]
    n0 = state_n[:, l].reshape(Bl, 2 * NH_A, DH_A)
    m0 = jnp.broadcast_to(state_m[:, l].reshape(Bl, 2 * NH_A, 1), (Bl, 2 * NH_A, DH_A))
    y_sample, _, w = _trunk(x_sample, mod_lat, True, (c0, n0, m0), True, w, cast_weights)

    y_prompt, (cn, nn, mn), _ = _trunk(x_prompt, mod_ctx, False, None, False, w)

    new_state_C = cn[:, None]
    new_state_n = nn.reshape(Bp, 1, 2, NH_A, DH_A)
    new_state_m = mn[:, :, 0].reshape(Bp, 1, 2, NH_A)
    return (y_prompt, y_sample, new_state_C, new_state_n, new_state_m)
```

```python
import functools

import jax
import jax.numpy as jnp
from jax import lax
from jax.experimental import pallas as pl
from jax.experimental.pallas import tpu as pltpu

D_MODEL = 1024
D_A = 512
NH_A = 4
DH_A = 128
CHUNK = 128
D_B = 512
CONV_W = 31
CONV_HALO = 16
CONV_ROWS = 64
D_FF = 2816
FF_CHUNK = 256
N_FF_CHUNKS = D_FF // FF_CHUNK
GRID_W = 64
N_MOD = 6
GATE_PAD = 128
EPS = 1e-6
Q_SCALE = DH_A ** -0.5
VMEM_LIMIT = 56 * 1024 * 1024
SUBLANES = 8
LANES = 128

f32 = jnp.float32
bf16 = jnp.bfloat16


def _rms(x, g):
    return x * lax.rsqrt(jnp.mean(x * x, axis=-1, keepdims=True) + EPS) * g


def _sigmoid(x):
    return 1.0 / (1.0 + jnp.exp(-x))


def _log_sigmoid(x):
    return jnp.minimum(x, 0.0) - jnp.log(1.0 + jnp.exp(-jnp.abs(x)))


GELU_C0 = 0.7978845608028654
GELU_C1 = GELU_C0 * 0.044715


def _dot(a, b):
    return jnp.dot(a, b, preferred_element_type=f32)


def _mod_kernel(ct_ref, w_ref, b_ref, o_ref, *, n_rows):
    ct = ct_ref[...]
    st = ct * _sigmoid(ct)
    w = w_ref[...]
    rows = []
    for r in range(n_rows):
        s_col = jnp.broadcast_to(st[:, r:r + 1], (D_MODEL, LANES))
        rows.append(jnp.concatenate(
            [jnp.sum(w[:, c:c + LANES] * s_col, axis=0, keepdims=True) for c in range(0, w.shape[1], LANES)], axis=1))
    rows.append(jnp.zeros((8 - n_rows, w.shape[1]), f32))
    o_ref[...] = jnp.concatenate(rows, axis=0) + b_ref[...]


def _adaln_mod(ct, w_ada, b_ada, n_rows):
    n = w_ada.shape[1]
    tn = 512
    return pl.pallas_call(
        functools.partial(_mod_kernel, n_rows=n_rows),
        grid=(n // tn,),
        in_specs=[
            pl.BlockSpec((D_MODEL, 8), lambda j: (0, 0)),
            pl.BlockSpec((D_MODEL, tn), lambda j: (0, j)),
            pl.BlockSpec((1, tn), lambda j: (0, j)),
        ],
        out_specs=pl.BlockSpec((8, tn), lambda j: (0, j)),
        out_shape=jax.ShapeDtypeStruct((8, n), f32),
        compiler_params=pltpu.CompilerParams(dimension_semantics=("arbitrary",), vmem_limit_bytes=VMEM_LIMIT),
        name="adaln_mod",
    )(ct, w_ada, b_ada)


def _inproj_kernel(*refs, o_a, o_b, n_cast):
    x_ref, mod_ref, g_ref, w_ref, ba_ref, bb_ref, wg_ref, bg_ref = refs[:8]
    cast_src = refs[8:8 + n_cast]
    q_ref, k_ref, v_ref, so_ref, glu_ref, sga_ref, sgb_ref, gates_ref = refs[8 + n_cast:16 + n_cast]
    cast_dst = refs[16 + n_cast:16 + 2 * n_cast]
    wa_ref, wb_ref = refs[16 + 2 * n_cast:]
    for src, dst in zip(cast_src, cast_dst):
        dst[...] = src[...].astype(bf16)

    @pl.when(jnp.logical_and(pl.program_id(0) == 0, pl.program_id(1) == 0))
    def _():
        wa_ref[...] = w_ref[:, 0:o_a]
        wb_ref[...] = w_ref[:, o_b:]

    x = x_ref[0]
    mod = mod_ref[0]
    sh1 = mod[:, 0:D_MODEL]
    sc1 = mod[:, D_MODEL:2 * D_MODEL]
    h = (_rms(x, g_ref[...]) * (1.0 + sc1) + sh1).astype(bf16)

    def proj(w_ref, b_ref, lo, n):
        return _dot(h, w_ref[:, lo:lo + n]) + b_ref[:, lo:lo + n]

    q_ref[0] = (proj(wa_ref, ba_ref, 0, D_A) * Q_SCALE).astype(bf16)
    k_ref[0] = proj(wa_ref, ba_ref, D_A, D_A).astype(bf16)
    v_ref[0] = proj(wa_ref, ba_ref, 2 * D_A, D_A).astype(bf16)
    so_ref[0] = _sigmoid(proj(wa_ref, ba_ref, 3 * D_A, D_A)).astype(bf16)
    glu_ref[0] = proj(wb_ref, bb_ref, 0, D_B) * _sigmoid(proj(wb_ref, bb_ref, D_B, D_B))
    sga_ref[0] = _sigmoid(proj(wb_ref, bb_ref, 2 * D_B, D_MODEL)).astype(bf16)
    sgb_ref[0] = _sigmoid(proj(wb_ref, bb_ref, 2 * D_B + D_MODEL, D_MODEL)).astype(bf16)
    gates_ref[0] = proj(wg_ref, bg_ref, 0, GATE_PAD)


def _in_proj(x, mod3, per_batch_mod, norm_g, wts, tm, o_a, o_b, cast_weights=()):
    B0, T0, _ = x.shape
    if tm > T0:
        assert not per_batch_mod and tm % T0 == 0 and B0 % (tm // T0) == 0
        x = x.reshape(B0 * T0 // tm, tm, D_MODEL)
    B, T, _ = x.shape
    nt = T // tm
    tok = lambda n: pl.BlockSpec((1, tm, n), lambda b, t: (b, t, 0))
    full = lambda a: pl.BlockSpec(a.shape, lambda b, t: (0,) * a.ndim)
    mod_map = (lambda b, t: (b, 0, 0)) if per_batch_mod else (lambda b, t: (0, 0, 0))
    sds = lambda n, dt: jax.ShapeDtypeStruct((B, T, n), dt)
    n_b = wts[0].shape[1] - o_b

    def walk(a):
        block = a.shape[0] // (B * nt)
        assert block * B * nt == a.shape[0] and block % 16 == 0
        return pl.BlockSpec((block, a.shape[1]), lambda b, t: (b * nt + t, 0))

    cast_specs = [walk(a) for a in cast_weights]
    outs = pl.pallas_call(
        functools.partial(_inproj_kernel, o_a=o_a, o_b=o_b, n_cast=len(cast_weights)),
        grid=(B, nt),
        in_specs=[tok(D_MODEL), pl.BlockSpec((1, 1, N_MOD * D_MODEL), mod_map), full(norm_g)]
        + [full(w) for w in wts] + cast_specs,
        out_specs=[tok(D_A), tok(D_A), tok(D_A), tok(D_A), tok(D_B), tok(D_MODEL), tok(D_MODEL), tok(GATE_PAD)]
        + cast_specs,
        out_shape=[sds(D_A, bf16), sds(D_A, bf16), sds(D_A, bf16), sds(D_A, bf16), sds(D_B, f32),
                   sds(D_MODEL, bf16), sds(D_MODEL, bf16), sds(GATE_PAD, f32)]
        + [jax.ShapeDtypeStruct(a.shape, bf16) for a in cast_weights],
        scratch_shapes=[pltpu.VMEM((D_MODEL, o_a), bf16), pltpu.VMEM((D_MODEL, n_b), bf16)],
        compiler_params=pltpu.CompilerParams(dimension_semantics=("arbitrary", "arbitrary"),
                                             vmem_limit_bytes=VMEM_LIMIT),
        name="in_proj",
    )(x, mod3, norm_g, *wts, *cast_weights)
    return [o.reshape(B0, T0, o.shape[-1]) for o in outs[:8]], outs[8:]


NEG_BIG = -1e30
ST_U, ST_INTER, ST_EMJ, ST_WKN, ST_DECAY = 0, 8, 16, 24, 32
ST_ROWS = 40


def _chunk_scan(x, op, fill, prefix, lane, width):
    k = 1
    while k < CHUNK:
        if prefix:
            shifted = jnp.where(lane >= k, pltpu.roll(x, k, axis=1), fill)
        else:
            shifted = jnp.where(lane < CHUNK - k, pltpu.roll(x, width - k, axis=1), fill)
        x = op(x, shifted)
        k *= 2
    return x


def _mlstm_kernel(*refs, nseq, nc, state_in, state_out):
    refs = list(refs)
    q_ref, k_ref, v_ref, g_ref = refs[:4]
    del refs[:4]
    if state_in:
        c0_ref, n0_ref, m0_ref = refs[:3]
        del refs[:3]
    hn_ref = refs.pop(0)
    if state_out:
        cn_ref, nn_ref, mn_ref = refs[:3]
        del refs[:3]
    st_ref, wt_ref, vt_ref, qt_ref, u_ref, cp_ref, npf_ref, npb_ref, cst_ref, nsf_ref, nsb_ref = refs
    L = CHUNK
    nct = nseq * nc
    T = nct * L
    NR = 2 * NH_A

    lane = jnp.bitwise_and(lax.broadcasted_iota(jnp.int32, (NR, T), 1), L - 1)
    is_fwd = lax.broadcasted_iota(jnp.int32, (NR, T), 0) < NH_A
    is_fwd_c = lax.broadcasted_iota(jnp.int32, (NR, L), 0) < NH_A
    i_parts, f_parts = [], []
    for c in range(nct):
        gt = g_ref[0, c * L:(c + 1) * L, :].T
        i_parts.append(gt[0:NR])
        f_parts.append(gt[NR:2 * NR])
    ig = jnp.concatenate(i_parts, axis=1)
    lf = _log_sigmoid(jnp.concatenate(f_parts, axis=1))
    scan = functools.partial(_chunk_scan, lane=lane, width=T)
    ps = scan(lf, jnp.add, 0.0, True)
    ss = scan(lf, jnp.add, 0.0, False)
    b = jnp.where(is_fwd, ps, ss)
    btot = ps + ss - lf
    w = ig - b
    cmw = jnp.where(is_fwd, scan(w, jnp.maximum, -jnp.inf, True), scan(w, jnp.maximum, -jnp.inf, False))
    wk = btot - b + ig
    a = jnp.maximum(scan(wk, jnp.maximum, -jnp.inf, True), scan(wk, jnp.maximum, -jnp.inf, False))
    chunk_lanes = lambda cg: slice(cg * L, (cg + 1) * L)
    pad_rows = jnp.zeros((L - NR, L), f32)
    for s in range(nseq):
        ms = [m0_ref[s] if state_in else jnp.zeros((NR, L), f32)]
        for t in range(nc):
            cf, cb = chunk_lanes(s * nc + t), chunk_lanes(s * nc + nc - 1 - t)
            bt = jnp.where(is_fwd_c, btot[:, cf], btot[:, cb])
            at = jnp.where(is_fwd_c, a[:, cf], a[:, cb])
            ms.append(jnp.maximum(bt + ms[-1], at))
        if state_out:
            mn_ref[s] = ms[nc]
        for c in range(nc):
            cg = s * nc + c
            sl = chunk_lanes(cg)
            m_prev = jnp.where(is_fwd_c, ms[c], ms[nc - 1 - c])
            m_new = jnp.where(is_fwd_c, ms[c + 1], ms[nc - c])
            mx = jnp.maximum(m_prev, cmw[:, sl])
            st_ref[cg, ST_U:ST_U + NR] = -mx
            st_ref[cg, ST_INTER:ST_INTER + NR] = jnp.exp(m_prev - mx)
            st_ref[cg, ST_EMJ:ST_EMJ + NR] = jnp.exp(-mx - b[:, sl])
            st_ref[cg, ST_WKN:ST_WKN + NR] = jnp.exp(wk[:, sl] - m_new)
            st_ref[cg, ST_DECAY:ST_DECAY + NR] = jnp.exp(btot[:, sl] + m_prev - m_new)
            wt_ref[cg] = jnp.concatenate([w[:, sl], pad_rows], axis=0).T

    row = lax.broadcasted_iota(jnp.int32, (L, L), 0)
    col = lax.broadcasted_iota(jnp.int32, (L, L), 1)
    masks = (row <= col, row >= col)
    first_of8 = lax.broadcasted_iota(jnp.int32, (8, L), 0) == 0

    def row_tile(x):
        return jnp.where(first_of8, x, 0.0)

    head_lanes = [slice(h * DH_A, (h + 1) * DH_A) for h in range(NH_A)]

    def increments(j, carry):
        rows = pl.ds(pl.multiple_of(j * L, L), L)
        for h in range(NH_A):
            kh = k_ref[0, rows, head_lanes[h]]
            vt = v_ref[0, rows, head_lanes[h]].astype(f32).T
            vt_ref[j, h] = vt.astype(bf16)
            qt_ref[j, h] = q_ref[0, rows, head_lanes[h]].astype(f32).T.astype(bf16)
            wf = st_ref[j, ST_WKN + h:ST_WKN + h + 1, :]
            wb = st_ref[j, ST_WKN + NH_A + h:ST_WKN + NH_A + h + 1, :]
            lhs = jnp.concatenate([vt * wf, vt * wb, row_tile(wf), row_tile(wb)], axis=0).astype(bf16)
            u_ref[j, h] = _dot(lhs, kh)
        return carry

    lax.fori_loop(0, nct, increments, 0, unroll=4)

    seq_heads = [(s, h) for s in range(nseq) for h in range(NH_A)]
    for s, h in seq_heads:
        i = s * NH_A + h
        if state_in:
            cst_ref[i, 0:DH_A] = c0_ref[s, 0, h].T
            cst_ref[i, DH_A:2 * DH_A] = c0_ref[s, 1, h].T
            nsf_ref[i] = row_tile(n0_ref[s, h:h + 1, :])
            nsb_ref[i] = row_tile(n0_ref[s, NH_A + h:NH_A + h + 1, :])
        else:
            cst_ref[i] = jnp.zeros((2 * DH_A, DH_A), f32)
            nsf_ref[i] = jnp.zeros((8, DH_A), f32)
            nsb_ref[i] = jnp.zeros((8, DH_A), f32)

    def recur(t, carry):
        for s, h in seq_heads:
            i = s * NH_A + h
            jf = s * nc + t
            jb = s * nc + nc - 1 - t
            dec_f = st_ref[jf, ST_DECAY + h:ST_DECAY + h + 1, :]
            dec_b = st_ref[jb, ST_DECAY + NH_A + h:ST_DECAY + NH_A + h + 1, :]
            c_f = cst_ref[i, 0:DH_A]
            c_b = cst_ref[i, DH_A:2 * DH_A]
            cp_ref[jf, h, 0:DH_A] = c_f.astype(bf16)
            cp_ref[jb, h, DH_A:2 * DH_A] = c_b.astype(bf16)
            cst_ref[i, 0:DH_A] = dec_f * c_f + u_ref[jf, h, 0:DH_A]
            cst_ref[i, DH_A:2 * DH_A] = dec_b * c_b + u_ref[jb, h, DH_A:2 * DH_A]
            n_f = nsf_ref[i]
            n_b = nsb_ref[i]
            npf_ref[jf, h] = n_f
            npb_ref[jb, h] = n_b
            nsf_ref[i] = dec_f * n_f + u_ref[jf, h, 2 * DH_A:2 * DH_A + 8]
            nsb_ref[i] = dec_b * n_b + u_ref[jb, h, 2 * DH_A + 8:2 * DH_A + 16]
        return carry

    lax.fori_loop(0, nc, recur, 0)
    for s, h in seq_heads if state_out else ():
        i = s * NH_A + h
        cn_ref[s, 0, h] = cst_ref[i, 0:DH_A].T
        cn_ref[s, 1, h] = cst_ref[i, DH_A:2 * DH_A].T
        nn_ref[s, h:h + 1, :] = nsf_ref[i, 0:1]
        nn_ref[s, NH_A + h:NH_A + h + 1, :] = nsb_ref[i, 0:1]

    def outputs(j, carry):
        rows = pl.ds(pl.multiple_of(j * L, L), L)
        wt = wt_ref[j]
        heads = range(NH_A)
        kqs = []
        for h in heads:
            kh = k_ref[0, rows, head_lanes[h]]
            n_rows = jnp.concatenate([npf_ref[j, h], npb_ref[j, h]], axis=0).astype(bf16)
            kqs.append(_dot(jnp.concatenate([kh, n_rows, cp_ref[j, h]], axis=0), qt_ref[j, h]))
        decays = [[jnp.exp(jnp.where(masks[d], wt[:, NH_A * d + h:NH_A * d + h + 1]
                                     + st_ref[j, ST_U + NH_A * d + h:ST_U + NH_A * d + h + 1, :], NEG_BIG))
                   for d in range(2)] for h in heads]
        s_sums, h_ts = [], []
        for h in heads:
            kq = kqs[h]
            s_sum = None
            h_t = None
            for d in range(2):
                r = NH_A * d + h
                inter = st_ref[j, ST_INTER + r:ST_INTER + r + 1, :]
                emj = st_ref[j, ST_EMJ + r:ST_EMJ + r + 1, :]
                s_t = kq[0:L] * decays[h][d]
                qn = kq[L + 8 * d:L + 8 * d + 1]
                den = inter * qn + jnp.sum(s_t, axis=0, keepdims=True)
                rr = 1.0 / jnp.maximum(jnp.abs(den), emj)
                s_sum = s_t * rr if d == 0 else s_sum + s_t * rr
                part = kq[L + 16 + d * DH_A:L + 16 + (d + 1) * DH_A] * (inter * rr)
                h_t = part if d == 0 else h_t + part
            s_sums.append(s_sum.astype(bf16))
            h_ts.append(h_t)
        h_ts = [h_ts[h] + _dot(vt_ref[j, h], s_sums[h]) for h in heads]
        h_ts = [x * lax.rsqrt(jnp.mean(x * x, axis=0, keepdims=True) + EPS) for x in h_ts]
        for h in heads:
            hn_ref[0, rows, head_lanes[h]] = h_ts[h].T.astype(bf16)
        return carry

    lax.fori_loop(0, nct, outputs, 0, unroll=4)


def _mlstm(q, k, v, gates, state, nseq, state_out):
    B, T, _ = q.shape
    nc = T // CHUNK
    nct = nseq * nc
    G = B // nseq
    fold = lambda a: a.reshape(G, nseq * T, a.shape[-1])
    seq = lambda n: pl.BlockSpec((1, nseq * T, n), lambda b: (b, 0, 0))
    st_c = pl.BlockSpec((nseq, 2, NH_A, DH_A, DH_A), lambda b: (b, 0, 0, 0, 0))
    st_v = pl.BlockSpec((nseq, 2 * NH_A, DH_A), lambda b: (b, 0, 0))
    state_specs = [st_c, st_v, st_v]
    state_shapes = [jax.ShapeDtypeStruct((B, 2, NH_A, DH_A, DH_A), f32),
                    jax.ShapeDtypeStruct((B, 2 * NH_A, DH_A), f32),
                    jax.ShapeDtypeStruct((B, 2 * NH_A, DH_A), f32)]
    outs = pl.pallas_call(
        functools.partial(_mlstm_kernel, nseq=nseq, nc=nc, state_in=state is not None, state_out=state_out),
        grid=(G,),
        in_specs=[seq(D_A), seq(D_A), seq(D_A), seq(GATE_PAD)] + (state_specs if state is not None else []),
        out_specs=[seq(D_A)] + (state_specs if state_out else []),
        scratch_shapes=[pltpu.VMEM((nct, ST_ROWS, CHUNK), f32), pltpu.VMEM((nct, CHUNK, GATE_PAD), f32),
                        pltpu.VMEM((nct, NH_A, DH_A, CHUNK), bf16), pltpu.VMEM((nct, NH_A, DH_A, CHUNK), bf16),
                        pltpu.VMEM((nct, NH_A, 2 * DH_A + 16, DH_A), f32),
                        pltpu.VMEM((nct, NH_A, 2 * DH_A, DH_A), bf16),
                        pltpu.VMEM((nct, NH_A, 8, DH_A), f32), pltpu.VMEM((nct, NH_A, 8, DH_A), f32),
                        pltpu.VMEM((nseq * NH_A, 2 * DH_A, DH_A), f32),
                        pltpu.VMEM((nseq * NH_A, 8, DH_A), f32), pltpu.VMEM((nseq * NH_A, 8, DH_A), f32)],
        out_shape=[jax.ShapeDtypeStruct((G, nseq * T, D_A), bf16)] + (state_shapes if state_out else []),
        compiler_params=pltpu.CompilerParams(dimension_semantics=("arbitrary",), vmem_limit_bytes=VMEM_LIMIT),
        name="mlstm",
    )(fold(q), fold(k), fold(v), fold(gates), *(state if state is not None else ()))
    return (outs[0].reshape(B, T, D_A),) + tuple(outs[1:])


def _mix_kernel(x_ref, mod_ref, hn_ref, so_ref, glu_ref, glu_prev_ref, glu_next_ref, sga_ref, sgb_ref,
                ng_ref, wpa_ref, cw_ref, cb_ref, lng_ref, lnb_ref, wpb_ref, wout_ref,
                x1_ref, ext_ref, zs_ref, *, tm, nt, nsub):
    t = pl.program_id(1)
    mod = mod_ref[0]
    g1 = mod[:, 2 * D_MODEL:3 * D_MODEL]

    a_in = (so_ref[0].astype(f32) * (hn_ref[0].astype(f32) * ng_ref[...])).astype(bf16)
    branch_a = _dot(a_in, wpa_ref[...])

    H = CONV_HALO
    SUB, LANE = SUBLANES, LANES
    tsub = tm // nsub
    te = tsub + 2 * H
    for s in range(nsub):
        inner = nsub == 1
        ext_ref[0, s, 0:H, :] = jnp.where(t > 0, glu_prev_ref[0], 0.0) if inner else jnp.zeros((H, D_B), f32)
        ext_ref[0, s, H:H + tsub, :] = glu_ref[0, s * tsub:(s + 1) * tsub, :]
        ext_ref[0, s, H + tsub:te, :] = jnp.where(t < nt - 1, glu_next_ref[0], 0.0) if inner else jnp.zeros((H, D_B), f32)
    for s in range(nsub):
        for c in range(D_B // LANE):
            lanes = slice(c * LANE, (c + 1) * LANE)
            base = ext_ref[0, s, :, lanes]
            for k in range(1, SUB):
                ext_ref[k, s, :, lanes] = pltpu.roll(base, te - k, axis=0)
    off = H - CONV_W // 2
    for s in range(nsub):
        for r0 in range(0, tsub, CONV_ROWS):
            parts = []
            for c in range(D_B // LANE):
                lanes = slice(c * LANE, (c + 1) * LANE)
                acc = jnp.broadcast_to(cb_ref[:, lanes], (CONV_ROWS, LANE))
                for w in range(CONV_W):
                    k, a = (off + w) % SUB, (off + w) // SUB * SUB
                    acc = acc + ext_ref[k, s, r0 + a:r0 + a + CONV_ROWS, lanes] * cw_ref[w:w + 1, lanes]
                parts.append(acc)
            z = jnp.concatenate(parts, axis=1)
            mu = jnp.mean(z, axis=-1, keepdims=True)
            zc = z - mu
            var = jnp.mean(zc * zc, axis=-1, keepdims=True)
            zn = zc * lax.rsqrt(var + EPS) * lng_ref[...] + lnb_ref[...]
            zs_ref[s * tsub + r0:s * tsub + r0 + CONV_ROWS, :] = (zn * _sigmoid(zn)).astype(bf16)
    branch_b = _dot(zs_ref[...], wpb_ref[...])

    merged = sga_ref[0].astype(f32) * branch_a + sgb_ref[0].astype(f32) * branch_b
    x1_ref[0] = x_ref[0] + g1 * _dot(merged.astype(bf16), wout_ref[...])


def _mix_out(x, mod3, per_batch_mod, hn, so, glu, sga, sgb, wts, tm):
    B0, T0, _ = x.shape
    nsub = 1
    if tm > T0:
        assert not per_batch_mod and tm % T0 == 0 and B0 % (tm // T0) == 0
        nsub = tm // T0
        x, hn, so, glu, sga, sgb = (a.reshape(B0 // nsub, tm, a.shape[-1]) for a in (x, hn, so, glu, sga, sgb))
    B, T, _ = x.shape
    nt = T // tm
    H = CONV_HALO
    r = tm // H
    tok = lambda n: pl.BlockSpec((1, tm, n), lambda b, t: (b, t, 0))
    full = lambda a: pl.BlockSpec(a.shape, lambda b, t: (0,) * a.ndim)
    mod_map = (lambda b, t: (b, 0, 0)) if per_batch_mod else (lambda b, t: (0, 0, 0))
    prev = pl.BlockSpec((1, H, D_B), lambda b, t: (b, jnp.maximum(t * r - 1, 0), 0))
    nxt = pl.BlockSpec((1, H, D_B), lambda b, t: (b, jnp.minimum((t + 1) * r, T // H - 1), 0))
    x1 = pl.pallas_call(
        functools.partial(_mix_kernel, tm=tm, nt=nt, nsub=nsub),
        grid=(B, nt),
        in_specs=[tok(D_MODEL), pl.BlockSpec((1, 1, N_MOD * D_MODEL), mod_map), tok(D_A), tok(D_A),
                  tok(D_B), prev, nxt, tok(D_MODEL), tok(D_MODEL)] + [full(w) for w in wts],
        out_specs=tok(D_MODEL),
        out_shape=jax.ShapeDtypeStruct((B, T, D_MODEL), f32),
        scratch_shapes=[pltpu.VMEM((SUBLANES, nsub, tm // nsub + 2 * H, D_B), f32), pltpu.VMEM((tm, D_B), bf16)],
        compiler_params=pltpu.CompilerParams(dimension_semantics=("arbitrary", "arbitrary"),
                                             vmem_limit_bytes=VMEM_LIMIT),
        name="mix_out",
    )(x, mod3, hn, so, glu, glu, glu, sga, sgb, *wts)
    return x1.reshape(B0, T0, D_MODEL)


def _ffn_kernel(*refs, tm, nt, on_grid, seq_len):
    if on_grid:
        (x_ref, xp_ref, xn_ref, mod_ref, g2n_ref, wu_ref, wd_hbm, cw_ref, cb_ref, fg_ref,
         y_ref, h_ref, g0_ref, g1_ref, v0_ref, v1_ref, gl_ref, gr_ref, a0_ref, a1_ref, a2_ref, a3_ref, acc_ref,
         wd_ref, wd_sem) = refs
    else:
        (x_ref, mod_ref, g2n_ref, wu_ref, wd_hbm, cw_ref, cb_ref, fg_ref,
         y_ref, h_ref, g0_ref, g1_ref, v0_ref, v1_ref, gl_ref, gr_ref, a0_ref, a1_ref, a2_ref, a3_ref, acc_ref,
         wd_ref, wd_sem) = refs
    gbufs, vbufs, acts = (g0_ref, g1_ref), (v0_ref, v1_ref), (a0_ref, a1_ref, a2_ref, a3_ref)
    t = pl.program_id(1)
    first_step = jnp.logical_and(pl.program_id(0) == 0, t == 0)
    wd_copy = pltpu.make_async_copy(wd_hbm, wd_ref, wd_sem)

    @pl.when(first_step)
    def _():
        wd_copy.start()

    halo = GRID_W if on_grid else 0
    te = tm + 2 * halo
    mod = mod_ref[0]
    sh2 = mod[:, 3 * D_MODEL:4 * D_MODEL]
    sc2 = mod[:, 4 * D_MODEL:5 * D_MODEL]
    g2 = mod[:, 5 * D_MODEL:6 * D_MODEL]

    def norm_mod(x):
        return _rms(x, g2n_ref[...]) * (1.0 + sc2) + sh2

    h_ref[halo:halo + tm, :] = norm_mod(x_ref[0]).astype(bf16)
    if on_grid:
        h_ref[0:halo, :] = jnp.where(t > 0, norm_mod(xp_ref[0]), 0.0).astype(bf16)
        h_ref[halo + tm:te, :] = jnp.where(t < nt - 1, norm_mod(xn_ref[0]), 0.0).astype(bf16)

    SUB, LANE = SUBLANES, LANES
    seg = GRID_W if on_grid else seq_len
    sub_row = lax.broadcasted_iota(jnp.int32, (SUB, LANE), 0)
    first_row = sub_row == 0
    last_row = sub_row == SUB - 1
    row_taps = (0, 1, 2) if on_grid else (1,)

    def ff_cols(fc, base):
        if isinstance(fc, int):
            return slice(base + fc * FF_CHUNK, base + (fc + 1) * FF_CHUNK)
        return pl.ds(pl.multiple_of(base + fc * FF_CHUNK, 128), FF_CHUNK)

    def up(fc, slot):
        gbufs[slot][...] = _dot(h_ref[...], wu_ref[:, ff_cols(fc, 0)])
        vbufs[slot][...] = _dot(h_ref[halo:halo + tm, :], wu_ref[:, ff_cols(fc, D_FF)])

    def gate_act(fc, slot, aslot):
        cw = cw_ref[:, ff_cols(fc, 0)]
        cb = cb_ref[:, ff_cols(fc, 0)]
        for c in range(FF_CHUNK // LANE):
            lanes = slice(c * LANE, (c + 1) * LANE)
            for s0 in range(0, te, seg):
                gate = gbufs[slot][s0:s0 + seg, lanes]
                g_l = pltpu.roll(gate, 1, axis=0)
                g_r = pltpu.roll(gate, seg - 1, axis=0)
                gl_ref[s0:s0 + SUB, lanes] = jnp.where(first_row, 0.0, g_l[0:SUB])
                gl_ref[s0 + SUB:s0 + seg, lanes] = g_l[SUB:seg]
                gr_ref[s0:s0 + seg - SUB, lanes] = g_r[0:seg - SUB]
                gr_ref[s0 + seg - SUB:s0 + seg, lanes] = jnp.where(last_row, 0.0, g_r[seg - SUB:seg])
            for r0 in range(0, tm, GRID_W):
                conv = jnp.broadcast_to(cb[:, lanes], (GRID_W, LANE))
                for kh in row_taps:
                    lo = r0 + kh * GRID_W if on_grid else r0
                    conv = conv + gl_ref[lo:lo + GRID_W, lanes] * cw[3 * kh:3 * kh + 1, lanes]
                    conv = conv + gbufs[slot][lo:lo + GRID_W, lanes] * cw[3 * kh + 1:3 * kh + 2, lanes]
                    conv = conv + gr_ref[lo:lo + GRID_W, lanes] * cw[3 * kh + 2:3 * kh + 3, lanes]
                gelu = conv * (0.5 + 0.5 * jnp.tanh(conv * (GELU_C0 + GELU_C1 * (conv * conv))))
                acts[aslot][r0:r0 + GRID_W, lanes] = (gelu * vbufs[slot][r0:r0 + GRID_W, lanes]).astype(bf16)

    def down(fc, aslot):
        return _dot(acts[aslot][...], wd_ref[fc])


    def pair(p, parity, with_down):
        fc = 2 * p
        wr, rd = 2 * parity, 2 * (1 - parity)
        if with_down:
            acc_ref[...] += down(fc - 2, rd) + down(fc - 1, rd + 1)
        up(fc + 1, 1)
        gate_act(fc, 0, wr)
        up(fc + 2, 0)
        gate_act(fc + 1, 1, wr + 1)

    n_pairs = (N_FF_CHUNKS - 1) // 2
    assert N_FF_CHUNKS == 2 * n_pairs + 1 and n_pairs % 2 == 1
    up(0, 0)
    acc_ref[...] = jnp.zeros_like(acc_ref)
    pair(0, 0, False)

    @pl.when(first_step)
    def _():
        wd_copy.wait()

    def two_pairs(i, carry):
        p = 2 * i + 1
        pair(p, 1, True)
        pair(p + 1, 0, True)
        return carry

    lax.fori_loop(0, (n_pairs - 1) // 2, two_pairs, 0)
    last = N_FF_CHUNKS - 1
    gate_act(last, 0, 2)
    ffn_out = acc_ref[...] + (down(last - 2, 0) + down(last - 1, 1) + down(last, 2))
    y_ref[0] = _rms(x_ref[0] + g2 * ffn_out, fg_ref[...])


FFN_W_DOWN = 2


def _ffn(x1, mod3, per_batch_mod, wts, tm, on_grid):
    B0, T0, _ = x1.shape
    if tm > T0:
        assert not per_batch_mod and not on_grid and tm % T0 == 0 and B0 % (tm // T0) == 0
        x1 = x1.reshape(B0 * T0 // tm, tm, D_MODEL)
    B, T, _ = x1.shape
    nt = T // tm
    halo = GRID_W if on_grid else 0
    te = tm + 2 * halo
    tok = pl.BlockSpec((1, tm, D_MODEL), lambda b, t: (b, t, 0))
    full = lambda a: pl.BlockSpec(a.shape, lambda b, t: (0,) * a.ndim)
    mod_map = (lambda b, t: (b, 0, 0)) if per_batch_mod else (lambda b, t: (0, 0, 0))
    in_specs = [tok]
    args = [x1]
    if on_grid:
        r = tm // GRID_W
        in_specs += [pl.BlockSpec((1, GRID_W, D_MODEL), lambda b, t: (b, jnp.maximum(t * r - 1, 0), 0)),
                     pl.BlockSpec((1, GRID_W, D_MODEL), lambda b, t: (b, jnp.minimum((t + 1) * r, T // GRID_W - 1), 0))]
        args += [x1, x1]
    in_specs += [pl.BlockSpec((1, 1, N_MOD * D_MODEL), mod_map)] + [
        pl.BlockSpec(memory_space=pl.ANY) if i == FFN_W_DOWN else full(w) for i, w in enumerate(wts)]
    args += [mod3] + list(wts)
    y = pl.pallas_call(
        functools.partial(_ffn_kernel, tm=tm, nt=nt, on_grid=on_grid, seq_len=T0),
        grid=(B, nt),
        in_specs=in_specs,
        out_specs=tok,
        out_shape=jax.ShapeDtypeStruct((B, T, D_MODEL), f32),
        scratch_shapes=[pltpu.VMEM((te, D_MODEL), bf16)]
        + [pltpu.VMEM((te, FF_CHUNK), f32)] * 2 + [pltpu.VMEM((tm, FF_CHUNK), f32)] * 2
        + [pltpu.VMEM((te, FF_CHUNK), f32)] * 2 + [pltpu.VMEM((tm, FF_CHUNK), bf16)] * 4
        + [pltpu.VMEM((tm, D_MODEL), f32)]
        + [pltpu.VMEM(wts[FFN_W_DOWN].shape, bf16), pltpu.SemaphoreType.DMA(())],
        compiler_params=pltpu.CompilerParams(dimension_semantics=("arbitrary", "arbitrary"),
                                             vmem_limit_bytes=VMEM_LIMIT),
        name="ffn_grid" if on_grid else "ffn_ctx",
    )(*args)
    return y.reshape(B0, T0, D_MODEL)


TOKEN_TILE = 512
MLSTM_TOKENS = 1024


def _tiles(T, per_batch_mod):
    span = TOKEN_TILE if not per_batch_mod else min(TOKEN_TILE, T)
    return span, span, span, max(1, MLSTM_TOKENS // T)


def _trunk(x, mod3, per_batch_mod, state, on_grid, w, cast_weights=()):
    tm_proj, tm_mix, tm_ffn, mlstm_nseq = _tiles(x.shape[1], per_batch_mod)
    (q, k, v, so, glu, sga, sgb, gates), casts = _in_proj(x, mod3, per_batch_mod, w["norm1_g"], w["in"], tm_proj,
                                                          *w["in_groups"], cast_weights=cast_weights)
    if cast_weights:
        w = dict(w, mix=w["mix"](casts), ffn=w["ffn"](casts))
    hn, *new_state = _mlstm(q, k, v, gates, state, mlstm_nseq, state is None)
    x1 = _mix_out(x, mod3, per_batch_mod, hn, so, glu, sga, sgb, w["mix"], tm_mix)
    y = _ffn(x1, mod3, per_batch_mod, w["ffn"], tm_ffn, on_grid)
    return y, new_state, w


def kernel(x_prompt, x_sample, c, state_C, state_n, state_m, c_ctx, w_ada, b_ada, norm1_g, w_in, b_in,
           mlstm_norm_g, w_proj_a, conv_dw_w, conv_dw_b, conv_ln_g, conv_ln_b, w_proj_b, w_out, norm2_g,
           w_up, ffn_dw_w, ffn_dw_b, w_down, final_norm_g):
    Bp = x_prompt.shape[0]
    Bl = x_sample.shape[0]
    l = 0
    row = lambda a: a.reshape(1, -1)

    ct = jnp.concatenate([c, c_ctx[None, :], jnp.zeros((8 - Bl - 1, D_MODEL), f32)], axis=0).T
    mod = _adaln_mod(ct, w_ada[l], row(b_ada[l]), Bl + 1)
    mod_lat = mod[0:Bl].reshape(Bl, 1, N_MOD * D_MODEL)
    mod_ctx = mod[Bl:Bl + 1].reshape(1, 1, N_MOD * D_MODEL)

    o_g = 4 * D_A
    o_u = o_g + 4 * NH_A
    nh = NH_A

    def gate_cols(a):
        g = a[..., o_g:o_u]
        pad = jnp.zeros(a.shape[:-1] + (GATE_PAD - 4 * nh,), a.dtype)
        return jnp.concatenate([g[..., 0:nh], g[..., 2 * nh:3 * nh], g[..., nh:2 * nh], g[..., 3 * nh:], pad], axis=-1)

    w_in_b = w_in[l].astype(bf16)
    bi = row(b_in[l])
    w_in_parts = (w_in_b, bi[:, :o_g], bi[:, o_u:], gate_cols(w_in_b), gate_cols(bi))
    cast_weights = [w_up[l], w_down[l], w_out[l], w_proj_a[l], w_proj_b[l]]

    def w_mix(casts):
        return (row(mlstm_norm_g[l]), casts[3],
                jnp.pad(conv_dw_w[l], ((0, 1), (0, 0))), row(conv_dw_b[l]), row(conv_ln_g[l]), row(conv_ln_b[l]),
                casts[4], casts[2])

    def w_ffn(casts):
        return (row(norm2_g[l]), casts[0], casts[1].reshape(N_FF_CHUNKS, FF_CHUNK, D_MODEL),
                ffn_dw_w[l].reshape(9, D_FF), row(ffn_dw_b[l]), row(final_norm_g))

    w = {"norm1_g": row(norm1_g[l]), "in": w_in_parts, "in_groups": (o_g, o_u), "mix": w_mix, "ffn": w_ffn}

    c0 = state_C[:, l]
    n0 = state_n[:, l].reshape(Bl, 2 * NH_A, DH_A)
    m0 = jnp.broadcast_to(state_m[:, l].reshape(Bl, 2 * NH_A, 1), (Bl, 2 * NH_A, DH_A))
    y_sample, _, w = _trunk(x_sample, mod_lat, True, (c0, n0, m0), True, w, cast_weights)

    y_prompt, (cn, nn, mn), _ = _trunk(x_prompt, mod_ctx, False, None, False, w)

    new_state_C = cn[:, None]
    new_state_n = nn.reshape(Bp, 1, 2, NH_A, DH_A)
    new_state_m = mn[:, :, 0].reshape(Bp, 1, 2, NH_A)
    return (y_prompt, y_sample, new_state_C, new_state_n, new_state_m)
```

```python
import functools

import jax
import jax.numpy as jnp
from jax import lax
from jax.experimental import pallas as pl
from jax.experimental.pallas import tpu as pltpu

D_MODEL = 1024
D_A = 512
NH_A = 4
DH_A = 128
CHUNK = 128
D_B = 512
CONV_W = 31
CONV_HALO = 16
CONV_ROWS = 64
D_FF = 2816
FF_CHUNK = 256
N_FF_CHUNKS = D_FF // FF_CHUNK
GRID_W = 64
N_MOD = 6
GATE_PAD = 128
EPS = 1e-6
Q_SCALE = DH_A ** -0.5
VMEM_LIMIT = 56 * 1024 * 1024
SUBLANES = 8
LANES = 128

f32 = jnp.float32
bf16 = jnp.bfloat16


def _rms(x, g):
    return x * lax.rsqrt(jnp.mean(x * x, axis=-1, keepdims=True) + EPS) * g


def _sigmoid(x):
    return 1.0 / (1.0 + jnp.exp(-x))


def _log_sigmoid(x):
    return jnp.minimum(x, 0.0) - jnp.log(1.0 + jnp.exp(-jnp.abs(x)))


GELU_C0 = 0.7978845608028654
GELU_C1 = GELU_C0 * 0.044715


def _dot(a, b):
    return jnp.dot(a, b, preferred_element_type=f32)


def _mod_kernel(ct_ref, w_ref, b_ref, o_ref, *, n_rows):
    ct = ct_ref[...]
    st = ct * _sigmoid(ct)
    w = w_ref[...]
    rows = []
    for r in range(n_rows):
        s_col = jnp.broadcast_to(st[:, r:r + 1], (D_MODEL, LANES))
        rows.append(jnp.concatenate(
            [jnp.sum(w[:, c:c + LANES] * s_col, axis=0, keepdims=True) for c in range(0, w.shape[1], LANES)], axis=1))
    rows.append(jnp.zeros((8 - n_rows, w.shape[1]), f32))
    o_ref[...] = jnp.concatenate(rows, axis=0) + b_ref[...]


def _adaln_mod(ct, w_ada, b_ada, n_rows):
    n = w_ada.shape[1]
    tn = 1024
    return pl.pallas_call(
        functools.partial(_mod_kernel, n_rows=n_rows),
        grid=(n // tn,),
        in_specs=[
            pl.BlockSpec((D_MODEL, 8), lambda j: (0, 0)),
            pl.BlockSpec((D_MODEL, tn), lambda j: (0, j)),
            pl.BlockSpec((1, tn), lambda j: (0, j)),
        ],
        out_specs=pl.BlockSpec((8, tn), lambda j: (0, j)),
        out_shape=jax.ShapeDtypeStruct((8, n), f32),
        compiler_params=pltpu.CompilerParams(dimension_semantics=("arbitrary",), vmem_limit_bytes=VMEM_LIMIT),
        name="adaln_mod",
    )(ct, w_ada, b_ada)


def _inproj_kernel(*refs, o_a, o_b, n_cast):
    x_ref, mod_ref, g_ref, w_ref, ba_ref, bb_ref, wg_ref, bg_ref = refs[:8]
    cast_src = refs[8:8 + n_cast]
    q_ref, k_ref, v_ref, so_ref, glu_ref, sga_ref, sgb_ref, gates_ref = refs[8 + n_cast:16 + n_cast]
    cast_dst = refs[16 + n_cast:16 + 2 * n_cast]
    wa_ref, wb_ref = refs[16 + 2 * n_cast:]
    for src, dst in zip(cast_src, cast_dst):
        dst[...] = src[...].astype(bf16)

    @pl.when(jnp.logical_and(pl.program_id(0) == 0, pl.program_id(1) == 0))
    def _():
        wa_ref[...] = w_ref[:, 0:o_a]
        wb_ref[...] = w_ref[:, o_b:]

    x = x_ref[0]
    mod = mod_ref[0]
    sh1 = mod[:, 0:D_MODEL]
    sc1 = mod[:, D_MODEL:2 * D_MODEL]
    h = (_rms(x, g_ref[...]) * (1.0 + sc1) + sh1).astype(bf16)

    def proj(w_ref, b_ref, lo, n):
        return _dot(h, w_ref[:, lo:lo + n]) + b_ref[:, lo:lo + n]

    q_ref[0] = (proj(wa_ref, ba_ref, 0, D_A) * Q_SCALE).astype(bf16)
    k_ref[0] = proj(wa_ref, ba_ref, D_A, D_A).astype(bf16)
    v_ref[0] = proj(wa_ref, ba_ref, 2 * D_A, D_A).astype(bf16)
    so_ref[0] = _sigmoid(proj(wa_ref, ba_ref, 3 * D_A, D_A)).astype(bf16)
    glu_ref[0] = proj(wb_ref, bb_ref, 0, D_B) * _sigmoid(proj(wb_ref, bb_ref, D_B, D_B))
    sga_ref[0] = _sigmoid(proj(wb_ref, bb_ref, 2 * D_B, D_MODEL)).astype(bf16)
    sgb_ref[0] = _sigmoid(proj(wb_ref, bb_ref, 2 * D_B + D_MODEL, D_MODEL)).astype(bf16)
    gates_ref[0] = proj(wg_ref, bg_ref, 0, GATE_PAD)


def _in_proj(x, mod3, per_batch_mod, norm_g, wts, tm, o_a, o_b, cast_weights=()):
    B0, T0, _ = x.shape
    if tm > T0:
        assert not per_batch_mod and tm % T0 == 0 and B0 % (tm // T0) == 0
        x = x.reshape(B0 * T0 // tm, tm, D_MODEL)
    B, T, _ = x.shape
    nt = T // tm
    tok = lambda n: pl.BlockSpec((1, tm, n), lambda b, t: (b, t, 0))
    full = lambda a: pl.BlockSpec(a.shape, lambda b, t: (0,) * a.ndim)
    mod_map = (lambda b, t: (b, 0, 0)) if per_batch_mod else (lambda b, t: (0, 0, 0))
    sds = lambda n, dt: jax.ShapeDtypeStruct((B, T, n), dt)
    n_b = wts[0].shape[1] - o_b

    def walk(a):
        block = a.shape[0] // (B * nt)
        assert block * B * nt == a.shape[0] and block % 16 == 0
        return pl.BlockSpec((block, a.shape[1]), lambda b, t: (b * nt + t, 0))

    cast_specs = [walk(a) for a in cast_weights]
    outs = pl.pallas_call(
        functools.partial(_inproj_kernel, o_a=o_a, o_b=o_b, n_cast=len(cast_weights)),
        grid=(B, nt),
        in_specs=[tok(D_MODEL), pl.BlockSpec((1, 1, N_MOD * D_MODEL), mod_map), full(norm_g)]
        + [full(w) for w in wts] + cast_specs,
        out_specs=[tok(D_A), tok(D_A), tok(D_A), tok(D_A), tok(D_B), tok(D_MODEL), tok(D_MODEL), tok(GATE_PAD)]
        + cast_specs,
        out_shape=[sds(D_A, bf16), sds(D_A, bf16), sds(D_A, bf16), sds(D_A, bf16), sds(D_B, f32),
                   sds(D_MODEL, bf16), sds(D_MODEL, bf16), sds(GATE_PAD, f32)]
        + [jax.ShapeDtypeStruct(a.shape, bf16) for a in cast_weights],
        scratch_shapes=[pltpu.VMEM((D_MODEL, o_a), bf16), pltpu.VMEM((D_MODEL, n_b), bf16)],
        compiler_params=pltpu.CompilerParams(dimension_semantics=("arbitrary", "arbitrary"),
                                             vmem_limit_bytes=VMEM_LIMIT),
        name="in_proj",
    )(x, mod3, norm_g, *wts, *cast_weights)
    return [o.reshape(B0, T0, o.shape[-1]) for o in outs[:8]], outs[8:]


NEG_BIG = -1e30
ST_U, ST_INTER, ST_EMJ, ST_WKN, ST_DECAY = 0, 8, 16, 24, 32
ST_ROWS = 40


def _chunk_scan(x, op, fill, prefix, lane, width):
    k = 1
    while k < CHUNK:
        if prefix:
            shifted = jnp.where(lane >= k, pltpu.roll(x, k, axis=1), fill)
        else:
            shifted = jnp.where(lane < CHUNK - k, pltpu.roll(x, width - k, axis=1), fill)
        x = op(x, shifted)
        k *= 2
    return x


def _mlstm_kernel(*refs, nseq, nc, state_in, state_out):
    refs = list(refs)
    q_ref, k_ref, v_ref, g_ref = refs[:4]
    del refs[:4]
    if state_in:
        c0_ref, n0_ref, m0_ref = refs[:3]
        del refs[:3]
    hn_ref = refs.pop(0)
    if state_out:
        cn_ref, nn_ref, mn_ref = refs[:3]
        del refs[:3]
    st_ref, wt_ref, vt_ref, qt_ref, u_ref, cp_ref, npf_ref, npb_ref, cst_ref, nsf_ref, nsb_ref = refs
    L = CHUNK
    nct = nseq * nc
    T = nct * L
    NR = 2 * NH_A

    lane = jnp.bitwise_and(lax.broadcasted_iota(jnp.int32, (NR, T), 1), L - 1)
    is_fwd = lax.broadcasted_iota(jnp.int32, (NR, T), 0) < NH_A
    is_fwd_c = lax.broadcasted_iota(jnp.int32, (NR, L), 0) < NH_A
    i_parts, f_parts = [], []
    for c in range(nct):
        gt = g_ref[0, c * L:(c + 1) * L, :].T
        i_parts.append(gt[0:NR])
        f_parts.append(gt[NR:2 * NR])
    ig = jnp.concatenate(i_parts, axis=1)
    lf = _log_sigmoid(jnp.concatenate(f_parts, axis=1))
    scan = functools.partial(_chunk_scan, lane=lane, width=T)
    ps = scan(lf, jnp.add, 0.0, True)
    ss = scan(lf, jnp.add, 0.0, False)
    b = jnp.where(is_fwd, ps, ss)
    btot = ps + ss - lf
    w = ig - b
    cmw = jnp.where(is_fwd, scan(w, jnp.maximum, -jnp.inf, True), scan(w, jnp.maximum, -jnp.inf, False))
    wk = btot - b + ig
    a = jnp.maximum(scan(wk, jnp.maximum, -jnp.inf, True), scan(wk, jnp.maximum, -jnp.inf, False))
    chunk_lanes = lambda cg: slice(cg * L, (cg + 1) * L)
    pad_rows = jnp.zeros((L - NR, L), f32)
    for s in range(nseq):
        ms = [m0_ref[s] if state_in else jnp.zeros((NR, L), f32)]
        for t in range(nc):
            cf, cb = chunk_lanes(s * nc + t), chunk_lanes(s * nc + nc - 1 - t)
            bt = jnp.where(is_fwd_c, btot[:, cf], btot[:, cb])
            at = jnp.where(is_fwd_c, a[:, cf], a[:, cb])
            ms.append(jnp.maximum(bt + ms[-1], at))
        if state_out:
            mn_ref[s] = ms[nc]
        for c in range(nc):
            cg = s * nc + c
            sl = chunk_lanes(cg)
            m_prev = jnp.where(is_fwd_c, ms[c], ms[nc - 1 - c])
            m_new = jnp.where(is_fwd_c, ms[c + 1], ms[nc - c])
            mx = jnp.maximum(m_prev, cmw[:, sl])
            st_ref[cg, ST_U:ST_U + NR] = -mx
            st_ref[cg, ST_INTER:ST_INTER + NR] = jnp.exp(m_prev - mx)
            st_ref[cg, ST_EMJ:ST_EMJ + NR] = jnp.exp(-mx - b[:, sl])
            st_ref[cg, ST_WKN:ST_WKN + NR] = jnp.exp(wk[:, sl] - m_new)
            st_ref[cg, ST_DECAY:ST_DECAY + NR] = jnp.exp(btot[:, sl] + m_prev - m_new)
            wt_ref[cg] = jnp.concatenate([w[:, sl], pad_rows], axis=0).T

    row = lax.broadcasted_iota(jnp.int32, (L, L), 0)
    col = lax.broadcasted_iota(jnp.int32, (L, L), 1)
    masks = (row <= col, row >= col)
    first_of8 = lax.broadcasted_iota(jnp.int32, (8, L), 0) == 0

    def row_tile(x):
        return jnp.where(first_of8, x, 0.0)

    head_lanes = [slice(h * DH_A, (h + 1) * DH_A) for h in range(NH_A)]

    def increments(j, carry):
        rows = pl.ds(pl.multiple_of(j * L, L), L)
        for h in range(NH_A):
            kh = k_ref[0, rows, head_lanes[h]]
            vt = v_ref[0, rows, head_lanes[h]].astype(f32).T
            vt_ref[j, h] = vt.astype(bf16)
            qt_ref[j, h] = q_ref[0, rows, head_lanes[h]].astype(f32).T.astype(bf16)
            wf = st_ref[j, ST_WKN + h:ST_WKN + h + 1, :]
            wb = st_ref[j, ST_WKN + NH_A + h:ST_WKN + NH_A + h + 1, :]
            lhs = jnp.concatenate([vt * wf, vt * wb, row_tile(wf), row_tile(wb)], axis=0).astype(bf16)
            u_ref[j, h] = _dot(lhs, kh)
        return carry

    lax.fori_loop(0, nct, increments, 0, unroll=4)

    seq_heads = [(s, h) for s in range(nseq) for h in range(NH_A)]
    for s, h in seq_heads:
        i = s * NH_A + h
        if state_in:
            cst_ref[i, 0:DH_A] = c0_ref[s, 0, h].T
            cst_ref[i, DH_A:2 * DH_A] = c0_ref[s, 1, h].T
            nsf_ref[i] = row_tile(n0_ref[s, h:h + 1, :])
            nsb_ref[i] = row_tile(n0_ref[s, NH_A + h:NH_A + h + 1, :])
        else:
            cst_ref[i] = jnp.zeros((2 * DH_A, DH_A), f32)
            nsf_ref[i] = jnp.zeros((8, DH_A), f32)
            nsb_ref[i] = jnp.zeros((8, DH_A), f32)

    def recur(t, carry):
        for s, h in seq_heads:
            i = s * NH_A + h
            jf = s * nc + t
            jb = s * nc + nc - 1 - t
            dec_f = st_ref[jf, ST_DECAY + h:ST_DECAY + h + 1, :]
            dec_b = st_ref[jb, ST_DECAY + NH_A + h:ST_DECAY + NH_A + h + 1, :]
            c_f = cst_ref[i, 0:DH_A]
            c_b = cst_ref[i, DH_A:2 * DH_A]
            cp_ref[jf, h, 0:DH_A] = c_f.astype(bf16)
            cp_ref[jb, h, DH_A:2 * DH_A] = c_b.astype(bf16)
            cst_ref[i, 0:DH_A] = dec_f * c_f + u_ref[jf, h, 0:DH_A]
            cst_ref[i, DH_A:2 * DH_A] = dec_b * c_b + u_ref[jb, h, DH_A:2 * DH_A]
            n_f = nsf_ref[i]
            n_b = nsb_ref[i]
            npf_ref[jf, h] = n_f
            npb_ref[jb, h] = n_b
            nsf_ref[i] = dec_f * n_f + u_ref[jf, h, 2 * DH_A:2 * DH_A + 8]
            nsb_ref[i] = dec_b * n_b + u_ref[jb, h, 2 * DH_A + 8:2 * DH_A + 16]
        return carry

    lax.fori_loop(0, nc, recur, 0)
    for s, h in seq_heads if state_out else ():
        i = s * NH_A + h
        cn_ref[s, 0, h] = cst_ref[i, 0:DH_A].T
        cn_ref[s, 1, h] = cst_ref[i, DH_A:2 * DH_A].T
        nn_ref[s, h:h + 1, :] = nsf_ref[i, 0:1]
        nn_ref[s, NH_A + h:NH_A + h + 1, :] = nsb_ref[i, 0:1]

    def outputs(j, carry):
        rows = pl.ds(pl.multiple_of(j * L, L), L)
        wt = wt_ref[j]
        heads = range(NH_A)
        kqs = []
        for h in heads:
            kh = k_ref[0, rows, head_lanes[h]]
            n_rows = jnp.concatenate([npf_ref[j, h], npb_ref[j, h]], axis=0).astype(bf16)
            kqs.append(_dot(jnp.concatenate([kh, n_rows, cp_ref[j, h]], axis=0), qt_ref[j, h]))
        decays = [[jnp.exp(jnp.where(masks[d], wt[:, NH_A * d + h:NH_A * d + h + 1]
                                     + st_ref[j, ST_U + NH_A * d + h:ST_U + NH_A * d + h + 1, :], NEG_BIG))
                   for d in range(2)] for h in heads]
        s_sums, h_ts = [], []
        for h in heads:
            kq = kqs[h]
            s_sum = None
            h_t = None
            for d in range(2):
                r = NH_A * d + h
                inter = st_ref[j, ST_INTER + r:ST_INTER + r + 1, :]
                emj = st_ref[j, ST_EMJ + r:ST_EMJ + r + 1, :]
                s_t = kq[0:L] * decays[h][d]
                qn = kq[L + 8 * d:L + 8 * d + 1]
                den = inter * qn + jnp.sum(s_t, axis=0, keepdims=True)
                rr = 1.0 / jnp.maximum(jnp.abs(den), emj)
                s_sum = s_t * rr if d == 0 else s_sum + s_t * rr
                part = kq[L + 16 + d * DH_A:L + 16 + (d + 1) * DH_A] * (inter * rr)
                h_t = part if d == 0 else h_t + part
            s_sums.append(s_sum.astype(bf16))
            h_ts.append(h_t)
        h_ts = [h_ts[h] + _dot(vt_ref[j, h], s_sums[h]) for h in heads]
        h_ts = [x * lax.rsqrt(jnp.mean(x * x, axis=0, keepdims=True) + EPS) for x in h_ts]
        for h in heads:
            hn_ref[0, rows, head_lanes[h]] = h_ts[h].T.astype(bf16)
        return carry

    lax.fori_loop(0, nct, outputs, 0, unroll=4)


def _mlstm(q, k, v, gates, state, nseq, state_out):
    B, T, _ = q.shape
    nc = T // CHUNK
    nct = nseq * nc
    G = B // nseq
    fold = lambda a: a.reshape(G, nseq * T, a.shape[-1])
    seq = lambda n: pl.BlockSpec((1, nseq * T, n), lambda b: (b, 0, 0))
    st_c = pl.BlockSpec((nseq, 2, NH_A, DH_A, DH_A), lambda b: (b, 0, 0, 0, 0))
    st_v = pl.BlockSpec((nseq, 2 * NH_A, DH_A), lambda b: (b, 0, 0))
    state_specs = [st_c, st_v, st_v]
    state_shapes = [jax.ShapeDtypeStruct((B, 2, NH_A, DH_A, DH_A), f32),
                    jax.ShapeDtypeStruct((B, 2 * NH_A, DH_A), f32),
                    jax.ShapeDtypeStruct((B, 2 * NH_A, DH_A), f32)]
    outs = pl.pallas_call(
        functools.partial(_mlstm_kernel, nseq=nseq, nc=nc, state_in=state is not None, state_out=state_out),
        grid=(G,),
        in_specs=[seq(D_A), seq(D_A), seq(D_A), seq(GATE_PAD)] + (state_specs if state is not None else []),
        out_specs=[seq(D_A)] + (state_specs if state_out else []),
        scratch_shapes=[pltpu.VMEM((nct, ST_ROWS, CHUNK), f32), pltpu.VMEM((nct, CHUNK, GATE_PAD), f32),
                        pltpu.VMEM((nct, NH_A, DH_A, CHUNK), bf16), pltpu.VMEM((nct, NH_A, DH_A, CHUNK), bf16),
                        pltpu.VMEM((nct, NH_A, 2 * DH_A + 16, DH_A), f32),
                        pltpu.VMEM((nct, NH_A, 2 * DH_A, DH_A), bf16),
                        pltpu.VMEM((nct, NH_A, 8, DH_A), f32), pltpu.VMEM((nct, NH_A, 8, DH_A), f32),
                        pltpu.VMEM((nseq * NH_A, 2 * DH_A, DH_A), f32),
                        pltpu.VMEM((nseq * NH_A, 8, DH_A), f32), pltpu.VMEM((nseq * NH_A, 8, DH_A), f32)],
        out_shape=[jax.ShapeDtypeStruct((G, nseq * T, D_A), bf16)] + (state_shapes if state_out else []),
        compiler_params=pltpu.CompilerParams(dimension_semantics=("arbitrary",), vmem_limit_bytes=VMEM_LIMIT),
        name="mlstm",
    )(fold(q), fold(k), fold(v), fold(gates), *(state if state is not None else ()))
    return (outs[0].reshape(B, T, D_A),) + tuple(outs[1:])


def _mix_kernel(x_ref, mod_ref, hn_ref, so_ref, glu_ref, glu_prev_ref, glu_next_ref, sga_ref, sgb_ref,
                ng_ref, wpa_ref, cw_ref, cb_ref, lng_ref, lnb_ref, wpb_ref, wout_ref,
                x1_ref, ext_ref, zs_ref, *, tm, nt, nsub):
    t = pl.program_id(1)
    mod = mod_ref[0]
    g1 = mod[:, 2 * D_MODEL:3 * D_MODEL]

    a_in = (so_ref[0].astype(f32) * (hn_ref[0].astype(f32) * ng_ref[...])).astype(bf16)
    branch_a = _dot(a_in, wpa_ref[...])

    H = CONV_HALO
    SUB, LANE = SUBLANES, LANES
    tsub = tm // nsub
    te = tsub + 2 * H
    for s in range(nsub):
        inner = nsub == 1
        ext_ref[0, s, 0:H, :] = jnp.where(t > 0, glu_prev_ref[0], 0.0) if inner else jnp.zeros((H, D_B), f32)
        ext_ref[0, s, H:H + tsub, :] = glu_ref[0, s * tsub:(s + 1) * tsub, :]
        ext_ref[0, s, H + tsub:te, :] = jnp.where(t < nt - 1, glu_next_ref[0], 0.0) if inner else jnp.zeros((H, D_B), f32)
    for s in range(nsub):
        for c in range(D_B // LANE):
            lanes = slice(c * LANE, (c + 1) * LANE)
            base = ext_ref[0, s, :, lanes]
            for k in range(1, SUB):
                ext_ref[k, s, :, lanes] = pltpu.roll(base, te - k, axis=0)
    off = H - CONV_W // 2
    for s in range(nsub):
        for r0 in range(0, tsub, CONV_ROWS):
            parts = []
            for c in range(D_B // LANE):
                lanes = slice(c * LANE, (c + 1) * LANE)
                acc = jnp.broadcast_to(cb_ref[:, lanes], (CONV_ROWS, LANE))
                for w in range(CONV_W):
                    k, a = (off + w) % SUB, (off + w) // SUB * SUB
                    acc = acc + ext_ref[k, s, r0 + a:r0 + a + CONV_ROWS, lanes] * cw_ref[w:w + 1, lanes]
                parts.append(acc)
            z = jnp.concatenate(parts, axis=1)
            mu = jnp.mean(z, axis=-1, keepdims=True)
            zc = z - mu
            var = jnp.mean(zc * zc, axis=-1, keepdims=True)
            zn = zc * lax.rsqrt(var + EPS) * lng_ref[...] + lnb_ref[...]
            zs_ref[s * tsub + r0:s * tsub + r0 + CONV_ROWS, :] = (zn * _sigmoid(zn)).astype(bf16)
    branch_b = _dot(zs_ref[...], wpb_ref[...])

    merged = sga_ref[0].astype(f32) * branch_a + sgb_ref[0].astype(f32) * branch_b
    x1_ref[0] = x_ref[0] + g1 * _dot(merged.astype(bf16), wout_ref[...])


def _mix_out(x, mod3, per_batch_mod, hn, so, glu, sga, sgb, wts, tm):
    B0, T0, _ = x.shape
    nsub = 1
    if tm > T0:
        assert not per_batch_mod and tm % T0 == 0 and B0 % (tm // T0) == 0
        nsub = tm // T0
        x, hn, so, glu, sga, sgb = (a.reshape(B0 // nsub, tm, a.shape[-1]) for a in (x, hn, so, glu, sga, sgb))
    B, T, _ = x.shape
    nt = T // tm
    H = CONV_HALO
    r = tm // H
    tok = lambda n: pl.BlockSpec((1, tm, n), lambda b, t: (b, t, 0))
    full = lambda a: pl.BlockSpec(a.shape, lambda b, t: (0,) * a.ndim)
    mod_map = (lambda b, t: (b, 0, 0)) if per_batch_mod else (lambda b, t: (0, 0, 0))
    prev = pl.BlockSpec((1, H, D_B), lambda b, t: (b, jnp.maximum(t * r - 1, 0), 0))
    nxt = pl.BlockSpec((1, H, D_B), lambda b, t: (b, jnp.minimum((t + 1) * r, T // H - 1), 0))
    x1 = pl.pallas_call(
        functools.partial(_mix_kernel, tm=tm, nt=nt, nsub=nsub),
        grid=(B, nt),
        in_specs=[tok(D_MODEL), pl.BlockSpec((1, 1, N_MOD * D_MODEL), mod_map), tok(D_A), tok(D_A),
                  tok(D_B), prev, nxt, tok(D_MODEL), tok(D_MODEL)] + [full(w) for w in wts],
        out_specs=tok(D_MODEL),
        out_shape=jax.ShapeDtypeStruct((B, T, D_MODEL), f32),
        scratch_shapes=[pltpu.VMEM((SUBLANES, nsub, tm // nsub + 2 * H, D_B), f32), pltpu.VMEM((tm, D_B), bf16)],
        compiler_params=pltpu.CompilerParams(dimension_semantics=("arbitrary", "arbitrary"),
                                             vmem_limit_bytes=VMEM_LIMIT),
        name="mix_out",
    )(x, mod3, hn, so, glu, glu, glu, sga, sgb, *wts)
    return x1.reshape(B0, T0, D_MODEL)


def _ffn_kernel(*refs, tm, nt, on_grid, seq_len):
    if on_grid:
        (x_ref, xp_ref, xn_ref, mod_ref, g2n_ref, wu_ref, wd_hbm, cw_ref, cb_ref, fg_ref,
         y_ref, h_ref, g0_ref, g1_ref, v0_ref, v1_ref, gl_ref, gr_ref, a0_ref, a1_ref, a2_ref, a3_ref, acc_ref,
         wd_ref, wd_sem) = refs
    else:
        (x_ref, mod_ref, g2n_ref, wu_ref, wd_hbm, cw_ref, cb_ref, fg_ref,
         y_ref, h_ref, g0_ref, g1_ref, v0_ref, v1_ref, gl_ref, gr_ref, a0_ref, a1_ref, a2_ref, a3_ref, acc_ref,
         wd_ref, wd_sem) = refs
    gbufs, vbufs, acts = (g0_ref, g1_ref), (v0_ref, v1_ref), (a0_ref, a1_ref, a2_ref, a3_ref)
    t = pl.program_id(1)
    first_step = jnp.logical_and(pl.program_id(0) == 0, t == 0)
    wd_copy = pltpu.make_async_copy(wd_hbm, wd_ref, wd_sem)

    @pl.when(first_step)
    def _():
        wd_copy.start()

    halo = GRID_W if on_grid else 0
    te = tm + 2 * halo
    mod = mod_ref[0]
    sh2 = mod[:, 3 * D_MODEL:4 * D_MODEL]
    sc2 = mod[:, 4 * D_MODEL:5 * D_MODEL]
    g2 = mod[:, 5 * D_MODEL:6 * D_MODEL]

    def norm_mod(x):
        return _rms(x, g2n_ref[...]) * (1.0 + sc2) + sh2

    h_ref[halo:halo + tm, :] = norm_mod(x_ref[0]).astype(bf16)
    if on_grid:
        h_ref[0:halo, :] = jnp.where(t > 0, norm_mod(xp_ref[0]), 0.0).astype(bf16)
        h_ref[halo + tm:te, :] = jnp.where(t < nt - 1, norm_mod(xn_ref[0]), 0.0).astype(bf16)

    SUB, LANE = SUBLANES, LANES
    seg = GRID_W if on_grid else seq_len
    sub_row = lax.broadcasted_iota(jnp.int32, (SUB, LANE), 0)
    first_row = sub_row == 0
    last_row = sub_row == SUB - 1
    row_taps = (0, 1, 2) if on_grid else (1,)

    def ff_cols(fc, base):
        if isinstance(fc, int):
            return slice(base + fc * FF_CHUNK, base + (fc + 1) * FF_CHUNK)
        return pl.ds(pl.multiple_of(base + fc * FF_CHUNK, 128), FF_CHUNK)

    def up(fc, slot):
        gbufs[slot][...] = _dot(h_ref[...], wu_ref[:, ff_cols(fc, 0)])
        vbufs[slot][...] = _dot(h_ref[halo:halo + tm, :], wu_ref[:, ff_cols(fc, D_FF)])

    def gate_act(fc, slot, aslot):
        cw = cw_ref[:, ff_cols(fc, 0)]
        cb = cb_ref[:, ff_cols(fc, 0)]
        for c in range(FF_CHUNK // LANE):
            lanes = slice(c * LANE, (c + 1) * LANE)
            for s0 in range(0, te, seg):
                gate = gbufs[slot][s0:s0 + seg, lanes]
                g_l = pltpu.roll(gate, 1, axis=0)
                g_r = pltpu.roll(gate, seg - 1, axis=0)
                gl_ref[s0:s0 + SUB, lanes] = jnp.where(first_row, 0.0, g_l[0:SUB])
                gl_ref[s0 + SUB:s0 + seg, lanes] = g_l[SUB:seg]
                gr_ref[s0:s0 + seg - SUB, lanes] = g_r[0:seg - SUB]
                gr_ref[s0 + seg - SUB:s0 + seg, lanes] = jnp.where(last_row, 0.0, g_r[seg - SUB:seg])
            for r0 in range(0, tm, GRID_W):
                conv = jnp.broadcast_to(cb[:, lanes], (GRID_W, LANE))
                for kh in row_taps:
                    lo = r0 + kh * GRID_W if on_grid else r0
                    conv = conv + gl_ref[lo:lo + GRID_W, lanes] * cw[3 * kh:3 * kh + 1, lanes]
                    conv = conv + gbufs[slot][lo:lo + GRID_W, lanes] * cw[3 * kh + 1:3 * kh + 2, lanes]
                    conv = conv + gr_ref[lo:lo + GRID_W, lanes] * cw[3 * kh + 2:3 * kh + 3, lanes]
                gelu = conv * (0.5 + 0.5 * jnp.tanh(conv * (GELU_C0 + GELU_C1 * (conv * conv))))
                acts[aslot][r0:r0 + GRID_W, lanes] = (gelu * vbufs[slot][r0:r0 + GRID_W, lanes]).astype(bf16)

    def down(fc, aslot):
        return _dot(acts[aslot][...], wd_ref[fc])


    def pair(p, parity, with_down):
        fc = 2 * p
        wr, rd = 2 * parity, 2 * (1 - parity)
        if with_down:
            acc_ref[...] += down(fc - 2, rd) + down(fc - 1, rd + 1)
        up(fc + 1, 1)
        gate_act(fc, 0, wr)
        up(fc + 2, 0)
        gate_act(fc + 1, 1, wr + 1)

    n_pairs = (N_FF_CHUNKS - 1) // 2
    assert N_FF_CHUNKS == 2 * n_pairs + 1 and n_pairs % 2 == 1
    up(0, 0)
    acc_ref[...] = jnp.zeros_like(acc_ref)
    pair(0, 0, False)

    @pl.when(first_step)
    def _():
        wd_copy.wait()

    def two_pairs(i, carry):
        p = 2 * i + 1
        pair(p, 1, True)
        pair(p + 1, 0, True)
        return carry

    lax.fori_loop(0, (n_pairs - 1) // 2, two_pairs, 0)
    last = N_FF_CHUNKS - 1
    gate_act(last, 0, 2)
    ffn_out = acc_ref[...] + (down(last - 2, 0) + down(last - 1, 1) + down(last, 2))
    y_ref[0] = _rms(x_ref[0] + g2 * ffn_out, fg_ref[...])


FFN_W_DOWN = 2


def _ffn(x1, mod3, per_batch_mod, wts, tm, on_grid):
    B0, T0, _ = x1.shape
    if tm > T0:
        assert not per_batch_mod and not on_grid and tm % T0 == 0 and B0 % (tm // T0) == 0
        x1 = x1.reshape(B0 * T0 // tm, tm, D_MODEL)
    B, T, _ = x1.shape
    nt = T // tm
    halo = GRID_W if on_grid else 0
    te = tm + 2 * halo
    tok = pl.BlockSpec((1, tm, D_MODEL), lambda b, t: (b, t, 0))
    full = lambda a: pl.BlockSpec(a.shape, lambda b, t: (0,) * a.ndim)
    mod_map = (lambda b, t: (b, 0, 0)) if per_batch_mod else (lambda b, t: (0, 0, 0))
    in_specs = [tok]
    args = [x1]
    if on_grid:
        r = tm // GRID_W
        in_specs += [pl.BlockSpec((1, GRID_W, D_MODEL), lambda b, t: (b, jnp.maximum(t * r - 1, 0), 0)),
                     pl.BlockSpec((1, GRID_W, D_MODEL), lambda b, t: (b, jnp.minimum((t + 1) * r, T // GRID_W - 1), 0))]
        args += [x1, x1]
    in_specs += [pl.BlockSpec((1, 1, N_MOD * D_MODEL), mod_map)] + [
        pl.BlockSpec(memory_space=pl.ANY) if i == FFN_W_DOWN else full(w) for i, w in enumerate(wts)]
    args += [mod3] + list(wts)
    y = pl.pallas_call(
        functools.partial(_ffn_kernel, tm=tm, nt=nt, on_grid=on_grid, seq_len=T0),
        grid=(B, nt),
        in_specs=in_specs,
        out_specs=tok,
        out_shape=jax.ShapeDtypeStruct((B, T, D_MODEL), f32),
        scratch_shapes=[pltpu.VMEM((te, D_MODEL), bf16)]
        + [pltpu.VMEM((te, FF_CHUNK), f32)] * 2 + [pltpu.VMEM((tm, FF_CHUNK), f32)] * 2
        + [pltpu.VMEM((te, FF_CHUNK), f32)] * 2 + [pltpu.VMEM((tm, FF_CHUNK), bf16)] * 4
        + [pltpu.VMEM((tm, D_MODEL), f32)]
        + [pltpu.VMEM(wts[FFN_W_DOWN].shape, bf16), pltpu.SemaphoreType.DMA(())],
        compiler_params=pltpu.CompilerParams(dimension_semantics=("arbitrary", "arbitrary"),
                                             vmem_limit_bytes=VMEM_LIMIT),
        name="ffn_grid" if on_grid else "ffn_ctx",
    )(*args)
    return y.reshape(B0, T0, D_MODEL)


TOKEN_TILE = 512
MLSTM_TOKENS = 1024


def _tiles(T, per_batch_mod):
    span = TOKEN_TILE if not per_batch_mod else min(TOKEN_TILE, T)
    return span, span, span, max(1, MLSTM_TOKENS // T)


def _trunk(x, mod3, per_batch_mod, state, on_grid, w, cast_weights=()):
    tm_proj, tm_mix, tm_ffn, mlstm_nseq = _tiles(x.shape[1], per_batch_mod)
    (q, k, v, so, glu, sga, sgb, gates), casts = _in_proj(x, mod3, per_batch_mod, w["norm1_g"], w["in"], tm_proj,
                                                          *w["in_groups"], cast_weights=cast_weights)
    if cast_weights:
        w = dict(w, mix=w["mix"](casts), ffn=w["ffn"](casts))
    hn, *new_state = _mlstm(q, k, v, gates, state, mlstm_nseq, state is None)
    x1 = _mix_out(x, mod3, per_batch_mod, hn, so, glu, sga, sgb, w["mix"], tm_mix)
    y = _ffn(x1, mod3, per_batch_mod, w["ffn"], tm_ffn, on_grid)
    return y, new_state, w


def kernel(x_prompt, x_sample, c, state_C, state_n, state_m, c_ctx, w_ada, b_ada, norm1_g, w_in, b_in,
           mlstm_norm_g, w_proj_a, conv_dw_w, conv_dw_b, conv_ln_g, conv_ln_b, w_proj_b, w_out, norm2_g,
           w_up, ffn_dw_w, ffn_dw_b, w_down, final_norm_g):
    Bp = x_prompt.shape[0]
    Bl = x_sample.shape[0]
    l = 0
    row = lambda a: a.reshape(1, -1)

    ct = jnp.concatenate([c, c_ctx[None, :], jnp.zeros((8 - Bl - 1, D_MODEL), f32)], axis=0).T
    mod = _adaln_mod(ct, w_ada[l], row(b_ada[l]), Bl + 1)
    mod_lat = mod[0:Bl].reshape(Bl, 1, N_MOD * D_MODEL)
    mod_ctx = mod[Bl:Bl + 1].reshape(1, 1, N_MOD * D_MODEL)

    o_g = 4 * D_A
    o_u = o_g + 4 * NH_A
    nh = NH_A

    def gate_cols(a):
        g = a[..., o_g:o_u]
        pad = jnp.zeros(a.shape[:-1] + (GATE_PAD - 4 * nh,), a.dtype)
        return jnp.concatenate([g[..., 0:nh], g[..., 2 * nh:3 * nh], g[..., nh:2 * nh], g[..., 3 * nh:], pad], axis=-1)

    w_in_b = w_in[l].astype(bf16)
    bi = row(b_in[l])
    w_in_parts = (w_in_b, bi[:, :o_g], bi[:, o_u:], gate_cols(w_in_b), gate_cols(bi))
    cast_weights = [w_up[l], w_down[l], w_out[l], w_proj_a[l], w_proj_b[l]]

    def w_mix(casts):
        return (row(mlstm_norm_g[l]), casts[3],
                jnp.pad(conv_dw_w[l], ((0, 1), (0, 0))), row(conv_dw_b[l]), row(conv_ln_g[l]), row(conv_ln_b[l]),
                casts[4], casts[2])

    def w_ffn(casts):
        return (row(norm2_g[l]), casts[0], casts[1].reshape(N_FF_CHUNKS, FF_CHUNK, D_MODEL),
                ffn_dw_w[l].reshape(9, D_FF), row(ffn_dw_b[l]), row(final_norm_g))

    w = {"norm1_g": row(norm1_g[l]), "in": w_in_parts, "in_groups": (o_g, o_u), "mix": w_mix, "ffn": w_ffn}

    c0 = state_C[:, l]
    n0 = state_n[:, l].reshape(Bl, 2 * NH_A, DH_A)
    m0 = jnp.broadcast_to(state_m[:, l].reshape(Bl, 2 * NH_A, 1), (Bl, 2 * NH_A, DH_A))
    y_sample, _, w = _trunk(x_sample, mod_lat, True, (c0, n0, m0), True, w, cast_weights)

    y_prompt, (cn, nn, mn), _ = _trunk(x_prompt, mod_ctx, False, None, False, w)

    new_state_C = cn[:, None]
    new_state_n = nn.reshape(Bp, 1, 2, NH_A, DH_A)
    new_state_m = mn[:, :, 0].reshape(Bp, 1, 2, NH_A)
    return (y_prompt, y_sample, new_state_C, new_state_n, new_state_m)
```

```python
import functools

import jax
import jax.numpy as jnp
from jax import lax
from jax.experimental import pallas as pl
from jax.experimental.pallas import tpu as pltpu

D_MODEL = 1024
D_A = 512
NH_A = 4
DH_A = 128
CHUNK = 128
D_B = 512
CONV_W = 31
CONV_HALO = 16
CONV_ROWS = 64
D_FF = 2816
FF_CHUNK = 256
N_FF_CHUNKS = D_FF // FF_CHUNK
GRID_W = 64
N_MOD = 6
GATE_PAD = 128
EPS = 1e-6
Q_SCALE = DH_A ** -0.5
VMEM_LIMIT = 56 * 1024 * 1024
SUBLANES = 8
LANES = 128

f32 = jnp.float32
bf16 = jnp.bfloat16


def _rms(x, g):
    return x * lax.rsqrt(jnp.mean(x * x, axis=-1, keepdims=True) + EPS) * g


def _sigmoid(x):
    return 1.0 / (1.0 + jnp.exp(-x))


def _log_sigmoid(x):
    return jnp.minimum(x, 0.0) - jnp.log(1.0 + jnp.exp(-jnp.abs(x)))


GELU_C0 = 0.7978845608028654
GELU_C1 = GELU_C0 * 0.044715


def _dot(a, b):
    return jnp.dot(a, b, preferred_element_type=f32)


def _dot_nt(a, b):
    return lax.dot_general(a, b, (((1,), (1,)), ((), ())), preferred_element_type=f32)


def _mod_kernel(ct_ref, w_ref, b_ref, o_ref, *, n_rows):
    ct = ct_ref[...]
    st = ct * _sigmoid(ct)
    w = w_ref[...]
    rows = []
    for r in range(n_rows):
        s_col = jnp.broadcast_to(st[:, r:r + 1], (D_MODEL, LANES))
        rows.append(jnp.concatenate(
            [jnp.sum(w[:, c:c + LANES] * s_col, axis=0, keepdims=True) for c in range(0, w.shape[1], LANES)], axis=1))
    rows.append(jnp.zeros((8 - n_rows, w.shape[1]), f32))
    o_ref[...] = jnp.concatenate(rows, axis=0) + b_ref[...]


def _adaln_mod(ct, w_ada, b_ada, n_rows):
    n = w_ada.shape[1]
    tn = 1024
    return pl.pallas_call(
        functools.partial(_mod_kernel, n_rows=n_rows),
        grid=(n // tn,),
        in_specs=[
            pl.BlockSpec((D_MODEL, 8), lambda j: (0, 0)),
            pl.BlockSpec((D_MODEL, tn), lambda j: (0, j)),
            pl.BlockSpec((1, tn), lambda j: (0, j)),
        ],
        out_specs=pl.BlockSpec((8, tn), lambda j: (0, j)),
        out_shape=jax.ShapeDtypeStruct((8, n), f32),
        compiler_params=pltpu.CompilerParams(dimension_semantics=("arbitrary",), vmem_limit_bytes=VMEM_LIMIT),
        name="adaln_mod",
    )(ct, w_ada, b_ada)


def _inproj_kernel(*refs, o_a, o_b, n_cast):
    x_ref, mod_ref, g_ref, w_ref, ba_ref, bb_ref, wg_ref, bg_ref = refs[:8]
    cast_src = refs[8:8 + n_cast]
    q_ref, k_ref, v_ref, so_ref, glu_ref, sga_ref, sgb_ref, gates_ref = refs[8 + n_cast:16 + n_cast]
    cast_dst = refs[16 + n_cast:16 + 2 * n_cast]
    for src, dst in zip(cast_src, cast_dst):
        dst[...] = src[...].astype(bf16)

    x = x_ref[0]
    mod = mod_ref[0]
    sh1 = mod[:, 0:D_MODEL]
    sc1 = mod[:, D_MODEL:2 * D_MODEL]
    h = (_rms(x, g_ref[...]) * (1.0 + sc1) + sh1).astype(bf16)

    def proj(base, b_ref, lo, n):
        return _dot_nt(h, w_ref[base + lo:base + lo + n, :]) + b_ref[:, lo:lo + n]

    q_ref[0] = (proj(0, ba_ref, 0, D_A) * Q_SCALE).astype(bf16)
    k_ref[0] = proj(0, ba_ref, D_A, D_A).astype(bf16)
    v_ref[0] = proj(0, ba_ref, 2 * D_A, D_A).astype(bf16)
    so_ref[0] = _sigmoid(proj(0, ba_ref, 3 * D_A, D_A)).astype(bf16)
    glu_ref[0] = proj(o_b, bb_ref, 0, D_B) * _sigmoid(proj(o_b, bb_ref, D_B, D_B))
    sga_ref[0] = _sigmoid(proj(o_b, bb_ref, 2 * D_B, D_MODEL)).astype(bf16)
    sgb_ref[0] = _sigmoid(proj(o_b, bb_ref, 2 * D_B + D_MODEL, D_MODEL)).astype(bf16)
    gates_ref[0] = _dot_nt(h, wg_ref[...]) + bg_ref[...]


def _in_proj(x, mod3, per_batch_mod, norm_g, wts, tm, o_a, o_b, cast_weights=()):
    B0, T0, _ = x.shape
    if tm > T0:
        assert not per_batch_mod and tm % T0 == 0 and B0 % (tm // T0) == 0
        x = x.reshape(B0 * T0 // tm, tm, D_MODEL)
    B, T, _ = x.shape
    nt = T // tm
    tok = lambda n: pl.BlockSpec((1, tm, n), lambda b, t: (b, t, 0))
    full = lambda a: pl.BlockSpec(a.shape, lambda b, t: (0,) * a.ndim)
    mod_map = (lambda b, t: (b, 0, 0)) if per_batch_mod else (lambda b, t: (0, 0, 0))
    sds = lambda n, dt: jax.ShapeDtypeStruct((B, T, n), dt)

    def walk(a):
        block = a.shape[0] // (B * nt)
        assert block * B * nt == a.shape[0] and block % 16 == 0
        return pl.BlockSpec((block, a.shape[1]), lambda b, t: (b * nt + t, 0))

    cast_specs = [walk(a) for a in cast_weights]
    outs = pl.pallas_call(
        functools.partial(_inproj_kernel, o_a=o_a, o_b=o_b, n_cast=len(cast_weights)),
        grid=(B, nt),
        in_specs=[tok(D_MODEL), pl.BlockSpec((1, 1, N_MOD * D_MODEL), mod_map), full(norm_g)]
        + [full(w) for w in wts] + cast_specs,
        out_specs=[tok(D_A), tok(D_A), tok(D_A), tok(D_A), tok(D_B), tok(D_MODEL), tok(D_MODEL), tok(GATE_PAD)]
        + cast_specs,
        out_shape=[sds(D_A, bf16), sds(D_A, bf16), sds(D_A, bf16), sds(D_A, bf16), sds(D_B, f32),
                   sds(D_MODEL, bf16), sds(D_MODEL, bf16), sds(GATE_PAD, f32)]
        + [jax.ShapeDtypeStruct(a.shape, bf16) for a in cast_weights],
        compiler_params=pltpu.CompilerParams(dimension_semantics=("arbitrary", "arbitrary"),
                                             vmem_limit_bytes=VMEM_LIMIT),
        name="in_proj",
    )(x, mod3, norm_g, *wts, *cast_weights)
    return [o.reshape(B0, T0, o.shape[-1]) for o in outs[:8]], outs[8:]


NEG_BIG = -1e30
ST_U, ST_INTER, ST_EMJ, ST_WKN, ST_DECAY = 0, 8, 16, 24, 32
ST_ROWS = 40


def _chunk_scan(x, op, fill, prefix, lane, width):
    k = 1
    while k < CHUNK:
        if prefix:
            shifted = jnp.where(lane >= k, pltpu.roll(x, k, axis=1), fill)
        else:
            shifted = jnp.where(lane < CHUNK - k, pltpu.roll(x, width - k, axis=1), fill)
        x = op(x, shifted)
        k *= 2
    return x


def _mlstm_kernel(*refs, nseq, nc, state_in, state_out):
    refs = list(refs)
    q_ref, k_ref, v_ref, g_ref = refs[:4]
    del refs[:4]
    if state_in:
        c0_ref, n0_ref, m0_ref = refs[:3]
        del refs[:3]
    hn_ref = refs.pop(0)
    if state_out:
        cn_ref, nn_ref, mn_ref = refs[:3]
        del refs[:3]
    st_ref, wt_ref, vt_ref, qt_ref, u_ref, cp_ref, npf_ref, npb_ref, cst_ref, nsf_ref, nsb_ref = refs
    L = CHUNK
    nct = nseq * nc
    T = nct * L
    NR = 2 * NH_A

    lane = jnp.bitwise_and(lax.broadcasted_iota(jnp.int32, (NR, T), 1), L - 1)
    is_fwd = lax.broadcasted_iota(jnp.int32, (NR, T), 0) < NH_A
    is_fwd_c = lax.broadcasted_iota(jnp.int32, (NR, L), 0) < NH_A
    i_parts, f_parts = [], []
    for c in range(nct):
        gt = g_ref[0, c * L:(c + 1) * L, :].T
        i_parts.append(gt[0:NR])
        f_parts.append(gt[NR:2 * NR])
    ig = jnp.concatenate(i_parts, axis=1)
    lf = _log_sigmoid(jnp.concatenate(f_parts, axis=1))
    scan = functools.partial(_chunk_scan, lane=lane, width=T)
    ps = scan(lf, jnp.add, 0.0, True)
    ss = scan(lf, jnp.add, 0.0, False)
    b = jnp.where(is_fwd, ps, ss)
    btot = ps + ss - lf
    w = ig - b
    cmw = jnp.where(is_fwd, scan(w, jnp.maximum, -jnp.inf, True), scan(w, jnp.maximum, -jnp.inf, False))
    wk = btot - b + ig
    a = jnp.maximum(scan(wk, jnp.maximum, -jnp.inf, True), scan(wk, jnp.maximum, -jnp.inf, False))
    chunk_lanes = lambda cg: slice(cg * L, (cg + 1) * L)
    pad_rows = jnp.zeros((L - NR, L), f32)
    for s in range(nseq):
        ms = [m0_ref[s] if state_in else jnp.zeros((NR, L), f32)]
        for t in range(nc):
            cf, cb = chunk_lanes(s * nc + t), chunk_lanes(s * nc + nc - 1 - t)
            bt = jnp.where(is_fwd_c, btot[:, cf], btot[:, cb])
            at = jnp.where(is_fwd_c, a[:, cf], a[:, cb])
            ms.append(jnp.maximum(bt + ms[-1], at))
        if state_out:
            mn_ref[s] = ms[nc]
        for c in range(nc):
            cg = s * nc + c
            sl = chunk_lanes(cg)
            m_prev = jnp.where(is_fwd_c, ms[c], ms[nc - 1 - c])
            m_new = jnp.where(is_fwd_c, ms[c + 1], ms[nc - c])
            mx = jnp.maximum(m_prev, cmw[:, sl])
            st_ref[cg, ST_U:ST_U + NR] = -mx
            st_ref[cg, ST_INTER:ST_INTER + NR] = jnp.exp(m_prev - mx)
            st_ref[cg, ST_EMJ:ST_EMJ + NR] = jnp.exp(-mx - b[:, sl])
            st_ref[cg, ST_WKN:ST_WKN + NR] = jnp.exp(wk[:, sl] - m_new)
            st_ref[cg, ST_DECAY:ST_DECAY + NR] = jnp.exp(btot[:, sl] + m_prev - m_new)
            wt_ref[cg] = jnp.concatenate([w[:, sl], pad_rows], axis=0).T

    row = lax.broadcasted_iota(jnp.int32, (L, L), 0)
    col = lax.broadcasted_iota(jnp.int32, (L, L), 1)
    masks = (row <= col, row >= col)
    first_of8 = lax.broadcasted_iota(jnp.int32, (8, L), 0) == 0

    def row_tile(x):
        return jnp.where(first_of8, x, 0.0)

    head_lanes = [slice(h * DH_A, (h + 1) * DH_A) for h in range(NH_A)]

    def increments(j, carry):
        rows = pl.ds(pl.multiple_of(j * L, L), L)
        for h in range(NH_A):
            kh = k_ref[0, rows, head_lanes[h]]
            vt = v_ref[0, rows, head_lanes[h]].astype(f32).T
            vt_ref[j, h] = vt.astype(bf16)
            qt_ref[j, h] = q_ref[0, rows, head_lanes[h]].astype(f32).T.astype(bf16)
            wf = st_ref[j, ST_WKN + h:ST_WKN + h + 1, :]
            wb = st_ref[j, ST_WKN + NH_A + h:ST_WKN + NH_A + h + 1, :]
            lhs = jnp.concatenate([vt * wf, vt * wb, row_tile(wf), row_tile(wb)], axis=0).astype(bf16)
            u_ref[j, h] = _dot(lhs, kh)
        return carry

    lax.fori_loop(0, nct, increments, 0, unroll=4)

    seq_heads = [(s, h) for s in range(nseq) for h in range(NH_A)]
    for s, h in seq_heads:
        i = s * NH_A + h
        if state_in:
            cst_ref[i, 0:DH_A] = c0_ref[s, 0, h].T
            cst_ref[i, DH_A:2 * DH_A] = c0_ref[s, 1, h].T
            nsf_ref[i] = row_tile(n0_ref[s, h:h + 1, :])
            nsb_ref[i] = row_tile(n0_ref[s, NH_A + h:NH_A + h + 1, :])
        else:
            cst_ref[i] = jnp.zeros((2 * DH_A, DH_A), f32)
            nsf_ref[i] = jnp.zeros((8, DH_A), f32)
            nsb_ref[i] = jnp.zeros((8, DH_A), f32)

    def recur(t, carry):
        for s, h in seq_heads:
            i = s * NH_A + h
            jf = s * nc + t
            jb = s * nc + nc - 1 - t
            dec_f = st_ref[jf, ST_DECAY + h:ST_DECAY + h + 1, :]
            dec_b = st_ref[jb, ST_DECAY + NH_A + h:ST_DECAY + NH_A + h + 1, :]
            c_f = cst_ref[i, 0:DH_A]
            c_b = cst_ref[i, DH_A:2 * DH_A]
            cp_ref[jf, h, 0:DH_A] = c_f.astype(bf16)
            cp_ref[jb, h, DH_A:2 * DH_A] = c_b.astype(bf16)
            cst_ref[i, 0:DH_A] = dec_f * c_f + u_ref[jf, h, 0:DH_A]
            cst_ref[i, DH_A:2 * DH_A] = dec_b * c_b + u_ref[jb, h, DH_A:2 * DH_A]
            n_f = nsf_ref[i]
            n_b = nsb_ref[i]
            npf_ref[jf, h] = n_f
            npb_ref[jb, h] = n_b
            nsf_ref[i] = dec_f * n_f + u_ref[jf, h, 2 * DH_A:2 * DH_A + 8]
            nsb_ref[i] = dec_b * n_b + u_ref[jb, h, 2 * DH_A + 8:2 * DH_A + 16]
        return carry

    lax.fori_loop(0, nc, recur, 0)
    for s, h in seq_heads if state_out else ():
        i = s * NH_A + h
        cn_ref[s, 0, h] = cst_ref[i, 0:DH_A].T
        cn_ref[s, 1, h] = cst_ref[i, DH_A:2 * DH_A].T
        nn_ref[s, h:h + 1, :] = nsf_ref[i, 0:1]
        nn_ref[s, NH_A + h:NH_A + h + 1, :] = nsb_ref[i, 0:1]

    def outputs(j, carry):
        rows = pl.ds(pl.multiple_of(j * L, L), L)
        wt = wt_ref[j]
        heads = range(NH_A)
        kqs = []
        for h in heads:
            kh = k_ref[0, rows, head_lanes[h]]
            n_rows = jnp.concatenate([npf_ref[j, h], npb_ref[j, h]], axis=0).astype(bf16)
            kqs.append(_dot(jnp.concatenate([kh, n_rows, cp_ref[j, h]], axis=0), qt_ref[j, h]))
        decays = [[jnp.exp(jnp.where(masks[d], wt[:, NH_A * d + h:NH_A * d + h + 1]
                                     + st_ref[j, ST_U + NH_A * d + h:ST_U + NH_A * d + h + 1, :], NEG_BIG))
                   for d in range(2)] for h in heads]
        s_sums, h_ts = [], []
        for h in heads:
            kq = kqs[h]
            s_sum = None
            h_t = None
            for d in range(2):
                r = NH_A * d + h
                inter = st_ref[j, ST_INTER + r:ST_INTER + r + 1, :]
                emj = st_ref[j, ST_EMJ + r:ST_EMJ + r + 1, :]
                s_t = kq[0:L] * decays[h][d]
                qn = kq[L + 8 * d:L + 8 * d + 1]
                den = inter * qn + jnp.sum(s_t, axis=0, keepdims=True)
                rr = 1.0 / jnp.maximum(jnp.abs(den), emj)
                s_sum = s_t * rr if d == 0 else s_sum + s_t * rr
                part = kq[L + 16 + d * DH_A:L + 16 + (d + 1) * DH_A] * (inter * rr)
                h_t = part if d == 0 else h_t + part
            s_sums.append(s_sum.astype(bf16))
            h_ts.append(h_t)
        h_ts = [h_ts[h] + _dot(vt_ref[j, h], s_sums[h]) for h in heads]
        h_ts = [x * lax.rsqrt(jnp.mean(x * x, axis=0, keepdims=True) + EPS) for x in h_ts]
        for h in heads:
            hn_ref[0, rows, head_lanes[h]] = h_ts[h].T.astype(bf16)
        return carry

    lax.fori_loop(0, nct, outputs, 0, unroll=4)


def _mlstm(q, k, v, gates, state, nseq, state_out):
    B, T, _ = q.shape
    nc = T // CHUNK
    nct = nseq * nc
    G = B // nseq
    fold = lambda a: a.reshape(G, nseq * T, a.shape[-1])
    seq = lambda n: pl.BlockSpec((1, nseq * T, n), lambda b: (b, 0, 0))
    st_c = pl.BlockSpec((nseq, 2, NH_A, DH_A, DH_A), lambda b: (b, 0, 0, 0, 0))
    st_v = pl.BlockSpec((nseq, 2 * NH_A, DH_A), lambda b: (b, 0, 0))
    state_specs = [st_c, st_v, st_v]
    state_shapes = [jax.ShapeDtypeStruct((B, 2, NH_A, DH_A, DH_A), f32),
                    jax.ShapeDtypeStruct((B, 2 * NH_A, DH_A), f32),
                    jax.ShapeDtypeStruct((B, 2 * NH_A, DH_A), f32)]
    outs = pl.pallas_call(
        functools.partial(_mlstm_kernel, nseq=nseq, nc=nc, state_in=state is not None, state_out=state_out),
        grid=(G,),
        in_specs=[seq(D_A), seq(D_A), seq(D_A), seq(GATE_PAD)] + (state_specs if state is not None else []),
        out_specs=[seq(D_A)] + (state_specs if state_out else []),
        scratch_shapes=[pltpu.VMEM((nct, ST_ROWS, CHUNK), f32), pltpu.VMEM((nct, CHUNK, GATE_PAD), f32),
                        pltpu.VMEM((nct, NH_A, DH_A, CHUNK), bf16), pltpu.VMEM((nct, NH_A, DH_A, CHUNK), bf16),
                        pltpu.VMEM((nct, NH_A, 2 * DH_A + 16, DH_A), f32),
                        pltpu.VMEM((nct, NH_A, 2 * DH_A, DH_A), bf16),
                        pltpu.VMEM((nct, NH_A, 8, DH_A), f32), pltpu.VMEM((nct, NH_A, 8, DH_A), f32),
                        pltpu.VMEM((nseq * NH_A, 2 * DH_A, DH_A), f32),
                        pltpu.VMEM((nseq * NH_A, 8, DH_A), f32), pltpu.VMEM((nseq * NH_A, 8, DH_A), f32)],
        out_shape=[jax.ShapeDtypeStruct((G, nseq * T, D_A), bf16)] + (state_shapes if state_out else []),
        compiler_params=pltpu.CompilerParams(dimension_semantics=("arbitrary",), vmem_limit_bytes=VMEM_LIMIT),
        name="mlstm",
    )(fold(q), fold(k), fold(v), fold(gates), *(state if state is not None else ()))
    return (outs[0].reshape(B, T, D_A),) + tuple(outs[1:])


def _mix_kernel(x_ref, mod_ref, hn_ref, so_ref, glu_ref, glu_prev_ref, glu_next_ref, sga_ref, sgb_ref,
                ng_ref, wpa_ref, cw_ref, cb_ref, lng_ref, lnb_ref, wpb_ref, wout_ref,
                x1_ref, ext_ref, zs_ref, *, tm, nt, nsub):
    t = pl.program_id(1)
    mod = mod_ref[0]
    g1 = mod[:, 2 * D_MODEL:3 * D_MODEL]

    a_in = (so_ref[0].astype(f32) * (hn_ref[0].astype(f32) * ng_ref[...])).astype(bf16)
    branch_a = _dot(a_in, wpa_ref[...])

    H = CONV_HALO
    SUB, LANE = SUBLANES, LANES
    tsub = tm // nsub
    te = tsub + 2 * H
    for s in range(nsub):
        inner = nsub == 1
        ext_ref[0, s, 0:H, :] = jnp.where(t > 0, glu_prev_ref[0], 0.0) if inner else jnp.zeros((H, D_B), f32)
        ext_ref[0, s, H:H + tsub, :] = glu_ref[0, s * tsub:(s + 1) * tsub, :]
        ext_ref[0, s, H + tsub:te, :] = jnp.where(t < nt - 1, glu_next_ref[0], 0.0) if inner else jnp.zeros((H, D_B), f32)
    for s in range(nsub):
        for c in range(D_B // LANE):
            lanes = slice(c * LANE, (c + 1) * LANE)
            base = ext_ref[0, s, :, lanes]
            for k in range(1, SUB):
                ext_ref[k, s, :, lanes] = pltpu.roll(base, te - k, axis=0)
    off = H - CONV_W // 2
    for s in range(nsub):
        for r0 in range(0, tsub, CONV_ROWS):
            parts = []
            for c in range(D_B // LANE):
                lanes = slice(c * LANE, (c + 1) * LANE)
                acc = jnp.broadcast_to(cb_ref[:, lanes], (CONV_ROWS, LANE))
                for w in range(CONV_W):
                    k, a = (off + w) % SUB, (off + w) // SUB * SUB
                    acc = acc + ext_ref[k, s, r0 + a:r0 + a + CONV_ROWS, lanes] * cw_ref[w:w + 1, lanes]
                parts.append(acc)
            z = jnp.concatenate(parts, axis=1)
            mu = jnp.mean(z, axis=-1, keepdims=True)
            zc = z - mu
            var = jnp.mean(zc * zc, axis=-1, keepdims=True)
            zn = zc * lax.rsqrt(var + EPS) * lng_ref[...] + lnb_ref[...]
            zs_ref[s * tsub + r0:s * tsub + r0 + CONV_ROWS, :] = (zn * _sigmoid(zn)).astype(bf16)
    branch_b = _dot(zs_ref[...], wpb_ref[...])

    merged = sga_ref[0].astype(f32) * branch_a + sgb_ref[0].astype(f32) * branch_b
    x1_ref[0] = x_ref[0] + g1 * _dot(merged.astype(bf16), wout_ref[...])


def _mix_out(x, mod3, per_batch_mod, hn, so, glu, sga, sgb, wts, tm):
    B0, T0, _ = x.shape
    nsub = 1
    if tm > T0:
        assert not per_batch_mod and tm % T0 == 0 and B0 % (tm // T0) == 0
        nsub = tm // T0
        x, hn, so, glu, sga, sgb = (a.reshape(B0 // nsub, tm, a.shape[-1]) for a in (x, hn, so, glu, sga, sgb))
    B, T, _ = x.shape
    nt = T // tm
    H = CONV_HALO
    r = tm // H
    tok = lambda n: pl.BlockSpec((1, tm, n), lambda b, t: (b, t, 0))
    full = lambda a: pl.BlockSpec(a.shape, lambda b, t: (0,) * a.ndim)
    mod_map = (lambda b, t: (b, 0, 0)) if per_batch_mod else (lambda b, t: (0, 0, 0))
    prev = pl.BlockSpec((1, H, D_B), lambda b, t: (b, jnp.maximum(t * r - 1, 0), 0))
    nxt = pl.BlockSpec((1, H, D_B), lambda b, t: (b, jnp.minimum((t + 1) * r, T // H - 1), 0))
    x1 = pl.pallas_call(
        functools.partial(_mix_kernel, tm=tm, nt=nt, nsub=nsub),
        grid=(B, nt),
        in_specs=[tok(D_MODEL), pl.BlockSpec((1, 1, N_MOD * D_MODEL), mod_map), tok(D_A), tok(D_A),
                  tok(D_B), prev, nxt, tok(D_MODEL), tok(D_MODEL)] + [full(w) for w in wts],
        out_specs=tok(D_MODEL),
        out_shape=jax.ShapeDtypeStruct((B, T, D_MODEL), f32),
        scratch_shapes=[pltpu.VMEM((SUBLANES, nsub, tm // nsub + 2 * H, D_B), f32), pltpu.VMEM((tm, D_B), bf16)],
        compiler_params=pltpu.CompilerParams(dimension_semantics=("arbitrary", "arbitrary"),
                                             vmem_limit_bytes=VMEM_LIMIT),
        name="mix_out",
    )(x, mod3, hn, so, glu, glu, glu, sga, sgb, *wts)
    return x1.reshape(B0, T0, D_MODEL)


def _ffn_kernel(*refs, tm, nt, on_grid, seq_len):
    if on_grid:
        (x_ref, xp_ref, xn_ref, mod_ref, g2n_ref, wu_ref, wd_hbm, cw_ref, cb_ref, fg_ref,
         y_ref, h_ref, g0_ref, g1_ref, v0_ref, v1_ref, gl_ref, gr_ref, a0_ref, a1_ref, a2_ref, a3_ref, acc_ref,
         wd_ref, wd_sem) = refs
    else:
        (x_ref, mod_ref, g2n_ref, wu_ref, wd_hbm, cw_ref, cb_ref, fg_ref,
         y_ref, h_ref, g0_ref, g1_ref, v0_ref, v1_ref, gl_ref, gr_ref, a0_ref, a1_ref, a2_ref, a3_ref, acc_ref,
         wd_ref, wd_sem) = refs
    gbufs, vbufs, acts = (g0_ref, g1_ref), (v0_ref, v1_ref), (a0_ref, a1_ref, a2_ref, a3_ref)
    t = pl.program_id(1)
    first_step = jnp.logical_and(pl.program_id(0) == 0, t == 0)
    wd_copy = pltpu.make_async_copy(wd_hbm, wd_ref, wd_sem)

    @pl.when(first_step)
    def _():
        wd_copy.start()

    halo = GRID_W if on_grid else 0
    te = tm + 2 * halo
    mod = mod_ref[0]
    sh2 = mod[:, 3 * D_MODEL:4 * D_MODEL]
    sc2 = mod[:, 4 * D_MODEL:5 * D_MODEL]
    g2 = mod[:, 5 * D_MODEL:6 * D_MODEL]

    def norm_mod(x):
        return _rms(x, g2n_ref[...]) * (1.0 + sc2) + sh2

    h_ref[halo:halo + tm, :] = norm_mod(x_ref[0]).astype(bf16)
    if on_grid:
        h_ref[0:halo, :] = jnp.where(t > 0, norm_mod(xp_ref[0]), 0.0).astype(bf16)
        h_ref[halo + tm:te, :] = jnp.where(t < nt - 1, norm_mod(xn_ref[0]), 0.0).astype(bf16)

    SUB, LANE = SUBLANES, LANES
    seg = GRID_W if on_grid else seq_len
    sub_row = lax.broadcasted_iota(jnp.int32, (SUB, LANE), 0)
    first_row = sub_row == 0
    last_row = sub_row == SUB - 1
    row_taps = (0, 1, 2) if on_grid else (1,)

    def ff_cols(fc, base):
        if isinstance(fc, int):
            return slice(base + fc * FF_CHUNK, base + (fc + 1) * FF_CHUNK)
        return pl.ds(pl.multiple_of(base + fc * FF_CHUNK, 128), FF_CHUNK)

    def up(fc, slot):
        gbufs[slot][...] = _dot(h_ref[...], wu_ref[:, ff_cols(fc, 0)])
        vbufs[slot][...] = _dot(h_ref[halo:halo + tm, :], wu_ref[:, ff_cols(fc, D_FF)])

    def gate_act(fc, slot, aslot):
        cw = cw_ref[:, ff_cols(fc, 0)]
        cb = cb_ref[:, ff_cols(fc, 0)]
        for c in range(FF_CHUNK // LANE):
            lanes = slice(c * LANE, (c + 1) * LANE)
            for s0 in range(0, te, seg):
                gate = gbufs[slot][s0:s0 + seg, lanes]
                g_l = pltpu.roll(gate, 1, axis=0)
                g_r = pltpu.roll(gate, seg - 1, axis=0)
                gl_ref[s0:s0 + SUB, lanes] = jnp.where(first_row, 0.0, g_l[0:SUB])
                gl_ref[s0 + SUB:s0 + seg, lanes] = g_l[SUB:seg]
                gr_ref[s0:s0 + seg - SUB, lanes] = g_r[0:seg - SUB]
                gr_ref[s0 + seg - SUB:s0 + seg, lanes] = jnp.where(last_row, 0.0, g_r[seg - SUB:seg])
            for r0 in range(0, tm, GRID_W):
                conv = jnp.broadcast_to(cb[:, lanes], (GRID_W, LANE))
                for kh in row_taps:
                    lo = r0 + kh * GRID_W if on_grid else r0
                    conv = conv + gl_ref[lo:lo + GRID_W, lanes] * cw[3 * kh:3 * kh + 1, lanes]
                    conv = conv + gbufs[slot][lo:lo + GRID_W, lanes] * cw[3 * kh + 1:3 * kh + 2, lanes]
                    conv = conv + gr_ref[lo:lo + GRID_W, lanes] * cw[3 * kh + 2:3 * kh + 3, lanes]
                gelu = conv * (0.5 + 0.5 * jnp.tanh(conv * (GELU_C0 + GELU_C1 * (conv * conv))))
                acts[aslot][r0:r0 + GRID_W, lanes] = (gelu * vbufs[slot][r0:r0 + GRID_W, lanes]).astype(bf16)

    def down(fc, aslot):
        return _dot(acts[aslot][...], wd_ref[fc])


    def pair(p, parity, with_down):
        fc = 2 * p
        wr, rd = 2 * parity, 2 * (1 - parity)
        if with_down:
            acc_ref[...] += down(fc - 2, rd) + down(fc - 1, rd + 1)
        up(fc + 1, 1)
        gate_act(fc, 0, wr)
        up(fc + 2, 0)
        gate_act(fc + 1, 1, wr + 1)

    n_pairs = (N_FF_CHUNKS - 1) // 2
    assert N_FF_CHUNKS == 2 * n_pairs + 1 and n_pairs % 2 == 1
    up(0, 0)
    acc_ref[...] = jnp.zeros_like(acc_ref)
    pair(0, 0, False)

    @pl.when(first_step)
    def _():
        wd_copy.wait()

    def two_pairs(i, carry):
        p = 2 * i + 1
        pair(p, 1, True)
        pair(p + 1, 0, True)
        return carry

    lax.fori_loop(0, (n_pairs - 1) // 2, two_pairs, 0)
    last = N_FF_CHUNKS - 1
    gate_act(last, 0, 2)
    ffn_out = acc_ref[...] + (down(last - 2, 0) + down(last - 1, 1) + down(last, 2))
    y_ref[0] = _rms(x_ref[0] + g2 * ffn_out, fg_ref[...])


FFN_W_DOWN = 2


def _ffn(x1, mod3, per_batch_mod, wts, tm, on_grid):
    B0, T0, _ = x1.shape
    if tm > T0:
        assert not per_batch_mod and not on_grid and tm % T0 == 0 and B0 % (tm // T0) == 0
        x1 = x1.reshape(B0 * T0 // tm, tm, D_MODEL)
    B, T, _ = x1.shape
    nt = T // tm
    halo = GRID_W if on_grid else 0
    te = tm + 2 * halo
    tok = pl.BlockSpec((1, tm, D_MODEL), lambda b, t: (b, t, 0))
    full = lambda a: pl.BlockSpec(a.shape, lambda b, t: (0,) * a.ndim)
    mod_map = (lambda b, t: (b, 0, 0)) if per_batch_mod else (lambda b, t: (0, 0, 0))
    in_specs = [tok]
    args = [x1]
    if on_grid:
        r = tm // GRID_W
        in_specs += [pl.BlockSpec((1, GRID_W, D_MODEL), lambda b, t: (b, jnp.maximum(t * r - 1, 0), 0)),
                     pl.BlockSpec((1, GRID_W, D_MODEL), lambda b, t: (b, jnp.minimum((t + 1) * r, T // GRID_W - 1), 0))]
        args += [x1, x1]
    in_specs += [pl.BlockSpec((1, 1, N_MOD * D_MODEL), mod_map)] + [
        pl.BlockSpec(memory_space=pl.ANY) if i == FFN_W_DOWN else full(w) for i, w in enumerate(wts)]
    args += [mod3] + list(wts)
    y = pl.pallas_call(
        functools.partial(_ffn_kernel, tm=tm, nt=nt, on_grid=on_grid, seq_len=T0),
        grid=(B, nt),
        in_specs=in_specs,
        out_specs=tok,
        out_shape=jax.ShapeDtypeStruct((B, T, D_MODEL), f32),
        scratch_shapes=[pltpu.VMEM((te, D_MODEL), bf16)]
        + [pltpu.VMEM((te, FF_CHUNK), f32)] * 2 + [pltpu.VMEM((tm, FF_CHUNK), f32)] * 2
        + [pltpu.VMEM((te, FF_CHUNK), f32)] * 2 + [pltpu.VMEM((tm, FF_CHUNK), bf16)] * 4
        + [pltpu.VMEM((tm, D_MODEL), f32)]
        + [pltpu.VMEM(wts[FFN_W_DOWN].shape, bf16), pltpu.SemaphoreType.DMA(())],
        compiler_params=pltpu.CompilerParams(dimension_semantics=("arbitrary", "arbitrary"),
                                             vmem_limit_bytes=VMEM_LIMIT),
        name="ffn_grid" if on_grid else "ffn_ctx",
    )(*args)
    return y.reshape(B0, T0, D_MODEL)


TOKEN_TILE = 512
MLSTM_TOKENS = 1024


def _tiles(T, per_batch_mod):
    span = TOKEN_TILE if not per_batch_mod else min(TOKEN_TILE, T)
    return span, span, span, max(1, MLSTM_TOKENS // T)


def _trunk(x, mod3, per_batch_mod, state, on_grid, w, cast_weights=()):
    tm_proj, tm_mix, tm_ffn, mlstm_nseq = _tiles(x.shape[1], per_batch_mod)
    (q, k, v, so, glu, sga, sgb, gates), casts = _in_proj(x, mod3, per_batch_mod, w["norm1_g"], w["in"], tm_proj,
                                                          *w["in_groups"], cast_weights=cast_weights)
    if cast_weights:
        w = dict(w, mix=w["mix"](casts), ffn=w["ffn"](casts))
    hn, *new_state = _mlstm(q, k, v, gates, state, mlstm_nseq, state is None)
    x1 = _mix_out(x, mod3, per_batch_mod, hn, so, glu, sga, sgb, w["mix"], tm_mix)
    y = _ffn(x1, mod3, per_batch_mod, w["ffn"], tm_ffn, on_grid)
    return y, new_state, w


def kernel(x_prompt, x_sample, c, state_C, state_n, state_m, c_ctx, w_ada, b_ada, norm1_g, w_in, b_in,
           mlstm_norm_g, w_proj_a, conv_dw_w, conv_dw_b, conv_ln_g, conv_ln_b, w_proj_b, w_out, norm2_g,
           w_up, ffn_dw_w, ffn_dw_b, w_down, final_norm_g):
    Bp = x_prompt.shape[0]
    Bl = x_sample.shape[0]
    l = 0
    row = lambda a: a.reshape(1, -1)

    ct = jnp.concatenate([c, c_ctx[None, :], jnp.zeros((8 - Bl - 1, D_MODEL), f32)], axis=0).T
    mod = _adaln_mod(ct, w_ada[l], row(b_ada[l]), Bl + 1)
    mod_lat = mod[0:Bl].reshape(Bl, 1, N_MOD * D_MODEL)
    mod_ctx = mod[Bl:Bl + 1].reshape(1, 1, N_MOD * D_MODEL)

    o_g = 4 * D_A
    o_u = o_g + 4 * NH_A
    nh = NH_A

    def gate_cols(a):
        g = a[..., o_g:o_u]
        pad = jnp.zeros(a.shape[:-1] + (GATE_PAD - 4 * nh,), a.dtype)
        return jnp.concatenate([g[..., 0:nh], g[..., 2 * nh:3 * nh], g[..., nh:2 * nh], g[..., 3 * nh:], pad], axis=-1)

    w_in_t = jnp.swapaxes(w_in[l], 0, 1).astype(bf16)
    bi = row(b_in[l])
    g_t = w_in_t[o_g:o_u]
    w_gates_t = jnp.concatenate([g_t[0:nh], g_t[2 * nh:3 * nh], g_t[nh:2 * nh], g_t[3 * nh:],
                                 jnp.zeros((GATE_PAD - 4 * nh, D_MODEL), bf16)], axis=0)
    w_in_parts = (w_in_t, bi[:, :o_g], bi[:, o_u:], w_gates_t, gate_cols(bi))
    cast_weights = [w_up[l], w_down[l], w_out[l], w_proj_a[l], w_proj_b[l]]

    def w_mix(casts):
        return (row(mlstm_norm_g[l]), casts[3],
                jnp.pad(conv_dw_w[l], ((0, 1), (0, 0))), row(conv_dw_b[l]), row(conv_ln_g[l]), row(conv_ln_b[l]),
                casts[4], casts[2])

    def w_ffn(casts):
        return (row(norm2_g[l]), casts[0], casts[1].reshape(N_FF_CHUNKS, FF_CHUNK, D_MODEL),
                ffn_dw_w[l].reshape(9, D_FF), row(ffn_dw_b[l]), row(final_norm_g))

    w = {"norm1_g": row(norm1_g[l]), "in": w_in_parts, "in_groups": (o_g, o_u), "mix": w_mix, "ffn": w_ffn}

    c0 = state_C[:, l]
    n0 = state_n[:, l].reshape(Bl, 2 * NH_A, DH_A)
    m0 = jnp.broadcast_to(state_m[:, l].reshape(Bl, 2 * NH_A, 1), (Bl, 2 * NH_A, DH_A))
    y_sample, _, w = _trunk(x_sample, mod_lat, True, (c0, n0, m0), True, w, cast_weights)

    y_prompt, (cn, nn, mn), _ = _trunk(x_prompt, mod_ctx, False, None, False, w)

    new_state_C = cn[:, None]
    new_state_n = nn.reshape(Bp, 1, 2, NH_A, DH_A)
    new_state_m = mn[:, :, 0].reshape(Bp, 1, 2, NH_A)
    return (y_prompt, y_sample, new_state_C, new_state_n, new_state_m)
```

```python
import functools

import jax
import jax.numpy as jnp
from jax import lax
from jax.experimental import pallas as pl
from jax.experimental.pallas import tpu as pltpu

D_MODEL = 1024
D_A = 512
NH_A = 4
DH_A = 128
CHUNK = 128
D_B = 512
CONV_W = 31
CONV_HALO = 16
CONV_ROWS = 64
D_FF = 2816
FF_CHUNK = 256
N_FF_CHUNKS = D_FF // FF_CHUNK
GRID_W = 64
N_MOD = 6
GATE_PAD = 128
EPS = 1e-6
Q_SCALE = DH_A ** -0.5
VMEM_LIMIT = 56 * 1024 * 1024
SUBLANES = 8
LANES = 128

f32 = jnp.float32
bf16 = jnp.bfloat16


def _rms(x, g):
    return x * lax.rsqrt(jnp.mean(x * x, axis=-1, keepdims=True) + EPS) * g


def _sigmoid(x):
    return 1.0 / (1.0 + jnp.exp(-x))


def _log_sigmoid(x):
    return jnp.minimum(x, 0.0) - jnp.log(1.0 + jnp.exp(-jnp.abs(x)))


GELU_C0 = 0.7978845608028654
GELU_C1 = GELU_C0 * 0.044715


def _dot(a, b):
    return jnp.dot(a, b, preferred_element_type=f32)


def _dot_nt(a, b):
    return lax.dot_general(a, b, (((1,), (1,)), ((), ())), preferred_element_type=f32)


MOD_CTX_ROW, MOD_LAT_ROW = 0, 1


def _mod_kernel(ct_ref, w_ref, b_ref, o_ref, *, n_rows):
    ct = ct_ref[...]
    st = ct * _sigmoid(ct)
    w = w_ref[...]
    bias = b_ref[...]
    for r in range(n_rows):
        s_col = jnp.broadcast_to(st[:, r:r + 1], (D_MODEL, LANES))
        o_ref[r] = bias + jnp.concatenate(
            [jnp.sum(w[:, c:c + LANES] * s_col, axis=0, keepdims=True) for c in range(0, w.shape[1], LANES)], axis=1)
    for r in range(n_rows, 8):
        o_ref[r] = jnp.zeros_like(bias)


def _adaln_mod(ct, w_ada, b_ada, n_rows):
    n = w_ada.shape[1]
    tn = 1024
    return pl.pallas_call(
        functools.partial(_mod_kernel, n_rows=n_rows),
        grid=(n // tn,),
        in_specs=[
            pl.BlockSpec((D_MODEL, 8), lambda j: (0, 0)),
            pl.BlockSpec((D_MODEL, tn), lambda j: (0, j)),
            pl.BlockSpec((1, tn), lambda j: (0, j)),
        ],
        out_specs=pl.BlockSpec((8, 1, tn), lambda j: (0, 0, j)),
        out_shape=jax.ShapeDtypeStruct((8, 1, n), f32),
        compiler_params=pltpu.CompilerParams(dimension_semantics=("arbitrary",), vmem_limit_bytes=VMEM_LIMIT),
        name="adaln_mod",
    )(ct, w_ada, b_ada)


def _inproj_kernel(*refs, o_a, o_b, n_cast):
    x_ref, mod_ref, g_ref, w_ref, ba_ref, bb_ref, wg_ref, bg_ref = refs[:8]
    cast_src = refs[8:8 + n_cast]
    q_ref, k_ref, v_ref, so_ref, glu_ref, sga_ref, sgb_ref, gates_ref = refs[8 + n_cast:16 + n_cast]
    cast_dst = refs[16 + n_cast:16 + 2 * n_cast]
    for src, dst in zip(cast_src, cast_dst):
        dst[...] = src[...].astype(bf16)

    x = x_ref[0]
    mod = mod_ref[0]
    sh1 = mod[:, 0:D_MODEL]
    sc1 = mod[:, D_MODEL:2 * D_MODEL]
    h = (_rms(x, g_ref[...]) * (1.0 + sc1) + sh1).astype(bf16)

    def proj(base, b_ref, lo, n):
        return _dot_nt(h, w_ref[base + lo:base + lo + n, :]) + b_ref[:, lo:lo + n]

    q_ref[0] = (proj(0, ba_ref, 0, D_A) * Q_SCALE).astype(bf16)
    k_ref[0] = proj(0, ba_ref, D_A, D_A).astype(bf16)
    v_ref[0] = proj(0, ba_ref, 2 * D_A, D_A).astype(bf16)
    so_ref[0] = _sigmoid(proj(0, ba_ref, 3 * D_A, D_A)).astype(bf16)
    glu_ref[0] = proj(o_b, bb_ref, 0, D_B) * _sigmoid(proj(o_b, bb_ref, D_B, D_B))
    sga_ref[0] = _sigmoid(proj(o_b, bb_ref, 2 * D_B, D_MODEL)).astype(bf16)
    sgb_ref[0] = _sigmoid(proj(o_b, bb_ref, 2 * D_B + D_MODEL, D_MODEL)).astype(bf16)
    gates_ref[0] = _dot_nt(h, wg_ref[...]) + bg_ref[...]


def _in_proj(x, mod3, per_batch_mod, norm_g, wts, tm, o_a, o_b, cast_weights=()):
    B0, T0, _ = x.shape
    if tm > T0:
        assert not per_batch_mod and tm % T0 == 0 and B0 % (tm // T0) == 0
        x = x.reshape(B0 * T0 // tm, tm, D_MODEL)
    B, T, _ = x.shape
    nt = T // tm
    tok = lambda n: pl.BlockSpec((1, tm, n), lambda b, t: (b, t, 0))
    full = lambda a: pl.BlockSpec(a.shape, lambda b, t: (0,) * a.ndim)
    mod_map = (lambda b, t: (b + MOD_LAT_ROW, 0, 0)) if per_batch_mod else (lambda b, t: (MOD_CTX_ROW, 0, 0))
    sds = lambda n, dt: jax.ShapeDtypeStruct((B, T, n), dt)

    def walk(a):
        block = a.shape[0] // (B * nt)
        assert block * B * nt == a.shape[0] and block % 16 == 0
        return pl.BlockSpec((block, a.shape[1]), lambda b, t: (b * nt + t, 0))

    cast_specs = [walk(a) for a in cast_weights]
    outs = pl.pallas_call(
        functools.partial(_inproj_kernel, o_a=o_a, o_b=o_b, n_cast=len(cast_weights)),
        grid=(B, nt),
        in_specs=[tok(D_MODEL), pl.BlockSpec((1, 1, N_MOD * D_MODEL), mod_map), full(norm_g)]
        + [full(w) for w in wts] + cast_specs,
        out_specs=[tok(D_A), tok(D_A), tok(D_A), tok(D_A), tok(D_B), tok(D_MODEL), tok(D_MODEL), tok(GATE_PAD)]
        + cast_specs,
        out_shape=[sds(D_A, bf16), sds(D_A, bf16), sds(D_A, bf16), sds(D_A, bf16), sds(D_B, f32),
                   sds(D_MODEL, bf16), sds(D_MODEL, bf16), sds(GATE_PAD, f32)]
        + [jax.ShapeDtypeStruct(a.shape, bf16) for a in cast_weights],
        compiler_params=pltpu.CompilerParams(dimension_semantics=("arbitrary", "arbitrary"),
                                             vmem_limit_bytes=VMEM_LIMIT),
        name="in_proj",
    )(x, mod3, norm_g, *wts, *cast_weights)
    return [o.reshape(B0, T0, o.shape[-1]) for o in outs[:8]], outs[8:]


NEG_BIG = -1e30
ST_U, ST_INTER, ST_EMJ, ST_WKN, ST_DECAY = 0, 8, 16, 24, 32
ST_ROWS = 40


def _chunk_scan(x, op, fill, prefix, lane, width):
    k = 1
    while k < CHUNK:
        if prefix:
            shifted = jnp.where(lane >= k, pltpu.roll(x, k, axis=1), fill)
        else:
            shifted = jnp.where(lane < CHUNK - k, pltpu.roll(x, width - k, axis=1), fill)
        x = op(x, shifted)
        k *= 2
    return x


def _mlstm_kernel(*refs, nseq, nc, state_in, state_out):
    refs = list(refs)
    q_ref, k_ref, v_ref, g_ref = refs[:4]
    del refs[:4]
    if state_in:
        c0_ref, n0_ref, m0_ref = refs[:3]
        del refs[:3]
    hn_ref = refs.pop(0)
    if state_out:
        cn_ref, nn_ref, mn_ref = refs[:3]
        del refs[:3]
    st_ref, wt_ref, vt_ref, qt_ref, u_ref, cp_ref, npf_ref, npb_ref, cst_ref, nsf_ref, nsb_ref = refs
    L = CHUNK
    nct = nseq * nc
    T = nct * L
    NR = 2 * NH_A

    lane = jnp.bitwise_and(lax.broadcasted_iota(jnp.int32, (NR, T), 1), L - 1)
    is_fwd = lax.broadcasted_iota(jnp.int32, (NR, T), 0) < NH_A
    is_fwd_c = lax.broadcasted_iota(jnp.int32, (NR, L), 0) < NH_A
    i_parts, f_parts = [], []
    for c in range(nct):
        gt = g_ref[0, c * L:(c + 1) * L, :].T
        i_parts.append(gt[0:NR])
        f_parts.append(gt[NR:2 * NR])
    ig = jnp.concatenate(i_parts, axis=1)
    lf = _log_sigmoid(jnp.concatenate(f_parts, axis=1))
    scan = functools.partial(_chunk_scan, lane=lane, width=T)
    ps = scan(lf, jnp.add, 0.0, True)
    ss = scan(lf, jnp.add, 0.0, False)
    b = jnp.where(is_fwd, ps, ss)
    btot = ps + ss - lf
    w = ig - b
    cmw = jnp.where(is_fwd, scan(w, jnp.maximum, -jnp.inf, True), scan(w, jnp.maximum, -jnp.inf, False))
    wk = btot - b + ig
    a = jnp.maximum(scan(wk, jnp.maximum, -jnp.inf, True), scan(wk, jnp.maximum, -jnp.inf, False))
    chunk_lanes = lambda cg: slice(cg * L, (cg + 1) * L)
    pad_rows = jnp.zeros((L - NR, L), f32)
    for s in range(nseq):
        ms = [m0_ref[s] if state_in else jnp.zeros((NR, L), f32)]
        for t in range(nc):
            cf, cb = chunk_lanes(s * nc + t), chunk_lanes(s * nc + nc - 1 - t)
            bt = jnp.where(is_fwd_c, btot[:, cf], btot[:, cb])
            at = jnp.where(is_fwd_c, a[:, cf], a[:, cb])
            ms.append(jnp.maximum(bt + ms[-1], at))
        if state_out:
            mn_ref[s] = ms[nc]
        for c in range(nc):
            cg = s * nc + c
            sl = chunk_lanes(cg)
            m_prev = jnp.where(is_fwd_c, ms[c], ms[nc - 1 - c])
            m_new = jnp.where(is_fwd_c, ms[c + 1], ms[nc - c])
            mx = jnp.maximum(m_prev, cmw[:, sl])
            st_ref[cg, ST_U:ST_U + NR] = -mx
            st_ref[cg, ST_INTER:ST_INTER + NR] = jnp.exp(m_prev - mx)
            st_ref[cg, ST_EMJ:ST_EMJ + NR] = jnp.exp(-mx - b[:, sl])
            st_ref[cg, ST_WKN:ST_WKN + NR] = jnp.exp(wk[:, sl] - m_new)
            st_ref[cg, ST_DECAY:ST_DECAY + NR] = jnp.exp(btot[:, sl] + m_prev - m_new)
            wt_ref[cg] = jnp.concatenate([w[:, sl], pad_rows], axis=0).T

    row = lax.broadcasted_iota(jnp.int32, (L, L), 0)
    col = lax.broadcasted_iota(jnp.int32, (L, L), 1)
    masks = (row <= col, row >= col)
    first_of8 = lax.broadcasted_iota(jnp.int32, (8, L), 0) == 0

    def row_tile(x):
        return jnp.where(first_of8, x, 0.0)

    head_lanes = [slice(h * DH_A, (h + 1) * DH_A) for h in range(NH_A)]

    def increments(j, carry):
        rows = pl.ds(pl.multiple_of(j * L, L), L)
        for h in range(NH_A):
            kh = k_ref[0, rows, head_lanes[h]]
            vt = v_ref[0, rows, head_lanes[h]].astype(f32).T
            vt_ref[j, h] = vt.astype(bf16)
            qt_ref[j, h] = q_ref[0, rows, head_lanes[h]].astype(f32).T.astype(bf16)
            wf = st_ref[j, ST_WKN + h:ST_WKN + h + 1, :]
            wb = st_ref[j, ST_WKN + NH_A + h:ST_WKN + NH_A + h + 1, :]
            lhs = jnp.concatenate([vt * wf, vt * wb, row_tile(wf), row_tile(wb)], axis=0).astype(bf16)
            u_ref[j, h] = _dot(lhs, kh)
        return carry

    lax.fori_loop(0, nct, increments, 0, unroll=4)

    seq_heads = [(s, h) for s in range(nseq) for h in range(NH_A)]
    for s, h in seq_heads:
        i = s * NH_A + h
        if state_in:
            cst_ref[i, 0:DH_A] = c0_ref[s, 0, h].T
            cst_ref[i, DH_A:2 * DH_A] = c0_ref[s, 1, h].T
            nsf_ref[i] = row_tile(n0_ref[s, h:h + 1, :])
            nsb_ref[i] = row_tile(n0_ref[s, NH_A + h:NH_A + h + 1, :])
        else:
            cst_ref[i] = jnp.zeros((2 * DH_A, DH_A), f32)
            nsf_ref[i] = jnp.zeros((8, DH_A), f32)
            nsb_ref[i] = jnp.zeros((8, DH_A), f32)

    def recur(t, carry):
        for s, h in seq_heads:
            i = s * NH_A + h
            jf = s * nc + t
            jb = s * nc + nc - 1 - t
            dec_f = st_ref[jf, ST_DECAY + h:ST_DECAY + h + 1, :]
            dec_b = st_ref[jb, ST_DECAY + NH_A + h:ST_DECAY + NH_A + h + 1, :]
            c_f = cst_ref[i, 0:DH_A]
            c_b = cst_ref[i, DH_A:2 * DH_A]
            cp_ref[jf, h, 0:DH_A] = c_f.astype(bf16)
            cp_ref[jb, h, DH_A:2 * DH_A] = c_b.astype(bf16)
            cst_ref[i, 0:DH_A] = dec_f * c_f + u_ref[jf, h, 0:DH_A]
            cst_ref[i, DH_A:2 * DH_A] = dec_b * c_b + u_ref[jb, h, DH_A:2 * DH_A]
            n_f = nsf_ref[i]
            n_b = nsb_ref[i]
            npf_ref[jf, h] = n_f
            npb_ref[jb, h] = n_b
            nsf_ref[i] = dec_f * n_f + u_ref[jf, h, 2 * DH_A:2 * DH_A + 8]
            nsb_ref[i] = dec_b * n_b + u_ref[jb, h, 2 * DH_A + 8:2 * DH_A + 16]
        return carry

    lax.fori_loop(0, nc, recur, 0)
    for s, h in seq_heads if state_out else ():
        i = s * NH_A + h
        cn_ref[s, 0, h] = cst_ref[i, 0:DH_A].T
        cn_ref[s, 1, h] = cst_ref[i, DH_A:2 * DH_A].T
        nn_ref[s, h:h + 1, :] = nsf_ref[i, 0:1]
        nn_ref[s, NH_A + h:NH_A + h + 1, :] = nsb_ref[i, 0:1]

    def outputs(j, carry):
        rows = pl.ds(pl.multiple_of(j * L, L), L)
        wt = wt_ref[j]
        heads = range(NH_A)
        kqs = []
        for h in heads:
            kh = k_ref[0, rows, head_lanes[h]]
            n_rows = jnp.concatenate([npf_ref[j, h], npb_ref[j, h]], axis=0).astype(bf16)
            kqs.append(_dot(jnp.concatenate([kh, n_rows, cp_ref[j, h]], axis=0), qt_ref[j, h]))
        decays = [[jnp.exp(jnp.where(masks[d], wt[:, NH_A * d + h:NH_A * d + h + 1]
                                     + st_ref[j, ST_U + NH_A * d + h:ST_U + NH_A * d + h + 1, :], NEG_BIG))
                   for d in range(2)] for h in heads]
        s_sums, h_ts = [], []
        for h in heads:
            kq = kqs[h]
            s_sum = None
            h_t = None
            for d in range(2):
                r = NH_A * d + h
                inter = st_ref[j, ST_INTER + r:ST_INTER + r + 1, :]
                emj = st_ref[j, ST_EMJ + r:ST_EMJ + r + 1, :]
                s_t = kq[0:L] * decays[h][d]
                qn = kq[L + 8 * d:L + 8 * d + 1]
                den = inter * qn + jnp.sum(s_t, axis=0, keepdims=True)
                rr = 1.0 / jnp.maximum(jnp.abs(den), emj)
                s_sum = s_t * rr if d == 0 else s_sum + s_t * rr
                part = kq[L + 16 + d * DH_A:L + 16 + (d + 1) * DH_A] * (inter * rr)
                h_t = part if d == 0 else h_t + part
            s_sums.append(s_sum.astype(bf16))
            h_ts.append(h_t)
        h_ts = [h_ts[h] + _dot(vt_ref[j, h], s_sums[h]) for h in heads]
        h_ts = [x * lax.rsqrt(jnp.mean(x * x, axis=0, keepdims=True) + EPS) for x in h_ts]
        for h in heads:
            hn_ref[0, rows, head_lanes[h]] = h_ts[h].T.astype(bf16)
        return carry

    lax.fori_loop(0, nct, outputs, 0, unroll=4)


def _mlstm(q, k, v, gates, state, nseq, state_out):
    B, T, _ = q.shape
    nc = T // CHUNK
    nct = nseq * nc
    G = B // nseq
    fold = lambda a: a.reshape(G, nseq * T, a.shape[-1])
    seq = lambda n: pl.BlockSpec((1, nseq * T, n), lambda b: (b, 0, 0))
    st_c = pl.BlockSpec((nseq, 2, NH_A, DH_A, DH_A), lambda b: (b, 0, 0, 0, 0))
    st_v = pl.BlockSpec((nseq, 2 * NH_A, DH_A), lambda b: (b, 0, 0))
    state_specs = [st_c, st_v, st_v]
    state_shapes = [jax.ShapeDtypeStruct((B, 2, NH_A, DH_A, DH_A), f32),
                    jax.ShapeDtypeStruct((B, 2 * NH_A, DH_A), f32),
                    jax.ShapeDtypeStruct((B, 2 * NH_A, DH_A), f32)]
    outs = pl.pallas_call(
        functools.partial(_mlstm_kernel, nseq=nseq, nc=nc, state_in=state is not None, state_out=state_out),
        grid=(G,),
        in_specs=[seq(D_A), seq(D_A), seq(D_A), seq(GATE_PAD)] + (state_specs if state is not None else []),
        out_specs=[seq(D_A)] + (state_specs if state_out else []),
        scratch_shapes=[pltpu.VMEM((nct, ST_ROWS, CHUNK), f32), pltpu.VMEM((nct, CHUNK, GATE_PAD), f32),
                        pltpu.VMEM((nct, NH_A, DH_A, CHUNK), bf16), pltpu.VMEM((nct, NH_A, DH_A, CHUNK), bf16),
                        pltpu.VMEM((nct, NH_A, 2 * DH_A + 16, DH_A), f32),
                        pltpu.VMEM((nct, NH_A, 2 * DH_A, DH_A), bf16),
                        pltpu.VMEM((nct, NH_A, 8, DH_A), f32), pltpu.VMEM((nct, NH_A, 8, DH_A), f32),
                        pltpu.VMEM((nseq * NH_A, 2 * DH_A, DH_A), f32),
                        pltpu.VMEM((nseq * NH_A, 8, DH_A), f32), pltpu.VMEM((nseq * NH_A, 8, DH_A), f32)],
        out_shape=[jax.ShapeDtypeStruct((G, nseq * T, D_A), bf16)] + (state_shapes if state_out else []),
        compiler_params=pltpu.CompilerParams(dimension_semantics=("arbitrary",), vmem_limit_bytes=VMEM_LIMIT),
        name="mlstm",
    )(fold(q), fold(k), fold(v), fold(gates), *(state if state is not None else ()))
    return (outs[0].reshape(B, T, D_A),) + tuple(outs[1:])


def _mix_kernel(x_ref, mod_ref, hn_ref, so_ref, glu_ref, glu_prev_ref, glu_next_ref, sga_ref, sgb_ref,
                ng_ref, wpa_ref, cw_ref, cb_ref, lng_ref, lnb_ref, wpb_ref, wout_ref,
                x1_ref, ext_ref, zs_ref, *, tm, nt, nsub):
    t = pl.program_id(1)
    mod = mod_ref[0]
    g1 = mod[:, 2 * D_MODEL:3 * D_MODEL]

    a_in = (so_ref[0].astype(f32) * (hn_ref[0].astype(f32) * ng_ref[...])).astype(bf16)
    branch_a = _dot(a_in, wpa_ref[...])

    H = CONV_HALO
    SUB, LANE = SUBLANES, LANES
    tsub = tm // nsub
    te = tsub + 2 * H
    for s in range(nsub):
        inner = nsub == 1
        ext_ref[0, s, 0:H, :] = jnp.where(t > 0, glu_prev_ref[0], 0.0) if inner else jnp.zeros((H, D_B), f32)
        ext_ref[0, s, H:H + tsub, :] = glu_ref[0, s * tsub:(s + 1) * tsub, :]
        ext_ref[0, s, H + tsub:te, :] = jnp.where(t < nt - 1, glu_next_ref[0], 0.0) if inner else jnp.zeros((H, D_B), f32)
    for s in range(nsub):
        for c in range(D_B // LANE):
            lanes = slice(c * LANE, (c + 1) * LANE)
            base = ext_ref[0, s, :, lanes]
            for k in range(1, SUB):
                ext_ref[k, s, :, lanes] = pltpu.roll(base, te - k, axis=0)
    off = H - CONV_W // 2
    for s in range(nsub):
        for r0 in range(0, tsub, CONV_ROWS):
            parts = []
            for c in range(D_B // LANE):
                lanes = slice(c * LANE, (c + 1) * LANE)
                acc = jnp.broadcast_to(cb_ref[:, lanes], (CONV_ROWS, LANE))
                for w in range(CONV_W):
                    k, a = (off + w) % SUB, (off + w) // SUB * SUB
                    acc = acc + ext_ref[k, s, r0 + a:r0 + a + CONV_ROWS, lanes] * cw_ref[w:w + 1, lanes]
                parts.append(acc)
            z = jnp.concatenate(parts, axis=1)
            mu = jnp.mean(z, axis=-1, keepdims=True)
            zc = z - mu
            var = jnp.mean(zc * zc, axis=-1, keepdims=True)
            zn = zc * lax.rsqrt(var + EPS) * lng_ref[...] + lnb_ref[...]
            zs_ref[s * tsub + r0:s * tsub + r0 + CONV_ROWS, :] = (zn * _sigmoid(zn)).astype(bf16)
    branch_b = _dot(zs_ref[...], wpb_ref[...])

    merged = sga_ref[0].astype(f32) * branch_a + sgb_ref[0].astype(f32) * branch_b
    x1_ref[0] = x_ref[0] + g1 * _dot(merged.astype(bf16), wout_ref[...])


def _mix_out(x, mod3, per_batch_mod, hn, so, glu, sga, sgb, wts, tm):
    B0, T0, _ = x.shape
    nsub = 1
    if tm > T0:
        assert not per_batch_mod and tm % T0 == 0 and B0 % (tm // T0) == 0
        nsub = tm // T0
        x, hn, so, glu, sga, sgb = (a.reshape(B0 // nsub, tm, a.shape[-1]) for a in (x, hn, so, glu, sga, sgb))
    B, T, _ = x.shape
    nt = T // tm
    H = CONV_HALO
    r = tm // H
    tok = lambda n: pl.BlockSpec((1, tm, n), lambda b, t: (b, t, 0))
    full = lambda a: pl.BlockSpec(a.shape, lambda b, t: (0,) * a.ndim)
    mod_map = (lambda b, t: (b + MOD_LAT_ROW, 0, 0)) if per_batch_mod else (lambda b, t: (MOD_CTX_ROW, 0, 0))
    prev = pl.BlockSpec((1, H, D_B), lambda b, t: (b, jnp.maximum(t * r - 1, 0), 0))
    nxt = pl.BlockSpec((1, H, D_B), lambda b, t: (b, jnp.minimum((t + 1) * r, T // H - 1), 0))
    x1 = pl.pallas_call(
        functools.partial(_mix_kernel, tm=tm, nt=nt, nsub=nsub),
        grid=(B, nt),
        in_specs=[tok(D_MODEL), pl.BlockSpec((1, 1, N_MOD * D_MODEL), mod_map), tok(D_A), tok(D_A),
                  tok(D_B), prev, nxt, tok(D_MODEL), tok(D_MODEL)] + [full(w) for w in wts],
        out_specs=tok(D_MODEL),
        out_shape=jax.ShapeDtypeStruct((B, T, D_MODEL), f32),
        scratch_shapes=[pltpu.VMEM((SUBLANES, nsub, tm // nsub + 2 * H, D_B), f32), pltpu.VMEM((tm, D_B), bf16)],
        compiler_params=pltpu.CompilerParams(dimension_semantics=("arbitrary", "arbitrary"),
                                             vmem_limit_bytes=VMEM_LIMIT),
        name="mix_out",
    )(x, mod3, hn, so, glu, glu, glu, sga, sgb, *wts)
    return x1.reshape(B0, T0, D_MODEL)


def _ffn_kernel(*refs, tm, nt, on_grid, seq_len):
    if on_grid:
        (x_ref, xp_ref, xn_ref, mod_ref, g2n_ref, wu_ref, wd_hbm, cw_ref, cb_ref, fg_ref,
         y_ref, h_ref, g0_ref, g1_ref, v0_ref, v1_ref, gl_ref, gr_ref, a0_ref, a1_ref, a2_ref, a3_ref, acc_ref,
         wd_ref, wd_sem) = refs
    else:
        (x_ref, mod_ref, g2n_ref, wu_ref, wd_hbm, cw_ref, cb_ref, fg_ref,
         y_ref, h_ref, g0_ref, g1_ref, v0_ref, v1_ref, gl_ref, gr_ref, a0_ref, a1_ref, a2_ref, a3_ref, acc_ref,
         wd_ref, wd_sem) = refs
    gbufs, vbufs, acts = (g0_ref, g1_ref), (v0_ref, v1_ref), (a0_ref, a1_ref, a2_ref, a3_ref)
    t = pl.program_id(1)
    first_step = jnp.logical_and(pl.program_id(0) == 0, t == 0)
    wd_copy = pltpu.make_async_copy(wd_hbm, wd_ref, wd_sem)

    @pl.when(first_step)
    def _():
        wd_copy.start()

    halo = GRID_W if on_grid else 0
    te = tm + 2 * halo
    mod = mod_ref[0]
    sh2 = mod[:, 3 * D_MODEL:4 * D_MODEL]
    sc2 = mod[:, 4 * D_MODEL:5 * D_MODEL]
    g2 = mod[:, 5 * D_MODEL:6 * D_MODEL]

    def norm_mod(x):
        return _rms(x, g2n_ref[...]) * (1.0 + sc2) + sh2

    h_ref[halo:halo + tm, :] = norm_mod(x_ref[0]).astype(bf16)
    if on_grid:
        h_ref[0:halo, :] = jnp.where(t > 0, norm_mod(xp_ref[0]), 0.0).astype(bf16)
        h_ref[halo + tm:te, :] = jnp.where(t < nt - 1, norm_mod(xn_ref[0]), 0.0).astype(bf16)

    SUB, LANE = SUBLANES, LANES
    seg = GRID_W if on_grid else seq_len
    sub_row = lax.broadcasted_iota(jnp.int32, (SUB, LANE), 0)
    first_row = sub_row == 0
    last_row = sub_row == SUB - 1
    row_taps = (0, 1, 2) if on_grid else (1,)

    def ff_cols(fc, base):
        if isinstance(fc, int):
            return slice(base + fc * FF_CHUNK, base + (fc + 1) * FF_CHUNK)
        return pl.ds(pl.multiple_of(base + fc * FF_CHUNK, 128), FF_CHUNK)

    def up(fc, slot):
        gbufs[slot][...] = _dot(h_ref[...], wu_ref[:, ff_cols(fc, 0)])
        vbufs[slot][...] = _dot(h_ref[halo:halo + tm, :], wu_ref[:, ff_cols(fc, D_FF)])

    def gate_act(fc, slot, aslot):
        cw = cw_ref[:, ff_cols(fc, 0)]
        cb = cb_ref[:, ff_cols(fc, 0)]
        for c in range(FF_CHUNK // LANE):
            lanes = slice(c * LANE, (c + 1) * LANE)
            for s0 in range(0, te, seg):
                gate = gbufs[slot][s0:s0 + seg, lanes]
                g_l = pltpu.roll(gate, 1, axis=0)
                g_r = pltpu.roll(gate, seg - 1, axis=0)
                gl_ref[s0:s0 + SUB, lanes] = jnp.where(first_row, 0.0, g_l[0:SUB])
                gl_ref[s0 + SUB:s0 + seg, lanes] = g_l[SUB:seg]
                gr_ref[s0:s0 + seg - SUB, lanes] = g_r[0:seg - SUB]
                gr_ref[s0 + seg - SUB:s0 + seg, lanes] = jnp.where(last_row, 0.0, g_r[seg - SUB:seg])
            for r0 in range(0, tm, GRID_W):
                conv = jnp.broadcast_to(cb[:, lanes], (GRID_W, LANE))
                for kh in row_taps:
                    lo = r0 + kh * GRID_W if on_grid else r0
                    conv = conv + gl_ref[lo:lo + GRID_W, lanes] * cw[3 * kh:3 * kh + 1, lanes]
                    conv = conv + gbufs[slot][lo:lo + GRID_W, lanes] * cw[3 * kh + 1:3 * kh + 2, lanes]
                    conv = conv + gr_ref[lo:lo + GRID_W, lanes] * cw[3 * kh + 2:3 * kh + 3, lanes]
                gelu = conv * (0.5 + 0.5 * jnp.tanh(conv * (GELU_C0 + GELU_C1 * (conv * conv))))
                acts[aslot][r0:r0 + GRID_W, lanes] = (gelu * vbufs[slot][r0:r0 + GRID_W, lanes]).astype(bf16)

    def down(fc, aslot):
        return _dot(acts[aslot][...], wd_ref[fc])


    def pair(p, parity, with_down):
        fc = 2 * p
        wr, rd = 2 * parity, 2 * (1 - parity)
        if with_down:
            acc_ref[...] += down(fc - 2, rd) + down(fc - 1, rd + 1)
        up(fc + 1, 1)
        gate_act(fc, 0, wr)
        up(fc + 2, 0)
        gate_act(fc + 1, 1, wr + 1)

    n_pairs = (N_FF_CHUNKS - 1) // 2
    assert N_FF_CHUNKS == 2 * n_pairs + 1 and n_pairs % 2 == 1
    up(0, 0)
    acc_ref[...] = jnp.zeros_like(acc_ref)
    pair(0, 0, False)

    @pl.when(first_step)
    def _():
        wd_copy.wait()

    def two_pairs(i, carry):
        p = 2 * i + 1
        pair(p, 1, True)
        pair(p + 1, 0, True)
        return carry

    lax.fori_loop(0, (n_pairs - 1) // 2, two_pairs, 0)
    last = N_FF_CHUNKS - 1
    gate_act(last, 0, 2)
    ffn_out = acc_ref[...] + (down(last - 2, 0) + down(last - 1, 1) + down(last, 2))
    y_ref[0] = _rms(x_ref[0] + g2 * ffn_out, fg_ref[...])


FFN_W_DOWN = 2


def _ffn(x1, mod3, per_batch_mod, wts, tm, on_grid):
    B0, T0, _ = x1.shape
    if tm > T0:
        assert not per_batch_mod and not on_grid and tm % T0 == 0 and B0 % (tm // T0) == 0
        x1 = x1.reshape(B0 * T0 // tm, tm, D_MODEL)
    B, T, _ = x1.shape
    nt = T // tm
    halo = GRID_W if on_grid else 0
    te = tm + 2 * halo
    tok = pl.BlockSpec((1, tm, D_MODEL), lambda b, t: (b, t, 0))
    full = lambda a: pl.BlockSpec(a.shape, lambda b, t: (0,) * a.ndim)
    mod_map = (lambda b, t: (b + MOD_LAT_ROW, 0, 0)) if per_batch_mod else (lambda b, t: (MOD_CTX_ROW, 0, 0))
    in_specs = [tok]
    args = [x1]
    if on_grid:
        r = tm // GRID_W
        in_specs += [pl.BlockSpec((1, GRID_W, D_MODEL), lambda b, t: (b, jnp.maximum(t * r - 1, 0), 0)),
                     pl.BlockSpec((1, GRID_W, D_MODEL), lambda b, t: (b, jnp.minimum((t + 1) * r, T // GRID_W - 1), 0))]
        args += [x1, x1]
    in_specs += [pl.BlockSpec((1, 1, N_MOD * D_MODEL), mod_map)] + [
        pl.BlockSpec(memory_space=pl.ANY) if i == FFN_W_DOWN else full(w) for i, w in enumerate(wts)]
    args += [mod3] + list(wts)
    y = pl.pallas_call(
        functools.partial(_ffn_kernel, tm=tm, nt=nt, on_grid=on_grid, seq_len=T0),
        grid=(B, nt),
        in_specs=in_specs,
        out_specs=tok,
        out_shape=jax.ShapeDtypeStruct((B, T, D_MODEL), f32),
        scratch_shapes=[pltpu.VMEM((te, D_MODEL), bf16)]
        + [pltpu.VMEM((te, FF_CHUNK), f32)] * 2 + [pltpu.VMEM((tm, FF_CHUNK), f32)] * 2
        + [pltpu.VMEM((te, FF_CHUNK), f32)] * 2 + [pltpu.VMEM((tm, FF_CHUNK), bf16)] * 4
        + [pltpu.VMEM((tm, D_MODEL), f32)]
        + [pltpu.VMEM(wts[FFN_W_DOWN].shape, bf16), pltpu.SemaphoreType.DMA(())],
        compiler_params=pltpu.CompilerParams(dimension_semantics=("arbitrary", "arbitrary"),
                                             vmem_limit_bytes=VMEM_LIMIT),
        name="ffn_grid" if on_grid else "ffn_ctx",
    )(*args)
    return y.reshape(B0, T0, D_MODEL)


TOKEN_TILE = 512
MLSTM_TOKENS = 1024


def _tiles(T, per_batch_mod):
    span = TOKEN_TILE if not per_batch_mod else min(TOKEN_TILE, T)
    return span, span, span, max(1, MLSTM_TOKENS // T)


def _trunk(x, mod3, per_batch_mod, state, on_grid, w, cast_weights=()):
    tm_proj, tm_mix, tm_ffn, mlstm_nseq = _tiles(x.shape[1], per_batch_mod)
    (q, k, v, so, glu, sga, sgb, gates), casts = _in_proj(x, mod3, per_batch_mod, w["norm1_g"], w["in"], tm_proj,
                                                          *w["in_groups"], cast_weights=cast_weights)
    if cast_weights:
        w = dict(w, mix=w["mix"](casts), ffn=w["ffn"](casts))
    hn, *new_state = _mlstm(q, k, v, gates, state, mlstm_nseq, state is None)
    x1 = _mix_out(x, mod3, per_batch_mod, hn, so, glu, sga, sgb, w["mix"], tm_mix)
    y = _ffn(x1, mod3, per_batch_mod, w["ffn"], tm_ffn, on_grid)
    return y, new_state, w


def kernel(x_prompt, x_sample, c, state_C, state_n, state_m, c_ctx, w_ada, b_ada, norm1_g, w_in, b_in,
           mlstm_norm_g, w_proj_a, conv_dw_w, conv_dw_b, conv_ln_g, conv_ln_b, w_proj_b, w_out, norm2_g,
           w_up, ffn_dw_w, ffn_dw_b, w_down, final_norm_g):
    Bp = x_prompt.shape[0]
    Bl = x_sample.shape[0]
    l = 0
    row = lambda a: a.reshape(1, -1)

    ct = jnp.concatenate([c_ctx[None, :], c, jnp.zeros((8 - Bl - 1, D_MODEL), f32)], axis=0).T
    mod = _adaln_mod(ct, w_ada[l], row(b_ada[l]), Bl + 1)

    o_g = 4 * D_A
    o_u = o_g + 4 * NH_A
    nh = NH_A

    def gate_cols(a):
        g = a[..., o_g:o_u]
        pad = jnp.zeros(a.shape[:-1] + (GATE_PAD - 4 * nh,), a.dtype)
        return jnp.concatenate([g[..., 0:nh], g[..., 2 * nh:3 * nh], g[..., nh:2 * nh], g[..., 3 * nh:], pad], axis=-1)

    w_in_t = jnp.swapaxes(w_in[l], 0, 1).astype(bf16)
    bi = row(b_in[l])
    g_t = w_in_t[o_g:o_u]
    w_gates_t = jnp.concatenate([g_t[0:nh], g_t[2 * nh:3 * nh], g_t[nh:2 * nh], g_t[3 * nh:],
                                 jnp.zeros((GATE_PAD - 4 * nh, D_MODEL), bf16)], axis=0)
    w_in_parts = (w_in_t, bi[:, :o_g], bi[:, o_u:], w_gates_t, gate_cols(bi))
    cast_weights = [w_up[l], w_down[l], w_out[l], w_proj_a[l], w_proj_b[l]]

    def w_mix(casts):
        return (row(mlstm_norm_g[l]), casts[3],
                jnp.pad(conv_dw_w[l], ((0, 1), (0, 0))), row(conv_dw_b[l]), row(conv_ln_g[l]), row(conv_ln_b[l]),
                casts[4], casts[2])

    def w_ffn(casts):
        return (row(norm2_g[l]), casts[0], casts[1].reshape(N_FF_CHUNKS, FF_CHUNK, D_MODEL),
                ffn_dw_w[l].reshape(9, D_FF), row(ffn_dw_b[l]), row(final_norm_g))

    w = {"norm1_g": row(norm1_g[l]), "in": w_in_parts, "in_groups": (o_g, o_u), "mix": w_mix, "ffn": w_ffn}

    c0 = state_C[:, l]
    n0 = state_n[:, l].reshape(Bl, 2 * NH_A, DH_A)
    m0 = jnp.broadcast_to(state_m[:, l].reshape(Bl, 2 * NH_A, 1), (Bl, 2 * NH_A, DH_A))
    y_sample, _, w = _trunk(x_sample, mod, True, (c0, n0, m0), True, w, cast_weights)

    y_prompt, (cn, nn, mn), _ = _trunk(x_prompt, mod, False, None, False, w)

    new_state_C = cn[:, None]
    new_state_n = nn.reshape(Bp, 1, 2, NH_A, DH_A)
    new_state_m = mn[:, :, 0].reshape(Bp, 1, 2, NH_A)
    return (y_prompt, y_sample, new_state_C, new_state_n, new_state_m)
```

```python
import functools

import jax
import jax.numpy as jnp
from jax import lax
from jax.experimental import pallas as pl
from jax.experimental.pallas import tpu as pltpu

D_MODEL = 1024
D_A = 512
NH_A = 4
DH_A = 128
CHUNK = 128
D_B = 512
CONV_W = 31
CONV_HALO = 16
CONV_ROWS = 64
D_FF = 2816
FF_CHUNK = 256
N_FF_CHUNKS = D_FF // FF_CHUNK
GRID_W = 64
N_MOD = 6
GATE_PAD = 128
EPS = 1e-6
Q_SCALE = DH_A ** -0.5
VMEM_LIMIT = 60 * 1024 * 1024
SUBLANES = 8
LANES = 128

f32 = jnp.float32
bf16 = jnp.bfloat16


def _rms(x, g):
    return x * lax.rsqrt(jnp.mean(x * x, axis=-1, keepdims=True) + EPS) * g


def _sigmoid(x):
    return 1.0 / (1.0 + jnp.exp(-x))


def _log_sigmoid(x):
    return jnp.minimum(x, 0.0) - jnp.log(1.0 + jnp.exp(-jnp.abs(x)))


GELU_C0 = 0.7978845608028654
GELU_C1 = GELU_C0 * 0.044715


def _dot(a, b):
    return jnp.dot(a, b, preferred_element_type=f32)


def _dot_nt(a, b):
    return lax.dot_general(a, b, (((1,), (1,)), ((), ())), preferred_element_type=f32)


MOD_CTX_ROW, MOD_LAT_ROW = 0, 1


def _mod_kernel(ct_ref, w_ref, b_ref, o_ref, *, n_rows):
    ct = ct_ref[...]
    st = ct * _sigmoid(ct)
    w = w_ref[...]
    bias = b_ref[...]
    for r in range(n_rows):
        s_col = jnp.broadcast_to(st[:, r:r + 1], (D_MODEL, LANES))
        o_ref[r] = bias + jnp.concatenate(
            [jnp.sum(w[:, c:c + LANES] * s_col, axis=0, keepdims=True) for c in range(0, w.shape[1], LANES)], axis=1)
    for r in range(n_rows, 8):
        o_ref[r] = jnp.zeros_like(bias)


def _adaln_mod(ct, w_ada, b_ada, n_rows):
    n = w_ada.shape[1]
    tn = 1024
    return pl.pallas_call(
        functools.partial(_mod_kernel, n_rows=n_rows),
        grid=(n // tn,),
        in_specs=[
            pl.BlockSpec((D_MODEL, 8), lambda j: (0, 0)),
            pl.BlockSpec((D_MODEL, tn), lambda j: (0, j)),
            pl.BlockSpec((1, tn), lambda j: (0, j)),
        ],
        out_specs=pl.BlockSpec((8, 1, tn), lambda j: (0, 0, j)),
        out_shape=jax.ShapeDtypeStruct((8, 1, n), f32),
        compiler_params=pltpu.CompilerParams(dimension_semantics=("arbitrary",), vmem_limit_bytes=VMEM_LIMIT),
        name="adaln_mod",
    )(ct, w_ada, b_ada)


def _inproj_kernel(*refs, o_a, o_b, n_cast):
    x_ref, mod_ref, g_ref, w_ref, ba_ref, bb_ref, wg_ref, bg_ref = refs[:8]
    cast_src = refs[8:8 + n_cast]
    q_ref, k_ref, v_ref, so_ref, glu_ref, sga_ref, sgb_ref, gates_ref = refs[8 + n_cast:16 + n_cast]
    cast_dst = refs[16 + n_cast:16 + 2 * n_cast]
    for src, dst in zip(cast_src, cast_dst):
        dst[...] = src[...].astype(bf16)

    x = x_ref[0]
    mod = mod_ref[0]
    sh1 = mod[:, 0:D_MODEL]
    sc1 = mod[:, D_MODEL:2 * D_MODEL]
    h = (_rms(x, g_ref[...]) * (1.0 + sc1) + sh1).astype(bf16)

    def proj(base, b_ref, lo, n):
        return _dot_nt(h, w_ref[base + lo:base + lo + n, :]) + b_ref[:, lo:lo + n]

    q_ref[0] = (proj(0, ba_ref, 0, D_A) * Q_SCALE).astype(bf16)
    k_ref[0] = proj(0, ba_ref, D_A, D_A).astype(bf16)
    v_ref[0] = proj(0, ba_ref, 2 * D_A, D_A).astype(bf16)
    so_ref[0] = _sigmoid(proj(0, ba_ref, 3 * D_A, D_A)).astype(bf16)
    glu_ref[0] = proj(o_b, bb_ref, 0, D_B) * _sigmoid(proj(o_b, bb_ref, D_B, D_B))
    sga_ref[0] = _sigmoid(proj(o_b, bb_ref, 2 * D_B, D_MODEL)).astype(bf16)
    sgb_ref[0] = _sigmoid(proj(o_b, bb_ref, 2 * D_B + D_MODEL, D_MODEL)).astype(bf16)
    gates_ref[0] = _dot_nt(h, wg_ref[...]) + bg_ref[...]


def _in_proj(x, mod3, per_batch_mod, norm_g, wts, tm, o_a, o_b, cast_weights=()):
    B0, T0, _ = x.shape
    if tm > T0:
        assert not per_batch_mod and tm % T0 == 0 and B0 % (tm // T0) == 0
        x = x.reshape(B0 * T0 // tm, tm, D_MODEL)
    B, T, _ = x.shape
    nt = T // tm
    tok = lambda n: pl.BlockSpec((1, tm, n), lambda b, t: (b, t, 0))
    full = lambda a: pl.BlockSpec(a.shape, lambda b, t: (0,) * a.ndim)
    mod_map = (lambda b, t: (b + MOD_LAT_ROW, 0, 0)) if per_batch_mod else (lambda b, t: (MOD_CTX_ROW, 0, 0))
    sds = lambda n, dt: jax.ShapeDtypeStruct((B, T, n), dt)

    def walk(a):
        block = a.shape[0] // (B * nt)
        assert block * B * nt == a.shape[0] and block % 16 == 0
        return pl.BlockSpec((block, a.shape[1]), lambda b, t: (b * nt + t, 0))

    cast_specs = [walk(a) for a in cast_weights]
    outs = pl.pallas_call(
        functools.partial(_inproj_kernel, o_a=o_a, o_b=o_b, n_cast=len(cast_weights)),
        grid=(B, nt),
        in_specs=[tok(D_MODEL), pl.BlockSpec((1, 1, N_MOD * D_MODEL), mod_map), full(norm_g)]
        + [full(w) for w in wts] + cast_specs,
        out_specs=[tok(D_A), tok(D_A), tok(D_A), tok(D_A), tok(D_B), tok(D_MODEL), tok(D_MODEL), tok(GATE_PAD)]
        + cast_specs,
        out_shape=[sds(D_A, bf16), sds(D_A, bf16), sds(D_A, bf16), sds(D_A, bf16), sds(D_B, f32),
                   sds(D_MODEL, bf16), sds(D_MODEL, bf16), sds(GATE_PAD, f32)]
        + [jax.ShapeDtypeStruct(a.shape, bf16) for a in cast_weights],
        compiler_params=pltpu.CompilerParams(dimension_semantics=("arbitrary", "arbitrary"),
                                             vmem_limit_bytes=VMEM_LIMIT),
        name="in_proj",
    )(x, mod3, norm_g, *wts, *cast_weights)
    return [o.reshape(B0, T0, o.shape[-1]) for o in outs[:8]], outs[8:]


NEG_BIG = -1e30
ST_U, ST_INTER, ST_EMJ, ST_WKN, ST_DECAY = 0, 8, 16, 24, 32
ST_ROWS = 40


def _chunk_scan(x, op, fill, prefix, lane, width):
    k = 1
    while k < CHUNK:
        if prefix:
            shifted = jnp.where(lane >= k, pltpu.roll(x, k, axis=1), fill)
        else:
            shifted = jnp.where(lane < CHUNK - k, pltpu.roll(x, width - k, axis=1), fill)
        x = op(x, shifted)
        k *= 2
    return x


def _mlstm_kernel(*refs, nseq, nc, state_in, state_out):
    refs = list(refs)
    q_ref, k_ref, v_ref, g_ref = refs[:4]
    del refs[:4]
    if state_in:
        c0_ref, n0_ref, m0_ref = refs[:3]
        del refs[:3]
    hn_ref = refs.pop(0)
    if state_out:
        cn_ref, nn_ref, mn_ref = refs[:3]
        del refs[:3]
    st_ref, wt_ref, vt_ref, qt_ref, u_ref, cp_ref, npf_ref, npb_ref, cst_ref, nsf_ref, nsb_ref = refs
    L = CHUNK
    nct = nseq * nc
    T = nct * L
    NR = 2 * NH_A

    lane = jnp.bitwise_and(lax.broadcasted_iota(jnp.int32, (NR, T), 1), L - 1)
    is_fwd = lax.broadcasted_iota(jnp.int32, (NR, T), 0) < NH_A
    is_fwd_c = lax.broadcasted_iota(jnp.int32, (NR, L), 0) < NH_A
    i_parts, f_parts = [], []
    for c in range(nct):
        gt = g_ref[0, c * L:(c + 1) * L, :].T
        i_parts.append(gt[0:NR])
        f_parts.append(gt[NR:2 * NR])
    ig = jnp.concatenate(i_parts, axis=1)
    lf = _log_sigmoid(jnp.concatenate(f_parts, axis=1))
    scan = functools.partial(_chunk_scan, lane=lane, width=T)
    ps = scan(lf, jnp.add, 0.0, True)
    ss = scan(lf, jnp.add, 0.0, False)
    b = jnp.where(is_fwd, ps, ss)
    btot = ps + ss - lf
    w = ig - b
    cmw = jnp.where(is_fwd, scan(w, jnp.maximum, -jnp.inf, True), scan(w, jnp.maximum, -jnp.inf, False))
    wk = btot - b + ig
    a = jnp.maximum(scan(wk, jnp.maximum, -jnp.inf, True), scan(wk, jnp.maximum, -jnp.inf, False))
    chunk_lanes = lambda cg: slice(cg * L, (cg + 1) * L)
    pad_rows = jnp.zeros((L - NR, L), f32)
    for s in range(nseq):
        ms = [m0_ref[s] if state_in else jnp.zeros((NR, L), f32)]
        for t in range(nc):
            cf, cb = chunk_lanes(s * nc + t), chunk_lanes(s * nc + nc - 1 - t)
            bt = jnp.where(is_fwd_c, btot[:, cf], btot[:, cb])
            at = jnp.where(is_fwd_c, a[:, cf], a[:, cb])
            ms.append(jnp.maximum(bt + ms[-1], at))
        if state_out:
            mn_ref[s] = ms[nc]
        for c in range(nc):
            cg = s * nc + c
            sl = chunk_lanes(cg)
            m_prev = jnp.where(is_fwd_c, ms[c], ms[nc - 1 - c])
            m_new = jnp.where(is_fwd_c, ms[c + 1], ms[nc - c])
            mx = jnp.maximum(m_prev, cmw[:, sl])
            st_ref[cg, ST_U:ST_U + NR] = -mx
            st_ref[cg, ST_INTER:ST_INTER + NR] = jnp.exp(m_prev - mx)
            st_ref[cg, ST_EMJ:ST_EMJ + NR] = jnp.exp(-mx - b[:, sl])
            st_ref[cg, ST_WKN:ST_WKN + NR] = jnp.exp(wk[:, sl] - m_new)
            st_ref[cg, ST_DECAY:ST_DECAY + NR] = jnp.exp(btot[:, sl] + m_prev - m_new)
            wt_ref[cg] = jnp.concatenate([w[:, sl], pad_rows], axis=0).T

    row = lax.broadcasted_iota(jnp.int32, (L, L), 0)
    col = lax.broadcasted_iota(jnp.int32, (L, L), 1)
    masks = (row <= col, row >= col)
    first_of8 = lax.broadcasted_iota(jnp.int32, (8, L), 0) == 0

    def row_tile(x):
        return jnp.where(first_of8, x, 0.0)

    head_lanes = [slice(h * DH_A, (h + 1) * DH_A) for h in range(NH_A)]

    def increments(j, carry):
        rows = pl.ds(pl.multiple_of(j * L, L), L)
        for h in range(NH_A):
            kh = k_ref[0, rows, head_lanes[h]]
            vt = v_ref[0, rows, head_lanes[h]].astype(f32).T
            vt_ref[j, h] = vt.astype(bf16)
            qt_ref[j, h] = q_ref[0, rows, head_lanes[h]].astype(f32).T.astype(bf16)
            wf = st_ref[j, ST_WKN + h:ST_WKN + h + 1, :]
            wb = st_ref[j, ST_WKN + NH_A + h:ST_WKN + NH_A + h + 1, :]
            lhs = jnp.concatenate([vt * wf, vt * wb, row_tile(wf), row_tile(wb)], axis=0).astype(bf16)
            u_ref[j, h] = _dot(lhs, kh)
        return carry

    lax.fori_loop(0, nct, increments, 0, unroll=4)

    seq_heads = [(s, h) for s in range(nseq) for h in range(NH_A)]
    for s, h in seq_heads:
        i = s * NH_A + h
        if state_in:
            cst_ref[i, 0:DH_A] = c0_ref[s, 0, h].T
            cst_ref[i, DH_A:2 * DH_A] = c0_ref[s, 1, h].T
            nsf_ref[i] = row_tile(n0_ref[s, h:h + 1, :])
            nsb_ref[i] = row_tile(n0_ref[s, NH_A + h:NH_A + h + 1, :])
        else:
            cst_ref[i] = jnp.zeros((2 * DH_A, DH_A), f32)
            nsf_ref[i] = jnp.zeros((8, DH_A), f32)
            nsb_ref[i] = jnp.zeros((8, DH_A), f32)

    def recur(t, carry):
        for s, h in seq_heads:
            i = s * NH_A + h
            jf = s * nc + t
            jb = s * nc + nc - 1 - t
            dec_f = st_ref[jf, ST_DECAY + h:ST_DECAY + h + 1, :]
            dec_b = st_ref[jb, ST_DECAY + NH_A + h:ST_DECAY + NH_A + h + 1, :]
            c_f = cst_ref[i, 0:DH_A]
            c_b = cst_ref[i, DH_A:2 * DH_A]
            cp_ref[jf, h, 0:DH_A] = c_f.astype(bf16)
            cp_ref[jb, h, DH_A:2 * DH_A] = c_b.astype(bf16)
            cst_ref[i, 0:DH_A] = dec_f * c_f + u_ref[jf, h, 0:DH_A]
            cst_ref[i, DH_A:2 * DH_A] = dec_b * c_b + u_ref[jb, h, DH_A:2 * DH_A]
            n_f = nsf_ref[i]
            n_b = nsb_ref[i]
            npf_ref[jf, h] = n_f
            npb_ref[jb, h] = n_b
            nsf_ref[i] = dec_f * n_f + u_ref[jf, h, 2 * DH_A:2 * DH_A + 8]
            nsb_ref[i] = dec_b * n_b + u_ref[jb, h, 2 * DH_A + 8:2 * DH_A + 16]
        return carry

    lax.fori_loop(0, nc, recur, 0)
    for s, h in seq_heads if state_out else ():
        i = s * NH_A + h
        cn_ref[s, 0, h] = cst_ref[i, 0:DH_A].T
        cn_ref[s, 1, h] = cst_ref[i, DH_A:2 * DH_A].T
        nn_ref[s, h:h + 1, :] = nsf_ref[i, 0:1]
        nn_ref[s, NH_A + h:NH_A + h + 1, :] = nsb_ref[i, 0:1]

    def outputs(j, carry):
        rows = pl.ds(pl.multiple_of(j * L, L), L)
        wt = wt_ref[j]
        heads = range(NH_A)
        kqs = []
        for h in heads:
            kh = k_ref[0, rows, head_lanes[h]]
            n_rows = jnp.concatenate([npf_ref[j, h], npb_ref[j, h]], axis=0).astype(bf16)
            kqs.append(_dot(jnp.concatenate([kh, n_rows, cp_ref[j, h]], axis=0), qt_ref[j, h]))
        decays = [[jnp.exp(jnp.where(masks[d], wt[:, NH_A * d + h:NH_A * d + h + 1]
                                     + st_ref[j, ST_U + NH_A * d + h:ST_U + NH_A * d + h + 1, :], NEG_BIG))
                   for d in range(2)] for h in heads]
        s_sums, h_ts = [], []
        for h in heads:
            kq = kqs[h]
            s_sum = None
            h_t = None
            for d in range(2):
                r = NH_A * d + h
                inter = st_ref[j, ST_INTER + r:ST_INTER + r + 1, :]
                emj = st_ref[j, ST_EMJ + r:ST_EMJ + r + 1, :]
                s_t = kq[0:L] * decays[h][d]
                qn = kq[L + 8 * d:L + 8 * d + 1]
                den = inter * qn + jnp.sum(s_t, axis=0, keepdims=True)
                rr = 1.0 / jnp.maximum(jnp.abs(den), emj)
                s_sum = s_t * rr if d == 0 else s_sum + s_t * rr
                part = kq[L + 16 + d * DH_A:L + 16 + (d + 1) * DH_A] * (inter * rr)
                h_t = part if d == 0 else h_t + part
            s_sums.append(s_sum.astype(bf16))
            h_ts.append(h_t)
        h_ts = [h_ts[h] + _dot(vt_ref[j, h], s_sums[h]) for h in heads]
        h_ts = [x * lax.rsqrt(jnp.mean(x * x, axis=0, keepdims=True) + EPS) for x in h_ts]
        for h in heads:
            hn_ref[0, rows, head_lanes[h]] = h_ts[h].T.astype(bf16)
        return carry

    lax.fori_loop(0, nct, outputs, 0, unroll=4)


def _mlstm(q, k, v, gates, state, nseq, state_out):
    B, T, _ = q.shape
    nc = T // CHUNK
    nct = nseq * nc
    G = B // nseq
    fold = lambda a: a.reshape(G, nseq * T, a.shape[-1])
    seq = lambda n: pl.BlockSpec((1, nseq * T, n), lambda b: (b, 0, 0))
    st_c = pl.BlockSpec((nseq, 2, NH_A, DH_A, DH_A), lambda b: (b, 0, 0, 0, 0))
    st_v = pl.BlockSpec((nseq, 2 * NH_A, DH_A), lambda b: (b, 0, 0))
    state_specs = [st_c, st_v, st_v]
    state_shapes = [jax.ShapeDtypeStruct((B, 2, NH_A, DH_A, DH_A), f32),
                    jax.ShapeDtypeStruct((B, 2 * NH_A, DH_A), f32),
                    jax.ShapeDtypeStruct((B, 2 * NH_A, DH_A), f32)]
    outs = pl.pallas_call(
        functools.partial(_mlstm_kernel, nseq=nseq, nc=nc, state_in=state is not None, state_out=state_out),
        grid=(G,),
        in_specs=[seq(D_A), seq(D_A), seq(D_A), seq(GATE_PAD)] + (state_specs if state is not None else []),
        out_specs=[seq(D_A)] + (state_specs if state_out else []),
        scratch_shapes=[pltpu.VMEM((nct, ST_ROWS, CHUNK), f32), pltpu.VMEM((nct, CHUNK, GATE_PAD), f32),
                        pltpu.VMEM((nct, NH_A, DH_A, CHUNK), bf16), pltpu.VMEM((nct, NH_A, DH_A, CHUNK), bf16),
                        pltpu.VMEM((nct, NH_A, 2 * DH_A + 16, DH_A), f32),
                        pltpu.VMEM((nct, NH_A, 2 * DH_A, DH_A), bf16),
                        pltpu.VMEM((nct, NH_A, 8, DH_A), f32), pltpu.VMEM((nct, NH_A, 8, DH_A), f32),
                        pltpu.VMEM((nseq * NH_A, 2 * DH_A, DH_A), f32),
                        pltpu.VMEM((nseq * NH_A, 8, DH_A), f32), pltpu.VMEM((nseq * NH_A, 8, DH_A), f32)],
        out_shape=[jax.ShapeDtypeStruct((G, nseq * T, D_A), bf16)] + (state_shapes if state_out else []),
        compiler_params=pltpu.CompilerParams(dimension_semantics=("arbitrary",), vmem_limit_bytes=VMEM_LIMIT),
        name="mlstm",
    )(fold(q), fold(k), fold(v), fold(gates), *(state if state is not None else ()))
    return (outs[0].reshape(B, T, D_A),) + tuple(outs[1:])


def _mix_kernel(x_ref, mod_ref, hn_ref, so_ref, glu_ref, glu_prev_ref, glu_next_ref, sga_ref, sgb_ref,
                ng_ref, wpa_ref, cw_ref, cb_ref, lng_ref, lnb_ref, wpb_ref, wout_ref,
                x1_ref, ext_ref, zs_ref, *, tm, nt, nsub):
    t = pl.program_id(1)
    mod = mod_ref[0]
    g1 = mod[:, 2 * D_MODEL:3 * D_MODEL]

    a_in = (so_ref[0].astype(f32) * (hn_ref[0].astype(f32) * ng_ref[...])).astype(bf16)
    branch_a = _dot(a_in, wpa_ref[...])

    H = CONV_HALO
    SUB, LANE = SUBLANES, LANES
    tsub = tm // nsub
    te = tsub + 2 * H
    for s in range(nsub):
        inner = nsub == 1
        ext_ref[0, s, 0:H, :] = jnp.where(t > 0, glu_prev_ref[0], 0.0) if inner else jnp.zeros((H, D_B), f32)
        ext_ref[0, s, H:H + tsub, :] = glu_ref[0, s * tsub:(s + 1) * tsub, :]
        ext_ref[0, s, H + tsub:te, :] = jnp.where(t < nt - 1, glu_next_ref[0], 0.0) if inner else jnp.zeros((H, D_B), f32)
    for s in range(nsub):
        for c in range(D_B // LANE):
            lanes = slice(c * LANE, (c + 1) * LANE)
            base = ext_ref[0, s, :, lanes]
            for k in range(1, SUB):
                ext_ref[k, s, :, lanes] = pltpu.roll(base, te - k, axis=0)
    off = H - CONV_W // 2
    for s in range(nsub):
        for r0 in range(0, tsub, CONV_ROWS):
            parts = []
            for c in range(D_B // LANE):
                lanes = slice(c * LANE, (c + 1) * LANE)
                acc = jnp.broadcast_to(cb_ref[:, lanes], (CONV_ROWS, LANE))
                for w in range(CONV_W):
                    k, a = (off + w) % SUB, (off + w) // SUB * SUB
                    acc = acc + ext_ref[k, s, r0 + a:r0 + a + CONV_ROWS, lanes] * cw_ref[w:w + 1, lanes]
                parts.append(acc)
            z = jnp.concatenate(parts, axis=1)
            mu = jnp.mean(z, axis=-1, keepdims=True)
            zc = z - mu
            var = jnp.mean(zc * zc, axis=-1, keepdims=True)
            zn = zc * lax.rsqrt(var + EPS) * lng_ref[...] + lnb_ref[...]
            zs_ref[s * tsub + r0:s * tsub + r0 + CONV_ROWS, :] = (zn * _sigmoid(zn)).astype(bf16)
    branch_b = _dot(zs_ref[...], wpb_ref[...])

    merged = sga_ref[0].astype(f32) * branch_a + sgb_ref[0].astype(f32) * branch_b
    x1_ref[0] = x_ref[0] + g1 * _dot(merged.astype(bf16), wout_ref[...])


def _mix_out(x, mod3, per_batch_mod, hn, so, glu, sga, sgb, wts, tm):
    B0, T0, _ = x.shape
    nsub = 1
    if tm > T0:
        assert not per_batch_mod and tm % T0 == 0 and B0 % (tm // T0) == 0
        nsub = tm // T0
        x, hn, so, glu, sga, sgb = (a.reshape(B0 // nsub, tm, a.shape[-1]) for a in (x, hn, so, glu, sga, sgb))
    B, T, _ = x.shape
    nt = T // tm
    H = CONV_HALO
    r = tm // H
    tok = lambda n: pl.BlockSpec((1, tm, n), lambda b, t: (b, t, 0))
    full = lambda a: pl.BlockSpec(a.shape, lambda b, t: (0,) * a.ndim)
    mod_map = (lambda b, t: (b + MOD_LAT_ROW, 0, 0)) if per_batch_mod else (lambda b, t: (MOD_CTX_ROW, 0, 0))
    prev = pl.BlockSpec((1, H, D_B), lambda b, t: (b, jnp.maximum(t * r - 1, 0), 0))
    nxt = pl.BlockSpec((1, H, D_B), lambda b, t: (b, jnp.minimum((t + 1) * r, T // H - 1), 0))
    x1 = pl.pallas_call(
        functools.partial(_mix_kernel, tm=tm, nt=nt, nsub=nsub),
        grid=(B, nt),
        in_specs=[tok(D_MODEL), pl.BlockSpec((1, 1, N_MOD * D_MODEL), mod_map), tok(D_A), tok(D_A),
                  tok(D_B), prev, nxt, tok(D_MODEL), tok(D_MODEL)] + [full(w) for w in wts],
        out_specs=tok(D_MODEL),
        out_shape=jax.ShapeDtypeStruct((B, T, D_MODEL), f32),
        scratch_shapes=[pltpu.VMEM((SUBLANES, nsub, tm // nsub + 2 * H, D_B), f32), pltpu.VMEM((tm, D_B), bf16)],
        compiler_params=pltpu.CompilerParams(dimension_semantics=("arbitrary", "arbitrary"),
                                             vmem_limit_bytes=VMEM_LIMIT),
        name="mix_out",
    )(x, mod3, hn, so, glu, glu, glu, sga, sgb, *wts)
    return x1.reshape(B0, T0, D_MODEL)


def _ffn_kernel(*refs, tm, nt, on_grid, seq_len):
    if on_grid:
        (x_ref, xp_ref, xn_ref, mod_ref, g2n_ref, wu_ref, wd_hbm, cw_ref, cb_ref, fg_ref,
         y_ref, h_ref, g0_ref, g1_ref, v0_ref, v1_ref, gl_ref, gr_ref, a0_ref, a1_ref, a2_ref, a3_ref, acc_ref,
         wd_ref, wd_sem) = refs
    else:
        (x_ref, mod_ref, g2n_ref, wu_ref, wd_hbm, cw_ref, cb_ref, fg_ref,
         y_ref, h_ref, g0_ref, g1_ref, v0_ref, v1_ref, gl_ref, gr_ref, a0_ref, a1_ref, a2_ref, a3_ref, acc_ref,
         wd_ref, wd_sem) = refs
    gbufs, vbufs, acts = (g0_ref, g1_ref), (v0_ref, v1_ref), (a0_ref, a1_ref, a2_ref, a3_ref)
    t = pl.program_id(1)
    first_step = jnp.logical_and(pl.program_id(0) == 0, t == 0)
    wd_copy = pltpu.make_async_copy(wd_hbm, wd_ref, wd_sem)

    @pl.when(first_step)
    def _():
        wd_copy.start()

    halo = GRID_W if on_grid else 0
    te = tm + 2 * halo
    mod = mod_ref[0]
    sh2 = mod[:, 3 * D_MODEL:4 * D_MODEL]
    sc2 = mod[:, 4 * D_MODEL:5 * D_MODEL]
    g2 = mod[:, 5 * D_MODEL:6 * D_MODEL]

    def norm_mod(x):
        return _rms(x, g2n_ref[...]) * (1.0 + sc2) + sh2

    h_ref[halo:halo + tm, :] = norm_mod(x_ref[0]).astype(bf16)
    if on_grid:
        h_ref[0:halo, :] = jnp.where(t > 0, norm_mod(xp_ref[0]), 0.0).astype(bf16)
        h_ref[halo + tm:te, :] = jnp.where(t < nt - 1, norm_mod(xn_ref[0]), 0.0).astype(bf16)

    SUB, LANE = SUBLANES, LANES
    seg = GRID_W if on_grid else seq_len
    sub_row = lax.broadcasted_iota(jnp.int32, (SUB, LANE), 0)
    first_row = sub_row == 0
    last_row = sub_row == SUB - 1
    row_taps = (0, 1, 2) if on_grid else (1,)

    def ff_cols(fc, base):
        if isinstance(fc, int):
            return slice(base + fc * FF_CHUNK, base + (fc + 1) * FF_CHUNK)
        return pl.ds(pl.multiple_of(base + fc * FF_CHUNK, 128), FF_CHUNK)

    def up(fc, slot):
        gbufs[slot][...] = _dot(h_ref[...], wu_ref[:, ff_cols(fc, 0)])
        vbufs[slot][...] = _dot(h_ref[halo:halo + tm, :], wu_ref[:, ff_cols(fc, D_FF)])

    def gate_act(fc, slot, aslot):
        cw = cw_ref[:, ff_cols(fc, 0)]
        cb = cb_ref[:, ff_cols(fc, 0)]
        for c in range(FF_CHUNK // LANE):
            lanes = slice(c * LANE, (c + 1) * LANE)
            for s0 in range(0, te, seg):
                gate = gbufs[slot][s0:s0 + seg, lanes]
                g_l = pltpu.roll(gate, 1, axis=0)
                g_r = pltpu.roll(gate, seg - 1, axis=0)
                gl_ref[s0:s0 + SUB, lanes] = jnp.where(first_row, 0.0, g_l[0:SUB])
                gl_ref[s0 + SUB:s0 + seg, lanes] = g_l[SUB:seg]
                gr_ref[s0:s0 + seg - SUB, lanes] = g_r[0:seg - SUB]
                gr_ref[s0 + seg - SUB:s0 + seg, lanes] = jnp.where(last_row, 0.0, g_r[seg - SUB:seg])
            for r0 in range(0, tm, GRID_W):
                conv = jnp.broadcast_to(cb[:, lanes], (GRID_W, LANE))
                for kh in row_taps:
                    lo = r0 + kh * GRID_W if on_grid else r0
                    conv = conv + gl_ref[lo:lo + GRID_W, lanes] * cw[3 * kh:3 * kh + 1, lanes]
                    conv = conv + gbufs[slot][lo:lo + GRID_W, lanes] * cw[3 * kh + 1:3 * kh + 2, lanes]
                    conv = conv + gr_ref[lo:lo + GRID_W, lanes] * cw[3 * kh + 2:3 * kh + 3, lanes]
                gelu = conv * (0.5 + 0.5 * jnp.tanh(conv * (GELU_C0 + GELU_C1 * (conv * conv))))
                acts[aslot][r0:r0 + GRID_W, lanes] = (gelu * vbufs[slot][r0:r0 + GRID_W, lanes]).astype(bf16)

    def down(fc, aslot):
        return _dot(acts[aslot][...], wd_ref[fc])


    def pair(p, parity, with_down):
        fc = 2 * p
        wr, rd = 2 * parity, 2 * (1 - parity)
        if with_down:
            acc_ref[...] += down(fc - 2, rd) + down(fc - 1, rd + 1)
        up(fc + 1, 1)
        gate_act(fc, 0, wr)
        up(fc + 2, 0)
        gate_act(fc + 1, 1, wr + 1)

    n_pairs = (N_FF_CHUNKS - 1) // 2
    assert N_FF_CHUNKS == 2 * n_pairs + 1 and n_pairs % 2 == 1
    up(0, 0)
    acc_ref[...] = jnp.zeros_like(acc_ref)
    pair(0, 0, False)

    @pl.when(first_step)
    def _():
        wd_copy.wait()

    def two_pairs(i, carry):
        p = 2 * i + 1
        pair(p, 1, True)
        pair(p + 1, 0, True)
        return carry

    lax.fori_loop(0, (n_pairs - 1) // 2, two_pairs, 0)
    last = N_FF_CHUNKS - 1
    gate_act(last, 0, 2)
    ffn_out = acc_ref[...] + (down(last - 2, 0) + down(last - 1, 1) + down(last, 2))
    y_ref[0] = _rms(x_ref[0] + g2 * ffn_out, fg_ref[...])


FFN_W_DOWN = 2


def _ffn(x1, mod3, per_batch_mod, wts, tm, on_grid):
    B0, T0, _ = x1.shape
    if tm > T0:
        assert not per_batch_mod and not on_grid and tm % T0 == 0 and B0 % (tm // T0) == 0
        x1 = x1.reshape(B0 * T0 // tm, tm, D_MODEL)
    B, T, _ = x1.shape
    nt = T // tm
    halo = GRID_W if on_grid else 0
    te = tm + 2 * halo
    tok = pl.BlockSpec((1, tm, D_MODEL), lambda b, t: (b, t, 0))
    full = lambda a: pl.BlockSpec(a.shape, lambda b, t: (0,) * a.ndim)
    mod_map = (lambda b, t: (b + MOD_LAT_ROW, 0, 0)) if per_batch_mod else (lambda b, t: (MOD_CTX_ROW, 0, 0))
    in_specs = [tok]
    args = [x1]
    if on_grid:
        r = tm // GRID_W
        in_specs += [pl.BlockSpec((1, GRID_W, D_MODEL), lambda b, t: (b, jnp.maximum(t * r - 1, 0), 0)),
                     pl.BlockSpec((1, GRID_W, D_MODEL), lambda b, t: (b, jnp.minimum((t + 1) * r, T // GRID_W - 1), 0))]
        args += [x1, x1]
    in_specs += [pl.BlockSpec((1, 1, N_MOD * D_MODEL), mod_map)] + [
        pl.BlockSpec(memory_space=pl.ANY) if i == FFN_W_DOWN else full(w) for i, w in enumerate(wts)]
    args += [mod3] + list(wts)
    y = pl.pallas_call(
        functools.partial(_ffn_kernel, tm=tm, nt=nt, on_grid=on_grid, seq_len=T0),
        grid=(B, nt),
        in_specs=in_specs,
        out_specs=tok,
        out_shape=jax.ShapeDtypeStruct((B, T, D_MODEL), f32),
        scratch_shapes=[pltpu.VMEM((te, D_MODEL), bf16)]
        + [pltpu.VMEM((te, FF_CHUNK), f32)] * 2 + [pltpu.VMEM((tm, FF_CHUNK), f32)] * 2
        + [pltpu.VMEM((te, FF_CHUNK), f32)] * 2 + [pltpu.VMEM((tm, FF_CHUNK), bf16)] * 4
        + [pltpu.VMEM((tm, D_MODEL), f32)]
        + [pltpu.VMEM(wts[FFN_W_DOWN].shape, bf16), pltpu.SemaphoreType.DMA(())],
        compiler_params=pltpu.CompilerParams(dimension_semantics=("arbitrary", "arbitrary"),
                                             vmem_limit_bytes=VMEM_LIMIT),
        name="ffn_grid" if on_grid else "ffn_ctx",
    )(*args)
    return y.reshape(B0, T0, D_MODEL)


TOKEN_TILE = 512
PROJ_TILE = 1024
MLSTM_TOKENS = 1024


def _tiles(T, per_batch_mod):
    span = TOKEN_TILE if not per_batch_mod else min(TOKEN_TILE, T)
    proj = PROJ_TILE if not per_batch_mod else min(PROJ_TILE, T)
    return proj, span, span, max(1, MLSTM_TOKENS // T)


def _trunk(x, mod3, per_batch_mod, state, on_grid, w, cast_weights=()):
    tm_proj, tm_mix, tm_ffn, mlstm_nseq = _tiles(x.shape[1], per_batch_mod)
    (q, k, v, so, glu, sga, sgb, gates), casts = _in_proj(x, mod3, per_batch_mod, w["norm1_g"], w["in"], tm_proj,
                                                          *w["in_groups"], cast_weights=cast_weights)
    if cast_weights:
        w = dict(w, mix=w["mix"](casts), ffn=w["ffn"](casts))
    hn, *new_state = _mlstm(q, k, v, gates, state, mlstm_nseq, state is None)
    x1 = _mix_out(x, mod3, per_batch_mod, hn, so, glu, sga, sgb, w["mix"], tm_mix)
    y = _ffn(x1, mod3, per_batch_mod, w["ffn"], tm_ffn, on_grid)
    return y, new_state, w


def kernel(x_prompt, x_sample, c, state_C, state_n, state_m, c_ctx, w_ada, b_ada, norm1_g, w_in, b_in,
           mlstm_norm_g, w_proj_a, conv_dw_w, conv_dw_b, conv_ln_g, conv_ln_b, w_proj_b, w_out, norm2_g,
           w_up, ffn_dw_w, ffn_dw_b, w_down, final_norm_g):
    Bp = x_prompt.shape[0]
    Bl = x_sample.shape[0]
    l = 0
    row = lambda a: a.reshape(1, -1)

    ct = jnp.concatenate([c_ctx[None, :], c, jnp.zeros((8 - Bl - 1, D_MODEL), f32)], axis=0).T
    mod = _adaln_mod(ct, w_ada[l], row(b_ada[l]), Bl + 1)

    o_g = 4 * D_A
    o_u = o_g + 4 * NH_A
    nh = NH_A

    def gate_cols(a):
        g = a[..., o_g:o_u]
        pad = jnp.zeros(a.shape[:-1] + (GATE_PAD - 4 * nh,), a.dtype)
        return jnp.concatenate([g[..., 0:nh], g[..., 2 * nh:3 * nh], g[..., nh:2 * nh], g[..., 3 * nh:], pad], axis=-1)

    w_in_t = jnp.swapaxes(w_in[l], 0, 1).astype(bf16)
    bi = row(b_in[l])
    g_t = w_in_t[o_g:o_u]
    w_gates_t = jnp.concatenate([g_t[0:nh], g_t[2 * nh:3 * nh], g_t[nh:2 * nh], g_t[3 * nh:],
                                 jnp.zeros((GATE_PAD - 4 * nh, D_MODEL), bf16)], axis=0)
    w_in_parts = (w_in_t, bi[:, :o_g], bi[:, o_u:], w_gates_t, gate_cols(bi))
    cast_weights = [w_up[l], w_down[l], w_out[l], w_proj_a[l], w_proj_b[l]]

    def w_mix(casts):
        return (row(mlstm_norm_g[l]), casts[3],
                jnp.pad(conv_dw_w[l], ((0, 1), (0, 0))), row(conv_dw_b[l]), row(conv_ln_g[l]), row(conv_ln_b[l]),
                casts[4], casts[2])

    def w_ffn(casts):
        return (row(norm2_g[l]), casts[0], casts[1].reshape(N_FF_CHUNKS, FF_CHUNK, D_MODEL),
                ffn_dw_w[l].reshape(9, D_FF), row(ffn_dw_b[l]), row(final_norm_g))

    w = {"norm1_g": row(norm1_g[l]), "in": w_in_parts, "in_groups": (o_g, o_u), "mix": w_mix, "ffn": w_ffn}

    c0 = state_C[:, l]
    n0 = state_n[:, l].reshape(Bl, 2 * NH_A, DH_A)
    m0 = jnp.broadcast_to(state_m[:, l].reshape(Bl, 2 * NH_A, 1), (Bl, 2 * NH_A, DH_A))
    y_sample, _, w = _trunk(x_sample, mod, True, (c0, n0, m0), True, w, cast_weights)

    y_prompt, (cn, nn, mn), _ = _trunk(x_prompt, mod, False, None, False, w)

    new_state_C = cn[:, None]
    new_state_n = nn.reshape(Bp, 1, 2, NH_A, DH_A)
    new_state_m = mn[:, :, 0].reshape(Bp, 1, 2, NH_A)
    return (y_prompt, y_sample, new_state_C, new_state_n, new_state_m)
```

```python
import functools

import jax
import jax.numpy as jnp
from jax import lax
from jax.experimental import pallas as pl
from jax.experimental.pallas import tpu as pltpu

D_MODEL = 1024
D_A = 512
NH_A = 4
DH_A = 128
CHUNK = 128
D_B = 512
CONV_W = 31
CONV_HALO = 16
CONV_ROWS = 64
D_FF = 2816
FF_CHUNK = 256
N_FF_CHUNKS = D_FF // FF_CHUNK
GRID_W = 64
N_MOD = 6
GATE_PAD = 128
EPS = 1e-6
Q_SCALE = DH_A ** -0.5
VMEM_LIMIT = 60 * 1024 * 1024
SUBLANES = 8
LANES = 128

f32 = jnp.float32
bf16 = jnp.bfloat16


def _rms(x, g):
    return x * lax.rsqrt(jnp.mean(x * x, axis=-1, keepdims=True) + EPS) * g


def _sigmoid(x):
    return 1.0 / (1.0 + jnp.exp(-x))


def _log_sigmoid(x):
    return jnp.minimum(x, 0.0) - jnp.log(1.0 + jnp.exp(-jnp.abs(x)))


GELU_C0 = 0.7978845608028654
GELU_C1 = GELU_C0 * 0.044715


def _dot(a, b):
    return jnp.dot(a, b, preferred_element_type=f32)


def _dot_nt(a, b):
    return lax.dot_general(a, b, (((1,), (1,)), ((), ())), preferred_element_type=f32)


MOD_CTX_ROW, MOD_LAT_ROW = 0, 1


def _mod_kernel(ct_ref, w_ref, b_ref, o_ref, *, n_rows):
    ct = ct_ref[...]
    st = ct * _sigmoid(ct)
    w = w_ref[...]
    bias = b_ref[...]
    for r in range(n_rows):
        s_col = jnp.broadcast_to(st[:, r:r + 1], (D_MODEL, LANES))
        o_ref[r] = bias + jnp.concatenate(
            [jnp.sum(w[:, c:c + LANES] * s_col, axis=0, keepdims=True) for c in range(0, w.shape[1], LANES)], axis=1)
    for r in range(n_rows, 8):
        o_ref[r] = jnp.zeros_like(bias)


def _adaln_mod(ct, w_ada, b_ada, n_rows):
    n = w_ada.shape[1]
    tn = 1024
    return pl.pallas_call(
        functools.partial(_mod_kernel, n_rows=n_rows),
        grid=(n // tn,),
        in_specs=[
            pl.BlockSpec((D_MODEL, 8), lambda j: (0, 0)),
            pl.BlockSpec((D_MODEL, tn), lambda j: (0, j)),
            pl.BlockSpec((1, tn), lambda j: (0, j)),
        ],
        out_specs=pl.BlockSpec((8, 1, tn), lambda j: (0, 0, j)),
        out_shape=jax.ShapeDtypeStruct((8, 1, n), f32),
        compiler_params=pltpu.CompilerParams(dimension_semantics=("arbitrary",), vmem_limit_bytes=VMEM_LIMIT),
        name="adaln_mod",
    )(ct, w_ada, b_ada)


def _inproj_kernel(*refs, o_a, o_b, n_cast):
    x_ref, mod_ref, g_ref, w_ref, ba_ref, bb_ref, wg_ref, bg_ref = refs[:8]
    cast_src = refs[8:8 + n_cast]
    q_ref, k_ref, v_ref, so_ref, glu_ref, sga_ref, sgb_ref, gates_ref = refs[8 + n_cast:16 + n_cast]
    cast_dst = refs[16 + n_cast:16 + 2 * n_cast]
    for src, dst in zip(cast_src, cast_dst):
        dst[...] = src[...].astype(bf16)

    x = x_ref[0]
    mod = mod_ref[0]
    sh1 = mod[:, 0:D_MODEL]
    sc1 = mod[:, D_MODEL:2 * D_MODEL]
    h = (_rms(x, g_ref[...]) * (1.0 + sc1) + sh1).astype(bf16)

    def proj(base, b_ref, lo, n):
        return _dot_nt(h, w_ref[base + lo:base + lo + n, :]) + b_ref[:, lo:lo + n]

    q_ref[0] = (proj(0, ba_ref, 0, D_A) * Q_SCALE).astype(bf16)
    k_ref[0] = proj(0, ba_ref, D_A, D_A).astype(bf16)
    v_ref[0] = proj(0, ba_ref, 2 * D_A, D_A).astype(bf16)
    so_ref[0] = _sigmoid(proj(0, ba_ref, 3 * D_A, D_A)).astype(bf16)
    glu_ref[0] = proj(o_b, bb_ref, 0, D_B) * _sigmoid(proj(o_b, bb_ref, D_B, D_B))
    sga_ref[0] = _sigmoid(proj(o_b, bb_ref, 2 * D_B, D_MODEL)).astype(bf16)
    sgb_ref[0] = _sigmoid(proj(o_b, bb_ref, 2 * D_B + D_MODEL, D_MODEL)).astype(bf16)
    gates_ref[0] = _dot_nt(h, wg_ref[...]) + bg_ref[...]


def _in_proj(x, mod3, per_batch_mod, norm_g, wts, tm, o_a, o_b, cast_weights=()):
    B0, T0, _ = x.shape
    if tm > T0:
        assert not per_batch_mod and tm % T0 == 0 and B0 % (tm // T0) == 0
        x = x.reshape(B0 * T0 // tm, tm, D_MODEL)
    B, T, _ = x.shape
    nt = T // tm
    tok = lambda n: pl.BlockSpec((1, tm, n), lambda b, t: (b, t, 0))
    full = lambda a: pl.BlockSpec(a.shape, lambda b, t: (0,) * a.ndim)
    mod_map = (lambda b, t: (b + MOD_LAT_ROW, 0, 0)) if per_batch_mod else (lambda b, t: (MOD_CTX_ROW, 0, 0))
    sds = lambda n, dt: jax.ShapeDtypeStruct((B, T, n), dt)

    def walk(a):
        block = a.shape[0] // (B * nt)
        assert block * B * nt == a.shape[0] and block % 16 == 0
        return pl.BlockSpec((block, a.shape[1]), lambda b, t: (b * nt + t, 0))

    cast_specs = [walk(a) for a in cast_weights]
    outs = pl.pallas_call(
        functools.partial(_inproj_kernel, o_a=o_a, o_b=o_b, n_cast=len(cast_weights)),
        grid=(B, nt),
        in_specs=[tok(D_MODEL), pl.BlockSpec((1, 1, N_MOD * D_MODEL), mod_map), full(norm_g)]
        + [full(w) for w in wts] + cast_specs,
        out_specs=[tok(D_A), tok(D_A), tok(D_A), tok(D_A), tok(D_B), tok(D_MODEL), tok(D_MODEL), tok(GATE_PAD)]
        + cast_specs,
        out_shape=[sds(D_A, bf16), sds(D_A, bf16), sds(D_A, bf16), sds(D_A, bf16), sds(D_B, f32),
                   sds(D_MODEL, bf16), sds(D_MODEL, bf16), sds(GATE_PAD, f32)]
        + [jax.ShapeDtypeStruct(a.shape, bf16) for a in cast_weights],
        compiler_params=pltpu.CompilerParams(dimension_semantics=("arbitrary", "arbitrary"),
                                             vmem_limit_bytes=VMEM_LIMIT),
        name="in_proj",
    )(x, mod3, norm_g, *wts, *cast_weights)
    return [o.reshape(B0, T0, o.shape[-1]) for o in outs[:8]], outs[8:]


NEG_BIG = -1e30
ST_U, ST_INTER, ST_EMJ, ST_WKN, ST_DECAY = 0, 8, 16, 24, 32
ST_ROWS = 40


def _chunk_scan(x, op, fill, prefix, lane, width):
    k = 1
    while k < CHUNK:
        if prefix:
            shifted = jnp.where(lane >= k, pltpu.roll(x, k, axis=1), fill)
        else:
            shifted = jnp.where(lane < CHUNK - k, pltpu.roll(x, width - k, axis=1), fill)
        x = op(x, shifted)
        k *= 2
    return x


def _mlstm_kernel(*refs, nseq, nc, state_in, state_out):
    refs = list(refs)
    q_ref, k_ref, v_ref, g_ref = refs[:4]
    del refs[:4]
    if state_in:
        c0_ref, n0_ref, m0_ref = refs[:3]
        del refs[:3]
    hn_ref = refs.pop(0)
    if state_out:
        cn_ref, nn_ref, mn_ref = refs[:3]
        del refs[:3]
    st_ref, wt_ref, vt_ref, qt_ref, u_ref, cp_ref, npf_ref, npb_ref, cst_ref, nsf_ref, nsb_ref = refs
    L = CHUNK
    nct = nseq * nc
    T = nct * L
    NR = 2 * NH_A

    lane = jnp.bitwise_and(lax.broadcasted_iota(jnp.int32, (NR, T), 1), L - 1)
    is_fwd = lax.broadcasted_iota(jnp.int32, (NR, T), 0) < NH_A
    is_fwd_c = lax.broadcasted_iota(jnp.int32, (NR, L), 0) < NH_A
    i_parts, f_parts = [], []
    for c in range(nct):
        gt = g_ref[0, c * L:(c + 1) * L, :].T
        i_parts.append(gt[0:NR])
        f_parts.append(gt[NR:2 * NR])
    ig = jnp.concatenate(i_parts, axis=1)
    lf = _log_sigmoid(jnp.concatenate(f_parts, axis=1))
    scan = functools.partial(_chunk_scan, lane=lane, width=T)
    ps = scan(lf, jnp.add, 0.0, True)
    ss = scan(lf, jnp.add, 0.0, False)
    b = jnp.where(is_fwd, ps, ss)
    btot = ps + ss - lf
    w = ig - b
    cmw = jnp.where(is_fwd, scan(w, jnp.maximum, -jnp.inf, True), scan(w, jnp.maximum, -jnp.inf, False))
    wk = btot - b + ig
    a = jnp.maximum(scan(wk, jnp.maximum, -jnp.inf, True), scan(wk, jnp.maximum, -jnp.inf, False))
    chunk_lanes = lambda cg: slice(cg * L, (cg + 1) * L)
    pad_rows = jnp.zeros((L - NR, L), f32)
    for s in range(nseq):
        ms = [m0_ref[s] if state_in else jnp.zeros((NR, L), f32)]
        for t in range(nc):
            cf, cb = chunk_lanes(s * nc + t), chunk_lanes(s * nc + nc - 1 - t)
            bt = jnp.where(is_fwd_c, btot[:, cf], btot[:, cb])
            at = jnp.where(is_fwd_c, a[:, cf], a[:, cb])
            ms.append(jnp.maximum(bt + ms[-1], at))
        if state_out:
            mn_ref[s] = ms[nc]
        for c in range(nc):
            cg = s * nc + c
            sl = chunk_lanes(cg)
            m_prev = jnp.where(is_fwd_c, ms[c], ms[nc - 1 - c])
            m_new = jnp.where(is_fwd_c, ms[c + 1], ms[nc - c])
            mx = jnp.maximum(m_prev, cmw[:, sl])
            st_ref[cg, ST_U:ST_U + NR] = -mx
            st_ref[cg, ST_INTER:ST_INTER + NR] = jnp.exp(m_prev - mx)
            st_ref[cg, ST_EMJ:ST_EMJ + NR] = jnp.exp(-mx - b[:, sl])
            st_ref[cg, ST_WKN:ST_WKN + NR] = jnp.exp(wk[:, sl] - m_new)
            st_ref[cg, ST_DECAY:ST_DECAY + NR] = jnp.exp(btot[:, sl] + m_prev - m_new)
            wt_ref[cg] = jnp.concatenate([w[:, sl], pad_rows], axis=0).T

    row = lax.broadcasted_iota(jnp.int32, (L, L), 0)
    col = lax.broadcasted_iota(jnp.int32, (L, L), 1)
    masks = (row <= col, row >= col)
    first_of8 = lax.broadcasted_iota(jnp.int32, (8, L), 0) == 0

    def row_tile(x):
        return jnp.where(first_of8, x, 0.0)

    head_lanes = [slice(h * DH_A, (h + 1) * DH_A) for h in range(NH_A)]

    def increments(j, carry):
        rows = pl.ds(pl.multiple_of(j * L, L), L)
        for h in range(NH_A):
            kh = k_ref[0, rows, head_lanes[h]]
            vt = v_ref[0, rows, head_lanes[h]].astype(f32).T
            vt_ref[j, h] = vt.astype(bf16)
            qt_ref[j, h] = q_ref[0, rows, head_lanes[h]].astype(f32).T.astype(bf16)
            wf = st_ref[j, ST_WKN + h:ST_WKN + h + 1, :]
            wb = st_ref[j, ST_WKN + NH_A + h:ST_WKN + NH_A + h + 1, :]
            lhs = jnp.concatenate([vt * wf, vt * wb, row_tile(wf), row_tile(wb)], axis=0).astype(bf16)
            u_ref[j, h] = _dot(lhs, kh)
        return carry

    lax.fori_loop(0, nct, increments, 0, unroll=4)

    seq_heads = [(s, h) for s in range(nseq) for h in range(NH_A)]
    for s, h in seq_heads:
        i = s * NH_A + h
        if state_in:
            cst_ref[i, 0:DH_A] = c0_ref[s, 0, h].T
            cst_ref[i, DH_A:2 * DH_A] = c0_ref[s, 1, h].T
            nsf_ref[i] = row_tile(n0_ref[s, h:h + 1, :])
            nsb_ref[i] = row_tile(n0_ref[s, NH_A + h:NH_A + h + 1, :])
        else:
            cst_ref[i] = jnp.zeros((2 * DH_A, DH_A), f32)
            nsf_ref[i] = jnp.zeros((8, DH_A), f32)
            nsb_ref[i] = jnp.zeros((8, DH_A), f32)

    def recur(t, carry):
        for s, h in seq_heads:
            i = s * NH_A + h
            jf = s * nc + t
            jb = s * nc + nc - 1 - t
            dec_f = st_ref[jf, ST_DECAY + h:ST_DECAY + h + 1, :]
            dec_b = st_ref[jb, ST_DECAY + NH_A + h:ST_DECAY + NH_A + h + 1, :]
            c_f = cst_ref[i, 0:DH_A]
            c_b = cst_ref[i, DH_A:2 * DH_A]
            cp_ref[jf, h, 0:DH_A] = c_f.astype(bf16)
            cp_ref[jb, h, DH_A:2 * DH_A] = c_b.astype(bf16)
            cst_ref[i, 0:DH_A] = dec_f * c_f + u_ref[jf, h, 0:DH_A]
            cst_ref[i, DH_A:2 * DH_A] = dec_b * c_b + u_ref[jb, h, DH_A:2 * DH_A]
            n_f = nsf_ref[i]
            n_b = nsb_ref[i]
            npf_ref[jf, h] = n_f
            npb_ref[jb, h] = n_b
            nsf_ref[i] = dec_f * n_f + u_ref[jf, h, 2 * DH_A:2 * DH_A + 8]
            nsb_ref[i] = dec_b * n_b + u_ref[jb, h, 2 * DH_A + 8:2 * DH_A + 16]
        return carry

    lax.fori_loop(0, nc, recur, 0)
    for s, h in seq_heads if state_out else ():
        i = s * NH_A + h
        cn_ref[s, 0, h] = cst_ref[i, 0:DH_A].T
        cn_ref[s, 1, h] = cst_ref[i, DH_A:2 * DH_A].T
        nn_ref[s, h:h + 1, :] = nsf_ref[i, 0:1]
        nn_ref[s, NH_A + h:NH_A + h + 1, :] = nsb_ref[i, 0:1]

    def outputs(j, carry):
        rows = pl.ds(pl.multiple_of(j * L, L), L)
        wt = wt_ref[j]
        heads = range(NH_A)
        kqs = []
        for h in heads:
            kh = k_ref[0, rows, head_lanes[h]]
            n_rows = jnp.concatenate([npf_ref[j, h], npb_ref[j, h]], axis=0).astype(bf16)
            kqs.append(_dot(jnp.concatenate([kh, n_rows, cp_ref[j, h]], axis=0), qt_ref[j, h]))
        decays = [[jnp.exp(jnp.where(masks[d], wt[:, NH_A * d + h:NH_A * d + h + 1]
                                     + st_ref[j, ST_U + NH_A * d + h:ST_U + NH_A * d + h + 1, :], NEG_BIG))
                   for d in range(2)] for h in heads]
        s_sums, h_ts = [], []
        for h in heads:
            kq = kqs[h]
            s_sum = None
            h_t = None
            for d in range(2):
                r = NH_A * d + h
                inter = st_ref[j, ST_INTER + r:ST_INTER + r + 1, :]
                emj = st_ref[j, ST_EMJ + r:ST_EMJ + r + 1, :]
                s_t = kq[0:L] * decays[h][d]
                qn = kq[L + 8 * d:L + 8 * d + 1]
                den = inter * qn + jnp.sum(s_t, axis=0, keepdims=True)
                rr = 1.0 / jnp.maximum(jnp.abs(den), emj)
                s_sum = s_t * rr if d == 0 else s_sum + s_t * rr
                part = kq[L + 16 + d * DH_A:L + 16 + (d + 1) * DH_A] * (inter * rr)
                h_t = part if d == 0 else h_t + part
            s_sums.append(s_sum.astype(bf16))
            h_ts.append(h_t)
        h_ts = [h_ts[h] + _dot(vt_ref[j, h], s_sums[h]) for h in heads]
        h_ts = [x * lax.rsqrt(jnp.mean(x * x, axis=0, keepdims=True) + EPS) for x in h_ts]
        for h in heads:
            hn_ref[0, rows, head_lanes[h]] = h_ts[h].T.astype(bf16)
        return carry

    lax.fori_loop(0, nct, outputs, 0, unroll=4)


def _mlstm(q, k, v, gates, state, nseq, state_out):
    B, T, _ = q.shape
    nc = T // CHUNK
    nct = nseq * nc
    G = B // nseq
    fold = lambda a: a.reshape(G, nseq * T, a.shape[-1])
    seq = lambda n: pl.BlockSpec((1, nseq * T, n), lambda b: (b, 0, 0))
    st_c = pl.BlockSpec((nseq, 2, NH_A, DH_A, DH_A), lambda b: (b, 0, 0, 0, 0))
    st_v = pl.BlockSpec((nseq, 2 * NH_A, DH_A), lambda b: (b, 0, 0))
    state_specs = [st_c, st_v, st_v]
    state_shapes = [jax.ShapeDtypeStruct((B, 2, NH_A, DH_A, DH_A), f32),
                    jax.ShapeDtypeStruct((B, 2 * NH_A, DH_A), f32),
                    jax.ShapeDtypeStruct((B, 2 * NH_A, DH_A), f32)]
    outs = pl.pallas_call(
        functools.partial(_mlstm_kernel, nseq=nseq, nc=nc, state_in=state is not None, state_out=state_out),
        grid=(G,),
        in_specs=[seq(D_A), seq(D_A), seq(D_A), seq(GATE_PAD)] + (state_specs if state is not None else []),
        out_specs=[seq(D_A)] + (state_specs if state_out else []),
        scratch_shapes=[pltpu.VMEM((nct, ST_ROWS, CHUNK), f32), pltpu.VMEM((nct, CHUNK, GATE_PAD), f32),
                        pltpu.VMEM((nct, NH_A, DH_A, CHUNK), bf16), pltpu.VMEM((nct, NH_A, DH_A, CHUNK), bf16),
                        pltpu.VMEM((nct, NH_A, 2 * DH_A + 16, DH_A), f32),
                        pltpu.VMEM((nct, NH_A, 2 * DH_A, DH_A), bf16),
                        pltpu.VMEM((nct, NH_A, 8, DH_A), f32), pltpu.VMEM((nct, NH_A, 8, DH_A), f32),
                        pltpu.VMEM((nseq * NH_A, 2 * DH_A, DH_A), f32),
                        pltpu.VMEM((nseq * NH_A, 8, DH_A), f32), pltpu.VMEM((nseq * NH_A, 8, DH_A), f32)],
        out_shape=[jax.ShapeDtypeStruct((G, nseq * T, D_A), bf16)] + (state_shapes if state_out else []),
        compiler_params=pltpu.CompilerParams(dimension_semantics=("arbitrary",), vmem_limit_bytes=VMEM_LIMIT),
        name="mlstm",
    )(fold(q), fold(k), fold(v), fold(gates), *(state if state is not None else ()))
    return (outs[0].reshape(B, T, D_A),) + tuple(outs[1:])


def _mix_kernel(x_ref, mod_ref, hn_ref, so_ref, glu_ref, glu_prev_ref, glu_next_ref, sga_ref, sgb_ref,
                ng_ref, wpa_ref, cw_ref, cb_ref, lng_ref, lnb_ref, wpb_ref, wout_ref,
                x1_ref, ext_ref, zs_ref, *, tm, nt, nsub):
    t = pl.program_id(1)
    mod = mod_ref[0]
    g1 = mod[:, 2 * D_MODEL:3 * D_MODEL]

    a_in = (so_ref[0].astype(f32) * (hn_ref[0].astype(f32) * ng_ref[...])).astype(bf16)
    branch_a = _dot(a_in, wpa_ref[...])

    H = CONV_HALO
    SUB, LANE = SUBLANES, LANES
    tsub = tm // nsub
    te = tsub + 2 * H
    for s in range(nsub):
        inner = nsub == 1
        ext_ref[0, s, 0:H, :] = jnp.where(t > 0, glu_prev_ref[0], 0.0) if inner else jnp.zeros((H, D_B), f32)
        ext_ref[0, s, H:H + tsub, :] = glu_ref[0, s * tsub:(s + 1) * tsub, :]
        ext_ref[0, s, H + tsub:te, :] = jnp.where(t < nt - 1, glu_next_ref[0], 0.0) if inner else jnp.zeros((H, D_B), f32)
    for s in range(nsub):
        for c in range(D_B // LANE):
            lanes = slice(c * LANE, (c + 1) * LANE)
            base = ext_ref[0, s, :, lanes]
            for k in range(1, SUB):
                ext_ref[k, s, :, lanes] = pltpu.roll(base, te - k, axis=0)
    off = H - CONV_W // 2
    for s in range(nsub):
        for r0 in range(0, tsub, CONV_ROWS):
            parts = []
            for c in range(D_B // LANE):
                lanes = slice(c * LANE, (c + 1) * LANE)
                acc = jnp.broadcast_to(cb_ref[:, lanes], (CONV_ROWS, LANE))
                for w in range(CONV_W):
                    k, a = (off + w) % SUB, (off + w) // SUB * SUB
                    acc = acc + ext_ref[k, s, r0 + a:r0 + a + CONV_ROWS, lanes] * cw_ref[w:w + 1, lanes]
                parts.append(acc)
            z = jnp.concatenate(parts, axis=1)
            mu = jnp.mean(z, axis=-1, keepdims=True)
            zc = z - mu
            var = jnp.mean(zc * zc, axis=-1, keepdims=True)
            zn = zc * lax.rsqrt(var + EPS) * lng_ref[...] + lnb_ref[...]
            zs_ref[s * tsub + r0:s * tsub + r0 + CONV_ROWS, :] = (zn * _sigmoid(zn)).astype(bf16)
    branch_b = _dot(zs_ref[...], wpb_ref[...])

    merged = sga_ref[0].astype(f32) * branch_a + sgb_ref[0].astype(f32) * branch_b
    x1_ref[0] = x_ref[0] + g1 * _dot(merged.astype(bf16), wout_ref[...])


def _mix_out(x, mod3, per_batch_mod, hn, so, glu, sga, sgb, wts, tm):
    B0, T0, _ = x.shape
    nsub = 1
    if tm > T0:
        assert not per_batch_mod and tm % T0 == 0 and B0 % (tm // T0) == 0
        nsub = tm // T0
        x, hn, so, glu, sga, sgb = (a.reshape(B0 // nsub, tm, a.shape[-1]) for a in (x, hn, so, glu, sga, sgb))
    B, T, _ = x.shape
    nt = T // tm
    H = CONV_HALO
    r = tm // H
    tok = lambda n: pl.BlockSpec((1, tm, n), lambda b, t: (b, t, 0))
    full = lambda a: pl.BlockSpec(a.shape, lambda b, t: (0,) * a.ndim)
    mod_map = (lambda b, t: (b + MOD_LAT_ROW, 0, 0)) if per_batch_mod else (lambda b, t: (MOD_CTX_ROW, 0, 0))
    prev = pl.BlockSpec((1, H, D_B), lambda b, t: (b, jnp.maximum(t * r - 1, 0), 0))
    nxt = pl.BlockSpec((1, H, D_B), lambda b, t: (b, jnp.minimum((t + 1) * r, T // H - 1), 0))
    x1 = pl.pallas_call(
        functools.partial(_mix_kernel, tm=tm, nt=nt, nsub=nsub),
        grid=(B, nt),
        in_specs=[tok(D_MODEL), pl.BlockSpec((1, 1, N_MOD * D_MODEL), mod_map), tok(D_A), tok(D_A),
                  tok(D_B), prev, nxt, tok(D_MODEL), tok(D_MODEL)] + [full(w) for w in wts],
        out_specs=tok(D_MODEL),
        out_shape=jax.ShapeDtypeStruct((B, T, D_MODEL), f32),
        scratch_shapes=[pltpu.VMEM((SUBLANES, nsub, tm // nsub + 2 * H, D_B), f32), pltpu.VMEM((tm, D_B), bf16)],
        compiler_params=pltpu.CompilerParams(dimension_semantics=("arbitrary", "arbitrary"),
                                             vmem_limit_bytes=VMEM_LIMIT),
        name="mix_out",
    )(x, mod3, hn, so, glu, glu, glu, sga, sgb, *wts)
    return x1.reshape(B0, T0, D_MODEL)


def _ffn_kernel(*refs, tm, nt, on_grid, seq_len):
    if on_grid:
        (x_ref, xp_ref, xn_ref, mod_ref, g2n_ref, wu_hbm, wd_hbm, cw_ref, cb_ref, fg_ref,
         y_ref, h_ref, g0_ref, g1_ref, v0_ref, v1_ref, gl_ref, gr_ref, a0_ref, a1_ref, a2_ref, a3_ref, acc_ref,
         wd_ref, wd_sem, wu_ref, wu_sem) = refs
    else:
        (x_ref, mod_ref, g2n_ref, wu_hbm, wd_hbm, cw_ref, cb_ref, fg_ref,
         y_ref, h_ref, g0_ref, g1_ref, v0_ref, v1_ref, gl_ref, gr_ref, a0_ref, a1_ref, a2_ref, a3_ref, acc_ref,
         wd_ref, wd_sem, wu_ref, wu_sem) = refs
    gbufs, vbufs, acts = (g0_ref, g1_ref), (v0_ref, v1_ref), (a0_ref, a1_ref, a2_ref, a3_ref)
    t = pl.program_id(1)
    first_step = jnp.logical_and(pl.program_id(0) == 0, t == 0)
    wd_copy = pltpu.make_async_copy(wd_hbm, wd_ref, wd_sem)
    early = 3 * FF_CHUNK

    def wu_copies(lo, n, slot):
        return [pltpu.make_async_copy(wu_hbm.at[:, pl.ds(base + lo, n)], wu_ref.at[:, pl.ds(base + lo, n)],
                                      wu_sem.at[slot + i]) for i, base in enumerate((0, D_FF))]

    wu_early, wu_late = wu_copies(0, early, 0), wu_copies(early, D_FF - early, 2)

    @pl.when(first_step)
    def _():
        for cp in wu_early + wu_late:
            cp.start()
        wd_copy.start()
        for cp in wu_early:
            cp.wait()

    halo = GRID_W if on_grid else 0
    te = tm + 2 * halo
    mod = mod_ref[0]
    sh2 = mod[:, 3 * D_MODEL:4 * D_MODEL]
    sc2 = mod[:, 4 * D_MODEL:5 * D_MODEL]
    g2 = mod[:, 5 * D_MODEL:6 * D_MODEL]

    def norm_mod(x):
        return _rms(x, g2n_ref[...]) * (1.0 + sc2) + sh2

    h_ref[halo:halo + tm, :] = norm_mod(x_ref[0]).astype(bf16)
    if on_grid:
        h_ref[0:halo, :] = jnp.where(t > 0, norm_mod(xp_ref[0]), 0.0).astype(bf16)
        h_ref[halo + tm:te, :] = jnp.where(t < nt - 1, norm_mod(xn_ref[0]), 0.0).astype(bf16)

    SUB, LANE = SUBLANES, LANES
    seg = GRID_W if on_grid else seq_len
    sub_row = lax.broadcasted_iota(jnp.int32, (SUB, LANE), 0)
    first_row = sub_row == 0
    last_row = sub_row == SUB - 1
    row_taps = (0, 1, 2) if on_grid else (1,)

    def ff_cols(fc, base):
        if isinstance(fc, int):
            return slice(base + fc * FF_CHUNK, base + (fc + 1) * FF_CHUNK)
        return pl.ds(pl.multiple_of(base + fc * FF_CHUNK, 128), FF_CHUNK)

    def up(fc, slot):
        gbufs[slot][...] = _dot(h_ref[...], wu_ref[:, ff_cols(fc, 0)])
        vbufs[slot][...] = _dot(h_ref[halo:halo + tm, :], wu_ref[:, ff_cols(fc, D_FF)])

    def gate_act(fc, slot, aslot):
        cw = cw_ref[:, ff_cols(fc, 0)]
        cb = cb_ref[:, ff_cols(fc, 0)]
        for c in range(FF_CHUNK // LANE):
            lanes = slice(c * LANE, (c + 1) * LANE)
            for s0 in range(0, te, seg):
                gate = gbufs[slot][s0:s0 + seg, lanes]
                g_l = pltpu.roll(gate, 1, axis=0)
                g_r = pltpu.roll(gate, seg - 1, axis=0)
                gl_ref[s0:s0 + SUB, lanes] = jnp.where(first_row, 0.0, g_l[0:SUB])
                gl_ref[s0 + SUB:s0 + seg, lanes] = g_l[SUB:seg]
                gr_ref[s0:s0 + seg - SUB, lanes] = g_r[0:seg - SUB]
                gr_ref[s0 + seg - SUB:s0 + seg, lanes] = jnp.where(last_row, 0.0, g_r[seg - SUB:seg])
            for r0 in range(0, tm, GRID_W):
                conv = jnp.broadcast_to(cb[:, lanes], (GRID_W, LANE))
                for kh in row_taps:
                    lo = r0 + kh * GRID_W if on_grid else r0
                    conv = conv + gl_ref[lo:lo + GRID_W, lanes] * cw[3 * kh:3 * kh + 1, lanes]
                    conv = conv + gbufs[slot][lo:lo + GRID_W, lanes] * cw[3 * kh + 1:3 * kh + 2, lanes]
                    conv = conv + gr_ref[lo:lo + GRID_W, lanes] * cw[3 * kh + 2:3 * kh + 3, lanes]
                gelu = conv * (0.5 + 0.5 * jnp.tanh(conv * (GELU_C0 + GELU_C1 * (conv * conv))))
                acts[aslot][r0:r0 + GRID_W, lanes] = (gelu * vbufs[slot][r0:r0 + GRID_W, lanes]).astype(bf16)

    def down(fc, aslot):
        return _dot(acts[aslot][...], wd_ref[fc])


    def pair(p, parity, with_down):
        fc = 2 * p
        wr, rd = 2 * parity, 2 * (1 - parity)
        if with_down:
            acc_ref[...] += down(fc - 2, rd) + down(fc - 1, rd + 1)
        up(fc + 1, 1)
        gate_act(fc, 0, wr)
        up(fc + 2, 0)
        gate_act(fc + 1, 1, wr + 1)

    n_pairs = (N_FF_CHUNKS - 1) // 2
    assert N_FF_CHUNKS == 2 * n_pairs + 1 and n_pairs % 2 == 1
    up(0, 0)
    acc_ref[...] = jnp.zeros_like(acc_ref)
    pair(0, 0, False)

    @pl.when(first_step)
    def _():
        for cp in wu_late:
            cp.wait()
        wd_copy.wait()

    def two_pairs(i, carry):
        p = 2 * i + 1
        pair(p, 1, True)
        pair(p + 1, 0, True)
        return carry

    lax.fori_loop(0, (n_pairs - 1) // 2, two_pairs, 0)
    last = N_FF_CHUNKS - 1
    gate_act(last, 0, 2)
    ffn_out = acc_ref[...] + (down(last - 2, 0) + down(last - 1, 1) + down(last, 2))
    y_ref[0] = _rms(x_ref[0] + g2 * ffn_out, fg_ref[...])


FFN_W_UP, FFN_W_DOWN = 1, 2


def _ffn(x1, mod3, per_batch_mod, wts, tm, on_grid):
    B0, T0, _ = x1.shape
    if tm > T0:
        assert not per_batch_mod and not on_grid and tm % T0 == 0 and B0 % (tm // T0) == 0
        x1 = x1.reshape(B0 * T0 // tm, tm, D_MODEL)
    B, T, _ = x1.shape
    nt = T // tm
    halo = GRID_W if on_grid else 0
    te = tm + 2 * halo
    tok = pl.BlockSpec((1, tm, D_MODEL), lambda b, t: (b, t, 0))
    full = lambda a: pl.BlockSpec(a.shape, lambda b, t: (0,) * a.ndim)
    mod_map = (lambda b, t: (b + MOD_LAT_ROW, 0, 0)) if per_batch_mod else (lambda b, t: (MOD_CTX_ROW, 0, 0))
    in_specs = [tok]
    args = [x1]
    if on_grid:
        r = tm // GRID_W
        in_specs += [pl.BlockSpec((1, GRID_W, D_MODEL), lambda b, t: (b, jnp.maximum(t * r - 1, 0), 0)),
                     pl.BlockSpec((1, GRID_W, D_MODEL), lambda b, t: (b, jnp.minimum((t + 1) * r, T // GRID_W - 1), 0))]
        args += [x1, x1]
    in_specs += [pl.BlockSpec((1, 1, N_MOD * D_MODEL), mod_map)] + [
        pl.BlockSpec(memory_space=pl.ANY) if i in (FFN_W_UP, FFN_W_DOWN) else full(w) for i, w in enumerate(wts)]
    args += [mod3] + list(wts)
    y = pl.pallas_call(
        functools.partial(_ffn_kernel, tm=tm, nt=nt, on_grid=on_grid, seq_len=T0),
        grid=(B, nt),
        in_specs=in_specs,
        out_specs=tok,
        out_shape=jax.ShapeDtypeStruct((B, T, D_MODEL), f32),
        scratch_shapes=[pltpu.VMEM((te, D_MODEL), bf16)]
        + [pltpu.VMEM((te, FF_CHUNK), f32)] * 2 + [pltpu.VMEM((tm, FF_CHUNK), f32)] * 2
        + [pltpu.VMEM((te, FF_CHUNK), f32)] * 2 + [pltpu.VMEM((tm, FF_CHUNK), bf16)] * 4
        + [pltpu.VMEM((tm, D_MODEL), f32)]
        + [pltpu.VMEM(wts[FFN_W_DOWN].shape, bf16), pltpu.SemaphoreType.DMA(()),
           pltpu.VMEM(wts[FFN_W_UP].shape, bf16), pltpu.SemaphoreType.DMA((4,))],
        compiler_params=pltpu.CompilerParams(dimension_semantics=("arbitrary", "arbitrary"),
                                             vmem_limit_bytes=VMEM_LIMIT),
        name="ffn_grid" if on_grid else "ffn_ctx",
    )(*args)
    return y.reshape(B0, T0, D_MODEL)


TOKEN_TILE = 512
PROJ_TILE = 1024
MLSTM_TOKENS = 1024


def _tiles(T, per_batch_mod):
    span = TOKEN_TILE if not per_batch_mod else min(TOKEN_TILE, T)
    proj = PROJ_TILE if not per_batch_mod else min(PROJ_TILE, T)
    return proj, span, span, max(1, MLSTM_TOKENS // T)


def _trunk(x, mod3, per_batch_mod, state, on_grid, w, cast_weights=()):
    tm_proj, tm_mix, tm_ffn, mlstm_nseq = _tiles(x.shape[1], per_batch_mod)
    (q, k, v, so, glu, sga, sgb, gates), casts = _in_proj(x, mod3, per_batch_mod, w["norm1_g"], w["in"], tm_proj,
                                                          *w["in_groups"], cast_weights=cast_weights)
    if cast_weights:
        w = dict(w, mix=w["mix"](casts), ffn=w["ffn"](casts))
    hn, *new_state = _mlstm(q, k, v, gates, state, mlstm_nseq, state is None)
    x1 = _mix_out(x, mod3, per_batch_mod, hn, so, glu, sga, sgb, w["mix"], tm_mix)
    y = _ffn(x1, mod3, per_batch_mod, w["ffn"], tm_ffn, on_grid)
    return y, new_state, w


def kernel(x_prompt, x_sample, c, state_C, state_n, state_m, c_ctx, w_ada, b_ada, norm1_g, w_in, b_in,
           mlstm_norm_g, w_proj_a, conv_dw_w, conv_dw_b, conv_ln_g, conv_ln_b, w_proj_b, w_out, norm2_g,
           w_up, ffn_dw_w, ffn_dw_b, w_down, final_norm_g):
    Bp = x_prompt.shape[0]
    Bl = x_sample.shape[0]
    l = 0
    row = lambda a: a.reshape(1, -1)

    ct = jnp.concatenate([c_ctx[None, :], c, jnp.zeros((8 - Bl - 1, D_MODEL), f32)], axis=0).T
    mod = _adaln_mod(ct, w_ada[l], row(b_ada[l]), Bl + 1)

    o_g = 4 * D_A
    o_u = o_g + 4 * NH_A
    nh = NH_A

    def gate_cols(a):
        g = a[..., o_g:o_u]
        pad = jnp.zeros(a.shape[:-1] + (GATE_PAD - 4 * nh,), a.dtype)
        return jnp.concatenate([g[..., 0:nh], g[..., 2 * nh:3 * nh], g[..., nh:2 * nh], g[..., 3 * nh:], pad], axis=-1)

    w_in_t = jnp.swapaxes(w_in[l], 0, 1).astype(bf16)
    bi = row(b_in[l])
    g_t = w_in_t[o_g:o_u]
    w_gates_t = jnp.concatenate([g_t[0:nh], g_t[2 * nh:3 * nh], g_t[nh:2 * nh], g_t[3 * nh:],
                                 jnp.zeros((GATE_PAD - 4 * nh, D_MODEL), bf16)], axis=0)
    w_in_parts = (w_in_t, bi[:, :o_g], bi[:, o_u:], w_gates_t, gate_cols(bi))
    cast_weights = [w_up[l], w_down[l], w_out[l], w_proj_a[l], w_proj_b[l]]

    def w_mix(casts):
        return (row(mlstm_norm_g[l]), casts[3],
                jnp.pad(conv_dw_w[l], ((0, 1), (0, 0))), row(conv_dw_b[l]), row(conv_ln_g[l]), row(conv_ln_b[l]),
                casts[4], casts[2])

    def w_ffn(casts):
        return (row(norm2_g[l]), casts[0], casts[1].reshape(N_FF_CHUNKS, FF_CHUNK, D_MODEL),
                ffn_dw_w[l].reshape(9, D_FF), row(ffn_dw_b[l]), row(final_norm_g))

    w = {"norm1_g": row(norm1_g[l]), "in": w_in_parts, "in_groups": (o_g, o_u), "mix": w_mix, "ffn": w_ffn}

    c0 = state_C[:, l]
    n0 = state_n[:, l].reshape(Bl, 2 * NH_A, DH_A)
    m0 = jnp.broadcast_to(state_m[:, l].reshape(Bl, 2 * NH_A, 1), (Bl, 2 * NH_A, DH_A))
    y_sample, _, w = _trunk(x_sample, mod, True, (c0, n0, m0), True, w, cast_weights)

    y_prompt, (cn, nn, mn), _ = _trunk(x_prompt, mod, False, None, False, w)

    new_state_C = cn[:, None]
    new_state_n = nn.reshape(Bp, 1, 2, NH_A, DH_A)
    new_state_m = mn[:, :, 0].reshape(Bp, 1, 2, NH_A)
    return (y_prompt, y_sample, new_state_C, new_state_n, new_state_m)
```
